```python
import jax, jax.numpy as jnp
from jax import lax
import numpy as np

D_MODEL = 4096
BATCH = 4
SEQ = 2048
DEPTH = 1
DEC_BATCH = 16
DEC_SEQ = 64
PAST_LEN = 1024

CHUNK = 64
D_MIX = D_MODEL
ATT_WIDTH = D_MIX // 2
HEAD_DIM = 64
N_HEADS = ATT_WIDTH // HEAD_DIM
N_KV_HEADS = N_HEADS // 8
KV_GROUP = N_HEADS // N_KV_HEADS
WINDOW = 128
WIN_CHUNKS = WINDOW // CHUNK
ROT_DIM = HEAD_DIM // 4
ROPE_THETA = 500000.0
LRU_WIDTH = D_MIX - ATT_WIDTH
LRU_BLOCKS = 16
LRU_BLOCK = LRU_WIDTH // LRU_BLOCKS
CONV_WIDTH = 4
LRU_C = 8.0
Q_COLS = N_HEADS * HEAD_DIM
KV_COLS = N_KV_HEADS * HEAD_DIM
IN_COLS = Q_COLS + 2 * KV_COLS + 2 * LRU_WIDTH
N_GROUPS = 8
EXPERTS_PER_GROUP = 4
N_EXPERTS = N_GROUPS * EXPERTS_PER_GROUP
TOP_K = 2
D_EXPERT = 256
EPS = 1e-6
NEG = -1e30

kernel_name = "hymba_swa_rglru_hiermoe_stream_step"


def rmsnorm(x, g):
    x32 = x.astype(jnp.float32)
    y = x32 * lax.rsqrt(jnp.mean(x32 * x32, axis=-1, keepdims=True) + EPS)
    return (y * g.astype(jnp.float32)).astype(x.dtype)


def partial_rope(x, pos):
    half = ROT_DIM // 2
    inv = ROPE_THETA ** (-jnp.arange(half, dtype=jnp.float32) / half)
    ang = pos.astype(jnp.float32)[:, None] * inv[None, :]
    cos = jnp.cos(ang)[None, :, None, :]
    sin = jnp.sin(ang)[None, :, None, :]
    xr = x[..., :ROT_DIM].astype(jnp.float32)
    x1, x2 = xr[..., :half], xr[..., half:]
    rot = jnp.concatenate([x1 * cos - x2 * sin, x2 * cos + x1 * sin], axis=-1).astype(x.dtype)
    return jnp.concatenate([rot, x[..., ROT_DIM:]], axis=-1)


def chunk_band_mask(qpos, kpos):
    qc = qpos[..., :, None] // CHUNK
    kc = kpos[..., None, :] // CHUNK
    return (kpos[..., None, :] >= 0) & (kc <= qc) & (kc >= qc - WIN_CHUNKS)


def sink_attention(q, k, v, mask, sinks):
    B, N, Lq = q.shape[:3]
    qg = q.reshape(B, N, Lq, N_KV_HEADS, KV_GROUP, HEAD_DIM).astype(jnp.float32)
    s = jnp.einsum('bnqhgd,bnshd->bnhgqs', qg, k.astype(jnp.float32)) * (HEAD_DIM ** -0.5)
    s = jnp.where(mask[None, :, None, None], s, NEG)
    sink = sinks.astype(jnp.float32).reshape(N_KV_HEADS, KV_GROUP)[None, None, :, :, None, None]
    m = jnp.maximum(jnp.max(s, axis=-1, keepdims=True), sink)
    p = jnp.exp(s - m)
    p = p / (jnp.sum(p, axis=-1, keepdims=True) + jnp.exp(sink - m))
    o = jnp.einsum('bnhgqs,bnshd->bnqhgd', p, v.astype(jnp.float32))
    return o.reshape(B, N, Lq, N_HEADS * HEAD_DIM).astype(q.dtype)


def swa_prompt(q, k, v, sinks):
    B, S = q.shape[:2]
    nc = S // CHUNK
    qb = q.reshape(B, nc, CHUNK, N_HEADS, HEAD_DIM)

    def band(t):
        tb = t.reshape(B, nc, CHUNK, N_KV_HEADS, HEAD_DIM)
        tp = jnp.pad(tb, ((0, 0), (WIN_CHUNKS, 0), (0, 0), (0, 0), (0, 0)))
        return jnp.concatenate([tp[:, j:j + nc] for j in range(WIN_CHUNKS + 1)], axis=2)

    qpos = jnp.arange(S).reshape(nc, CHUNK)
    kpos = (jnp.arange(nc)[:, None] - WIN_CHUNKS) * CHUNK + jnp.arange((WIN_CHUNKS + 1) * CHUNK)[None, :]
    mask = chunk_band_mask(qpos, kpos)
    o = sink_attention(qb, band(k), band(v), mask, sinks)
    return o.reshape(B, S, ATT_WIDTH)


def swa_sample(q, k, v, pos, k_cache, v_cache, sinks):
    B, L = q.shape[:2]
    W = k_cache.shape[1]
    kk = jnp.concatenate([k_cache.astype(k.dtype), k], axis=1)
    vv = jnp.concatenate([v_cache.astype(v.dtype), v], axis=1)
    qpos = pos[None, :]
    kpos = jnp.concatenate([pos[0] - W + jnp.arange(W), pos])[None, :]
    mask = chunk_band_mask(qpos, kpos)
    o = sink_attention(q[:, None], kk[:, None], vv[:, None], mask, sinks)
    return o.reshape(B, L, ATT_WIDTH), kk[:, -W:], vv[:, -W:]


def rglru(xb, conv_buf, h0, conv_w, conv_b, w_gate_a, b_gate_a, w_gate_x, b_gate_x, lru_lambda):
    B, L, _ = xb.shape
    xp = jnp.concatenate([conv_buf.astype(xb.dtype), xb], axis=1)
    xc = conv_b
    for j in range(CONV_WIDTH):
        xc = xc + xp[:, j:j + L] * conv_w[j]
    xc = xc.astype(xb.dtype)
    new_buf = xp[:, -(CONV_WIDTH - 1):]
    xblk = xc.reshape(B, L, LRU_BLOCKS, LRU_BLOCK)
    r = jax.nn.sigmoid(jnp.einsum('blnc,ncd->blnd', xblk, w_gate_a) + b_gate_a).reshape(B, L, LRU_WIDTH)
    i = jax.nn.sigmoid(jnp.einsum('blnc,ncd->blnd', xblk, w_gate_x) + b_gate_x).reshape(B, L, LRU_WIDTH)
    log_a = -LRU_C * r.astype(jnp.float32) * jax.nn.softplus(-lru_lambda.astype(jnp.float32))
    a = jnp.exp(log_a)
    b = jnp.sqrt(-jnp.expm1(2.0 * log_a)) * i.astype(jnp.float32) * xc.astype(jnp.float32)
    b = b.at[:, 0].add(a[:, 0] * h0.astype(jnp.float32))

    def combine(e1, e2):
        a1, b1 = e1
        a2, b2 = e2
        return a1 * a2, a2 * b1 + b2

    _, h = lax.associative_scan(combine, (a, b), axis=1)
    return h.astype(xb.dtype), new_buf, h[:, -1]


def hier_moe(h, w_group, b_group, w_expert_router, b_expert_router, w1, w3, w2):
    B, L, D = h.shape
    t = h.reshape(B * L, D)
    g_logits = (t @ w_group + b_group).astype(jnp.float32)
    g_prob = jax.nn.softmax(g_logits, axis=-1)
    g_idx = jnp.argmax(g_logits, axis=-1)
    g_w = jnp.take_along_axis(g_prob, g_idx[:, None], axis=1)[:, 0]
    e_logits = (t @ w_expert_router + b_expert_router).astype(jnp.float32)
    e_logits = e_logits.reshape(-1, N_GROUPS, EXPERTS_PER_GROUP)
    e_in = jnp.take_along_axis(e_logits, g_idx[:, None, None], axis=1)[:, 0]
    top_v, top_i = lax.top_k(e_in, TOP_K)
    top_w = jax.nn.softmax(top_v, axis=-1) * g_w[:, None]
    expert_id = g_idx[:, None] * EXPERTS_PER_GROUP + top_i
    gates = jnp.einsum('tk,tke->te', top_w, jax.nn.one_hot(expert_id, N_EXPERTS, dtype=jnp.float32))
    hid = jax.nn.silu(jnp.einsum('td,edf->tef', t, w1)) * jnp.einsum('td,edf->tef', t, w3)
    hid = hid * gates.astype(hid.dtype)[..., None]
    y = jnp.einsum('tef,efd->td', hid, w2)
    return y.reshape(B, L, D)


def encoder_layer(x, pos, k_cache, v_cache, conv_buf, h0,
                  norm_mix, w_in, conv_w, conv_b, w_gate_a, b_gate_a, w_gate_x, b_gate_x,
                  lru_lambda, sinks, norm_attn_out, norm_lru_out, w_out,
                  norm_ffn, w_group, b_group, w_expert_router, b_expert_router, w1, w3, w2):
    B, L, _ = x.shape
    h = rmsnorm(x, norm_mix)
    z = h @ w_in
    q, k, v, xb, yb = jnp.split(z, [Q_COLS, Q_COLS + KV_COLS, Q_COLS + 2 * KV_COLS,
                                    Q_COLS + 2 * KV_COLS + LRU_WIDTH], axis=-1)
    q = partial_rope(q.reshape(B, L, N_HEADS, HEAD_DIM), pos)
    k = partial_rope(k.reshape(B, L, N_KV_HEADS, HEAD_DIM), pos)
    v = v.reshape(B, L, N_KV_HEADS, HEAD_DIM)
    if k_cache is None:
        att = swa_prompt(q, k, v, sinks)
        win = min(WINDOW, L)
        k_new, v_new = k[:, -win:], v[:, -win:]
        conv_buf = jnp.zeros((B, CONV_WIDTH - 1, LRU_WIDTH), x.dtype)
        h0 = jnp.zeros((B, LRU_WIDTH), jnp.float32)
    else:
        att, k_new, v_new = swa_sample(q, k, v, pos, k_cache, v_cache, sinks)
    lru, conv_new, h_last = rglru(xb, conv_buf, h0, conv_w, conv_b, w_gate_a, b_gate_a,
                                  w_gate_x, b_gate_x, lru_lambda)
    lru = jax.nn.gelu(yb) * lru
    mix = jnp.concatenate([rmsnorm(att, norm_attn_out), rmsnorm(lru, norm_lru_out)], axis=-1)
    x = x + mix @ w_out
    x = x + hier_moe(rmsnorm(x, norm_ffn), w_group, b_group, w_expert_router, b_expert_router, w1, w3, w2)
    return x, k_new, v_new, conv_new, h_last


def setup_inputs(seed: int = 0) -> dict:
    key = jax.random.key(seed)
    ks = jax.random.split(key, 32)
    f32 = jnp.float32

    def nrm(k, shape, scale):
        return jax.random.normal(k, shape, f32) * scale

    cw = min(WINDOW, PAST_LEN)
    u = jax.random.uniform(ks[14], (DEPTH, LRU_WIDTH), f32, 0.9, 0.999)
    base = u ** (1.0 / LRU_C)
    lam = jnp.log(base) - jnp.log1p(-base)
    return {
        "x_prompt": nrm(ks[0], (BATCH, SEQ, D_MODEL), 1.0),
        "x_sample": nrm(ks[1], (DEC_BATCH, DEC_SEQ, D_MODEL), 1.0),
        "cache_k": nrm(ks[2], (DEPTH, DEC_BATCH, cw, N_KV_HEADS, HEAD_DIM), 1.0),
        "cache_v": nrm(ks[3], (DEPTH, DEC_BATCH, cw, N_KV_HEADS, HEAD_DIM), 1.0),
        "state_conv": nrm(ks[4], (DEPTH, DEC_BATCH, CONV_WIDTH - 1, LRU_WIDTH), 1.0),
        "state_h": nrm(ks[5], (DEPTH, DEC_BATCH, LRU_WIDTH), 0.5),
        "norm_mix": 1.0 + nrm(ks[6], (DEPTH, D_MODEL), 0.02),
        "w_in": nrm(ks[7], (DEPTH, D_MODEL, IN_COLS), D_MODEL ** -0.5),
        "conv_w": nrm(ks[8], (DEPTH, CONV_WIDTH, LRU_WIDTH), CONV_WIDTH ** -0.5),
        "conv_b": nrm(ks[9], (DEPTH, LRU_WIDTH), 0.01),
        "w_gate_a": nrm(ks[10], (DEPTH, LRU_BLOCKS, LRU_BLOCK, LRU_BLOCK), LRU_BLOCK ** -0.5),
        "b_gate_a": nrm(ks[11], (DEPTH, LRU_BLOCKS, LRU_BLOCK), 0.01),
        "w_gate_x": nrm(ks[12], (DEPTH, LRU_BLOCKS, LRU_BLOCK, LRU_BLOCK), LRU_BLOCK ** -0.5),
        "b_gate_x": nrm(ks[13], (DEPTH, LRU_BLOCKS, LRU_BLOCK), 0.01),
        "lru_lambda": lam,
        "sinks": nrm(ks[15], (DEPTH, N_HEADS), 0.5),
        "norm_attn_out": 1.0 + nrm(ks[16], (DEPTH, ATT_WIDTH), 0.02),
        "norm_lru_out": 1.0 + nrm(ks[17], (DEPTH, LRU_WIDTH), 0.02),
        "w_out": nrm(ks[18], (DEPTH, D_MIX, D_MODEL), D_MIX ** -0.5),
        "norm_ffn": 1.0 + nrm(ks[19], (DEPTH, D_MODEL), 0.02),
        "w_group": nrm(ks[20], (DEPTH, D_MODEL, N_GROUPS), D_MODEL ** -0.5),
        "b_group": nrm(ks[21], (DEPTH, N_GROUPS), 0.01),
        "w_expert_router": nrm(ks[22], (DEPTH, D_MODEL, N_EXPERTS), D_MODEL ** -0.5),
        "b_expert_router": nrm(ks[23], (DEPTH, N_EXPERTS), 0.01),
        "w1": nrm(ks[24], (DEPTH, N_EXPERTS, D_MODEL, D_EXPERT), D_MODEL ** -0.5),
        "w3": nrm(ks[25], (DEPTH, N_EXPERTS, D_MODEL, D_EXPERT), D_MODEL ** -0.5),
        "w2": nrm(ks[26], (DEPTH, N_EXPERTS, D_EXPERT, D_MODEL), D_EXPERT ** -0.5),
        "norm_final": 1.0 + nrm(ks[27], (D_MODEL,), 0.02),
    }


def reference(x_prompt, x_sample, cache_k, cache_v, state_conv, state_h,
              norm_mix, w_in, conv_w, conv_b, w_gate_a, b_gate_a, w_gate_x, b_gate_x,
              lru_lambda, sinks, norm_attn_out, norm_lru_out, w_out,
              norm_ffn, w_group, b_group, w_expert_router, b_expert_router, w1, w3, w2,
              norm_final):
    pos_p = jnp.arange(x_prompt.shape[1])
    pos_s = PAST_LEN + jnp.arange(x_sample.shape[1])
    yp, ys = x_prompt, x_sample
    kp, vp, cp, hp = [], [], [], []
    ksl, vsl, csl, hsl = [], [], [], []
    for l in range(DEPTH):
        lw = dict(norm_mix=norm_mix[l], w_in=w_in[l], conv_w=conv_w[l], conv_b=conv_b[l],
                  w_gate_a=w_gate_a[l], b_gate_a=b_gate_a[l], w_gate_x=w_gate_x[l], b_gate_x=b_gate_x[l],
                  lru_lambda=lru_lambda[l], sinks=sinks[l], norm_attn_out=norm_attn_out[l],
                  norm_lru_out=norm_lru_out[l], w_out=w_out[l], norm_ffn=norm_ffn[l],
                  w_group=w_group[l], b_group=b_group[l], w_expert_router=w_expert_router[l],
                  b_expert_router=b_expert_router[l], w1=w1[l], w3=w3[l], w2=w2[l])
        yp, k_a, v_a, c_a, h_a = encoder_layer(yp, pos_p, None, None, None, None, **lw)
        ys, k_b, v_b, c_b, h_b = encoder_layer(ys, pos_s, cache_k[l], cache_v[l],
                                               state_conv[l], state_h[l], **lw)
        kp.append(k_a); vp.append(v_a); cp.append(c_a); hp.append(h_a)
        ksl.append(k_b); vsl.append(v_b); csl.append(c_b); hsl.append(h_b)
    y_prompt = rmsnorm(yp, norm_final)
    y_sample = rmsnorm(ys, norm_final)
    return (y_prompt, y_sample,
            jnp.stack(kp), jnp.stack(vp), jnp.stack(cp), jnp.stack(hp),
            jnp.stack(ksl), jnp.stack(vsl), jnp.stack(csl), jnp.stack(hsl))
```

```python
import functools

import jax
import jax.numpy as jnp
from jax import lax
from jax.experimental import pallas as pl
from jax.experimental.pallas import tpu as pltpu

F32 = jnp.float32
BF16 = jnp.bfloat16

CHUNK = 64
HEAD_DIM = 64
KV_GROUP = 8
WIN_CHUNKS = 2
ROT_DIM = 16
ROPE_THETA = 500000.0
LRU_BLOCK = 128
CONV_WIDTH = 4
LRU_C = 8.0
N_GROUPS = 8
EXPERTS_PER_GROUP = 4
EPS = 1e-6
NEG = -1e30
LANES = 128
VMEM_LIMIT = 56 * 1024 * 1024


def _params(n_axes, vmem=VMEM_LIMIT):
    return pltpu.CompilerParams(
        dimension_semantics=("arbitrary",) * n_axes, vmem_limit_bytes=vmem)


def _rms_scale(x):
    return lax.rsqrt(jnp.mean(x * x, axis=-1, keepdims=True) + EPS)


def _rope_block(zb, c, s_hi, s_lo):
    return (zb * c + pltpu.roll(zb, LANES - ROT_DIM // 2, 1) * s_hi
            + pltpu.roll(zb, ROT_DIM // 2, 1) * s_lo)


def _inproj_kernel(xp_ref, xs_ref, g_ref, w_ref, c_ref, shi_ref, slo_ref, o_ref, h_ref,
                   *, n_prompt_tiles, n_q_tiles, kv_tile):
    i = pl.program_id(0)
    j = pl.program_id(1)

    def norm(x_ref):
        x = x_ref[...]
        h_ref[...] = (x * _rms_scale(x) * g_ref[...]).astype(BF16)

    @pl.when(jnp.logical_and(j == 0, i < n_prompt_tiles))
    def _():
        norm(xp_ref)

    @pl.when(jnp.logical_and(j == 0, i >= n_prompt_tiles))
    def _():
        norm(xs_ref)

    z = jnp.dot(h_ref[...], w_ref[...], preferred_element_type=F32)
    n_blk = z.shape[1] // LANES

    def rope_cols(n_rope):
        c, shi, slo = c_ref[...], shi_ref[...], slo_ref[...]
        for b in range(n_blk):
            zb = z[:, b * LANES:(b + 1) * LANES]
            if b < n_rope:
                zb = _rope_block(zb, c, shi, slo)
            o_ref[:, b * LANES:(b + 1) * LANES] = zb

    @pl.when(j < n_q_tiles)
    def _():
        rope_cols(n_blk)

    @pl.when(j == kv_tile)
    def _():
        rope_cols(n_blk // 2)

    @pl.when(jnp.logical_and(j >= n_q_tiles, j != kv_tile))
    def _():
        o_ref[...] = z


def _in_proj(xp, xs, g, w, c, shi, slo, *, tm, tn, n_q_tiles, kv_tile):
    n_p, d = xp.shape
    n_s = xs.shape[0]
    t = n_p + n_s
    npt = n_p // tm
    n_cols = w.shape[1]
    kern = functools.partial(_inproj_kernel, n_prompt_tiles=npt, n_q_tiles=n_q_tiles,
                             kv_tile=kv_tile)
    return pl.pallas_call(
        kern,
        grid=(t // tm, n_cols // tn),
        in_specs=[
            pl.BlockSpec((tm, d), lambda i, j: (jnp.minimum(i, npt - 1), 0)),
            pl.BlockSpec((tm, d), lambda i, j: (jnp.maximum(i - npt, 0), 0),
                         pipeline_mode=pl.Buffered(1)),
            pl.BlockSpec((1, d), lambda i, j: (0, 0)),
            pl.BlockSpec((d, tn), lambda i, j: (0, j)),
            pl.BlockSpec((tm, LANES), lambda i, j: (i, 0)),
            pl.BlockSpec((tm, LANES), lambda i, j: (i, 0)),
            pl.BlockSpec((tm, LANES), lambda i, j: (i, 0)),
        ],
        out_specs=pl.BlockSpec((tm, tn), lambda i, j: (i, j)),
        out_shape=jax.ShapeDtypeStruct((t, n_cols), F32),
        scratch_shapes=[pltpu.VMEM((tm, d), BF16)],
        compiler_params=_params(2),
        name="in_proj",
    )(xp, xs, g, w, c, shi, slo)


def _attn_kernel(sinks_ref, q_ref, k0_ref, k1_ref, k2_ref, v0_ref, v1_ref, v2_ref,
                 kc_ref, vc_ref, g_ref, o_ref, *, n_prompt_chunks, chunks_per_seq, n_kv):
    n = pl.program_id(0)
    is_s = n >= n_prompt_chunks
    c = n % chunks_per_seq
    lo = jnp.where(is_s, 0, jnp.where(c >= 2, 0, jnp.where(c == 1, CHUNK, 2 * CHUNK)))

    kc = kc_ref[0]
    vc = vc_ref[0]
    k_band = jnp.concatenate([
        jnp.where(is_s, kc[:CHUNK], k0_ref[...]),
        jnp.where(is_s, kc[CHUNK:], k1_ref[...]),
        k2_ref[...]], axis=0).astype(BF16)
    v_band = jnp.concatenate([
        jnp.where(is_s, vc[:CHUNK], v0_ref[...]),
        jnp.where(is_s, vc[CHUNK:], v1_ref[...]),
        v2_ref[...]], axis=0).astype(BF16)

    q = (q_ref[...] * (HEAD_DIM ** -0.5)).astype(BF16)
    n_keys = (WIN_CHUNKS + 1) * CHUNK
    col = lax.broadcasted_iota(jnp.int32, (KV_GROUP * CHUNK, n_keys), 1)
    valid = col >= lo

    outs = []
    for gi in range(n_kv):
        kg = k_band[:, gi * HEAD_DIM:(gi + 1) * HEAD_DIM]
        vg = v_band[:, gi * HEAD_DIM:(gi + 1) * HEAD_DIM]
        heads = [gi * KV_GROUP + h for h in range(KV_GROUP)]
        qg = jnp.concatenate([q[:, h * HEAD_DIM:(h + 1) * HEAD_DIM] for h in heads], axis=0)
        s = lax.dot_general(qg, kg, (((1,), (1,)), ((), ())), preferred_element_type=F32)
        s = jnp.where(valid, s, NEG)
        sink = jnp.concatenate(
            [jnp.full((CHUNK, 1), sinks_ref[h], F32) for h in heads], axis=0)
        m = jnp.maximum(jnp.max(s, axis=-1, keepdims=True), sink)
        p = jnp.exp(s - m)
        denom = jnp.sum(p, axis=-1, keepdims=True) + jnp.exp(sink - m)
        og = jnp.dot(p.astype(BF16), vg, preferred_element_type=F32) / denom
        outs.extend(og[h * CHUNK:(h + 1) * CHUNK] for h in range(KV_GROUP))
    att = jnp.concatenate(outs, axis=1)
    o_ref[...] = (att * _rms_scale(att) * g_ref[...]).astype(o_ref.dtype)


def _attention(z, cache_k, cache_v, sinks, g, *, n_prompt_chunks, chunks_per_seq,
               att_width, n_kv, k_col_blk, v_col_blk):
    t = z.shape[0]
    n_items = t // CHUNK
    kvw = n_kv * HEAD_DIM
    npc = n_prompt_chunks

    def hist(d):
        return lambda n, s: (n - jnp.minimum(d, n % chunks_per_seq), k_col_blk)

    def histv(d):
        return lambda n, s: (n - jnp.minimum(d, n % chunks_per_seq), v_col_blk)

    cache_map = lambda n, s: (jnp.maximum(n - npc, 0), 0, 0)
    kern = functools.partial(_attn_kernel, n_prompt_chunks=npc,
                             chunks_per_seq=chunks_per_seq, n_kv=n_kv)
    grid_spec = pltpu.PrefetchScalarGridSpec(
        num_scalar_prefetch=1,
        grid=(n_items,),
        in_specs=[
            pl.BlockSpec((CHUNK, att_width), lambda n, s: (n, 0)),
            pl.BlockSpec((CHUNK, kvw), hist(2)),
            pl.BlockSpec((CHUNK, kvw), hist(1)),
            pl.BlockSpec((CHUNK, kvw), hist(0)),
            pl.BlockSpec((CHUNK, kvw), histv(2)),
            pl.BlockSpec((CHUNK, kvw), histv(1)),
            pl.BlockSpec((CHUNK, kvw), histv(0)),
            pl.BlockSpec((1, WIN_CHUNKS * CHUNK, kvw), cache_map),
            pl.BlockSpec((1, WIN_CHUNKS * CHUNK, kvw), cache_map),
            pl.BlockSpec((1, att_width), lambda n, s: (0, 0)),
        ],
        out_specs=pl.BlockSpec((CHUNK, att_width), lambda n, s: (n, 0)),
    )
    return pl.pallas_call(
        kern,
        grid_spec=grid_spec,
        out_shape=jax.ShapeDtypeStruct((t, att_width), BF16),
        compiler_params=_params(1),
        name="attention",
    )(sinks, z, z, z, z, z, z, z, cache_k, cache_v, g)


def _shift_rows(x, d, fill, row):
    return jnp.where(row >= d, pltpu.roll(x, d, 0), fill)


def _lru_kernel(xb_ref, yb_ref, sconv_ref, sh_ref, cw_ref, cb_ref, wa_ref, ba_ref,
                wx_ref, bx_ref, lam_ref, g_ref, o_ref, hl_ref, hist_ref, hcar_ref,
                *, n_prompt_chunks, chunks_per_seq):
    n = pl.program_id(0)
    is_s = n >= n_prompt_chunks
    hist_rows = hist_ref.shape[0]

    @pl.when(jnp.logical_and(jnp.logical_not(is_s), n % chunks_per_seq == 0))
    def _():
        hist_ref[...] = jnp.zeros_like(hist_ref)
        hcar_ref[...] = jnp.zeros_like(hcar_ref)

    @pl.when(is_s)
    def _():
        hist_ref[...] = jnp.zeros_like(hist_ref)
        hist_ref[hist_rows - (CONV_WIDTH - 1):, :] = sconv_ref[0]
        hcar_ref[...] = sh_ref[0]

    x = xb_ref[...]
    rows, width = x.shape
    ext = jnp.concatenate([hist_ref[...], x], axis=0)
    xc = cb_ref[...]
    for j in range(CONV_WIDTH):
        off = hist_rows - (CONV_WIDTH - 1) + j
        xc = xc + ext[off:off + rows] * cw_ref[j:j + 1, :]
    hist_ref[...] = x[rows - hist_rows:]

    xcb = xc.astype(BF16)
    ra, rx = [], []
    for nb in range(width // LRU_BLOCK):
        blk = xcb[:, nb * LRU_BLOCK:(nb + 1) * LRU_BLOCK]
        ra.append(jnp.dot(blk, wa_ref[nb], preferred_element_type=F32))
        rx.append(jnp.dot(blk, wx_ref[nb], preferred_element_type=F32))
    r = jax.nn.sigmoid(jnp.concatenate(ra, axis=1) + ba_ref[...])
    ig = jax.nn.sigmoid(jnp.concatenate(rx, axis=1) + bx_ref[...])
    log_a = (-LRU_C * r) * jax.nn.softplus(-lam_ref[...])
    a = jnp.exp(log_a)
    b = jnp.sqrt(-jnp.tanh(log_a) * (1.0 + a * a)) * ig * xc

    row = lax.broadcasted_iota(jnp.int32, (rows, width), 0)
    d = 1
    while d < rows:
        a_sh = _shift_rows(a, d, 1.0, row)
        b_sh = _shift_rows(b, d, 0.0, row)
        b = a * b_sh + b
        a = a * a_sh
        d *= 2
    h = b + a * hcar_ref[...]
    h_last = h[rows - 1:rows, :]
    hcar_ref[...] = h_last
    hl_ref[0] = h_last

    y = jax.nn.gelu(yb_ref[...]) * h
    o_ref[...] = (y * _rms_scale(y) * g_ref[...]).astype(o_ref.dtype)


def _rglru(z, state_conv, state_h, conv_w, conv_b, wa, ba, wx, bx, lam, g,
           *, n_prompt_chunks, chunks_per_seq, width, xb_col_blk, yb_col_blk):
    t = z.shape[0]
    n_items = t // CHUNK
    npc = n_prompt_chunks
    nblk = width // LRU_BLOCK
    state_map = lambda n: (jnp.maximum(n - npc, 0), 0, 0)
    full2 = lambda n: (0, 0)
    full3 = lambda n: (0, 0, 0)
    kern = functools.partial(_lru_kernel, n_prompt_chunks=npc, chunks_per_seq=chunks_per_seq)
    return pl.pallas_call(
        kern,
        grid=(n_items,),
        in_specs=[
            pl.BlockSpec((CHUNK, width), lambda n: (n, xb_col_blk)),
            pl.BlockSpec((CHUNK, width), lambda n: (n, yb_col_blk)),
            pl.BlockSpec((1, CONV_WIDTH - 1, width), state_map),
            pl.BlockSpec((1, 1, width), state_map),
            pl.BlockSpec((CONV_WIDTH, width), full2),
            pl.BlockSpec((1, width), full2),
            pl.BlockSpec((nblk, LRU_BLOCK, LRU_BLOCK), full3),
            pl.BlockSpec((1, width), full2),
            pl.BlockSpec((nblk, LRU_BLOCK, LRU_BLOCK), full3),
            pl.BlockSpec((1, width), full2),
            pl.BlockSpec((1, width), full2),
            pl.BlockSpec((1, width), full2),
        ],
        out_specs=[
            pl.BlockSpec((CHUNK, width), lambda n: (n, 0)),
            pl.BlockSpec((1, 1, width), lambda n: (n, 0, 0)),
        ],
        out_shape=[
            jax.ShapeDtypeStruct((t, width), BF16),
            jax.ShapeDtypeStruct((n_items, 1, width), F32),
        ],
        scratch_shapes=[pltpu.VMEM((8, width), F32), pltpu.VMEM((1, width), F32)],
        compiler_params=_params(1),
        name="rglru",
    )(z, z, state_conv, state_h, conv_w, conv_b, wa, ba, wx, bx, lam, g)


def _outproj_kernel(att_ref, lru_ref, wa_ref, wl_ref, xp_ref, xs_ref, o_ref, *, n_prompt_tiles):
    i = pl.program_id(0)
    acc = jnp.dot(att_ref[...], wa_ref[...], preferred_element_type=F32)
    acc = acc + jnp.dot(lru_ref[...], wl_ref[...], preferred_element_type=F32)

    @pl.when(i < n_prompt_tiles)
    def _():
        o_ref[...] = xp_ref[...] + acc

    @pl.when(i >= n_prompt_tiles)
    def _():
        o_ref[...] = xs_ref[...] + acc


def _out_proj(att, lru, w_att, w_lru, xp, xs, *, tm, tn):
    t, aw = att.shape
    lw = lru.shape[1]
    d = w_att.shape[1]
    npt = xp.shape[0] // tm
    kern = functools.partial(_outproj_kernel, n_prompt_tiles=npt)
    return pl.pallas_call(
        kern,
        grid=(t // tm, d // tn),
        in_specs=[
            pl.BlockSpec((tm, aw), lambda i, j: (i, 0)),
            pl.BlockSpec((tm, lw), lambda i, j: (i, 0)),
            pl.BlockSpec((aw, tn), lambda i, j: (0, j)),
            pl.BlockSpec((lw, tn), lambda i, j: (0, j)),
            pl.BlockSpec((tm, tn), lambda i, j: (jnp.minimum(i, npt - 1), j)),
            pl.BlockSpec((tm, tn), lambda i, j: (jnp.maximum(i - npt, 0), j)),
        ],
        out_specs=pl.BlockSpec((tm, tn), lambda i, j: (i, j)),
        out_shape=jax.ShapeDtypeStruct((t, d), F32),
        compiler_params=_params(2),
        name="out_proj",
    )(att, lru, w_att, w_lru, xp, xs)


def _router_kernel(x_ref, g_ref, wr_ref, br_ref, h_ref, gate_ref):
    x = x_ref[...]
    h = (x * _rms_scale(x) * g_ref[...]).astype(BF16)
    h_ref[...] = h
    logits = jnp.dot(h, wr_ref[...], preferred_element_type=F32) + br_ref[...]
    lane = lax.broadcasted_iota(jnp.int32, logits.shape, 1).astype(F32)
    ninf = -jnp.inf

    def first_argmax(v, vmax):
        return jnp.min(jnp.where(v == vmax, lane, float(LANES)), axis=-1, keepdims=True)

    gl = jnp.where(lane < N_GROUPS, logits, ninf)
    gm = jnp.max(gl, axis=-1, keepdims=True)
    g_idx = first_argmax(gl, gm)
    g_w = 1.0 / jnp.sum(jnp.exp(gl - gm), axis=-1, keepdims=True)

    lo = N_GROUPS + EXPERTS_PER_GROUP * g_idx
    el = jnp.where(jnp.logical_and(lane >= lo, lane < lo + EXPERTS_PER_GROUP), logits, ninf)
    v1 = jnp.max(el, axis=-1, keepdims=True)
    i1 = first_argmax(el, v1)
    el2 = jnp.where(lane == i1, ninf, el)
    v2 = jnp.max(el2, axis=-1, keepdims=True)
    i2 = first_argmax(el2, v2)
    e2 = jnp.exp(v2 - v1)
    w1 = (1.0 / (1.0 + e2)) * g_w
    w2 = (e2 / (1.0 + e2)) * g_w
    gate_ref[...] = (jnp.where(lane == i1 - N_GROUPS, w1, 0.0)
                     + jnp.where(lane == i2 - N_GROUPS, w2, 0.0))


def _router(x2, g, wr, br, *, tm):
    t, d = x2.shape
    return pl.pallas_call(
        _router_kernel,
        grid=(t // tm,),
        in_specs=[
            pl.BlockSpec((tm, d), lambda i: (i, 0)),
            pl.BlockSpec((1, d), lambda i: (0, 0)),
            pl.BlockSpec((d, LANES), lambda i: (0, 0)),
            pl.BlockSpec((1, LANES), lambda i: (0, 0)),
        ],
        out_specs=[
            pl.BlockSpec((tm, d), lambda i: (i, 0)),
            pl.BlockSpec((tm, LANES), lambda i: (i, 0)),
        ],
        out_shape=[
            jax.ShapeDtypeStruct((t, d), BF16),
            jax.ShapeDtypeStruct((t, LANES), F32),
        ],
        compiler_params=_params(1),
        name="router",
    )(x2, g, wr, br)


def _moe_kernel(h_ref, gate_ref, w1_ref, w3_ref, w2_ref, o_ref):
    e = pl.program_id(1)
    h = h_ref[...]
    a = jnp.dot(h, w1_ref[0].astype(BF16), preferred_element_type=F32)
    b = jnp.dot(h, w3_ref[0].astype(BF16), preferred_element_type=F32)
    lane = lax.broadcasted_iota(jnp.int32, gate_ref.shape, 1)
    gate = jnp.sum(jnp.where(lane == e, gate_ref[...], 0.0), axis=-1, keepdims=True)
    hid = (jax.nn.silu(a) * b * gate).astype(BF16)
    y = jnp.dot(hid, w2_ref[0].astype(BF16), preferred_element_type=F32)

    @pl.when(e == 0)
    def _():
        o_ref[...] = y

    @pl.when(e > 0)
    def _():
        o_ref[...] += y


def _moe(h2, gates, w1, w3, w2, *, tm):
    t, d = h2.shape
    n_e, _, f = w1.shape
    return pl.pallas_call(
        _moe_kernel,
        grid=(t // tm, n_e),
        in_specs=[
            pl.BlockSpec((tm, d), lambda i, e: (i, 0), pipeline_mode=pl.Buffered(1)),
            pl.BlockSpec((tm, LANES), lambda i, e: (i, 0)),
            pl.BlockSpec((1, d, f), lambda i, e: (e, 0, 0)),
            pl.BlockSpec((1, d, f), lambda i, e: (e, 0, 0)),
            pl.BlockSpec((1, f, d), lambda i, e: (e, 0, 0)),
        ],
        out_specs=pl.BlockSpec((tm, d), lambda i, e: (i, 0)),
        out_shape=jax.ShapeDtypeStruct((t, d), F32),
        compiler_params=_params(2),
        name="moe",
    )(h2, gates, w1, w3, w2)


def _final_kernel(x_ref, m_ref, g_ref, o_ref):
    x = x_ref[...] + m_ref[...]
    o_ref[...] = x * _rms_scale(x) * g_ref[...]


def _final(x2, moe, g, *, row0, n_rows, tm):
    d = x2.shape[1]
    off = row0 // tm
    return pl.pallas_call(
        _final_kernel,
        grid=(n_rows // tm,),
        in_specs=[
            pl.BlockSpec((tm, d), lambda i: (i + off, 0)),
            pl.BlockSpec((tm, d), lambda i: (i + off, 0)),
            pl.BlockSpec((1, d), lambda i: (0, 0)),
        ],
        out_specs=pl.BlockSpec((tm, d), lambda i: (i, 0)),
        out_shape=jax.ShapeDtypeStruct((n_rows, d), F32),
        compiler_params=_params(1),
        name="final_norm",
    )(x2, moe, g)


def _rope_tables(pos):
    half = ROT_DIM // 2
    inv = ROPE_THETA ** (-jnp.arange(half, dtype=F32) / half)
    ang = pos.astype(F32)[:, None] * inv[None, :]
    cos, sin = jnp.cos(ang), jnp.sin(ang)
    ones = jnp.ones((pos.shape[0], HEAD_DIM - ROT_DIM), F32)
    zeros = jnp.zeros((pos.shape[0], HEAD_DIM - half), F32)
    c = jnp.concatenate([cos, cos, ones], axis=1)
    s_hi = jnp.concatenate([-sin, zeros], axis=1)
    s_lo = jnp.concatenate([jnp.zeros_like(sin), sin, ones * 0.0], axis=1)
    rep = LANES // HEAD_DIM
    return jnp.tile(c, (1, rep)), jnp.tile(s_hi, (1, rep)), jnp.tile(s_lo, (1, rep))


def kernel(x_prompt, x_sample, cache_k, cache_v, state_conv, state_h, norm_mix, w_in, conv_w,
           conv_b, w_gate_a, b_gate_a, w_gate_x, b_gate_x, lru_lambda, sinks, norm_attn_out,
           norm_lru_out, w_out, norm_ffn, w_group, b_group, w_expert_router, b_expert_router,
           w1, w3, w2, norm_final):
    batch, seq, d_model = x_prompt.shape
    dec_batch, dec_seq, _ = x_sample.shape
    depth = w_in.shape[0]
    assert depth == 1 and dec_seq == CHUNK and seq % CHUNK == 0
    n_heads = sinks.shape[1]
    att_width = n_heads * HEAD_DIM
    n_kv = cache_k.shape[3]
    kv_cols = n_kv * HEAD_DIM
    lru_width = lru_lambda.shape[1]
    past_len = 1024
    cw = cache_k.shape[2]
    assert cw == WIN_CHUNKS * CHUNK
    n_p, n_s = batch * seq, dec_batch * dec_seq
    chunks_per_seq = seq // CHUNK
    n_prompt_chunks = n_p // CHUNK

    xp = x_prompt.reshape(n_p, d_model)
    xs = x_sample.reshape(n_s, d_model)

    w = w_in[0]
    q_end, k_end, v_end = att_width, att_width + kv_cols, att_width + 2 * kv_cols
    w_r = jnp.concatenate([w[:, :q_end], w[:, v_end:], w[:, q_end:v_end]], axis=1).astype(BF16)
    tn = 2 * kv_cols
    assert tn == 512 and att_width % tn == 0 and lru_width % tn == 0
    n_q_tiles = att_width // tn
    kv_tile = (att_width + 2 * lru_width) // tn

    pos = jnp.concatenate([jnp.tile(jnp.arange(seq), batch),
                           jnp.tile(past_len + jnp.arange(dec_seq), dec_batch)])
    c_tab, shi_tab, slo_tab = _rope_tables(pos)

    z = _in_proj(xp, xs, norm_mix, w_r, c_tab, shi_tab, slo_tab,
                 tm=512, tn=tn, n_q_tiles=n_q_tiles, kv_tile=kv_tile)

    k_col = att_width + 2 * lru_width
    v_col = k_col + kv_cols
    att = _attention(
        z, cache_k[0].reshape(dec_batch, cw, kv_cols), cache_v[0].reshape(dec_batch, cw, kv_cols),
        sinks[0], norm_attn_out,
        n_prompt_chunks=n_prompt_chunks, chunks_per_seq=chunks_per_seq, att_width=att_width,
        n_kv=n_kv, k_col_blk=k_col // kv_cols, v_col_blk=v_col // kv_cols)

    nblk = lru_width // LRU_BLOCK
    lru, h_tiles = _rglru(
        z, state_conv[0], state_h[0].reshape(dec_batch, 1, lru_width), conv_w[0], conv_b,
        w_gate_a[0].astype(BF16), b_gate_a[0].reshape(1, lru_width),
        w_gate_x[0].astype(BF16), b_gate_x[0].reshape(1, lru_width),
        lru_lambda, norm_lru_out,
        n_prompt_chunks=n_prompt_chunks, chunks_per_seq=chunks_per_seq, width=lru_width,
        xb_col_blk=att_width // lru_width, yb_col_blk=att_width // lru_width + 1)
    del nblk

    wo = w_out[0].astype(BF16)
    x2 = _out_proj(att, lru, wo[:att_width], wo[att_width:], xp, xs, tm=1024, tn=512)

    n_routes = N_GROUPS + N_GROUPS * EXPERTS_PER_GROUP
    wr = jnp.concatenate([w_group[0], w_expert_router[0],
                          jnp.zeros((d_model, LANES - n_routes), F32)], axis=1).astype(BF16)
    br = jnp.concatenate([b_group[0], b_expert_router[0],
                          jnp.zeros((LANES - n_routes,), F32)]).reshape(1, LANES)
    h2, gates = _router(x2, norm_ffn, wr, br, tm=256)

    moe = _moe(h2, gates, w1[0], w3[0], w2[0], tm=512)

    g_fin = norm_final.reshape(1, d_model)
    y_prompt = _final(x2, moe, g_fin, row0=0, n_rows=n_p, tm=256).reshape(batch, seq, d_model)
    y_sample = _final(x2, moe, g_fin, row0=n_p, n_rows=n_s, tm=256).reshape(
        dec_batch, dec_seq, d_model)

    zk = z[:, k_col:k_col + kv_cols]
    zv = z[:, v_col:v_col + kv_cols]
    zx = z[:, att_width:att_width + lru_width]
    win = min(WIN_CHUNKS * CHUNK, seq)

    def prompt_tail(a, rows):
        return a[:n_p].reshape(batch, seq, -1)[:, seq - rows:]

    k_prompt = prompt_tail(zk, win).reshape(1, batch, win, n_kv, HEAD_DIM)
    v_prompt = prompt_tail(zv, win).reshape(1, batch, win, n_kv, HEAD_DIM)
    conv_prompt = prompt_tail(zx, CONV_WIDTH - 1)[None]
    h_prompt = h_tiles[:n_prompt_chunks, 0].reshape(batch, chunks_per_seq, lru_width)[:, -1][None]

    ks = zk[n_p:].reshape(dec_batch, dec_seq, n_kv, HEAD_DIM)
    vs = zv[n_p:].reshape(dec_batch, dec_seq, n_kv, HEAD_DIM)
    k_sample = jnp.concatenate([cache_k[0], ks], axis=1)[:, -cw:][None]
    v_sample = jnp.concatenate([cache_v[0], vs], axis=1)[:, -cw:][None]
    xs_rows = zx[n_p:].reshape(dec_batch, dec_seq, lru_width)
    conv_sample = jnp.concatenate([state_conv[0], xs_rows], axis=1)[:, -(CONV_WIDTH - 1):][None]
    h_sample = h_tiles[n_prompt_chunks:, 0][None]

    return (y_prompt, y_sample, k_prompt, v_prompt, conv_prompt, h_prompt,
            k_sample, v_sample, conv_sample, h_sample)
```

```python
import functools

import jax
import jax.numpy as jnp
from jax import lax
from jax.experimental import pallas as pl
from jax.experimental.pallas import tpu as pltpu

F32 = jnp.float32
BF16 = jnp.bfloat16

CHUNK = 64
HEAD_DIM = 64
KV_GROUP = 8
WIN_CHUNKS = 2
ROT_DIM = 16
ROPE_THETA = 500000.0
LRU_BLOCK = 128
CONV_WIDTH = 4
LRU_C = 8.0
N_GROUPS = 8
EXPERTS_PER_GROUP = 4
EPS = 1e-6
NEG = -1e30
LANES = 128
MOE_TILE = 256
MOE_DOWN_HALVES = 2
VMEM_LIMIT = 56 * 1024 * 1024


def _params(n_axes, vmem=VMEM_LIMIT):
    return pltpu.CompilerParams(
        dimension_semantics=("arbitrary",) * n_axes, vmem_limit_bytes=vmem)


def _rms_scale(x):
    return lax.rsqrt(jnp.mean(x * x, axis=-1, keepdims=True) + EPS)


def _rope_block(zb, c, s_hi, s_lo):
    return (zb * c + pltpu.roll(zb, LANES - ROT_DIM // 2, 1) * s_hi
            + pltpu.roll(zb, ROT_DIM // 2, 1) * s_lo)


def _inproj_kernel(xp_ref, xs_ref, g_ref, w_ref, c_ref, shi_ref, slo_ref, o_ref, h_ref,
                   *, n_prompt_tiles, n_q_tiles, kv_tile):
    i = pl.program_id(0)
    j = pl.program_id(1)

    def norm(x_ref):
        x = x_ref[...]
        h_ref[...] = (x * _rms_scale(x) * g_ref[...]).astype(BF16)

    @pl.when(jnp.logical_and(j == 0, i < n_prompt_tiles))
    def _():
        norm(xp_ref)

    @pl.when(jnp.logical_and(j == 0, i >= n_prompt_tiles))
    def _():
        norm(xs_ref)

    z = jnp.dot(h_ref[...], w_ref[...], preferred_element_type=F32)
    n_blk = z.shape[1] // LANES

    def rope_cols(n_rope):
        c, shi, slo = c_ref[...], shi_ref[...], slo_ref[...]
        for b in range(n_blk):
            zb = z[:, b * LANES:(b + 1) * LANES]
            if b < n_rope:
                zb = _rope_block(zb, c, shi, slo)
            o_ref[:, b * LANES:(b + 1) * LANES] = zb

    @pl.when(j < n_q_tiles)
    def _():
        rope_cols(n_blk)

    @pl.when(j == kv_tile)
    def _():
        rope_cols(n_blk // 2)

    @pl.when(jnp.logical_and(j >= n_q_tiles, j != kv_tile))
    def _():
        o_ref[...] = z


def _in_proj(xp, xs, g, w, c, shi, slo, *, tm, tn, n_q_tiles, kv_tile):
    n_p, d = xp.shape
    n_s = xs.shape[0]
    t = n_p + n_s
    npt = n_p // tm
    n_cols = w.shape[1]
    kern = functools.partial(_inproj_kernel, n_prompt_tiles=npt, n_q_tiles=n_q_tiles,
                             kv_tile=kv_tile)
    return pl.pallas_call(
        kern,
        grid=(t // tm, n_cols // tn),
        in_specs=[
            pl.BlockSpec((tm, d), lambda i, j: (jnp.minimum(i, npt - 1), 0)),
            pl.BlockSpec((tm, d), lambda i, j: (jnp.maximum(i - npt, 0), 0),
                         pipeline_mode=pl.Buffered(1)),
            pl.BlockSpec((1, d), lambda i, j: (0, 0)),
            pl.BlockSpec((d, tn), lambda i, j: (0, j)),
            pl.BlockSpec((tm, LANES), lambda i, j: (i, 0)),
            pl.BlockSpec((tm, LANES), lambda i, j: (i, 0)),
            pl.BlockSpec((tm, LANES), lambda i, j: (i, 0)),
        ],
        out_specs=pl.BlockSpec((tm, tn), lambda i, j: (i, j)),
        out_shape=jax.ShapeDtypeStruct((t, n_cols), F32),
        scratch_shapes=[pltpu.VMEM((tm, d), BF16)],
        compiler_params=_params(2),
        name="in_proj",
    )(xp, xs, g, w, c, shi, slo)


def _attn_kernel(sinks_ref, q_ref, k0_ref, k1_ref, k2_ref, v0_ref, v1_ref, v2_ref,
                 kc_ref, vc_ref, g_ref, o_ref, *, n_prompt_chunks, chunks_per_seq, n_kv):
    n = pl.program_id(0)
    is_s = n >= n_prompt_chunks
    c = n % chunks_per_seq
    lo = jnp.where(is_s, 0, jnp.where(c >= 2, 0, jnp.where(c == 1, CHUNK, 2 * CHUNK)))

    kc = kc_ref[0]
    vc = vc_ref[0]
    k_band = jnp.concatenate([
        jnp.where(is_s, kc[:CHUNK], k0_ref[...]),
        jnp.where(is_s, kc[CHUNK:], k1_ref[...]),
        k2_ref[...]], axis=0).astype(BF16)
    v_band = jnp.concatenate([
        jnp.where(is_s, vc[:CHUNK], v0_ref[...]),
        jnp.where(is_s, vc[CHUNK:], v1_ref[...]),
        v2_ref[...]], axis=0).astype(BF16)

    q = (q_ref[...] * (HEAD_DIM ** -0.5)).astype(BF16)
    n_keys = (WIN_CHUNKS + 1) * CHUNK
    col = lax.broadcasted_iota(jnp.int32, (KV_GROUP * CHUNK, n_keys), 1)
    valid = col >= lo

    outs = []
    for gi in range(n_kv):
        kg = k_band[:, gi * HEAD_DIM:(gi + 1) * HEAD_DIM]
        vg = v_band[:, gi * HEAD_DIM:(gi + 1) * HEAD_DIM]
        heads = [gi * KV_GROUP + h for h in range(KV_GROUP)]
        qg = jnp.concatenate([q[:, h * HEAD_DIM:(h + 1) * HEAD_DIM] for h in heads], axis=0)
        s = lax.dot_general(qg, kg, (((1,), (1,)), ((), ())), preferred_element_type=F32)
        s = jnp.where(valid, s, NEG)
        sink = jnp.concatenate(
            [jnp.full((CHUNK, 1), sinks_ref[h], F32) for h in heads], axis=0)
        m = jnp.maximum(jnp.max(s, axis=-1, keepdims=True), sink)
        p = jnp.exp(s - m)
        denom = jnp.sum(p, axis=-1, keepdims=True) + jnp.exp(sink - m)
        og = jnp.dot(p.astype(BF16), vg, preferred_element_type=F32) / denom
        outs.extend(og[h * CHUNK:(h + 1) * CHUNK] for h in range(KV_GROUP))
    att = jnp.concatenate(outs, axis=1)
    o_ref[...] = (att * _rms_scale(att) * g_ref[...]).astype(o_ref.dtype)


def _attention(z, cache_k, cache_v, sinks, g, *, n_prompt_chunks, chunks_per_seq,
               att_width, n_kv, k_col_blk, v_col_blk):
    t = z.shape[0]
    n_items = t // CHUNK
    kvw = n_kv * HEAD_DIM
    npc = n_prompt_chunks

    def hist(d):
        return lambda n, s: (n - jnp.minimum(d, n % chunks_per_seq), k_col_blk)

    def histv(d):
        return lambda n, s: (n - jnp.minimum(d, n % chunks_per_seq), v_col_blk)

    cache_map = lambda n, s: (jnp.maximum(n - npc, 0), 0, 0)
    kern = functools.partial(_attn_kernel, n_prompt_chunks=npc,
                             chunks_per_seq=chunks_per_seq, n_kv=n_kv)
    grid_spec = pltpu.PrefetchScalarGridSpec(
        num_scalar_prefetch=1,
        grid=(n_items,),
        in_specs=[
            pl.BlockSpec((CHUNK, att_width), lambda n, s: (n, 0)),
            pl.BlockSpec((CHUNK, kvw), hist(2)),
            pl.BlockSpec((CHUNK, kvw), hist(1)),
            pl.BlockSpec((CHUNK, kvw), hist(0)),
            pl.BlockSpec((CHUNK, kvw), histv(2)),
            pl.BlockSpec((CHUNK, kvw), histv(1)),
            pl.BlockSpec((CHUNK, kvw), histv(0)),
            pl.BlockSpec((1, WIN_CHUNKS * CHUNK, kvw), cache_map),
            pl.BlockSpec((1, WIN_CHUNKS * CHUNK, kvw), cache_map),
            pl.BlockSpec((1, att_width), lambda n, s: (0, 0)),
        ],
        out_specs=pl.BlockSpec((CHUNK, att_width), lambda n, s: (n, 0)),
    )
    return pl.pallas_call(
        kern,
        grid_spec=grid_spec,
        out_shape=jax.ShapeDtypeStruct((t, att_width), BF16),
        compiler_params=_params(1),
        name="attention",
    )(sinks, z, z, z, z, z, z, z, cache_k, cache_v, g)


def _shift_rows(x, d, fill, row):
    return jnp.where(row >= d, pltpu.roll(x, d, 0), fill)


def _lru_kernel(xb_ref, yb_ref, sconv_ref, sh_ref, cw_ref, cb_ref, wa_ref, ba_ref,
                wx_ref, bx_ref, lam_ref, g_ref, o_ref, hl_ref, hist_ref, hcar_ref,
                *, n_prompt_chunks, chunks_per_seq):
    n = pl.program_id(0)
    is_s = n >= n_prompt_chunks
    hist_rows = hist_ref.shape[0]

    @pl.when(jnp.logical_and(jnp.logical_not(is_s), n % chunks_per_seq == 0))
    def _():
        hist_ref[...] = jnp.zeros_like(hist_ref)
        hcar_ref[...] = jnp.zeros_like(hcar_ref)

    @pl.when(is_s)
    def _():
        hist_ref[...] = jnp.zeros_like(hist_ref)
        hist_ref[hist_rows - (CONV_WIDTH - 1):, :] = sconv_ref[0]
        hcar_ref[...] = sh_ref[0]

    x = xb_ref[...]
    rows, width = x.shape
    ext = jnp.concatenate([hist_ref[...], x], axis=0)
    xc = cb_ref[...]
    for j in range(CONV_WIDTH):
        off = hist_rows - (CONV_WIDTH - 1) + j
        xc = xc + ext[off:off + rows] * cw_ref[j:j + 1, :]
    hist_ref[...] = x[rows - hist_rows:]

    xcb = xc.astype(BF16)
    ra, rx = [], []
    for nb in range(width // LRU_BLOCK):
        blk = xcb[:, nb * LRU_BLOCK:(nb + 1) * LRU_BLOCK]
        ra.append(jnp.dot(blk, wa_ref[nb], preferred_element_type=F32))
        rx.append(jnp.dot(blk, wx_ref[nb], preferred_element_type=F32))
    r = jax.nn.sigmoid(jnp.concatenate(ra, axis=1) + ba_ref[...])
    ig = jax.nn.sigmoid(jnp.concatenate(rx, axis=1) + bx_ref[...])
    log_a = (-LRU_C * r) * jax.nn.softplus(-lam_ref[...])
    a = jnp.exp(log_a)
    b = jnp.sqrt(-jnp.tanh(log_a) * (1.0 + a * a)) * ig * xc

    row = lax.broadcasted_iota(jnp.int32, (rows, width), 0)
    d = 1
    while d < rows:
        a_sh = _shift_rows(a, d, 1.0, row)
        b_sh = _shift_rows(b, d, 0.0, row)
        b = a * b_sh + b
        a = a * a_sh
        d *= 2
    h = b + a * hcar_ref[...]
    h_last = h[rows - 1:rows, :]
    hcar_ref[...] = h_last
    hl_ref[0] = h_last

    y = jax.nn.gelu(yb_ref[...]) * h
    o_ref[...] = (y * _rms_scale(y) * g_ref[...]).astype(o_ref.dtype)


def _rglru(z, state_conv, state_h, conv_w, conv_b, wa, ba, wx, bx, lam, g,
           *, n_prompt_chunks, chunks_per_seq, width, xb_col_blk, yb_col_blk):
    t = z.shape[0]
    n_items = t // CHUNK
    npc = n_prompt_chunks
    nblk = width // LRU_BLOCK
    state_map = lambda n: (jnp.maximum(n - npc, 0), 0, 0)
    full2 = lambda n: (0, 0)
    full3 = lambda n: (0, 0, 0)
    kern = functools.partial(_lru_kernel, n_prompt_chunks=npc, chunks_per_seq=chunks_per_seq)
    return pl.pallas_call(
        kern,
        grid=(n_items,),
        in_specs=[
            pl.BlockSpec((CHUNK, width), lambda n: (n, xb_col_blk)),
            pl.BlockSpec((CHUNK, width), lambda n: (n, yb_col_blk)),
            pl.BlockSpec((1, CONV_WIDTH - 1, width), state_map),
            pl.BlockSpec((1, 1, width), state_map),
            pl.BlockSpec((CONV_WIDTH, width), full2),
            pl.BlockSpec((1, width), full2),
            pl.BlockSpec((nblk, LRU_BLOCK, LRU_BLOCK), full3),
            pl.BlockSpec((1, width), full2),
            pl.BlockSpec((nblk, LRU_BLOCK, LRU_BLOCK), full3),
            pl.BlockSpec((1, width), full2),
            pl.BlockSpec((1, width), full2),
            pl.BlockSpec((1, width), full2),
        ],
        out_specs=[
            pl.BlockSpec((CHUNK, width), lambda n: (n, 0)),
            pl.BlockSpec((1, 1, width), lambda n: (n, 0, 0)),
        ],
        out_shape=[
            jax.ShapeDtypeStruct((t, width), BF16),
            jax.ShapeDtypeStruct((n_items, 1, width), F32),
        ],
        scratch_shapes=[pltpu.VMEM((8, width), F32), pltpu.VMEM((1, width), F32)],
        compiler_params=_params(1),
        name="rglru",
    )(z, z, state_conv, state_h, conv_w, conv_b, wa, ba, wx, bx, lam, g)


def _outproj_kernel(att_ref, lru_ref, wa_ref, wl_ref, xp_ref, xs_ref, o_ref, *, n_prompt_tiles):
    i = pl.program_id(0)
    acc = jnp.dot(att_ref[...], wa_ref[...], preferred_element_type=F32)
    acc = acc + jnp.dot(lru_ref[...], wl_ref[...], preferred_element_type=F32)

    @pl.when(i < n_prompt_tiles)
    def _():
        o_ref[...] = xp_ref[...] + acc

    @pl.when(i >= n_prompt_tiles)
    def _():
        o_ref[...] = xs_ref[...] + acc


def _out_proj(att, lru, w, xp, xs, *, tm, tn):
    t, aw = att.shape
    lw = lru.shape[1]
    d = w.shape[1]
    assert aw == lw and w.shape[0] == aw + lw
    npt = xp.shape[0] // tm
    kern = functools.partial(_outproj_kernel, n_prompt_tiles=npt)
    return pl.pallas_call(
        kern,
        grid=(t // tm, d // tn),
        in_specs=[
            pl.BlockSpec((tm, aw), lambda i, j: (i, 0)),
            pl.BlockSpec((tm, lw), lambda i, j: (i, 0)),
            pl.BlockSpec((aw, tn), lambda i, j: (0, j)),
            pl.BlockSpec((lw, tn), lambda i, j: (1, j)),
            pl.BlockSpec((tm, tn), lambda i, j: (jnp.minimum(i, npt - 1), j)),
            pl.BlockSpec((tm, tn), lambda i, j: (jnp.maximum(i - npt, 0), j)),
        ],
        out_specs=pl.BlockSpec((tm, tn), lambda i, j: (i, j)),
        out_shape=jax.ShapeDtypeStruct((t, d), F32),
        compiler_params=_params(2),
        name="out_proj",
    )(att, lru, w, w, xp, xs)


ROUTE_GROUP_LANE = EXPERTS_PER_GROUP
ROUTE_RANK_LANE = EXPERTS_PER_GROUP + 1


def _router_kernel(x_ref, g_ref, wr_ref, br_ref, route_ref, cnt_ref):
    @pl.when(pl.program_id(0) == 0)
    def _():
        cnt_ref[...] = jnp.zeros_like(cnt_ref)

    x = x_ref[...]
    h = (x * _rms_scale(x) * g_ref[...]).astype(BF16)
    logits = jnp.dot(h, wr_ref[...], preferred_element_type=F32) + br_ref[...]
    lane = lax.broadcasted_iota(jnp.int32, logits.shape, 1).astype(F32)
    ninf = -jnp.inf

    def first_argmax(v, vmax):
        return jnp.min(jnp.where(v == vmax, lane, float(LANES)), axis=-1, keepdims=True)

    gl = jnp.where(lane < N_GROUPS, logits, ninf)
    gm = jnp.max(gl, axis=-1, keepdims=True)
    g_idx = first_argmax(gl, gm)
    g_w = 1.0 / jnp.sum(jnp.exp(gl - gm), axis=-1, keepdims=True)

    lo = N_GROUPS + EXPERTS_PER_GROUP * g_idx
    el = jnp.where(jnp.logical_and(lane >= lo, lane < lo + EXPERTS_PER_GROUP), logits, ninf)
    v1 = jnp.max(el, axis=-1, keepdims=True)
    i1 = first_argmax(el, v1)
    el2 = jnp.where(lane == i1, ninf, el)
    v2 = jnp.max(el2, axis=-1, keepdims=True)
    i2 = first_argmax(el2, v2)
    e2 = jnp.exp(v2 - v1)
    w1 = (1.0 / (1.0 + e2)) * g_w
    w2 = (e2 / (1.0 + e2)) * g_w
    gates = jnp.where(lane == i1 - lo, w1, 0.0) + jnp.where(lane == i2 - lo, w2, 0.0)

    tm = x.shape[0]
    onehot = jnp.where(lane == g_idx, 1.0, 0.0)
    r_i = lax.broadcasted_iota(jnp.int32, (tm, tm), 0)
    c_i = lax.broadcasted_iota(jnp.int32, (tm, tm), 1)
    tri = jnp.where(c_i < r_i, 1.0, 0.0).astype(BF16)
    before = jnp.dot(tri, onehot.astype(BF16), preferred_element_type=F32) + cnt_ref[...]
    rank = jnp.sum(onehot * before, axis=-1, keepdims=True)
    cnt_ref[...] += jnp.sum(onehot, axis=0, keepdims=True)
    route_ref[...] = (gates + jnp.where(lane == ROUTE_GROUP_LANE, g_idx, 0.0)
                      + jnp.where(lane == ROUTE_RANK_LANE, rank, 0.0))


def _router(x2, g, wr, br, *, tm):
    t, d = x2.shape
    return pl.pallas_call(
        _router_kernel,
        grid=(t // tm,),
        in_specs=[
            pl.BlockSpec((tm, d), lambda i: (i, 0)),
            pl.BlockSpec((1, d), lambda i: (0, 0)),
            pl.BlockSpec((d, LANES), lambda i: (0, 0)),
            pl.BlockSpec((1, LANES), lambda i: (0, 0)),
        ],
        out_specs=[
            pl.BlockSpec((tm, LANES), lambda i: (i, 0)),
            pl.BlockSpec((1, LANES), lambda i: (0, 0)),
        ],
        out_shape=[
            jax.ShapeDtypeStruct((t, LANES), F32),
            jax.ShapeDtypeStruct((1, LANES), F32),
        ],
        compiler_params=_params(1),
        name="router",
    )(x2, g, wr, br)


def _routing_tables(route, counts, *, tile, n_tiles_max, n_halves):
    t = route.shape[0]
    i32 = jnp.int32
    g = route[:, ROUTE_GROUP_LANE].astype(i32)
    rank = route[:, ROUTE_RANK_LANE].astype(i32)
    cnt = counts[0, :N_GROUPS].astype(i32)
    tiles_g = (cnt + tile - 1) // tile
    tend = jnp.cumsum(tiles_g)
    tstart = tend - tiles_g
    n_used = tend[-1]
    dest = tstart[g] * tile + rank
    src = jnp.zeros((n_tiles_max * tile,), i32).at[dest].set(jnp.arange(t, dtype=i32))

    def items(per_tile):
        n_items = per_tile * n_used
        w_all = jnp.arange(per_tile * n_tiles_max, dtype=i32)
        w = jnp.minimum(w_all, n_items - 1)
        gi = jnp.searchsorted(per_tile * tend, w, side="right").astype(i32)
        local = w - per_tile * tstart[gi]
        spare = w_all - n_items
        part = jnp.where(spare < 0, local // tiles_g[gi], spare % per_tile)
        p = jnp.where(spare < 0, tstart[gi] + local % tiles_g[gi], n_used + spare // per_tile)
        return p, part, gi, n_items.reshape(1)

    up_p, up_e, up_g, up_n = items(EXPERTS_PER_GROUP)
    dn_p, dn_h, dn_g, dn_n = items(n_halves)
    return dict(dest=dest, src=src, n_used=n_used.reshape(1),
                up=(up_p, up_e, up_g * EXPERTS_PER_GROUP + up_e, up_n),
                down=(dn_p, dn_h, dn_g, dn_n))


def _row_copy(src_hbm, dst_vmem, src_row, dst_row, sem):
    return pltpu.make_async_copy(src_hbm.at[pl.ds(src_row, 1)], dst_vmem.at[pl.ds(dst_row, 1)], sem)


def _start_row_gather(idx_ref, base, n_rows, pairs):
    def body(r, carry):
        i = idx_ref[base + r]
        for src_hbm, dst_vmem, sem in pairs:
            _row_copy(src_hbm, dst_vmem, i, r, sem).start()
        return carry
    lax.fori_loop(0, n_rows, body, 0, unroll=8)


def _wait_row_gather(n_rows, pairs):
    def body(r, carry):
        for src_hbm, dst_vmem, sem in pairs:
            _row_copy(src_hbm, dst_vmem, 0, r, sem).wait()
        return carry
    lax.fori_loop(0, n_rows, body, 0, unroll=8)


def _dispatch_kernel(src_ref, nused_ref, x2_hbm, route_hbm, g_ref, xs_ref, gs_ref,
                     xbuf, gbuf, sem):
    p = pl.program_id(0)
    n_used = nused_ref[0]
    tile = xbuf.shape[1]
    slot = p % 2

    def pairs(s):
        return [(x2_hbm, xbuf.at[s], sem.at[0, s]), (route_hbm, gbuf.at[s], sem.at[1, s])]

    @pl.when(p == 0)
    def _():
        _start_row_gather(src_ref, 0, tile, pairs(0))

    @pl.when(p + 1 < n_used)
    def _():
        _start_row_gather(src_ref, (p + 1) * tile, tile, pairs(1 - slot))

    @pl.when(p < n_used)
    def _():
        _wait_row_gather(tile, pairs(slot))
        x = xbuf[slot]
        xs_ref[...] = (x * _rms_scale(x) * g_ref[...]).astype(xs_ref.dtype)
        gs_ref[...] = gbuf[slot]

    @pl.when(p >= n_used)
    def _():
        xs_ref[...] = jnp.zeros_like(xs_ref)
        gs_ref[...] = jnp.zeros_like(gs_ref)


def _dispatch(x2, route, g, src, n_used, *, tile, n_tiles_max):
    t, d = x2.shape
    tile_map = lambda p, src_ref, n_ref: (p, 0)
    grid_spec = pltpu.PrefetchScalarGridSpec(
        num_scalar_prefetch=2,
        grid=(n_tiles_max,),
        in_specs=[
            pl.BlockSpec(memory_space=pl.ANY),
            pl.BlockSpec(memory_space=pl.ANY),
            pl.BlockSpec((1, d), lambda p, s, n: (0, 0)),
        ],
        out_specs=[
            pl.BlockSpec((tile, d), tile_map),
            pl.BlockSpec((tile, LANES), tile_map),
        ],
        scratch_shapes=[
            pltpu.VMEM((2, tile, d), F32),
            pltpu.VMEM((2, tile, LANES), F32),
            pltpu.SemaphoreType.DMA((2, 2)),
        ],
    )
    return pl.pallas_call(
        _dispatch_kernel,
        grid_spec=grid_spec,
        out_shape=[
            jax.ShapeDtypeStruct((n_tiles_max * tile, d), BF16),
            jax.ShapeDtypeStruct((n_tiles_max * tile, LANES), F32),
        ],
        compiler_params=_params(1),
        name="moe_dispatch",
    )(src, n_used, x2, route, g)


def _moe_up_kernel(ip_ref, ie_ref, ige_ref, n_ref, xs_ref, gs_ref, w1_ref, w3_ref, hid_ref):
    w = pl.program_id(0)

    @pl.when(w < n_ref[0])
    def _():
        x = xs_ref[...]
        a = jnp.dot(x, w1_ref[0].astype(BF16), preferred_element_type=F32)
        b = jnp.dot(x, w3_ref[0].astype(BF16), preferred_element_type=F32)
        lane = lax.broadcasted_iota(jnp.int32, gs_ref.shape, 1)
        gate = jnp.sum(jnp.where(lane == ie_ref[w], gs_ref[...], 0.0), axis=-1, keepdims=True)
        hid_ref[...] = (jax.nn.silu(a) * b * gate).astype(hid_ref.dtype)

    @pl.when(w >= n_ref[0])
    def _():
        hid_ref[...] = jnp.zeros_like(hid_ref)


def _moe_up(xs, gs, w1, w3, items, *, tile):
    rows, d = xs.shape
    f = w1.shape[2]
    ip, ie, ige, n_items = items
    grid_spec = pltpu.PrefetchScalarGridSpec(
        num_scalar_prefetch=4,
        grid=(ip.shape[0],),
        in_specs=[
            pl.BlockSpec((tile, d), lambda w, ip, ie, ige, n: (ip[w], 0)),
            pl.BlockSpec((tile, LANES), lambda w, ip, ie, ige, n: (ip[w], 0)),
            pl.BlockSpec((1, d, f), lambda w, ip, ie, ige, n: (ige[w], 0, 0)),
            pl.BlockSpec((1, d, f), lambda w, ip, ie, ige, n: (ige[w], 0, 0)),
        ],
        out_specs=pl.BlockSpec((tile, f), lambda w, ip, ie, ige, n: (ip[w], ie[w])),
    )
    return pl.pallas_call(
        _moe_up_kernel,
        grid_spec=grid_spec,
        out_shape=jax.ShapeDtypeStruct((rows, EXPERTS_PER_GROUP * f), BF16),
        compiler_params=_params(1),
        name="moe_up",
    )(ip, ie, ige, n_items, xs, gs, w1, w3)


def _moe_down_kernel(ip_ref, ih_ref, ig_ref, n_ref, hid_ref, w2_ref, y_ref):
    @pl.when(pl.program_id(0) < n_ref[0])
    def _():
        y_ref[...] = jnp.dot(hid_ref[...], w2_ref[0].astype(BF16), preferred_element_type=F32)

    @pl.when(pl.program_id(0) >= n_ref[0])
    def _():
        y_ref[...] = jnp.zeros_like(y_ref)


def _moe_down(hid, w2g, items, *, tile, n_halves):
    rows, k = hid.shape
    d = w2g.shape[2]
    tn = d // n_halves
    ip, ih, ig, n_items = items
    grid_spec = pltpu.PrefetchScalarGridSpec(
        num_scalar_prefetch=4,
        grid=(ip.shape[0],),
        in_specs=[
            pl.BlockSpec((tile, k), lambda w, ip, ih, ig, n: (ip[w], 0)),
            pl.BlockSpec((1, k, tn), lambda w, ip, ih, ig, n: (ig[w], 0, ih[w])),
        ],
        out_specs=pl.BlockSpec((tile, tn), lambda w, ip, ih, ig, n: (ip[w], ih[w])),
    )
    return pl.pallas_call(
        _moe_down_kernel,
        grid_spec=grid_spec,
        out_shape=jax.ShapeDtypeStruct((rows, d), F32),
        compiler_params=_params(1),
        name="moe_down",
    )(ip, ih, ig, n_items, hid, w2g)


def _final_kernel(dest_ref, x_ref, ys_hbm, g_ref, o_ref, ybuf, sem, *, tile0):
    i = pl.program_id(0)
    tm = ybuf.shape[1]
    slot = i % 2

    def pairs(s):
        return [(ys_hbm, ybuf.at[s], sem.at[s])]

    @pl.when(i == 0)
    def _():
        _start_row_gather(dest_ref, tile0 * tm, tm, pairs(0))

    @pl.when(i + 1 < pl.num_programs(0))
    def _():
        _start_row_gather(dest_ref, (tile0 + i + 1) * tm, tm, pairs(1 - slot))

    _wait_row_gather(tm, pairs(slot))
    x = x_ref[...] + ybuf[slot]
    o_ref[...] = x * _rms_scale(x) * g_ref[...]


def _final(x2, ys, dest, g, *, row0, n_rows, tm):
    d = x2.shape[1]
    off = row0 // tm
    grid_spec = pltpu.PrefetchScalarGridSpec(
        num_scalar_prefetch=1,
        grid=(n_rows // tm,),
        in_specs=[
            pl.BlockSpec((tm, d), lambda i, dest: (i + off, 0)),
            pl.BlockSpec(memory_space=pl.ANY),
            pl.BlockSpec((1, d), lambda i, dest: (0, 0)),
        ],
        out_specs=pl.BlockSpec((tm, d), lambda i, dest: (i, 0)),
        scratch_shapes=[pltpu.VMEM((2, tm, d), F32), pltpu.SemaphoreType.DMA((2,))],
    )
    return pl.pallas_call(
        functools.partial(_final_kernel, tile0=off),
        grid_spec=grid_spec,
        out_shape=jax.ShapeDtypeStruct((n_rows, d), F32),
        compiler_params=_params(1),
        name="final_norm",
    )(dest, x2, ys, g)


def _rope_tables(pos):
    half = ROT_DIM // 2
    inv = ROPE_THETA ** (-jnp.arange(half, dtype=F32) / half)
    ang = pos.astype(F32)[:, None] * inv[None, :]
    cos, sin = jnp.cos(ang), jnp.sin(ang)
    ones = jnp.ones((pos.shape[0], HEAD_DIM - ROT_DIM), F32)
    zeros = jnp.zeros((pos.shape[0], HEAD_DIM - half), F32)
    c = jnp.concatenate([cos, cos, ones], axis=1)
    s_hi = jnp.concatenate([-sin, zeros], axis=1)
    s_lo = jnp.concatenate([jnp.zeros_like(sin), sin, ones * 0.0], axis=1)
    rep = LANES // HEAD_DIM
    return jnp.tile(c, (1, rep)), jnp.tile(s_hi, (1, rep)), jnp.tile(s_lo, (1, rep))


def kernel(x_prompt, x_sample, cache_k, cache_v, state_conv, state_h, norm_mix, w_in, conv_w,
           conv_b, w_gate_a, b_gate_a, w_gate_x, b_gate_x, lru_lambda, sinks, norm_attn_out,
           norm_lru_out, w_out, norm_ffn, w_group, b_group, w_expert_router, b_expert_router,
           w1, w3, w2, norm_final):
    batch, seq, d_model = x_prompt.shape
    dec_batch, dec_seq, _ = x_sample.shape
    depth = w_in.shape[0]
    assert depth == 1 and dec_seq == CHUNK and seq % CHUNK == 0
    n_heads = sinks.shape[1]
    att_width = n_heads * HEAD_DIM
    n_kv = cache_k.shape[3]
    kv_cols = n_kv * HEAD_DIM
    lru_width = lru_lambda.shape[1]
    past_len = 1024
    cw = cache_k.shape[2]
    assert cw == WIN_CHUNKS * CHUNK
    n_p, n_s = batch * seq, dec_batch * dec_seq
    chunks_per_seq = seq // CHUNK
    n_prompt_chunks = n_p // CHUNK

    xp = x_prompt.reshape(n_p, d_model)
    xs = x_sample.reshape(n_s, d_model)

    w = w_in[0]
    q_end, k_end, v_end = att_width, att_width + kv_cols, att_width + 2 * kv_cols
    w_r = jnp.concatenate([w[:, :q_end], w[:, v_end:], w[:, q_end:v_end]], axis=1).astype(BF16)
    tn = 2 * kv_cols
    assert tn == 512 and att_width % tn == 0 and lru_width % tn == 0
    n_q_tiles = att_width // tn
    kv_tile = (att_width + 2 * lru_width) // tn

    pos = jnp.concatenate([jnp.tile(jnp.arange(seq), batch),
                           jnp.tile(past_len + jnp.arange(dec_seq), dec_batch)])
    c_tab, shi_tab, slo_tab = _rope_tables(pos)

    z = _in_proj(xp, xs, norm_mix, w_r, c_tab, shi_tab, slo_tab,
                 tm=512, tn=tn, n_q_tiles=n_q_tiles, kv_tile=kv_tile)

    k_col = att_width + 2 * lru_width
    v_col = k_col + kv_cols
    att = _attention(
        z, cache_k[0].reshape(dec_batch, cw, kv_cols), cache_v[0].reshape(dec_batch, cw, kv_cols),
        sinks[0], norm_attn_out,
        n_prompt_chunks=n_prompt_chunks, chunks_per_seq=chunks_per_seq, att_width=att_width,
        n_kv=n_kv, k_col_blk=k_col // kv_cols, v_col_blk=v_col // kv_cols)

    nblk = lru_width // LRU_BLOCK
    lru, h_tiles = _rglru(
        z, state_conv[0], state_h[0].reshape(dec_batch, 1, lru_width), conv_w[0], conv_b,
        w_gate_a[0].astype(BF16), b_gate_a[0].reshape(1, lru_width),
        w_gate_x[0].astype(BF16), b_gate_x[0].reshape(1, lru_width),
        lru_lambda, norm_lru_out,
        n_prompt_chunks=n_prompt_chunks, chunks_per_seq=chunks_per_seq, width=lru_width,
        xb_col_blk=att_width // lru_width, yb_col_blk=att_width // lru_width + 1)
    del nblk

    x2 = _out_proj(att, lru, w_out[0].astype(BF16), xp, xs, tm=1024, tn=512)

    n_routes = N_GROUPS + N_GROUPS * EXPERTS_PER_GROUP
    wr = jnp.concatenate([w_group[0], w_expert_router[0],
                          jnp.zeros((d_model, LANES - n_routes), F32)], axis=1).astype(BF16)
    br = jnp.concatenate([b_group[0], b_expert_router[0],
                          jnp.zeros((LANES - n_routes,), F32)]).reshape(1, LANES)
    route, counts = _router(x2, norm_ffn, wr, br, tm=MOE_TILE)

    n_tiles_max = (n_p + n_s) // MOE_TILE + N_GROUPS
    tabs = _routing_tables(route, counts, tile=MOE_TILE, n_tiles_max=n_tiles_max,
                           n_halves=MOE_DOWN_HALVES)
    xsort, gsort = _dispatch(x2, route, norm_ffn, tabs["src"], tabs["n_used"],
                             tile=MOE_TILE, n_tiles_max=n_tiles_max)
    hid = _moe_up(xsort, gsort, w1[0], w3[0], tabs["up"], tile=MOE_TILE)
    d_expert = w2.shape[2]
    w2g = w2[0].reshape(N_GROUPS, EXPERTS_PER_GROUP * d_expert, d_model)
    ysort = _moe_down(hid, w2g, tabs["down"], tile=MOE_TILE, n_halves=MOE_DOWN_HALVES)

    g_fin = norm_final.reshape(1, d_model)
    y_prompt = _final(x2, ysort, tabs["dest"], g_fin, row0=0, n_rows=n_p,
                      tm=MOE_TILE).reshape(batch, seq, d_model)
    y_sample = _final(x2, ysort, tabs["dest"], g_fin, row0=n_p, n_rows=n_s,
                      tm=MOE_TILE).reshape(dec_batch, dec_seq, d_model)

    win = min(WIN_CHUNKS * CHUNK, seq)
    n_cols = z.shape[1]
    zp = z[:n_p].reshape(batch, seq, n_cols)
    zs = z[n_p:].reshape(dec_batch, dec_seq, n_cols)

    def prompt_tail(col, width, rows):
        return lax.slice(zp, (0, seq - rows, col), (batch, seq, col + width))

    k_prompt = prompt_tail(k_col, kv_cols, win).reshape(1, batch, win, n_kv, HEAD_DIM)
    v_prompt = prompt_tail(v_col, kv_cols, win).reshape(1, batch, win, n_kv, HEAD_DIM)
    conv_prompt = prompt_tail(att_width, lru_width, CONV_WIDTH - 1)[None]
    h_prompt = h_tiles[:n_prompt_chunks, 0].reshape(batch, chunks_per_seq, lru_width)[:, -1][None]

    ks = zs[:, :, k_col:k_col + kv_cols].reshape(dec_batch, dec_seq, n_kv, HEAD_DIM)
    vs = zs[:, :, v_col:v_col + kv_cols].reshape(dec_batch, dec_seq, n_kv, HEAD_DIM)
    k_sample = jnp.concatenate([cache_k[0], ks], axis=1)[:, -cw:][None]
    v_sample = jnp.concatenate([cache_v[0], vs], axis=1)[:, -cw:][None]
    xs_rows = zs[:, dec_seq - (CONV_WIDTH - 1):, att_width:att_width + lru_width]
    conv_sample = jnp.concatenate([state_conv[0], xs_rows], axis=1)[:, -(CONV_WIDTH - 1):][None]
    h_sample = h_tiles[n_prompt_chunks:, 0][None]

    return (y_prompt, y_sample, k_prompt, v_prompt, conv_prompt, h_prompt,
            k_sample, v_sample, conv_sample, h_sample)
```

```python
import functools

import jax
import jax.numpy as jnp
from jax import lax
from jax.experimental import pallas as pl
from jax.experimental.pallas import tpu as pltpu

F32 = jnp.float32
BF16 = jnp.bfloat16

CHUNK = 64
HEAD_DIM = 64
KV_GROUP = 8
WIN_CHUNKS = 2
ROT_DIM = 16
ROPE_THETA = 500000.0
LRU_BLOCK = 128
CONV_WIDTH = 4
LRU_C = 8.0
N_GROUPS = 8
EXPERTS_PER_GROUP = 4
EPS = 1e-6
NEG = -1e30
LANES = 128
MOE_TILE = 256
MOE_DOWN_HALVES = 2
VMEM_LIMIT = 56 * 1024 * 1024


def _params(n_axes, vmem=VMEM_LIMIT):
    return pltpu.CompilerParams(
        dimension_semantics=("arbitrary",) * n_axes, vmem_limit_bytes=vmem)


def _rms_scale(x):
    return lax.rsqrt(jnp.mean(x * x, axis=-1, keepdims=True) + EPS)


def _rope_block(zb, c, s_hi, s_lo):
    return (zb * c + pltpu.roll(zb, LANES - ROT_DIM // 2, 1) * s_hi
            + pltpu.roll(zb, ROT_DIM // 2, 1) * s_lo)


def _inproj_kernel(xp_ref, xs_ref, g_ref, w_ref, c_ref, shi_ref, slo_ref, o_ref, h_ref,
                   *, n_prompt_tiles, n_q_tiles, kv_tile):
    i = pl.program_id(0)
    j = pl.program_id(1)

    def norm(x_ref):
        x = x_ref[...]
        h_ref[...] = (x * _rms_scale(x) * g_ref[...]).astype(BF16)

    @pl.when(jnp.logical_and(j == 0, i < n_prompt_tiles))
    def _():
        norm(xp_ref)

    @pl.when(jnp.logical_and(j == 0, i >= n_prompt_tiles))
    def _():
        norm(xs_ref)

    z = jnp.dot(h_ref[...], w_ref[...], preferred_element_type=F32)
    n_blk = z.shape[1] // LANES

    def rope_cols(n_rope):
        c, shi, slo = c_ref[...], shi_ref[...], slo_ref[...]
        for b in range(n_blk):
            zb = z[:, b * LANES:(b + 1) * LANES]
            if b < n_rope:
                zb = _rope_block(zb, c, shi, slo)
            o_ref[:, b * LANES:(b + 1) * LANES] = zb

    @pl.when(j < n_q_tiles)
    def _():
        rope_cols(n_blk)

    @pl.when(j == kv_tile)
    def _():
        rope_cols(n_blk // 2)

    @pl.when(jnp.logical_and(j >= n_q_tiles, j != kv_tile))
    def _():
        o_ref[...] = z


def _in_proj(xp, xs, g, w, c, shi, slo, *, tm, tn, n_q_tiles, kv_tile):
    n_p, d = xp.shape
    n_s = xs.shape[0]
    t = n_p + n_s
    npt = n_p // tm
    n_cols = w.shape[1]
    kern = functools.partial(_inproj_kernel, n_prompt_tiles=npt, n_q_tiles=n_q_tiles,
                             kv_tile=kv_tile)
    return pl.pallas_call(
        kern,
        grid=(t // tm, n_cols // tn),
        in_specs=[
            pl.BlockSpec((tm, d), lambda i, j: (jnp.minimum(i, npt - 1), 0)),
            pl.BlockSpec((tm, d), lambda i, j: (jnp.maximum(i - npt, 0), 0),
                         pipeline_mode=pl.Buffered(1)),
            pl.BlockSpec((1, d), lambda i, j: (0, 0)),
            pl.BlockSpec((d, tn), lambda i, j: (0, j)),
            pl.BlockSpec((tm, LANES), lambda i, j: (i, 0)),
            pl.BlockSpec((tm, LANES), lambda i, j: (i, 0)),
            pl.BlockSpec((tm, LANES), lambda i, j: (i, 0)),
        ],
        out_specs=pl.BlockSpec((tm, tn), lambda i, j: (i, j)),
        out_shape=jax.ShapeDtypeStruct((t, n_cols), F32),
        scratch_shapes=[pltpu.VMEM((tm, d), BF16)],
        compiler_params=_params(2),
        name="in_proj",
    )(xp, xs, g, w, c, shi, slo)


def _attn_kernel(q_ref, k0_ref, k1_ref, k2_ref, v0_ref, v1_ref, v2_ref,
                 kc_ref, vc_ref, sink_ref, g_ref, o_ref, *, n_prompt_chunks, chunks_per_seq, n_kv):
    n = pl.program_id(0)
    is_s = n >= n_prompt_chunks
    c = n % chunks_per_seq
    lo = jnp.where(is_s, 0, jnp.where(c >= 2, 0, jnp.where(c == 1, CHUNK, 2 * CHUNK)))

    kc = kc_ref[0]
    vc = vc_ref[0]
    k_band = jnp.concatenate([
        jnp.where(is_s, kc[:CHUNK], k0_ref[...]),
        jnp.where(is_s, kc[CHUNK:], k1_ref[...]),
        k2_ref[...]], axis=0).astype(BF16)
    v_band = jnp.concatenate([
        jnp.where(is_s, vc[:CHUNK], v0_ref[...]),
        jnp.where(is_s, vc[CHUNK:], v1_ref[...]),
        v2_ref[...]], axis=0).astype(BF16)

    q = (q_ref[...] * (HEAD_DIM ** -0.5)).astype(BF16)
    n_keys = (WIN_CHUNKS + 1) * CHUNK
    key = lax.broadcasted_iota(jnp.int32, (n_keys, KV_GROUP * CHUNK), 0)
    valid = key >= lo

    outs = []
    for gi in range(n_kv):
        kg = k_band[:, gi * HEAD_DIM:(gi + 1) * HEAD_DIM]
        vg = v_band[:, gi * HEAD_DIM:(gi + 1) * HEAD_DIM]
        heads = [gi * KV_GROUP + h for h in range(KV_GROUP)]
        qg = jnp.concatenate([q[:, h * HEAD_DIM:(h + 1) * HEAD_DIM] for h in heads], axis=0)
        st = lax.dot_general(kg, qg, (((1,), (1,)), ((), ())), preferred_element_type=F32)
        st = jnp.where(valid, st, NEG)
        sink = sink_ref[gi:gi + 1, :]
        m = jnp.maximum(jnp.max(st, axis=0, keepdims=True), sink)
        p = jnp.exp(st - m)
        denom = jnp.sum(p, axis=0, keepdims=True) + jnp.exp(sink - m)
        pn = (p * (1.0 / denom)).astype(BF16)
        og = lax.dot_general(pn, vg, (((0,), (0,)), ((), ())), preferred_element_type=F32)
        outs.extend(og[h * CHUNK:(h + 1) * CHUNK] for h in range(KV_GROUP))
    att = jnp.concatenate(outs, axis=1)
    o_ref[...] = (att * _rms_scale(att) * g_ref[...]).astype(o_ref.dtype)


def _attention(z, cache_k, cache_v, sinks, g, *, n_prompt_chunks, chunks_per_seq,
               att_width, n_kv, k_col_blk, v_col_blk):
    t = z.shape[0]
    n_items = t // CHUNK
    kvw = n_kv * HEAD_DIM
    npc = n_prompt_chunks

    def hist(d):
        return lambda n: (n - jnp.minimum(d, n % chunks_per_seq), k_col_blk)

    def histv(d):
        return lambda n: (n - jnp.minimum(d, n % chunks_per_seq), v_col_blk)

    cache_map = lambda n: (jnp.maximum(n - npc, 0), 0, 0)
    kern = functools.partial(_attn_kernel, n_prompt_chunks=npc,
                             chunks_per_seq=chunks_per_seq, n_kv=n_kv)
    sink_rows = jnp.repeat(sinks.reshape(n_kv, KV_GROUP), CHUNK, axis=1)
    return pl.pallas_call(
        kern,
        grid=(n_items,),
        in_specs=[
            pl.BlockSpec((CHUNK, att_width), lambda n: (n, 0)),
            pl.BlockSpec((CHUNK, kvw), hist(2)),
            pl.BlockSpec((CHUNK, kvw), hist(1)),
            pl.BlockSpec((CHUNK, kvw), hist(0)),
            pl.BlockSpec((CHUNK, kvw), histv(2)),
            pl.BlockSpec((CHUNK, kvw), histv(1)),
            pl.BlockSpec((CHUNK, kvw), histv(0)),
            pl.BlockSpec((1, WIN_CHUNKS * CHUNK, kvw), cache_map),
            pl.BlockSpec((1, WIN_CHUNKS * CHUNK, kvw), cache_map),
            pl.BlockSpec((n_kv, KV_GROUP * CHUNK), lambda n: (0, 0)),
            pl.BlockSpec((1, att_width), lambda n: (0, 0)),
        ],
        out_specs=pl.BlockSpec((CHUNK, att_width), lambda n: (n, 0)),
        out_shape=jax.ShapeDtypeStruct((t, att_width), BF16),
        compiler_params=_params(1),
        name="attention",
    )(z, z, z, z, z, z, z, cache_k, cache_v, sink_rows, g)


def _shift_rows(x, d, fill, row):
    return jnp.where(row >= d, pltpu.roll(x, d, 0), fill)


def _lru_kernel(xb_ref, yb_ref, sconv_ref, sh_ref, cw_ref, cb_ref, wa_ref, ba_ref,
                wx_ref, bx_ref, lam_ref, g_ref, o_ref, hl_ref, hist_ref, hcar_ref,
                *, n_prompt_chunks, chunks_per_seq):
    n = pl.program_id(0)
    is_s = n >= n_prompt_chunks
    hist_rows = hist_ref.shape[0]

    @pl.when(jnp.logical_and(jnp.logical_not(is_s), n % chunks_per_seq == 0))
    def _():
        hist_ref[...] = jnp.zeros_like(hist_ref)
        hcar_ref[...] = jnp.zeros_like(hcar_ref)

    @pl.when(is_s)
    def _():
        hist_ref[...] = jnp.zeros_like(hist_ref)
        hist_ref[hist_rows - (CONV_WIDTH - 1):, :] = sconv_ref[0]
        hcar_ref[...] = sh_ref[0]

    x = xb_ref[...]
    rows, width = x.shape
    ext = jnp.concatenate([hist_ref[...], x], axis=0)
    xc = cb_ref[...]
    for j in range(CONV_WIDTH):
        off = hist_rows - (CONV_WIDTH - 1) + j
        xc = xc + ext[off:off + rows] * cw_ref[j:j + 1, :]
    hist_ref[...] = x[rows - hist_rows:]

    xcb = xc.astype(BF16)
    ra, rx = [], []
    for nb in range(width // LRU_BLOCK):
        blk = xcb[:, nb * LRU_BLOCK:(nb + 1) * LRU_BLOCK]
        ra.append(jnp.dot(blk, wa_ref[nb], preferred_element_type=F32))
        rx.append(jnp.dot(blk, wx_ref[nb], preferred_element_type=F32))
    r = jax.nn.sigmoid(jnp.concatenate(ra, axis=1) + ba_ref[...])
    ig = jax.nn.sigmoid(jnp.concatenate(rx, axis=1) + bx_ref[...])
    log_a = (-LRU_C * r) * jax.nn.softplus(-lam_ref[...])
    a = jnp.exp(log_a)
    b = jnp.sqrt(-jnp.tanh(log_a) * (1.0 + a * a)) * ig * xc

    row = lax.broadcasted_iota(jnp.int32, (rows, width), 0)
    d = 1
    while d < rows:
        a_sh = _shift_rows(a, d, 1.0, row)
        b_sh = _shift_rows(b, d, 0.0, row)
        b = a * b_sh + b
        a = a * a_sh
        d *= 2
    h = b + a * hcar_ref[...]
    h_last = h[rows - 1:rows, :]
    hcar_ref[...] = h_last
    hl_ref[0] = h_last

    y = jax.nn.gelu(yb_ref[...]) * h
    o_ref[...] = (y * _rms_scale(y) * g_ref[...]).astype(o_ref.dtype)


def _rglru(z, state_conv, state_h, conv_w, conv_b, wa, ba, wx, bx, lam, g,
           *, n_prompt_chunks, chunks_per_seq, width, xb_col_blk, yb_col_blk):
    t = z.shape[0]
    n_items = t // CHUNK
    npc = n_prompt_chunks
    nblk = width // LRU_BLOCK
    state_map = lambda n: (jnp.maximum(n - npc, 0), 0, 0)
    full2 = lambda n: (0, 0)
    full3 = lambda n: (0, 0, 0)
    kern = functools.partial(_lru_kernel, n_prompt_chunks=npc, chunks_per_seq=chunks_per_seq)
    return pl.pallas_call(
        kern,
        grid=(n_items,),
        in_specs=[
            pl.BlockSpec((CHUNK, width), lambda n: (n, xb_col_blk)),
            pl.BlockSpec((CHUNK, width), lambda n: (n, yb_col_blk)),
            pl.BlockSpec((1, CONV_WIDTH - 1, width), state_map),
            pl.BlockSpec((1, 1, width), state_map),
            pl.BlockSpec((CONV_WIDTH, width), full2),
            pl.BlockSpec((1, width), full2),
            pl.BlockSpec((nblk, LRU_BLOCK, LRU_BLOCK), full3),
            pl.BlockSpec((1, width), full2),
            pl.BlockSpec((nblk, LRU_BLOCK, LRU_BLOCK), full3),
            pl.BlockSpec((1, width), full2),
            pl.BlockSpec((1, width), full2),
            pl.BlockSpec((1, width), full2),
        ],
        out_specs=[
            pl.BlockSpec((CHUNK, width), lambda n: (n, 0)),
            pl.BlockSpec((1, 1, width), lambda n: (n, 0, 0)),
        ],
        out_shape=[
            jax.ShapeDtypeStruct((t, width), BF16),
            jax.ShapeDtypeStruct((n_items, 1, width), F32),
        ],
        scratch_shapes=[pltpu.VMEM((8, width), F32), pltpu.VMEM((1, width), F32)],
        compiler_params=_params(1),
        name="rglru",
    )(z, z, state_conv, state_h, conv_w, conv_b, wa, ba, wx, bx, lam, g)


def _outproj_kernel(att_ref, lru_ref, wa_ref, wl_ref, xp_ref, xs_ref, o_ref, *, n_prompt_tiles):
    i = pl.program_id(0)
    acc = jnp.dot(att_ref[...], wa_ref[...], preferred_element_type=F32)
    acc = acc + jnp.dot(lru_ref[...], wl_ref[...], preferred_element_type=F32)

    @pl.when(i < n_prompt_tiles)
    def _():
        o_ref[...] = xp_ref[...] + acc

    @pl.when(i >= n_prompt_tiles)
    def _():
        o_ref[...] = xs_ref[...] + acc


def _out_proj(att, lru, w, xp, xs, *, tm, tn):
    t, aw = att.shape
    lw = lru.shape[1]
    d = w.shape[1]
    assert aw == lw and w.shape[0] == aw + lw
    npt = xp.shape[0] // tm
    kern = functools.partial(_outproj_kernel, n_prompt_tiles=npt)
    return pl.pallas_call(
        kern,
        grid=(t // tm, d // tn),
        in_specs=[
            pl.BlockSpec((tm, aw), lambda i, j: (i, 0)),
            pl.BlockSpec((tm, lw), lambda i, j: (i, 0)),
            pl.BlockSpec((aw, tn), lambda i, j: (0, j)),
            pl.BlockSpec((lw, tn), lambda i, j: (1, j)),
            pl.BlockSpec((tm, tn), lambda i, j: (jnp.minimum(i, npt - 1), j)),
            pl.BlockSpec((tm, tn), lambda i, j: (jnp.maximum(i - npt, 0), j)),
        ],
        out_specs=pl.BlockSpec((tm, tn), lambda i, j: (i, j)),
        out_shape=jax.ShapeDtypeStruct((t, d), F32),
        compiler_params=_params(2),
        name="out_proj",
    )(att, lru, w, w, xp, xs)


ROUTE_GROUP_LANE = EXPERTS_PER_GROUP
ROUTE_RANK_LANE = EXPERTS_PER_GROUP + 1


def _router_kernel(x_ref, g_ref, wr_ref, br_ref, route_ref, cnt_ref):
    @pl.when(pl.program_id(0) == 0)
    def _():
        cnt_ref[...] = jnp.zeros_like(cnt_ref)

    x = x_ref[...]
    h = (x * _rms_scale(x) * g_ref[...]).astype(BF16)
    logits = jnp.dot(h, wr_ref[...], preferred_element_type=F32) + br_ref[...]
    lane = lax.broadcasted_iota(jnp.int32, logits.shape, 1).astype(F32)
    ninf = -jnp.inf

    def first_argmax(v, vmax):
        return jnp.min(jnp.where(v == vmax, lane, float(LANES)), axis=-1, keepdims=True)

    gl = jnp.where(lane < N_GROUPS, logits, ninf)
    gm = jnp.max(gl, axis=-1, keepdims=True)
    g_idx = first_argmax(gl, gm)
    g_w = 1.0 / jnp.sum(jnp.exp(gl - gm), axis=-1, keepdims=True)

    lo = N_GROUPS + EXPERTS_PER_GROUP * g_idx
    el = jnp.where(jnp.logical_and(lane >= lo, lane < lo + EXPERTS_PER_GROUP), logits, ninf)
    v1 = jnp.max(el, axis=-1, keepdims=True)
    i1 = first_argmax(el, v1)
    el2 = jnp.where(lane == i1, ninf, el)
    v2 = jnp.max(el2, axis=-1, keepdims=True)
    i2 = first_argmax(el2, v2)
    e2 = jnp.exp(v2 - v1)
    w1 = (1.0 / (1.0 + e2)) * g_w
    w2 = (e2 / (1.0 + e2)) * g_w
    gates = jnp.where(lane == i1 - lo, w1, 0.0) + jnp.where(lane == i2 - lo, w2, 0.0)

    tm = x.shape[0]
    onehot = jnp.where(lane == g_idx, 1.0, 0.0)
    r_i = lax.broadcasted_iota(jnp.int32, (tm, tm), 0)
    c_i = lax.broadcasted_iota(jnp.int32, (tm, tm), 1)
    tri = jnp.where(c_i < r_i, 1.0, 0.0).astype(BF16)
    before = jnp.dot(tri, onehot.astype(BF16), preferred_element_type=F32) + cnt_ref[...]
    rank = jnp.sum(onehot * before, axis=-1, keepdims=True)
    cnt_ref[...] += jnp.sum(onehot, axis=0, keepdims=True)
    route_ref[...] = (gates + jnp.where(lane == ROUTE_GROUP_LANE, g_idx, 0.0)
                      + jnp.where(lane == ROUTE_RANK_LANE, rank, 0.0))


def _router(x2, g, wr, br, *, tm):
    t, d = x2.shape
    return pl.pallas_call(
        _router_kernel,
        grid=(t // tm,),
        in_specs=[
            pl.BlockSpec((tm, d), lambda i: (i, 0)),
            pl.BlockSpec((1, d), lambda i: (0, 0)),
            pl.BlockSpec((d, LANES), lambda i: (0, 0)),
            pl.BlockSpec((1, LANES), lambda i: (0, 0)),
        ],
        out_specs=[
            pl.BlockSpec((tm, LANES), lambda i: (i, 0)),
            pl.BlockSpec((1, LANES), lambda i: (0, 0)),
        ],
        out_shape=[
            jax.ShapeDtypeStruct((t, LANES), F32),
            jax.ShapeDtypeStruct((1, LANES), F32),
        ],
        compiler_params=_params(1),
        name="router",
    )(x2, g, wr, br)


def _routing_tables(route, counts, *, tile, n_tiles_max, n_halves):
    t = route.shape[0]
    i32 = jnp.int32
    g = route[:, ROUTE_GROUP_LANE].astype(i32)
    rank = route[:, ROUTE_RANK_LANE].astype(i32)
    cnt = counts[0, :N_GROUPS].astype(i32)
    tiles_g = (cnt + tile - 1) // tile
    gids = jnp.arange(N_GROUPS, dtype=i32)
    tend = jnp.sum(jnp.where(gids[None, :] <= gids[:, None], tiles_g[None, :], 0), axis=1)
    tstart = tend - tiles_g
    n_used = tend[-1]
    dest = tstart[g] * tile + rank
    src = jnp.zeros((n_tiles_max * tile,), i32).at[dest].set(jnp.arange(t, dtype=i32))

    def items(per_tile):
        n_items = per_tile * n_used
        w_all = jnp.arange(per_tile * n_tiles_max, dtype=i32)
        w = jnp.minimum(w_all, n_items - 1)
        gi = jnp.sum((w[:, None] >= per_tile * tend[None, :]).astype(i32), axis=1)
        local = w - per_tile * tstart[gi]
        spare = w_all - n_items
        part = jnp.where(spare < 0, local // tiles_g[gi], spare % per_tile)
        p = jnp.where(spare < 0, tstart[gi] + local % tiles_g[gi], n_used + spare // per_tile)
        return p, part, gi, n_items.reshape(1)

    up_p, up_e, up_g, up_n = items(EXPERTS_PER_GROUP)
    dn_p, dn_h, dn_g, dn_n = items(n_halves)
    return dict(dest=dest, src=src, n_used=n_used.reshape(1),
                up=(up_p, up_e, up_g * EXPERTS_PER_GROUP + up_e, up_n),
                down=(dn_p, dn_h, dn_g, dn_n))


def _row_copy(src_hbm, dst_vmem, src_row, dst_row, sem):
    return pltpu.make_async_copy(src_hbm.at[pl.ds(src_row, 1)], dst_vmem.at[pl.ds(dst_row, 1)], sem)


def _start_row_gather(idx_ref, base, n_rows, pairs):
    def body(r, carry):
        i = idx_ref[base + r]
        for src_hbm, dst_vmem, sem in pairs:
            _row_copy(src_hbm, dst_vmem, i, r, sem).start()
        return carry
    lax.fori_loop(0, n_rows, body, 0, unroll=8)


def _wait_row_gather(n_rows, pairs):
    def body(r, carry):
        for src_hbm, dst_vmem, sem in pairs:
            _row_copy(src_hbm, dst_vmem, 0, r, sem).wait()
        return carry
    lax.fori_loop(0, n_rows, body, 0, unroll=8)


def _dispatch_kernel(src_ref, nused_ref, x2_hbm, route_hbm, g_ref, xs_ref, gs_ref,
                     xbuf, gbuf, sem):
    p = pl.program_id(0)
    n_used = nused_ref[0]
    tile = xbuf.shape[1]
    slot = p % 2

    def pairs(s):
        return [(x2_hbm, xbuf.at[s], sem.at[0, s]), (route_hbm, gbuf.at[s], sem.at[1, s])]

    @pl.when(p == 0)
    def _():
        _start_row_gather(src_ref, 0, tile, pairs(0))

    @pl.when(p + 1 < n_used)
    def _():
        _start_row_gather(src_ref, (p + 1) * tile, tile, pairs(1 - slot))

    @pl.when(p < n_used)
    def _():
        _wait_row_gather(tile, pairs(slot))
        x = xbuf[slot]
        xs_ref[...] = (x * _rms_scale(x) * g_ref[...]).astype(xs_ref.dtype)
        gs_ref[...] = gbuf[slot]

    @pl.when(p >= n_used)
    def _():
        xs_ref[...] = jnp.zeros_like(xs_ref)
        gs_ref[...] = jnp.zeros_like(gs_ref)


def _dispatch(x2, route, g, src, n_used, *, tile, n_tiles_max):
    t, d = x2.shape
    tile_map = lambda p, src_ref, n_ref: (p, 0)
    grid_spec = pltpu.PrefetchScalarGridSpec(
        num_scalar_prefetch=2,
        grid=(n_tiles_max,),
        in_specs=[
            pl.BlockSpec(memory_space=pl.ANY),
            pl.BlockSpec(memory_space=pl.ANY),
            pl.BlockSpec((1, d), lambda p, s, n: (0, 0)),
        ],
        out_specs=[
            pl.BlockSpec((tile, d), tile_map),
            pl.BlockSpec((tile, LANES), tile_map),
        ],
        scratch_shapes=[
            pltpu.VMEM((2, tile, d), F32),
            pltpu.VMEM((2, tile, LANES), F32),
            pltpu.SemaphoreType.DMA((2, 2)),
        ],
    )
    return pl.pallas_call(
        _dispatch_kernel,
        grid_spec=grid_spec,
        out_shape=[
            jax.ShapeDtypeStruct((n_tiles_max * tile, d), BF16),
            jax.ShapeDtypeStruct((n_tiles_max * tile, LANES), F32),
        ],
        compiler_params=_params(1),
        name="moe_dispatch",
    )(src, n_used, x2, route, g)


def _moe_up_kernel(ip_ref, ie_ref, ige_ref, n_ref, xs_ref, gs_ref, w1_ref, w3_ref, hid_ref):
    w = pl.program_id(0)

    @pl.when(w < n_ref[0])
    def _():
        x = xs_ref[...]
        a = jnp.dot(x, w1_ref[0].astype(BF16), preferred_element_type=F32)
        b = jnp.dot(x, w3_ref[0].astype(BF16), preferred_element_type=F32)
        lane = lax.broadcasted_iota(jnp.int32, gs_ref.shape, 1)
        gate = jnp.sum(jnp.where(lane == ie_ref[w], gs_ref[...], 0.0), axis=-1, keepdims=True)
        hid_ref[...] = (jax.nn.silu(a) * b * gate).astype(hid_ref.dtype)

    @pl.when(w >= n_ref[0])
    def _():
        hid_ref[...] = jnp.zeros_like(hid_ref)


def _moe_up(xs, gs, w1, w3, items, *, tile):
    rows, d = xs.shape
    f = w1.shape[2]
    ip, ie, ige, n_items = items
    grid_spec = pltpu.PrefetchScalarGridSpec(
        num_scalar_prefetch=4,
        grid=(ip.shape[0],),
        in_specs=[
            pl.BlockSpec((tile, d), lambda w, ip, ie, ige, n: (ip[w], 0)),
            pl.BlockSpec((tile, LANES), lambda w, ip, ie, ige, n: (ip[w], 0)),
            pl.BlockSpec((1, d, f), lambda w, ip, ie, ige, n: (ige[w], 0, 0)),
            pl.BlockSpec((1, d, f), lambda w, ip, ie, ige, n: (ige[w], 0, 0)),
        ],
        out_specs=pl.BlockSpec((tile, f), lambda w, ip, ie, ige, n: (ip[w], ie[w])),
    )
    return pl.pallas_call(
        _moe_up_kernel,
        grid_spec=grid_spec,
        out_shape=jax.ShapeDtypeStruct((rows, EXPERTS_PER_GROUP * f), BF16),
        compiler_params=_params(1),
        name="moe_up",
    )(ip, ie, ige, n_items, xs, gs, w1, w3)


def _moe_down_kernel(ip_ref, ih_ref, ig_ref, n_ref, hid_ref, w2_ref, y_ref):
    @pl.when(pl.program_id(0) < n_ref[0])
    def _():
        y_ref[...] = jnp.dot(hid_ref[...], w2_ref[0].astype(BF16), preferred_element_type=F32)

    @pl.when(pl.program_id(0) >= n_ref[0])
    def _():
        y_ref[...] = jnp.zeros_like(y_ref)


def _moe_down(hid, w2g, items, *, tile, n_halves):
    rows, k = hid.shape
    d = w2g.shape[2]
    tn = d // n_halves
    ip, ih, ig, n_items = items
    grid_spec = pltpu.PrefetchScalarGridSpec(
        num_scalar_prefetch=4,
        grid=(ip.shape[0],),
        in_specs=[
            pl.BlockSpec((tile, k), lambda w, ip, ih, ig, n: (ip[w], 0)),
            pl.BlockSpec((1, k, tn), lambda w, ip, ih, ig, n: (ig[w], 0, ih[w])),
        ],
        out_specs=pl.BlockSpec((tile, tn), lambda w, ip, ih, ig, n: (ip[w], ih[w])),
    )
    return pl.pallas_call(
        _moe_down_kernel,
        grid_spec=grid_spec,
        out_shape=jax.ShapeDtypeStruct((rows, d), F32),
        compiler_params=_params(1),
        name="moe_down",
    )(ip, ih, ig, n_items, hid, w2g)


def _final_kernel(dest_ref, x_ref, ys_hbm, g_ref, o_ref, ybuf, sem, *, tile0):
    i = pl.program_id(0)
    tm = ybuf.shape[1]
    slot = i % 2

    def pairs(s):
        return [(ys_hbm, ybuf.at[s], sem.at[s])]

    @pl.when(i == 0)
    def _():
        _start_row_gather(dest_ref, tile0 * tm, tm, pairs(0))

    @pl.when(i + 1 < pl.num_programs(0))
    def _():
        _start_row_gather(dest_ref, (tile0 + i + 1) * tm, tm, pairs(1 - slot))

    _wait_row_gather(tm, pairs(slot))
    x = x_ref[...] + ybuf[slot]
    o_ref[...] = x * _rms_scale(x) * g_ref[...]


def _final(x2, ys, dest, g, *, row0, n_rows, tm):
    d = x2.shape[1]
    off = row0 // tm
    grid_spec = pltpu.PrefetchScalarGridSpec(
        num_scalar_prefetch=1,
        grid=(n_rows // tm,),
        in_specs=[
            pl.BlockSpec((tm, d), lambda i, dest: (i + off, 0)),
            pl.BlockSpec(memory_space=pl.ANY),
            pl.BlockSpec((1, d), lambda i, dest: (0, 0)),
        ],
        out_specs=pl.BlockSpec((tm, d), lambda i, dest: (i, 0)),
        scratch_shapes=[pltpu.VMEM((2, tm, d), F32), pltpu.SemaphoreType.DMA((2,))],
    )
    return pl.pallas_call(
        functools.partial(_final_kernel, tile0=off),
        grid_spec=grid_spec,
        out_shape=jax.ShapeDtypeStruct((n_rows, d), F32),
        compiler_params=_params(1),
        name="final_norm",
    )(dest, x2, ys, g)


def _rope_tables(pos):
    half = ROT_DIM // 2
    inv = ROPE_THETA ** (-jnp.arange(half, dtype=F32) / half)
    ang = pos.astype(F32)[:, None] * inv[None, :]
    cos, sin = jnp.cos(ang), jnp.sin(ang)
    ones = jnp.ones((pos.shape[0], HEAD_DIM - ROT_DIM), F32)
    zeros = jnp.zeros((pos.shape[0], HEAD_DIM - half), F32)
    c = jnp.concatenate([cos, cos, ones], axis=1)
    s_hi = jnp.concatenate([-sin, zeros], axis=1)
    s_lo = jnp.concatenate([jnp.zeros_like(sin), sin, ones * 0.0], axis=1)
    rep = LANES // HEAD_DIM
    return jnp.tile(c, (1, rep)), jnp.tile(s_hi, (1, rep)), jnp.tile(s_lo, (1, rep))


def kernel(x_prompt, x_sample, cache_k, cache_v, state_conv, state_h, norm_mix, w_in, conv_w,
           conv_b, w_gate_a, b_gate_a, w_gate_x, b_gate_x, lru_lambda, sinks, norm_attn_out,
           norm_lru_out, w_out, norm_ffn, w_group, b_group, w_expert_router, b_expert_router,
           w1, w3, w2, norm_final):
    batch, seq, d_model = x_prompt.shape
    dec_batch, dec_seq, _ = x_sample.shape
    depth = w_in.shape[0]
    assert depth == 1 and dec_seq == CHUNK and seq % CHUNK == 0
    n_heads = sinks.shape[1]
    att_width = n_heads * HEAD_DIM
    n_kv = cache_k.shape[3]
    kv_cols = n_kv * HEAD_DIM
    lru_width = lru_lambda.shape[1]
    past_len = 1024
    cw = cache_k.shape[2]
    assert cw == WIN_CHUNKS * CHUNK
    n_p, n_s = batch * seq, dec_batch * dec_seq
    chunks_per_seq = seq // CHUNK
    n_prompt_chunks = n_p // CHUNK

    xp = x_prompt.reshape(n_p, d_model)
    xs = x_sample.reshape(n_s, d_model)

    w = w_in[0]
    q_end, k_end, v_end = att_width, att_width + kv_cols, att_width + 2 * kv_cols
    w_r = jnp.concatenate([w[:, :q_end], w[:, v_end:], w[:, q_end:v_end]], axis=1).astype(BF16)
    tn = 2 * kv_cols
    assert tn == 512 and att_width % tn == 0 and lru_width % tn == 0
    n_q_tiles = att_width // tn
    kv_tile = (att_width + 2 * lru_width) // tn

    pos = jnp.concatenate([jnp.tile(jnp.arange(seq), batch),
                           jnp.tile(past_len + jnp.arange(dec_seq), dec_batch)])
    c_tab, shi_tab, slo_tab = _rope_tables(pos)

    z = _in_proj(xp, xs, norm_mix, w_r, c_tab, shi_tab, slo_tab,
                 tm=512, tn=tn, n_q_tiles=n_q_tiles, kv_tile=kv_tile)

    k_col = att_width + 2 * lru_width
    v_col = k_col + kv_cols
    att = _attention(
        z, cache_k[0].reshape(dec_batch, cw, kv_cols), cache_v[0].reshape(dec_batch, cw, kv_cols),
        sinks[0], norm_attn_out,
        n_prompt_chunks=n_prompt_chunks, chunks_per_seq=chunks_per_seq, att_width=att_width,
        n_kv=n_kv, k_col_blk=k_col // kv_cols, v_col_blk=v_col // kv_cols)

    nblk = lru_width // LRU_BLOCK
    lru, h_tiles = _rglru(
        z, state_conv[0], state_h[0].reshape(dec_batch, 1, lru_width), conv_w[0], conv_b,
        w_gate_a[0].astype(BF16), b_gate_a[0].reshape(1, lru_width),
        w_gate_x[0].astype(BF16), b_gate_x[0].reshape(1, lru_width),
        lru_lambda, norm_lru_out,
        n_prompt_chunks=n_prompt_chunks, chunks_per_seq=chunks_per_seq, width=lru_width,
        xb_col_blk=att_width // lru_width, yb_col_blk=att_width // lru_width + 1)
    del nblk

    x2 = _out_proj(att, lru, w_out[0].astype(BF16), xp, xs, tm=1024, tn=512)

    n_routes = N_GROUPS + N_GROUPS * EXPERTS_PER_GROUP
    wr = jnp.concatenate([w_group[0], w_expert_router[0],
                          jnp.zeros((d_model, LANES - n_routes), F32)], axis=1).astype(BF16)
    br = jnp.concatenate([b_group[0], b_expert_router[0],
                          jnp.zeros((LANES - n_routes,), F32)]).reshape(1, LANES)
    route, counts = _router(x2, norm_ffn, wr, br, tm=MOE_TILE)

    n_tiles_max = (n_p + n_s) // MOE_TILE + N_GROUPS
    tabs = _routing_tables(route, counts, tile=MOE_TILE, n_tiles_max=n_tiles_max,
                           n_halves=MOE_DOWN_HALVES)
    xsort, gsort = _dispatch(x2, route, norm_ffn, tabs["src"], tabs["n_used"],
                             tile=MOE_TILE, n_tiles_max=n_tiles_max)
    hid = _moe_up(xsort, gsort, w1[0], w3[0], tabs["up"], tile=MOE_TILE)
    d_expert = w2.shape[2]
    w2g = w2[0].reshape(N_GROUPS, EXPERTS_PER_GROUP * d_expert, d_model)
    ysort = _moe_down(hid, w2g, tabs["down"], tile=MOE_TILE, n_halves=MOE_DOWN_HALVES)

    g_fin = norm_final.reshape(1, d_model)
    y_prompt = _final(x2, ysort, tabs["dest"], g_fin, row0=0, n_rows=n_p,
                      tm=MOE_TILE).reshape(batch, seq, d_model)
    y_sample = _final(x2, ysort, tabs["dest"], g_fin, row0=n_p, n_rows=n_s,
                      tm=MOE_TILE).reshape(dec_batch, dec_seq, d_model)

    win = min(WIN_CHUNKS * CHUNK, seq)

    def prompt_tail(col, width, rows):
        return jnp.stack([lax.slice(z, ((b + 1) * seq - rows, col), ((b + 1) * seq, col + width))
                          for b in range(batch)])

    def sample_rows(col, width):
        return lax.slice(z, (n_p, col), (n_p + n_s, col + width)).reshape(dec_batch, dec_seq, width)

    k_prompt = prompt_tail(k_col, kv_cols, win).reshape(1, batch, win, n_kv, HEAD_DIM)
    v_prompt = prompt_tail(v_col, kv_cols, win).reshape(1, batch, win, n_kv, HEAD_DIM)
    conv_prompt = prompt_tail(att_width, lru_width, CONV_WIDTH - 1)[None]
    h_prompt = h_tiles[:n_prompt_chunks, 0].reshape(batch, chunks_per_seq, lru_width)[:, -1][None]

    ks = sample_rows(k_col, kv_cols).reshape(dec_batch, dec_seq, n_kv, HEAD_DIM)
    vs = sample_rows(v_col, kv_cols).reshape(dec_batch, dec_seq, n_kv, HEAD_DIM)
    k_sample = jnp.concatenate([cache_k[0], ks], axis=1)[:, -cw:][None]
    v_sample = jnp.concatenate([cache_v[0], vs], axis=1)[:, -cw:][None]
    xs_rows = sample_rows(att_width, lru_width)[:, dec_seq - (CONV_WIDTH - 1):]
    conv_sample = jnp.concatenate([state_conv[0], xs_rows], axis=1)[:, -(CONV_WIDTH - 1):][None]
    h_sample = h_tiles[n_prompt_chunks:, 0][None]

    return (y_prompt, y_sample, k_prompt, v_prompt, conv_prompt, h_prompt,
            k_sample, v_sample, conv_sample, h_sample)
```

```python
import functools

import jax
import jax.numpy as jnp
from jax import lax
from jax.experimental import pallas as pl
from jax.experimental.pallas import tpu as pltpu

F32 = jnp.float32
BF16 = jnp.bfloat16

CHUNK = 64
HEAD_DIM = 64
KV_GROUP = 8
WIN_CHUNKS = 2
ROT_DIM = 16
ROPE_THETA = 500000.0
LRU_BLOCK = 128
CONV_WIDTH = 4
LRU_C = 8.0
N_GROUPS = 8
EXPERTS_PER_GROUP = 4
EPS = 1e-6
NEG = -1e30
LANES = 128
INPROJ_ROW_SPLITS = 4
MOE_TILE = 256
MOE_DOWN_HALVES = 2
VMEM_LIMIT = 56 * 1024 * 1024


def _params(n_axes, vmem=VMEM_LIMIT):
    return pltpu.CompilerParams(
        dimension_semantics=("arbitrary",) * n_axes, vmem_limit_bytes=vmem)


def _rms_scale(x):
    return lax.rsqrt(jnp.mean(x * x, axis=-1, keepdims=True) + EPS)


NORM_ROWS = 16


def _rmsnorm_rows(load_rows, g_ref, o_ref, n_rows):
    def body(r, carry):
        rows = pl.ds(pl.multiple_of(r * NORM_ROWS, NORM_ROWS), NORM_ROWS)
        x = load_rows(rows)
        o_ref[rows, :] = (x * _rms_scale(x) * g_ref[...]).astype(o_ref.dtype)
        return carry
    lax.fori_loop(0, n_rows // NORM_ROWS, body, 0, unroll=4)


def _rope_block(zb, c, s_hi, s_lo):
    return (zb * c + pltpu.roll(zb, LANES - ROT_DIM // 2, 1) * s_hi
            + pltpu.roll(zb, ROT_DIM // 2, 1) * s_lo)


def _inproj_kernel(xp_ref, xs_ref, g_ref, w_ref, c_ref, shi_ref, slo_ref, o_ref, h_ref,
                   *, n_prompt_tiles, n_q_tiles, kv_tile):
    i = pl.program_id(0)
    j = pl.program_id(1)

    def norm(x_ref):
        _rmsnorm_rows(lambda rows: x_ref[rows, :], g_ref, h_ref, x_ref.shape[0])

    @pl.when(jnp.logical_and(j == 0, i < n_prompt_tiles))
    def _():
        norm(xp_ref)

    @pl.when(jnp.logical_and(j == 0, i >= n_prompt_tiles))
    def _():
        norm(xs_ref)

    tm, tn = o_ref.shape
    n_blk = tn // LANES
    half = tm // INPROJ_ROW_SPLITS
    for r0 in range(0, tm, half):
        z = jnp.dot(h_ref[r0:r0 + half, :], w_ref[...], preferred_element_type=F32)
        c = c_ref[r0:r0 + half, :]
        shi = shi_ref[r0:r0 + half, :]
        slo = slo_ref[r0:r0 + half, :]
        for b in range(n_blk):
            is_rope = j < n_q_tiles
            if b < n_blk // 2:
                is_rope = jnp.logical_or(is_rope, j == kv_tile)
            zb = z[:, b * LANES:(b + 1) * LANES]
            o_ref[r0:r0 + half, b * LANES:(b + 1) * LANES] = _rope_block(
                zb, jnp.where(is_rope, c, 1.0), jnp.where(is_rope, shi, 0.0),
                jnp.where(is_rope, slo, 0.0))


def _in_proj(xp, xs, g, w, c, shi, slo, *, tm, tn, n_q_tiles, kv_tile):
    n_p, d = xp.shape
    n_s = xs.shape[0]
    t = n_p + n_s
    npt = n_p // tm
    n_cols = w.shape[1]
    kern = functools.partial(_inproj_kernel, n_prompt_tiles=npt, n_q_tiles=n_q_tiles,
                             kv_tile=kv_tile)
    return pl.pallas_call(
        kern,
        grid=(t // tm, n_cols // tn),
        in_specs=[
            pl.BlockSpec((tm, d), lambda i, j: (jnp.minimum(i, npt - 1), 0)),
            pl.BlockSpec((tm, d), lambda i, j: (jnp.maximum(i - npt, 0), 0),
                         pipeline_mode=pl.Buffered(1)),
            pl.BlockSpec((1, d), lambda i, j: (0, 0)),
            pl.BlockSpec((d, tn), lambda i, j: (0, j)),
            pl.BlockSpec((tm, LANES), lambda i, j: (i, 0)),
            pl.BlockSpec((tm, LANES), lambda i, j: (i, 0)),
            pl.BlockSpec((tm, LANES), lambda i, j: (i, 0)),
        ],
        out_specs=pl.BlockSpec((tm, tn), lambda i, j: (i, j)),
        out_shape=jax.ShapeDtypeStruct((t, n_cols), F32),
        scratch_shapes=[pltpu.VMEM((tm, d), BF16)],
        compiler_params=_params(2),
        name="in_proj",
    )(xp, xs, g, w, c, shi, slo)


def _attn_kernel(q_ref, k0_ref, k1_ref, k2_ref, v0_ref, v1_ref, v2_ref,
                 kc_ref, vc_ref, sink_ref, g_ref, o_ref, *, n_prompt_chunks, chunks_per_seq, n_kv):
    n = pl.program_id(0)
    is_s = n >= n_prompt_chunks
    c = n % chunks_per_seq
    lo = jnp.where(is_s, 0, jnp.where(c >= 2, 0, jnp.where(c == 1, CHUNK, 2 * CHUNK)))

    kc = kc_ref[0]
    vc = vc_ref[0]
    k_band = jnp.concatenate([
        jnp.where(is_s, kc[:CHUNK], k0_ref[...]),
        jnp.where(is_s, kc[CHUNK:], k1_ref[...]),
        k2_ref[...]], axis=0).astype(BF16)
    v_band = jnp.concatenate([
        jnp.where(is_s, vc[:CHUNK], v0_ref[...]),
        jnp.where(is_s, vc[CHUNK:], v1_ref[...]),
        v2_ref[...]], axis=0).astype(BF16)

    q = (q_ref[...] * (HEAD_DIM ** -0.5)).astype(BF16)
    n_keys = (WIN_CHUNKS + 1) * CHUNK
    key = lax.broadcasted_iota(jnp.int32, (n_keys, KV_GROUP * CHUNK), 0)
    valid = key >= lo

    outs = []
    for gi in range(n_kv):
        kg = k_band[:, gi * HEAD_DIM:(gi + 1) * HEAD_DIM]
        vg = v_band[:, gi * HEAD_DIM:(gi + 1) * HEAD_DIM]
        heads = [gi * KV_GROUP + h for h in range(KV_GROUP)]
        qg = jnp.concatenate([q[:, h * HEAD_DIM:(h + 1) * HEAD_DIM] for h in heads], axis=0)
        st = lax.dot_general(kg, qg, (((1,), (1,)), ((), ())), preferred_element_type=F32)
        st = jnp.where(valid, st, NEG)
        sink = sink_ref[gi:gi + 1, :]
        m = jnp.maximum(jnp.max(st, axis=0, keepdims=True), sink)
        p = jnp.exp(st - m)
        denom = jnp.sum(p, axis=0, keepdims=True) + jnp.exp(sink - m)
        pn = (p * (1.0 / denom)).astype(BF16)
        og = lax.dot_general(pn, vg, (((0,), (0,)), ((), ())), preferred_element_type=F32)
        outs.extend(og[h * CHUNK:(h + 1) * CHUNK] for h in range(KV_GROUP))
    att = jnp.concatenate(outs, axis=1)
    o_ref[...] = (att * _rms_scale(att) * g_ref[...]).astype(o_ref.dtype)


def _attention(z, cache_k, cache_v, sinks, g, *, n_prompt_chunks, chunks_per_seq,
               att_width, n_kv, k_col_blk, v_col_blk):
    t = z.shape[0]
    n_items = t // CHUNK
    kvw = n_kv * HEAD_DIM
    npc = n_prompt_chunks

    def hist(d):
        return lambda n: (n - jnp.minimum(d, n % chunks_per_seq), k_col_blk)

    def histv(d):
        return lambda n: (n - jnp.minimum(d, n % chunks_per_seq), v_col_blk)

    cache_map = lambda n: (jnp.maximum(n - npc, 0), 0, 0)
    kern = functools.partial(_attn_kernel, n_prompt_chunks=npc,
                             chunks_per_seq=chunks_per_seq, n_kv=n_kv)
    sink_rows = jnp.repeat(sinks.reshape(n_kv, KV_GROUP), CHUNK, axis=1)
    return pl.pallas_call(
        kern,
        grid=(n_items,),
        in_specs=[
            pl.BlockSpec((CHUNK, att_width), lambda n: (n, 0)),
            pl.BlockSpec((CHUNK, kvw), hist(2)),
            pl.BlockSpec((CHUNK, kvw), hist(1)),
            pl.BlockSpec((CHUNK, kvw), hist(0)),
            pl.BlockSpec((CHUNK, kvw), histv(2)),
            pl.BlockSpec((CHUNK, kvw), histv(1)),
            pl.BlockSpec((CHUNK, kvw), histv(0)),
            pl.BlockSpec((1, WIN_CHUNKS * CHUNK, kvw), cache_map),
            pl.BlockSpec((1, WIN_CHUNKS * CHUNK, kvw), cache_map),
            pl.BlockSpec((n_kv, KV_GROUP * CHUNK), lambda n: (0, 0)),
            pl.BlockSpec((1, att_width), lambda n: (0, 0)),
        ],
        out_specs=pl.BlockSpec((CHUNK, att_width), lambda n: (n, 0)),
        out_shape=jax.ShapeDtypeStruct((t, att_width), BF16),
        compiler_params=_params(1),
        name="attention",
    )(z, z, z, z, z, z, z, cache_k, cache_v, sink_rows, g)


SUBLANES = 8


def _lru_kernel(xb_ref, yb_ref, sconv_ref, sh_ref, cw_ref, cb_ref, wa_ref, ba_ref,
                wx_ref, bx_ref, lam_ref, g_ref, o_ref, hl_ref, hist_ref, hcar_ref,
                a_sc, b_sc, h_sc, *, n_prompt_chunks, chunks_per_seq):
    n = pl.program_id(0)
    is_s = n >= n_prompt_chunks
    hist_rows = hist_ref.shape[0]

    @pl.when(jnp.logical_and(jnp.logical_not(is_s), n % chunks_per_seq == 0))
    def _():
        hist_ref[...] = jnp.zeros_like(hist_ref)
        hcar_ref[...] = jnp.zeros_like(hcar_ref)

    @pl.when(is_s)
    def _():
        hist_ref[...] = jnp.zeros_like(hist_ref)
        hist_ref[hist_rows - (CONV_WIDTH - 1):, :] = sconv_ref[0]
        hcar_ref[...] = sh_ref[0]

    x = xb_ref[...]
    rows, width = x.shape
    hist = hist_ref[...]
    row8 = lax.broadcasted_iota(jnp.int32, (hist_rows, width), 0)
    xc = cb_ref[...]
    for j in range(CONV_WIDTH):
        s = CONV_WIDTH - 1 - j
        if s == 0:
            xs = x
        else:
            xr = pltpu.roll(x, s, 0)
            head = jnp.where(row8 < s, pltpu.roll(hist, s, 0), xr[:hist_rows])
            xs = jnp.concatenate([head, xr[hist_rows:]], axis=0)
        xc = xc + xs * cw_ref[j:j + 1, :]
    hist_ref[...] = x[rows - hist_rows:]

    xcb = xc.astype(BF16)
    ra, rx = [], []
    for nb in range(width // LRU_BLOCK):
        blk = xcb[:, nb * LRU_BLOCK:(nb + 1) * LRU_BLOCK]
        ra.append(jnp.dot(blk, wa_ref[nb], preferred_element_type=F32))
        rx.append(jnp.dot(blk, wx_ref[nb], preferred_element_type=F32))
    r = jax.nn.sigmoid(jnp.concatenate(ra, axis=1) + ba_ref[...])
    ig = jax.nn.sigmoid(jnp.concatenate(rx, axis=1) + bx_ref[...])
    log_a = (-LRU_C * r) * jax.nn.softplus(-lam_ref[...])
    a = jnp.exp(log_a)
    b = jnp.sqrt(-jnp.tanh(log_a) * (1.0 + a * a)) * ig * xc

    seg_len = rows // SUBLANES
    h_blocks, carries = [], []
    for nb in range(width // LANES):
        lanes = slice(nb * LANES, (nb + 1) * LANES)
        a_sc[nb] = a[:, lanes]
        b_sc[nb] = b[:, lanes]
        hloc, ploc = [], []
        for k in range(seg_len):
            a_k = a_sc[nb, pl.ds(k, SUBLANES, stride=seg_len), :]
            b_k = b_sc[nb, pl.ds(k, SUBLANES, stride=seg_len), :]
            hloc.append(b_k if k == 0 else a_k * hloc[-1] + b_k)
            ploc.append(a_k if k == 0 else a_k * ploc[-1])
        carry = hcar_ref[:, lanes]
        seg_in = []
        for s in range(SUBLANES):
            seg_in.append(carry)
            carry = ploc[-1][s:s + 1] * carry + hloc[-1][s:s + 1]
        seg_in = jnp.concatenate(seg_in, axis=0)
        carries.append(carry)
        for k in range(seg_len):
            h_sc[nb, pl.ds(k, SUBLANES, stride=seg_len), :] = hloc[k] + ploc[k] * seg_in
        h_blocks.append(h_sc[nb])
    h = jnp.concatenate(h_blocks, axis=1)
    h_last = jnp.concatenate(carries, axis=1)
    hcar_ref[...] = h_last
    hl_ref[0] = h_last

    y = jax.nn.gelu(yb_ref[...]) * h
    o_ref[...] = (y * _rms_scale(y) * g_ref[...]).astype(o_ref.dtype)


def _rglru(z, state_conv, state_h, conv_w, conv_b, wa, ba, wx, bx, lam, g,
           *, n_prompt_chunks, chunks_per_seq, width, xb_col_blk, yb_col_blk):
    t = z.shape[0]
    n_items = t // CHUNK
    npc = n_prompt_chunks
    nblk = width // LRU_BLOCK
    state_map = lambda n: (jnp.maximum(n - npc, 0), 0, 0)
    full2 = lambda n: (0, 0)
    full3 = lambda n: (0, 0, 0)
    kern = functools.partial(_lru_kernel, n_prompt_chunks=npc, chunks_per_seq=chunks_per_seq)
    return pl.pallas_call(
        kern,
        grid=(n_items,),
        in_specs=[
            pl.BlockSpec((CHUNK, width), lambda n: (n, xb_col_blk)),
            pl.BlockSpec((CHUNK, width), lambda n: (n, yb_col_blk)),
            pl.BlockSpec((1, CONV_WIDTH - 1, width), state_map),
            pl.BlockSpec((1, 1, width), state_map),
            pl.BlockSpec((CONV_WIDTH, width), full2),
            pl.BlockSpec((1, width), full2),
            pl.BlockSpec((nblk, LRU_BLOCK, LRU_BLOCK), full3),
            pl.BlockSpec((1, width), full2),
            pl.BlockSpec((nblk, LRU_BLOCK, LRU_BLOCK), full3),
            pl.BlockSpec((1, width), full2),
            pl.BlockSpec((1, width), full2),
            pl.BlockSpec((1, width), full2),
        ],
        out_specs=[
            pl.BlockSpec((CHUNK, width), lambda n: (n, 0)),
            pl.BlockSpec((1, 1, width), lambda n: (n, 0, 0)),
        ],
        out_shape=[
            jax.ShapeDtypeStruct((t, width), BF16),
            jax.ShapeDtypeStruct((n_items, 1, width), F32),
        ],
        scratch_shapes=[
            pltpu.VMEM((SUBLANES, width), F32),
            pltpu.VMEM((1, width), F32),
            pltpu.VMEM((width // LANES, CHUNK, LANES), F32),
            pltpu.VMEM((width // LANES, CHUNK, LANES), F32),
            pltpu.VMEM((width // LANES, CHUNK, LANES), F32),
        ],
        compiler_params=_params(1),
        name="rglru",
    )(z, z, state_conv, state_h, conv_w, conv_b, wa, ba, wx, bx, lam, g)


def _outproj_kernel(att_ref, lru_ref, wa_ref, wl_ref, xp_ref, xs_ref, o_ref, *, n_prompt_tiles):
    i = pl.program_id(0)
    acc = jnp.dot(att_ref[...], wa_ref[...], preferred_element_type=F32)
    acc = acc + jnp.dot(lru_ref[...], wl_ref[...], preferred_element_type=F32)

    @pl.when(i < n_prompt_tiles)
    def _():
        o_ref[...] = xp_ref[...] + acc

    @pl.when(i >= n_prompt_tiles)
    def _():
        o_ref[...] = xs_ref[...] + acc


def _out_proj(att, lru, w, xp, xs, *, tm, tn):
    t, aw = att.shape
    lw = lru.shape[1]
    d = w.shape[1]
    assert aw == lw and w.shape[0] == aw + lw
    npt = xp.shape[0] // tm
    kern = functools.partial(_outproj_kernel, n_prompt_tiles=npt)
    return pl.pallas_call(
        kern,
        grid=(t // tm, d // tn),
        in_specs=[
            pl.BlockSpec((tm, aw), lambda i, j: (i, 0)),
            pl.BlockSpec((tm, lw), lambda i, j: (i, 0)),
            pl.BlockSpec((aw, tn), lambda i, j: (0, j)),
            pl.BlockSpec((lw, tn), lambda i, j: (1, j)),
            pl.BlockSpec((tm, tn), lambda i, j: (jnp.minimum(i, npt - 1), j)),
            pl.BlockSpec((tm, tn), lambda i, j: (jnp.maximum(i - npt, 0), j)),
        ],
        out_specs=pl.BlockSpec((tm, tn), lambda i, j: (i, j)),
        out_shape=jax.ShapeDtypeStruct((t, d), F32),
        compiler_params=_params(2),
        name="out_proj",
    )(att, lru, w, w, xp, xs)


ROUTE_GROUP_LANE = EXPERTS_PER_GROUP
ROUTE_RANK_LANE = EXPERTS_PER_GROUP + 1


def _router_kernel(x_ref, g_ref, wr_ref, br_ref, route_ref, cnt_ref, h_ref):
    @pl.when(pl.program_id(0) == 0)
    def _():
        cnt_ref[...] = jnp.zeros_like(cnt_ref)

    _rmsnorm_rows(lambda rows: x_ref[rows, :], g_ref, h_ref, x_ref.shape[0])
    logits = jnp.dot(h_ref[...], wr_ref[...], preferred_element_type=F32) + br_ref[...]
    lane = lax.broadcasted_iota(jnp.int32, logits.shape, 1).astype(F32)
    ninf = -jnp.inf

    def first_argmax(v, vmax):
        return jnp.min(jnp.where(v == vmax, lane, float(LANES)), axis=-1, keepdims=True)

    gl = jnp.where(lane < N_GROUPS, logits, ninf)
    gm = jnp.max(gl, axis=-1, keepdims=True)
    g_idx = first_argmax(gl, gm)
    g_w = 1.0 / jnp.sum(jnp.exp(gl - gm), axis=-1, keepdims=True)

    lo = N_GROUPS + EXPERTS_PER_GROUP * g_idx
    el = jnp.where(jnp.logical_and(lane >= lo, lane < lo + EXPERTS_PER_GROUP), logits, ninf)
    v1 = jnp.max(el, axis=-1, keepdims=True)
    i1 = first_argmax(el, v1)
    el2 = jnp.where(lane == i1, ninf, el)
    v2 = jnp.max(el2, axis=-1, keepdims=True)
    i2 = first_argmax(el2, v2)
    e2 = jnp.exp(v2 - v1)
    w1 = (1.0 / (1.0 + e2)) * g_w
    w2 = (e2 / (1.0 + e2)) * g_w
    gates = jnp.where(lane == i1 - lo, w1, 0.0) + jnp.where(lane == i2 - lo, w2, 0.0)

    tm = x_ref.shape[0]
    onehot = jnp.where(lane == g_idx, 1.0, 0.0)
    r_i = lax.broadcasted_iota(jnp.int32, (tm, tm), 0)
    c_i = lax.broadcasted_iota(jnp.int32, (tm, tm), 1)
    tri = jnp.where(c_i < r_i, 1.0, 0.0).astype(BF16)
    before = jnp.dot(tri, onehot.astype(BF16), preferred_element_type=F32) + cnt_ref[...]
    rank = jnp.sum(onehot * before, axis=-1, keepdims=True)
    cnt_ref[...] += jnp.sum(onehot, axis=0, keepdims=True)
    route_ref[...] = (gates + jnp.where(lane == ROUTE_GROUP_LANE, g_idx, 0.0)
                      + jnp.where(lane == ROUTE_RANK_LANE, rank, 0.0))


def _router(x2, g, wr, br, *, tm):
    t, d = x2.shape
    return pl.pallas_call(
        _router_kernel,
        grid=(t // tm,),
        in_specs=[
            pl.BlockSpec((tm, d), lambda i: (i, 0)),
            pl.BlockSpec((1, d), lambda i: (0, 0)),
            pl.BlockSpec((d, LANES), lambda i: (0, 0)),
            pl.BlockSpec((1, LANES), lambda i: (0, 0)),
        ],
        out_specs=[
            pl.BlockSpec((tm, LANES), lambda i: (i, 0)),
            pl.BlockSpec((1, LANES), lambda i: (0, 0)),
        ],
        out_shape=[
            jax.ShapeDtypeStruct((t, LANES), F32),
            jax.ShapeDtypeStruct((1, LANES), F32),
        ],
        scratch_shapes=[pltpu.VMEM((tm, d), BF16)],
        compiler_params=_params(1),
        name="router",
    )(x2, g, wr, br)


def _routing_tables(route, counts, *, tile, n_tiles_max, n_halves):
    t = route.shape[0]
    i32 = jnp.int32
    g = route[:, ROUTE_GROUP_LANE].astype(i32)
    rank = route[:, ROUTE_RANK_LANE].astype(i32)
    cnt = counts[0, :N_GROUPS].astype(i32)
    tiles_g = (cnt + tile - 1) // tile
    gids = jnp.arange(N_GROUPS, dtype=i32)
    tend = jnp.sum(jnp.where(gids[None, :] <= gids[:, None], tiles_g[None, :], 0), axis=1)
    tstart = tend - tiles_g
    n_used = tend[-1]
    dest = tstart[g] * tile + rank
    src = jnp.zeros((n_tiles_max * tile,), i32).at[dest].set(jnp.arange(t, dtype=i32))

    def items(per_tile):
        n_items = per_tile * n_used
        w_all = jnp.arange(per_tile * n_tiles_max, dtype=i32)
        w = jnp.minimum(w_all, n_items - 1)
        gi = jnp.sum((w[:, None] >= per_tile * tend[None, :]).astype(i32), axis=1)
        local = w - per_tile * tstart[gi]
        spare = w_all - n_items
        part = jnp.where(spare < 0, local // tiles_g[gi], spare % per_tile)
        p = jnp.where(spare < 0, tstart[gi] + local % tiles_g[gi], n_used + spare // per_tile)
        return p, part, gi, n_items.reshape(1)

    up_p, up_e, up_g, up_n = items(EXPERTS_PER_GROUP)
    dn_p, dn_h, dn_g, dn_n = items(n_halves)
    return dict(dest=dest, src=src, n_used=n_used.reshape(1),
                up=(up_p, up_e, up_g * EXPERTS_PER_GROUP + up_e, up_n),
                down=(dn_p, dn_h, dn_g, dn_n))


def _row_copy(src_hbm, dst_vmem, src_row, dst_row, sem):
    return pltpu.make_async_copy(src_hbm.at[pl.ds(src_row, 1)], dst_vmem.at[pl.ds(dst_row, 1)], sem)


def _start_row_gather(idx_ref, base, n_rows, pairs):
    def body(r, carry):
        i = idx_ref[base + r]
        for src_hbm, dst_vmem, sem in pairs:
            _row_copy(src_hbm, dst_vmem, i, r, sem).start()
        return carry
    lax.fori_loop(0, n_rows, body, 0, unroll=8)


def _wait_row_gather(n_rows, pairs):
    def body(r, carry):
        for src_hbm, dst_vmem, sem in pairs:
            _row_copy(src_hbm, dst_vmem, 0, r, sem).wait()
        return carry
    lax.fori_loop(0, n_rows, body, 0, unroll=8)


def _dispatch_kernel(src_ref, nused_ref, x2_hbm, route_hbm, g_ref, xs_ref, gs_ref,
                     xbuf, gbuf, sem):
    p = pl.program_id(0)
    n_used = nused_ref[0]
    tile = xbuf.shape[1]
    slot = p % 2

    def pairs(s):
        return [(x2_hbm, xbuf.at[s], sem.at[0, s]), (route_hbm, gbuf.at[s], sem.at[1, s])]

    @pl.when(p == 0)
    def _():
        _start_row_gather(src_ref, 0, tile, pairs(0))

    @pl.when(p + 1 < n_used)
    def _():
        _start_row_gather(src_ref, (p + 1) * tile, tile, pairs(1 - slot))

    @pl.when(p < n_used)
    def _():
        _wait_row_gather(tile, pairs(slot))
        _rmsnorm_rows(lambda rows: xbuf[slot, rows, :], g_ref, xs_ref, tile)
        gs_ref[...] = gbuf[slot]

    @pl.when(p >= n_used)
    def _():
        xs_ref[...] = jnp.zeros_like(xs_ref)
        gs_ref[...] = jnp.zeros_like(gs_ref)


def _dispatch(x2, route, g, src, n_used, *, tile, n_tiles_max):
    t, d = x2.shape
    tile_map = lambda p, src_ref, n_ref: (p, 0)
    grid_spec = pltpu.PrefetchScalarGridSpec(
        num_scalar_prefetch=2,
        grid=(n_tiles_max,),
        in_specs=[
            pl.BlockSpec(memory_space=pl.ANY),
            pl.BlockSpec(memory_space=pl.ANY),
            pl.BlockSpec((1, d), lambda p, s, n: (0, 0)),
        ],
        out_specs=[
            pl.BlockSpec((tile, d), tile_map),
            pl.BlockSpec((tile, LANES), tile_map),
        ],
        scratch_shapes=[
            pltpu.VMEM((2, tile, d), F32),
            pltpu.VMEM((2, tile, LANES), F32),
            pltpu.SemaphoreType.DMA((2, 2)),
        ],
    )
    return pl.pallas_call(
        _dispatch_kernel,
        grid_spec=grid_spec,
        out_shape=[
            jax.ShapeDtypeStruct((n_tiles_max * tile, d), BF16),
            jax.ShapeDtypeStruct((n_tiles_max * tile, LANES), F32),
        ],
        compiler_params=_params(1),
        name="moe_dispatch",
    )(src, n_used, x2, route, g)


def _moe_up_kernel(ip_ref, ie_ref, ige_ref, n_ref, xs_ref, gs_ref, w1_ref, w3_ref, hid_ref):
    w = pl.program_id(0)

    @pl.when(w < n_ref[0])
    def _():
        x = xs_ref[...]
        a = jnp.dot(x, w1_ref[0].astype(BF16), preferred_element_type=F32)
        b = jnp.dot(x, w3_ref[0].astype(BF16), preferred_element_type=F32)
        lane = lax.broadcasted_iota(jnp.int32, gs_ref.shape, 1)
        gate = jnp.sum(jnp.where(lane == ie_ref[w], gs_ref[...], 0.0), axis=-1, keepdims=True)
        hid_ref[...] = (jax.nn.silu(a) * b * gate).astype(hid_ref.dtype)

    @pl.when(w >= n_ref[0])
    def _():
        hid_ref[...] = jnp.zeros_like(hid_ref)


def _moe_up(xs, gs, w1, w3, items, *, tile):
    rows, d = xs.shape
    f = w1.shape[2]
    ip, ie, ige, n_items = items
    grid_spec = pltpu.PrefetchScalarGridSpec(
        num_scalar_prefetch=4,
        grid=(ip.shape[0],),
        in_specs=[
            pl.BlockSpec((tile, d), lambda w, ip, ie, ige, n: (ip[w], 0)),
            pl.BlockSpec((tile, LANES), lambda w, ip, ie, ige, n: (ip[w], 0)),
            pl.BlockSpec((1, d, f), lambda w, ip, ie, ige, n: (ige[w], 0, 0)),
            pl.BlockSpec((1, d, f), lambda w, ip, ie, ige, n: (ige[w], 0, 0)),
        ],
        out_specs=pl.BlockSpec((tile, f), lambda w, ip, ie, ige, n: (ip[w], ie[w])),
    )
    return pl.pallas_call(
        _moe_up_kernel,
        grid_spec=grid_spec,
        out_shape=jax.ShapeDtypeStruct((rows, EXPERTS_PER_GROUP * f), BF16),
        compiler_params=_params(1),
        name="moe_up",
    )(ip, ie, ige, n_items, xs, gs, w1, w3)


def _moe_down_kernel(ip_ref, ih_ref, ig_ref, n_ref, hid_ref, w2_ref, y_ref):
    @pl.when(pl.program_id(0) < n_ref[0])
    def _():
        y_ref[...] = jnp.dot(hid_ref[...], w2_ref[0].astype(BF16), preferred_element_type=F32)

    @pl.when(pl.program_id(0) >= n_ref[0])
    def _():
        y_ref[...] = jnp.zeros_like(y_ref)


def _moe_down(hid, w2g, items, *, tile, n_halves):
    rows, k = hid.shape
    d = w2g.shape[2]
    tn = d // n_halves
    ip, ih, ig, n_items = items
    grid_spec = pltpu.PrefetchScalarGridSpec(
        num_scalar_prefetch=4,
        grid=(ip.shape[0],),
        in_specs=[
            pl.BlockSpec((tile, k), lambda w, ip, ih, ig, n: (ip[w], 0)),
            pl.BlockSpec((1, k, tn), lambda w, ip, ih, ig, n: (ig[w], 0, ih[w])),
        ],
        out_specs=pl.BlockSpec((tile, tn), lambda w, ip, ih, ig, n: (ip[w], ih[w])),
    )
    return pl.pallas_call(
        _moe_down_kernel,
        grid_spec=grid_spec,
        out_shape=jax.ShapeDtypeStruct((rows, d), F32),
        compiler_params=_params(1),
        name="moe_down",
    )(ip, ih, ig, n_items, hid, w2g)


def _final_kernel(dest_ref, x_ref, ys_hbm, g_ref, o_ref, ybuf, sem, *, tile0):
    i = pl.program_id(0)
    tm = ybuf.shape[1]
    slot = i % 2

    def pairs(s):
        return [(ys_hbm, ybuf.at[s], sem.at[s])]

    @pl.when(i == 0)
    def _():
        _start_row_gather(dest_ref, tile0 * tm, tm, pairs(0))

    @pl.when(i + 1 < pl.num_programs(0))
    def _():
        _start_row_gather(dest_ref, (tile0 + i + 1) * tm, tm, pairs(1 - slot))

    _wait_row_gather(tm, pairs(slot))
    _rmsnorm_rows(lambda rows: x_ref[rows, :] + ybuf[slot, rows, :], g_ref, o_ref, tm)


def _final(x2, ys, dest, g, *, row0, n_rows, tm):
    d = x2.shape[1]
    off = row0 // tm
    grid_spec = pltpu.PrefetchScalarGridSpec(
        num_scalar_prefetch=1,
        grid=(n_rows // tm,),
        in_specs=[
            pl.BlockSpec((tm, d), lambda i, dest: (i + off, 0)),
            pl.BlockSpec(memory_space=pl.ANY),
            pl.BlockSpec((1, d), lambda i, dest: (0, 0)),
        ],
        out_specs=pl.BlockSpec((tm, d), lambda i, dest: (i, 0)),
        scratch_shapes=[pltpu.VMEM((2, tm, d), F32), pltpu.SemaphoreType.DMA((2,))],
    )
    return pl.pallas_call(
        functools.partial(_final_kernel, tile0=off),
        grid_spec=grid_spec,
        out_shape=jax.ShapeDtypeStruct((n_rows, d), F32),
        compiler_params=_params(1),
        name="final_norm",
    )(dest, x2, ys, g)


def _rope_tables(pos):
    half = ROT_DIM // 2
    inv = ROPE_THETA ** (-jnp.arange(half, dtype=F32) / half)
    ang = pos.astype(F32)[:, None] * inv[None, :]
    cos, sin = jnp.cos(ang), jnp.sin(ang)
    ones = jnp.ones((pos.shape[0], HEAD_DIM - ROT_DIM), F32)
    zeros = jnp.zeros((pos.shape[0], HEAD_DIM - half), F32)
    c = jnp.concatenate([cos, cos, ones], axis=1)
    s_hi = jnp.concatenate([-sin, zeros], axis=1)
    s_lo = jnp.concatenate([jnp.zeros_like(sin), sin, ones * 0.0], axis=1)
    rep = LANES // HEAD_DIM
    return jnp.tile(c, (1, rep)), jnp.tile(s_hi, (1, rep)), jnp.tile(s_lo, (1, rep))


def kernel(x_prompt, x_sample, cache_k, cache_v, state_conv, state_h, norm_mix, w_in, conv_w,
           conv_b, w_gate_a, b_gate_a, w_gate_x, b_gate_x, lru_lambda, sinks, norm_attn_out,
           norm_lru_out, w_out, norm_ffn, w_group, b_group, w_expert_router, b_expert_router,
           w1, w3, w2, norm_final):
    batch, seq, d_model = x_prompt.shape
    dec_batch, dec_seq, _ = x_sample.shape
    depth = w_in.shape[0]
    assert depth == 1 and dec_seq == CHUNK and seq % CHUNK == 0
    n_heads = sinks.shape[1]
    att_width = n_heads * HEAD_DIM
    n_kv = cache_k.shape[3]
    kv_cols = n_kv * HEAD_DIM
    lru_width = lru_lambda.shape[1]
    past_len = 1024
    cw = cache_k.shape[2]
    assert cw == WIN_CHUNKS * CHUNK
    n_p, n_s = batch * seq, dec_batch * dec_seq
    chunks_per_seq = seq // CHUNK
    n_prompt_chunks = n_p // CHUNK

    xp = x_prompt.reshape(n_p, d_model)
    xs = x_sample.reshape(n_s, d_model)

    w = w_in[0]
    q_end, k_end, v_end = att_width, att_width + kv_cols, att_width + 2 * kv_cols
    w_r = jnp.concatenate([w[:, :q_end], w[:, v_end:], w[:, q_end:v_end]], axis=1).astype(BF16)
    tn = 2 * kv_cols
    assert tn == 512 and att_width % tn == 0 and lru_width % tn == 0
    n_q_tiles = att_width // tn
    kv_tile = (att_width + 2 * lru_width) // tn

    pos = jnp.concatenate([jnp.tile(jnp.arange(seq), batch),
                           jnp.tile(past_len + jnp.arange(dec_seq), dec_batch)])
    c_tab, shi_tab, slo_tab = _rope_tables(pos)

    z = _in_proj(xp, xs, norm_mix, w_r, c_tab, shi_tab, slo_tab,
                 tm=512, tn=tn, n_q_tiles=n_q_tiles, kv_tile=kv_tile)

    k_col = att_width + 2 * lru_width
    v_col = k_col + kv_cols
    att = _attention(
        z, cache_k[0].reshape(dec_batch, cw, kv_cols), cache_v[0].reshape(dec_batch, cw, kv_cols),
        sinks[0], norm_attn_out,
        n_prompt_chunks=n_prompt_chunks, chunks_per_seq=chunks_per_seq, att_width=att_width,
        n_kv=n_kv, k_col_blk=k_col // kv_cols, v_col_blk=v_col // kv_cols)

    nblk = lru_width // LRU_BLOCK
    lru, h_tiles = _rglru(
        z, state_conv[0], state_h[0].reshape(dec_batch, 1, lru_width), conv_w[0], conv_b,
        w_gate_a[0].astype(BF16), b_gate_a[0].reshape(1, lru_width),
        w_gate_x[0].astype(BF16), b_gate_x[0].reshape(1, lru_width),
        lru_lambda, norm_lru_out,
        n_prompt_chunks=n_prompt_chunks, chunks_per_seq=chunks_per_seq, width=lru_width,
        xb_col_blk=att_width // lru_width, yb_col_blk=att_width // lru_width + 1)
    del nblk

    x2 = _out_proj(att, lru, w_out[0].astype(BF16), xp, xs, tm=1024, tn=512)

    n_routes = N_GROUPS + N_GROUPS * EXPERTS_PER_GROUP
    wr = jnp.concatenate([w_group[0], w_expert_router[0],
                          jnp.zeros((d_model, LANES - n_routes), F32)], axis=1).astype(BF16)
    br = jnp.concatenate([b_group[0], b_expert_router[0],
                          jnp.zeros((LANES - n_routes,), F32)]).reshape(1, LANES)
    route, counts = _router(x2, norm_ffn, wr, br, tm=MOE_TILE)

    n_tiles_max = (n_p + n_s) // MOE_TILE + N_GROUPS
    tabs = _routing_tables(route, counts, tile=MOE_TILE, n_tiles_max=n_tiles_max,
                           n_halves=MOE_DOWN_HALVES)
    xsort, gsort = _dispatch(x2, route, norm_ffn, tabs["src"], tabs["n_used"],
                             tile=MOE_TILE, n_tiles_max=n_tiles_max)
    hid = _moe_up(xsort, gsort, w1[0], w3[0], tabs["up"], tile=MOE_TILE)
    d_expert = w2.shape[2]
    w2g = w2[0].reshape(N_GROUPS, EXPERTS_PER_GROUP * d_expert, d_model)
    ysort = _moe_down(hid, w2g, tabs["down"], tile=MOE_TILE, n_halves=MOE_DOWN_HALVES)

    g_fin = norm_final.reshape(1, d_model)
    y_prompt = _final(x2, ysort, tabs["dest"], g_fin, row0=0, n_rows=n_p,
                      tm=MOE_TILE).reshape(batch, seq, d_model)
    y_sample = _final(x2, ysort, tabs["dest"], g_fin, row0=n_p, n_rows=n_s,
                      tm=MOE_TILE).reshape(dec_batch, dec_seq, d_model)

    win = min(WIN_CHUNKS * CHUNK, seq)

    def prompt_tail(col, width, rows):
        return jnp.stack([lax.slice(z, ((b + 1) * seq - rows, col), ((b + 1) * seq, col + width))
                          for b in range(batch)])

    def sample_rows(col, width):
        return lax.slice(z, (n_p, col), (n_p + n_s, col + width)).reshape(dec_batch, dec_seq, width)

    k_prompt = prompt_tail(k_col, kv_cols, win).reshape(1, batch, win, n_kv, HEAD_DIM)
    v_prompt = prompt_tail(v_col, kv_cols, win).reshape(1, batch, win, n_kv, HEAD_DIM)
    conv_prompt = prompt_tail(att_width, lru_width, CONV_WIDTH - 1)[None]
    h_prompt = h_tiles[:n_prompt_chunks, 0].reshape(batch, chunks_per_seq, lru_width)[:, -1][None]

    ks = sample_rows(k_col, kv_cols).reshape(dec_batch, dec_seq, n_kv, HEAD_DIM)
    vs = sample_rows(v_col, kv_cols).reshape(dec_batch, dec_seq, n_kv, HEAD_DIM)
    k_sample = jnp.concatenate([cache_k[0], ks], axis=1)[:, -cw:][None]
    v_sample = jnp.concatenate([cache_v[0], vs], axis=1)[:, -cw:][None]
    xs_rows = sample_rows(att_width, lru_width)[:, dec_seq - (CONV_WIDTH - 1):]
    conv_sample = jnp.concatenate([state_conv[0], xs_rows], axis=1)[:, -(CONV_WIDTH - 1):][None]
    h_sample = h_tiles[n_prompt_chunks:, 0][None]

    return (y_prompt, y_sample, k_prompt, v_prompt, conv_prompt, h_prompt,
            k_sample, v_sample, conv_sample, h_sample)
```

```python
import functools

import jax
import jax.numpy as jnp
from jax import lax
from jax.experimental import pallas as pl
from jax.experimental.pallas import tpu as pltpu

F32 = jnp.float32
BF16 = jnp.bfloat16

CHUNK = 64
HEAD_DIM = 64
KV_GROUP = 8
WIN_CHUNKS = 2
ROT_DIM = 16
ROPE_THETA = 500000.0
LRU_BLOCK = 128
CONV_WIDTH = 4
LRU_C = 8.0
N_GROUPS = 8
EXPERTS_PER_GROUP = 4
EPS = 1e-6
NEG = -1e30
LANES = 128
INPROJ_ROW_SPLITS = 4
MOE_TILE = 256
MOE_DOWN_HALVES = 2
VMEM_LIMIT = 56 * 1024 * 1024


def _params(n_axes, vmem=VMEM_LIMIT):
    return pltpu.CompilerParams(
        dimension_semantics=("arbitrary",) * n_axes, vmem_limit_bytes=vmem)


def _rms_scale(x):
    return lax.rsqrt(jnp.mean(x * x, axis=-1, keepdims=True) + EPS)


NORM_ROWS = 16


def _rmsnorm_rows(load_rows, g_ref, o_ref, n_rows, per_trip=None):
    def body(r, carry):
        row0 = r * NORM_ROWS if isinstance(r, int) else pl.multiple_of(r * NORM_ROWS, NORM_ROWS)
        rows = pl.ds(row0, NORM_ROWS)
        x = load_rows(rows)
        o_ref[rows, :] = (x * _rms_scale(x) * g_ref[...]).astype(o_ref.dtype)
        if per_trip is not None:
            per_trip(row0)
        return carry
    if per_trip is not None:
        for r in range(n_rows // NORM_ROWS):
            body(r, 0)
    else:
        lax.fori_loop(0, n_rows // NORM_ROWS, body, 0, unroll=4)


def _rope_block(zb, c, s_hi, s_lo):
    return (zb * c + pltpu.roll(zb, LANES - ROT_DIM // 2, 1) * s_hi
            + pltpu.roll(zb, ROT_DIM // 2, 1) * s_lo)


def _inproj_kernel(xp_ref, xs_ref, g_ref, w_ref, c_ref, shi_ref, slo_ref, o_ref, h_ref,
                   *, n_prompt_tiles, n_q_tiles, kv_tile):
    i = pl.program_id(0)
    j = pl.program_id(1)

    def norm(x_ref):
        _rmsnorm_rows(lambda rows: x_ref[rows, :], g_ref, h_ref, x_ref.shape[0])

    @pl.when(jnp.logical_and(j == 0, i < n_prompt_tiles))
    def _():
        norm(xp_ref)

    @pl.when(jnp.logical_and(j == 0, i >= n_prompt_tiles))
    def _():
        norm(xs_ref)

    tm, tn = o_ref.shape
    n_blk = tn // LANES
    half = tm // INPROJ_ROW_SPLITS
    for r0 in range(0, tm, half):
        z = jnp.dot(h_ref[r0:r0 + half, :], w_ref[...], preferred_element_type=F32)
        c = c_ref[r0:r0 + half, :]
        shi = shi_ref[r0:r0 + half, :]
        slo = slo_ref[r0:r0 + half, :]
        for b in range(n_blk):
            is_rope = j < n_q_tiles
            if b < n_blk // 2:
                is_rope = jnp.logical_or(is_rope, j == kv_tile)
            zb = z[:, b * LANES:(b + 1) * LANES]
            o_ref[r0:r0 + half, b * LANES:(b + 1) * LANES] = _rope_block(
                zb, jnp.where(is_rope, c, 1.0), jnp.where(is_rope, shi, 0.0),
                jnp.where(is_rope, slo, 0.0))


def _in_proj(xp, xs, g, w, c, shi, slo, *, tm, tn, n_q_tiles, kv_tile):
    n_p, d = xp.shape
    n_s = xs.shape[0]
    t = n_p + n_s
    npt = n_p // tm
    n_cols = w.shape[1]
    kern = functools.partial(_inproj_kernel, n_prompt_tiles=npt, n_q_tiles=n_q_tiles,
                             kv_tile=kv_tile)
    n_tiles = n_cols // tn
    assert kv_tile == n_q_tiles

    def out_tile(i, j):
        return i, jnp.where(j < kv_tile, j, jnp.where(j == kv_tile, n_tiles - 1, j - 1))

    return pl.pallas_call(
        kern,
        grid=(t // tm, n_tiles),
        in_specs=[
            pl.BlockSpec((tm, d), lambda i, j: (jnp.minimum(i, npt - 1), 0)),
            pl.BlockSpec((tm, d), lambda i, j: (jnp.maximum(i - npt, 0), 0),
                         pipeline_mode=pl.Buffered(1)),
            pl.BlockSpec((1, d), lambda i, j: (0, 0)),
            pl.BlockSpec((d, tn), lambda i, j: (0, j)),
            pl.BlockSpec((tm, LANES), lambda i, j: (i, 0)),
            pl.BlockSpec((tm, LANES), lambda i, j: (i, 0)),
            pl.BlockSpec((tm, LANES), lambda i, j: (i, 0)),
        ],
        out_specs=pl.BlockSpec((tm, tn), out_tile),
        out_shape=jax.ShapeDtypeStruct((t, n_cols), F32),
        scratch_shapes=[pltpu.VMEM((tm, d), BF16)],
        compiler_params=_params(2),
        name="in_proj",
    )(xp, xs, g, w, c, shi, slo)


def _attn_kernel(q_ref, k0_ref, k1_ref, k2_ref, v0_ref, v1_ref, v2_ref,
                 kc_ref, vc_ref, sink_ref, g_ref, o_ref, *, n_prompt_chunks, chunks_per_seq, n_kv):
    n = pl.program_id(0)
    is_s = n >= n_prompt_chunks
    c = n % chunks_per_seq
    lo = jnp.where(is_s, 0, jnp.where(c >= 2, 0, jnp.where(c == 1, CHUNK, 2 * CHUNK)))

    kc = kc_ref[0]
    vc = vc_ref[0]
    k_band = jnp.concatenate([
        jnp.where(is_s, kc[:CHUNK], k0_ref[...]),
        jnp.where(is_s, kc[CHUNK:], k1_ref[...]),
        k2_ref[...]], axis=0).astype(BF16)
    v_band = jnp.concatenate([
        jnp.where(is_s, vc[:CHUNK], v0_ref[...]),
        jnp.where(is_s, vc[CHUNK:], v1_ref[...]),
        v2_ref[...]], axis=0).astype(BF16)

    q = (q_ref[...] * (HEAD_DIM ** -0.5)).astype(BF16)
    n_keys = (WIN_CHUNKS + 1) * CHUNK
    key = lax.broadcasted_iota(jnp.int32, (n_keys, KV_GROUP * CHUNK), 0)
    valid = key >= lo

    outs = []
    for gi in range(n_kv):
        kg = k_band[:, gi * HEAD_DIM:(gi + 1) * HEAD_DIM]
        vg = v_band[:, gi * HEAD_DIM:(gi + 1) * HEAD_DIM]
        heads = [gi * KV_GROUP + h for h in range(KV_GROUP)]
        qg = jnp.concatenate([q[:, h * HEAD_DIM:(h + 1) * HEAD_DIM] for h in heads], axis=0)
        st = lax.dot_general(kg, qg, (((1,), (1,)), ((), ())), preferred_element_type=F32)
        st = jnp.where(valid, st, NEG)
        sink = sink_ref[gi:gi + 1, :]
        m = jnp.maximum(jnp.max(st, axis=0, keepdims=True), sink)
        p = jnp.exp(st - m)
        denom = jnp.sum(p, axis=0, keepdims=True) + jnp.exp(sink - m)
        pn = (p * (1.0 / denom)).astype(BF16)
        og = lax.dot_general(pn, vg, (((0,), (0,)), ((), ())), preferred_element_type=F32)
        outs.extend(og[h * CHUNK:(h + 1) * CHUNK] for h in range(KV_GROUP))
    att = jnp.concatenate(outs, axis=1)
    o_ref[...] = (att * _rms_scale(att) * g_ref[...]).astype(o_ref.dtype)


def _attention(z, cache_k, cache_v, sinks, g, *, n_prompt_chunks, chunks_per_seq,
               att_width, n_kv, k_col_blk, v_col_blk):
    t = z.shape[0]
    n_items = t // CHUNK
    kvw = n_kv * HEAD_DIM
    npc = n_prompt_chunks

    def hist(d):
        return lambda n: (n - jnp.minimum(d, n % chunks_per_seq), k_col_blk)

    def histv(d):
        return lambda n: (n - jnp.minimum(d, n % chunks_per_seq), v_col_blk)

    cache_map = lambda n: (jnp.maximum(n - npc, 0), 0, 0)
    kern = functools.partial(_attn_kernel, n_prompt_chunks=npc,
                             chunks_per_seq=chunks_per_seq, n_kv=n_kv)
    sink_rows = jnp.repeat(sinks.reshape(n_kv, KV_GROUP), CHUNK, axis=1)
    return pl.pallas_call(
        kern,
        grid=(n_items,),
        in_specs=[
            pl.BlockSpec((CHUNK, att_width), lambda n: (n, 0)),
            pl.BlockSpec((CHUNK, kvw), hist(2)),
            pl.BlockSpec((CHUNK, kvw), hist(1)),
            pl.BlockSpec((CHUNK, kvw), hist(0)),
            pl.BlockSpec((CHUNK, kvw), histv(2)),
            pl.BlockSpec((CHUNK, kvw), histv(1)),
            pl.BlockSpec((CHUNK, kvw), histv(0)),
            pl.BlockSpec((1, WIN_CHUNKS * CHUNK, kvw), cache_map),
            pl.BlockSpec((1, WIN_CHUNKS * CHUNK, kvw), cache_map),
            pl.BlockSpec((n_kv, KV_GROUP * CHUNK), lambda n: (0, 0)),
            pl.BlockSpec((1, att_width), lambda n: (0, 0)),
        ],
        out_specs=pl.BlockSpec((CHUNK, att_width), lambda n: (n, 0)),
        out_shape=jax.ShapeDtypeStruct((t, att_width), BF16),
        compiler_params=_params(1),
        name="attention",
    )(z, z, z, z, z, z, z, cache_k, cache_v, sink_rows, g)


SUBLANES = 8


def _lru_kernel(xb_ref, yb_ref, sconv_ref, sh_ref, cw_ref, cb_ref, wa_ref, ba_ref,
                wx_ref, bx_ref, lam_ref, g_ref, o_ref, hl_ref, hist_ref, hcar_ref,
                a_sc, b_sc, h_sc, *, n_prompt_chunks, chunks_per_seq):
    n = pl.program_id(0)
    is_s = n >= n_prompt_chunks
    hist_rows = hist_ref.shape[0]

    @pl.when(jnp.logical_and(jnp.logical_not(is_s), n % chunks_per_seq == 0))
    def _():
        hist_ref[...] = jnp.zeros_like(hist_ref)
        hcar_ref[...] = jnp.zeros_like(hcar_ref)

    @pl.when(is_s)
    def _():
        hist_ref[...] = jnp.zeros_like(hist_ref)
        hist_ref[hist_rows - (CONV_WIDTH - 1):, :] = sconv_ref[0]
        hcar_ref[...] = sh_ref[0]

    x = xb_ref[...]
    rows, width = x.shape
    hist = hist_ref[...]
    row8 = lax.broadcasted_iota(jnp.int32, (hist_rows, width), 0)
    xc = cb_ref[...]
    for j in range(CONV_WIDTH):
        s = CONV_WIDTH - 1 - j
        if s == 0:
            xs = x
        else:
            xr = pltpu.roll(x, s, 0)
            head = jnp.where(row8 < s, pltpu.roll(hist, s, 0), xr[:hist_rows])
            xs = jnp.concatenate([head, xr[hist_rows:]], axis=0)
        xc = xc + xs * cw_ref[j:j + 1, :]
    hist_ref[...] = x[rows - hist_rows:]

    xcb = xc.astype(BF16)
    ra, rx = [], []
    for nb in range(width // LRU_BLOCK):
        blk = xcb[:, nb * LRU_BLOCK:(nb + 1) * LRU_BLOCK]
        ra.append(jnp.dot(blk, wa_ref[nb], preferred_element_type=F32))
        rx.append(jnp.dot(blk, wx_ref[nb], preferred_element_type=F32))
    r = jax.nn.sigmoid(jnp.concatenate(ra, axis=1) + ba_ref[...])
    ig = jax.nn.sigmoid(jnp.concatenate(rx, axis=1) + bx_ref[...])
    log_a = (-LRU_C * r) * jax.nn.softplus(-lam_ref[...])
    a = jnp.exp(log_a)
    b = jnp.sqrt(-jnp.tanh(log_a) * (1.0 + a * a)) * ig * xc

    seg_len = rows // SUBLANES
    h_blocks, carries = [], []
    for nb in range(width // LANES):
        lanes = slice(nb * LANES, (nb + 1) * LANES)
        a_sc[nb] = a[:, lanes]
        b_sc[nb] = b[:, lanes]
        hloc, ploc = [], []
        for k in range(seg_len):
            a_k = a_sc[nb, pl.ds(k, SUBLANES, stride=seg_len), :]
            b_k = b_sc[nb, pl.ds(k, SUBLANES, stride=seg_len), :]
            hloc.append(b_k if k == 0 else a_k * hloc[-1] + b_k)
            ploc.append(a_k if k == 0 else a_k * ploc[-1])
        carry = hcar_ref[:, lanes]
        seg_in = []
        for s in range(SUBLANES):
            seg_in.append(carry)
            carry = ploc[-1][s:s + 1] * carry + hloc[-1][s:s + 1]
        seg_in = jnp.concatenate(seg_in, axis=0)
        carries.append(carry)
        for k in range(seg_len):
            h_sc[nb, pl.ds(k, SUBLANES, stride=seg_len), :] = hloc[k] + ploc[k] * seg_in
        h_blocks.append(h_sc[nb])
    h = jnp.concatenate(h_blocks, axis=1)
    h_last = jnp.concatenate(carries, axis=1)
    hcar_ref[...] = h_last
    hl_ref[0] = h_last

    y = jax.nn.gelu(yb_ref[...]) * h
    o_ref[...] = (y * _rms_scale(y) * g_ref[...]).astype(o_ref.dtype)


def _rglru(z, state_conv, state_h, conv_w, conv_b, wa, ba, wx, bx, lam, g,
           *, n_prompt_chunks, chunks_per_seq, width, xb_col_blk, yb_col_blk):
    t = z.shape[0]
    n_items = t // CHUNK
    npc = n_prompt_chunks
    nblk = width // LRU_BLOCK
    state_map = lambda n: (jnp.maximum(n - npc, 0), 0, 0)
    full2 = lambda n: (0, 0)
    full3 = lambda n: (0, 0, 0)
    kern = functools.partial(_lru_kernel, n_prompt_chunks=npc, chunks_per_seq=chunks_per_seq)
    return pl.pallas_call(
        kern,
        grid=(n_items,),
        in_specs=[
            pl.BlockSpec((CHUNK, width), lambda n: (n, xb_col_blk)),
            pl.BlockSpec((CHUNK, width), lambda n: (n, yb_col_blk)),
            pl.BlockSpec((1, CONV_WIDTH - 1, width), state_map),
            pl.BlockSpec((1, 1, width), state_map),
            pl.BlockSpec((CONV_WIDTH, width), full2),
            pl.BlockSpec((1, width), full2),
            pl.BlockSpec((nblk, LRU_BLOCK, LRU_BLOCK), full3),
            pl.BlockSpec((1, width), full2),
            pl.BlockSpec((nblk, LRU_BLOCK, LRU_BLOCK), full3),
            pl.BlockSpec((1, width), full2),
            pl.BlockSpec((1, width), full2),
            pl.BlockSpec((1, width), full2),
        ],
        out_specs=[
            pl.BlockSpec((CHUNK, width), lambda n: (n, 0)),
            pl.BlockSpec((1, 1, width), lambda n: (n, 0, 0)),
        ],
        out_shape=[
            jax.ShapeDtypeStruct((t, width), BF16),
            jax.ShapeDtypeStruct((n_items, 1, width), F32),
        ],
        scratch_shapes=[
            pltpu.VMEM((SUBLANES, width), F32),
            pltpu.VMEM((1, width), F32),
            pltpu.VMEM((width // LANES, CHUNK, LANES), F32),
            pltpu.VMEM((width // LANES, CHUNK, LANES), F32),
            pltpu.VMEM((width // LANES, CHUNK, LANES), F32),
        ],
        compiler_params=_params(1),
        name="rglru",
    )(z, z, state_conv, state_h, conv_w, conv_b, wa, ba, wx, bx, lam, g)


def _outproj_kernel(att_ref, lru_ref, wa_ref, wl_ref, xp_ref, xs_ref, o_ref, *, n_prompt_tiles):
    i = pl.program_id(0)
    acc = jnp.dot(att_ref[...], wa_ref[...], preferred_element_type=F32)
    acc = acc + jnp.dot(lru_ref[...], wl_ref[...], preferred_element_type=F32)

    @pl.when(i < n_prompt_tiles)
    def _():
        o_ref[...] = xp_ref[...] + acc

    @pl.when(i >= n_prompt_tiles)
    def _():
        o_ref[...] = xs_ref[...] + acc


def _out_proj(att, lru, w, xp, xs, *, tm, tn):
    t, aw = att.shape
    lw = lru.shape[1]
    d = w.shape[1]
    assert aw == lw and w.shape[0] == aw + lw
    npt = xp.shape[0] // tm
    kern = functools.partial(_outproj_kernel, n_prompt_tiles=npt)
    return pl.pallas_call(
        kern,
        grid=(t // tm, d // tn),
        in_specs=[
            pl.BlockSpec((tm, aw), lambda i, j: (i, 0)),
            pl.BlockSpec((tm, lw), lambda i, j: (i, 0)),
            pl.BlockSpec((aw, tn), lambda i, j: (0, j)),
            pl.BlockSpec((lw, tn), lambda i, j: (1, j)),
            pl.BlockSpec((tm, tn), lambda i, j: (jnp.minimum(i, npt - 1), j)),
            pl.BlockSpec((tm, tn), lambda i, j: (jnp.maximum(i - npt, 0), j)),
        ],
        out_specs=pl.BlockSpec((tm, tn), lambda i, j: (i, j)),
        out_shape=jax.ShapeDtypeStruct((t, d), F32),
        compiler_params=_params(2),
        name="out_proj",
    )(att, lru, w, w, xp, xs)


ROUTE_GROUP_LANE = EXPERTS_PER_GROUP
ROUTE_RANK_LANE = EXPERTS_PER_GROUP + 1


def _router_kernel(x_ref, g_ref, wr_ref, br_ref, route_ref, cnt_ref, h_ref):
    @pl.when(pl.program_id(0) == 0)
    def _():
        cnt_ref[...] = jnp.zeros_like(cnt_ref)

    _rmsnorm_rows(lambda rows: x_ref[rows, :], g_ref, h_ref, x_ref.shape[0])
    logits = jnp.dot(h_ref[...], wr_ref[...], preferred_element_type=F32) + br_ref[...]
    lane = lax.broadcasted_iota(jnp.int32, logits.shape, 1).astype(F32)
    ninf = -jnp.inf

    def first_argmax(v, vmax):
        return jnp.min(jnp.where(v == vmax, lane, float(LANES)), axis=-1, keepdims=True)

    gl = jnp.where(lane < N_GROUPS, logits, ninf)
    gm = jnp.max(gl, axis=-1, keepdims=True)
    g_idx = first_argmax(gl, gm)
    g_w = 1.0 / jnp.sum(jnp.exp(gl - gm), axis=-1, keepdims=True)

    lo = N_GROUPS + EXPERTS_PER_GROUP * g_idx
    el = jnp.where(jnp.logical_and(lane >= lo, lane < lo + EXPERTS_PER_GROUP), logits, ninf)
    v1 = jnp.max(el, axis=-1, keepdims=True)
    i1 = first_argmax(el, v1)
    el2 = jnp.where(lane == i1, ninf, el)
    v2 = jnp.max(el2, axis=-1, keepdims=True)
    i2 = first_argmax(el2, v2)
    e2 = jnp.exp(v2 - v1)
    w1 = (1.0 / (1.0 + e2)) * g_w
    w2 = (e2 / (1.0 + e2)) * g_w
    gates = jnp.where(lane == i1 - lo, w1, 0.0) + jnp.where(lane == i2 - lo, w2, 0.0)

    tm = x_ref.shape[0]
    onehot = jnp.where(lane == g_idx, 1.0, 0.0)
    r_i = lax.broadcasted_iota(jnp.int32, (tm, tm), 0)
    c_i = lax.broadcasted_iota(jnp.int32, (tm, tm), 1)
    tri = jnp.where(c_i < r_i, 1.0, 0.0).astype(BF16)
    before = jnp.dot(tri, onehot.astype(BF16), preferred_element_type=F32) + cnt_ref[...]
    rank = jnp.sum(onehot * before, axis=-1, keepdims=True)
    cnt_ref[...] += jnp.sum(onehot, axis=0, keepdims=True)
    route_ref[...] = (gates + jnp.where(lane == ROUTE_GROUP_LANE, g_idx, 0.0)
                      + jnp.where(lane == ROUTE_RANK_LANE, rank, 0.0))


def _router(x2, g, wr, br, *, tm):
    t, d = x2.shape
    return pl.pallas_call(
        _router_kernel,
        grid=(t // tm,),
        in_specs=[
            pl.BlockSpec((tm, d), lambda i: (i, 0)),
            pl.BlockSpec((1, d), lambda i: (0, 0)),
            pl.BlockSpec((d, LANES), lambda i: (0, 0)),
            pl.BlockSpec((1, LANES), lambda i: (0, 0)),
        ],
        out_specs=[
            pl.BlockSpec((tm, LANES), lambda i: (i, 0)),
            pl.BlockSpec((1, LANES), lambda i: (0, 0)),
        ],
        out_shape=[
            jax.ShapeDtypeStruct((t, LANES), F32),
            jax.ShapeDtypeStruct((1, LANES), F32),
        ],
        scratch_shapes=[pltpu.VMEM((tm, d), BF16)],
        compiler_params=_params(1),
        name="router",
    )(x2, g, wr, br)


def _routing_tables(route, counts, *, tile, n_tiles_max, n_halves):
    t = route.shape[0]
    i32 = jnp.int32
    g = route[:, ROUTE_GROUP_LANE].astype(i32)
    rank = route[:, ROUTE_RANK_LANE].astype(i32)
    cnt = counts[0, :N_GROUPS].astype(i32)
    tiles_g = (cnt + tile - 1) // tile
    gids = jnp.arange(N_GROUPS, dtype=i32)
    tend = jnp.sum(jnp.where(gids[None, :] <= gids[:, None], tiles_g[None, :], 0), axis=1)
    tstart = tend - tiles_g
    n_used = tend[-1]
    dest = tstart[g] * tile + rank
    src = jnp.zeros((n_tiles_max * tile,), i32).at[dest].set(jnp.arange(t, dtype=i32))

    def items(per_tile):
        n_items = per_tile * n_used
        w_all = jnp.arange(per_tile * n_tiles_max, dtype=i32)
        w = jnp.minimum(w_all, n_items - 1)
        gi = jnp.sum((w[:, None] >= per_tile * tend[None, :]).astype(i32), axis=1)
        local = w - per_tile * tstart[gi]
        spare = w_all - n_items
        part = jnp.where(spare < 0, local // tiles_g[gi], spare % per_tile)
        p = jnp.where(spare < 0, tstart[gi] + local % tiles_g[gi], n_used + spare // per_tile)
        valid = spare < 0
        wkey = gi * per_tile + part
        first = valid & ((w_all == 0) | (wkey != jnp.roll(wkey, 1)))
        before = w_all[None, :] <= w_all[:, None]
        run_no = jnp.sum(jnp.where(before & first[None, :], 1, 0), axis=1) - 1
        later_start = (w_all[None, :] > w_all[:, None]) & first[None, :]
        n_all = w_all.shape[0]
        nxt = jnp.min(jnp.where(later_start, w_all[None, :], n_all), axis=1)
        nxt_key = jnp.where(nxt < n_all, wkey[jnp.minimum(nxt, n_all - 1)], -1)
        run = jnp.stack([first.astype(i32), run_no % 2, nxt_key], axis=1).reshape(-1)
        return p, part, wkey, n_items.reshape(1), run

    up_p, up_e, up_key, up_n, up_run = items(EXPERTS_PER_GROUP)
    dn_p, dn_h, dn_key, dn_n, dn_run = items(n_halves)
    return dict(dest=dest, src=src, n_used=n_used.reshape(1),
                up=(up_p, up_e, up_key, up_n, up_run),
                down=(dn_p, dn_h, dn_key, dn_n, dn_run))


def _row_copy(src_hbm, dst_vmem, src_row, dst_row, sem):
    return pltpu.make_async_copy(src_hbm.at[pl.ds(src_row, 1)], dst_vmem.at[pl.ds(dst_row, 1)], sem)


def _start_row_gather(idx_ref, base, n_rows, pairs):
    def body(r, carry):
        i = idx_ref[base + r]
        for src_hbm, dst_vmem, sem in pairs:
            _row_copy(src_hbm, dst_vmem, i, r, sem).start()
        return carry
    lax.fori_loop(0, n_rows, body, 0, unroll=8)


def _start_rows(idx_ref, idx_base, row0, n_rows, pairs):
    for k in range(n_rows):
        i = idx_ref[idx_base + row0 + k]
        for src_hbm, dst_vmem, sem in pairs:
            _row_copy(src_hbm, dst_vmem, i, row0 + k, sem).start()


def _norm_tile_and_prefetch(load_rows, g_ref, o_ref, n_rows, has_next, idx_ref, next_base,
                            next_pairs):
    @pl.when(has_next)
    def _():
        _rmsnorm_rows(load_rows, g_ref, o_ref, n_rows,
                      per_trip=lambda row0: _start_rows(idx_ref, next_base, row0, NORM_ROWS,
                                                        next_pairs))

    @pl.when(jnp.logical_not(has_next))
    def _():
        _rmsnorm_rows(load_rows, g_ref, o_ref, n_rows)


def _wait_row_gather(n_rows, pairs):
    for src_hbm, dst_vmem, sem in pairs:
        pltpu.make_async_copy(src_hbm.at[pl.ds(0, n_rows)], dst_vmem, sem).wait()


def _dispatch_kernel(src_ref, nused_ref, x2_hbm, route_hbm, g_ref, xs_ref, gs_ref,
                     xbuf, gbuf, sem):
    p = pl.program_id(0)
    n_used = nused_ref[0]
    tile = xbuf.shape[1]
    slot = p % 2

    def pairs(s):
        return [(x2_hbm, xbuf.at[s], sem.at[0, s]), (route_hbm, gbuf.at[s], sem.at[1, s])]

    @pl.when(p == 0)
    def _():
        _start_row_gather(src_ref, 0, tile, pairs(0))

    @pl.when(p < n_used)
    def _():
        _wait_row_gather(tile, pairs(slot))
        gs_ref[...] = gbuf[slot]
        _norm_tile_and_prefetch(lambda rows: xbuf[slot, rows, :], g_ref, xs_ref, tile,
                                p + 1 < n_used, src_ref, (p + 1) * tile, pairs(1 - slot))

    @pl.when(p >= n_used)
    def _():
        xs_ref[...] = jnp.zeros_like(xs_ref)
        gs_ref[...] = jnp.zeros_like(gs_ref)


def _dispatch(x2, route, g, src, n_used, *, tile, n_tiles_max):
    t, d = x2.shape
    tile_map = lambda p, src_ref, n_ref: (p, 0)
    grid_spec = pltpu.PrefetchScalarGridSpec(
        num_scalar_prefetch=2,
        grid=(n_tiles_max,),
        in_specs=[
            pl.BlockSpec(memory_space=pl.ANY),
            pl.BlockSpec(memory_space=pl.ANY),
            pl.BlockSpec((1, d), lambda p, s, n: (0, 0)),
        ],
        out_specs=[
            pl.BlockSpec((tile, d), tile_map),
            pl.BlockSpec((tile, LANES), tile_map),
        ],
        scratch_shapes=[
            pltpu.VMEM((2, tile, d), F32),
            pltpu.VMEM((2, tile, LANES), F32),
            pltpu.SemaphoreType.DMA((2, 2)),
        ],
    )
    return pl.pallas_call(
        _dispatch_kernel,
        grid_spec=grid_spec,
        out_shape=[
            jax.ShapeDtypeStruct((n_tiles_max * tile, d), BF16),
            jax.ShapeDtypeStruct((n_tiles_max * tile, LANES), F32),
        ],
        compiler_params=_params(1),
        name="moe_dispatch",
    )(src, n_used, x2, route, g)


def _stream_run_weights(w, n_items, key_ref, run_ref, copies):
    is_first = run_ref[3 * w]
    slot = run_ref[3 * w + 1]
    nxt_key = run_ref[3 * w + 2]
    valid = w < n_items

    @pl.when(jnp.logical_and(valid, w == 0))
    def _():
        for cp in copies(key_ref[0], 0):
            cp.start()

    @pl.when(jnp.logical_and(valid, is_first == 1))
    def _():
        for cp in copies(key_ref[w], slot):
            cp.wait()

        @pl.when(nxt_key >= 0)
        def _():
            for cp in copies(nxt_key, 1 - slot):
                cp.start()

    return slot


def _moe_up_kernel(ip_ref, ie_ref, key_ref, n_ref, run_ref, xs_ref, gs_ref, w1_hbm, w3_hbm,
                   hid_ref, w1_buf, w3_buf, sem):
    w = pl.program_id(0)

    def copies(key, slot):
        return [pltpu.make_async_copy(w1_hbm.at[key], w1_buf.at[slot], sem.at[0, slot]),
                pltpu.make_async_copy(w3_hbm.at[key], w3_buf.at[slot], sem.at[1, slot])]

    slot = _stream_run_weights(w, n_ref[0], key_ref, run_ref, copies)

    @pl.when(w < n_ref[0])
    def _():
        x = xs_ref[...]
        a = jnp.dot(x, w1_buf[slot].astype(BF16), preferred_element_type=F32)
        b = jnp.dot(x, w3_buf[slot].astype(BF16), preferred_element_type=F32)
        lane = lax.broadcasted_iota(jnp.int32, gs_ref.shape, 1)
        gate = jnp.sum(jnp.where(lane == ie_ref[w], gs_ref[...], 0.0), axis=-1, keepdims=True)
        hid_ref[...] = (jax.nn.silu(a) * b * gate).astype(hid_ref.dtype)

    @pl.when(w >= n_ref[0])
    def _():
        hid_ref[...] = jnp.zeros_like(hid_ref)


def _moe_up(xs, gs, w1, w3, items, *, tile):
    rows, d = xs.shape
    f = w1.shape[2]
    ip, ie, key, n_items, run = items
    grid_spec = pltpu.PrefetchScalarGridSpec(
        num_scalar_prefetch=5,
        grid=(ip.shape[0],),
        in_specs=[
            pl.BlockSpec((tile, d), lambda w, ip, ie, key, n, run: (ip[w], 0)),
            pl.BlockSpec((tile, LANES), lambda w, ip, ie, key, n, run: (ip[w], 0)),
            pl.BlockSpec(memory_space=pl.ANY),
            pl.BlockSpec(memory_space=pl.ANY),
        ],
        out_specs=pl.BlockSpec((tile, f), lambda w, ip, ie, key, n, run: (ip[w], ie[w])),
        scratch_shapes=[
            pltpu.VMEM((2, d, f), w1.dtype),
            pltpu.VMEM((2, d, f), w3.dtype),
            pltpu.SemaphoreType.DMA((2, 2)),
        ],
    )
    return pl.pallas_call(
        _moe_up_kernel,
        grid_spec=grid_spec,
        out_shape=jax.ShapeDtypeStruct((rows, EXPERTS_PER_GROUP * f), BF16),
        compiler_params=_params(1),
        name="moe_up",
    )(ip, ie, key, n_items, run, xs, gs, w1, w3)


def _moe_down_kernel(ip_ref, ih_ref, key_ref, n_ref, run_ref, hid_ref, w2_hbm, y_ref,
                     w2_buf, sem, *, n_halves):
    w = pl.program_id(0)
    tn = w2_buf.shape[2]

    def copies(key, slot):
        g = key // n_halves
        col = pl.multiple_of((key % n_halves) * tn, tn)
        return [pltpu.make_async_copy(w2_hbm.at[g, :, pl.ds(col, tn)], w2_buf.at[slot],
                                      sem.at[slot])]

    slot = _stream_run_weights(w, n_ref[0], key_ref, run_ref, copies)

    @pl.when(w < n_ref[0])
    def _():
        y_ref[...] = jnp.dot(hid_ref[...], w2_buf[slot].astype(BF16),
                             preferred_element_type=F32)

    @pl.when(w >= n_ref[0])
    def _():
        y_ref[...] = jnp.zeros_like(y_ref)


def _moe_down(hid, w2g, items, *, tile, n_halves):
    rows, k = hid.shape
    d = w2g.shape[2]
    tn = d // n_halves
    ip, ih, key, n_items, run = items
    grid_spec = pltpu.PrefetchScalarGridSpec(
        num_scalar_prefetch=5,
        grid=(ip.shape[0],),
        in_specs=[
            pl.BlockSpec((tile, k), lambda w, ip, ih, key, n, run: (ip[w], 0)),
            pl.BlockSpec(memory_space=pl.ANY),
        ],
        out_specs=pl.BlockSpec((tile, tn), lambda w, ip, ih, key, n, run: (ip[w], ih[w])),
        scratch_shapes=[
            pltpu.VMEM((2, k, tn), w2g.dtype),
            pltpu.SemaphoreType.DMA((2,)),
        ],
    )
    return pl.pallas_call(
        functools.partial(_moe_down_kernel, n_halves=n_halves),
        grid_spec=grid_spec,
        out_shape=jax.ShapeDtypeStruct((rows, d), F32),
        compiler_params=_params(1),
        name="moe_down",
    )(ip, ih, key, n_items, run, hid, w2g)


def _final_kernel(dest_ref, x_ref, ys_hbm, g_ref, o_ref, ybuf, sem, *, tile0):
    i = pl.program_id(0)
    tm = ybuf.shape[1]
    slot = i % 2

    def pairs(s):
        return [(ys_hbm, ybuf.at[s], sem.at[s])]

    @pl.when(i == 0)
    def _():
        _start_row_gather(dest_ref, tile0 * tm, tm, pairs(0))

    _wait_row_gather(tm, pairs(slot))
    _norm_tile_and_prefetch(lambda rows: x_ref[rows, :] + ybuf[slot, rows, :], g_ref, o_ref, tm,
                            i + 1 < pl.num_programs(0), dest_ref, (tile0 + i + 1) * tm,
                            pairs(1 - slot))


def _final(x2, ys, dest, g, *, row0, n_rows, tm):
    d = x2.shape[1]
    off = row0 // tm
    grid_spec = pltpu.PrefetchScalarGridSpec(
        num_scalar_prefetch=1,
        grid=(n_rows // tm,),
        in_specs=[
            pl.BlockSpec((tm, d), lambda i, dest: (i + off, 0)),
            pl.BlockSpec(memory_space=pl.ANY),
            pl.BlockSpec((1, d), lambda i, dest: (0, 0)),
        ],
        out_specs=pl.BlockSpec((tm, d), lambda i, dest: (i, 0)),
        scratch_shapes=[pltpu.VMEM((2, tm, d), F32), pltpu.SemaphoreType.DMA((2,))],
    )
    return pl.pallas_call(
        functools.partial(_final_kernel, tile0=off),
        grid_spec=grid_spec,
        out_shape=jax.ShapeDtypeStruct((n_rows, d), F32),
        compiler_params=_params(1),
        name="final_norm",
    )(dest, x2, ys, g)


def _rope_tables(pos):
    half = ROT_DIM // 2
    inv = ROPE_THETA ** (-jnp.arange(half, dtype=F32) / half)
    ang = pos.astype(F32)[:, None] * inv[None, :]
    cos, sin = jnp.cos(ang), jnp.sin(ang)
    ones = jnp.ones((pos.shape[0], HEAD_DIM - ROT_DIM), F32)
    zeros = jnp.zeros((pos.shape[0], HEAD_DIM - half), F32)
    c = jnp.concatenate([cos, cos, ones], axis=1)
    s_hi = jnp.concatenate([-sin, zeros], axis=1)
    s_lo = jnp.concatenate([jnp.zeros_like(sin), sin, ones * 0.0], axis=1)
    rep = LANES // HEAD_DIM
    return jnp.tile(c, (1, rep)), jnp.tile(s_hi, (1, rep)), jnp.tile(s_lo, (1, rep))


def kernel(x_prompt, x_sample, cache_k, cache_v, state_conv, state_h, norm_mix, w_in, conv_w,
           conv_b, w_gate_a, b_gate_a, w_gate_x, b_gate_x, lru_lambda, sinks, norm_attn_out,
           norm_lru_out, w_out, norm_ffn, w_group, b_group, w_expert_router, b_expert_router,
           w1, w3, w2, norm_final):
    batch, seq, d_model = x_prompt.shape
    dec_batch, dec_seq, _ = x_sample.shape
    depth = w_in.shape[0]
    assert depth == 1 and dec_seq == CHUNK and seq % CHUNK == 0
    n_heads = sinks.shape[1]
    att_width = n_heads * HEAD_DIM
    n_kv = cache_k.shape[3]
    kv_cols = n_kv * HEAD_DIM
    lru_width = lru_lambda.shape[1]
    past_len = 1024
    cw = cache_k.shape[2]
    assert cw == WIN_CHUNKS * CHUNK
    n_p, n_s = batch * seq, dec_batch * dec_seq
    chunks_per_seq = seq // CHUNK
    n_prompt_chunks = n_p // CHUNK

    xp = x_prompt.reshape(n_p, d_model)
    xs = x_sample.reshape(n_s, d_model)

    w_r = w_in[0].astype(BF16)
    tn = 2 * kv_cols
    assert tn == 512 and att_width % tn == 0 and lru_width % tn == 0
    n_q_tiles = att_width // tn
    kv_tile = n_q_tiles

    pos = jnp.concatenate([jnp.tile(jnp.arange(seq), batch),
                           jnp.tile(past_len + jnp.arange(dec_seq), dec_batch)])
    c_tab, shi_tab, slo_tab = _rope_tables(pos)

    z = _in_proj(xp, xs, norm_mix, w_r, c_tab, shi_tab, slo_tab,
                 tm=512, tn=tn, n_q_tiles=n_q_tiles, kv_tile=kv_tile)

    k_col = att_width + 2 * lru_width
    v_col = k_col + kv_cols
    att = _attention(
        z, cache_k[0].reshape(dec_batch, cw, kv_cols), cache_v[0].reshape(dec_batch, cw, kv_cols),
        sinks[0], norm_attn_out,
        n_prompt_chunks=n_prompt_chunks, chunks_per_seq=chunks_per_seq, att_width=att_width,
        n_kv=n_kv, k_col_blk=k_col // kv_cols, v_col_blk=v_col // kv_cols)

    nblk = lru_width // LRU_BLOCK
    lru, h_tiles = _rglru(
        z, state_conv[0], state_h[0].reshape(dec_batch, 1, lru_width), conv_w[0], conv_b,
        w_gate_a[0].astype(BF16), b_gate_a[0].reshape(1, lru_width),
        w_gate_x[0].astype(BF16), b_gate_x[0].reshape(1, lru_width),
        lru_lambda, norm_lru_out,
        n_prompt_chunks=n_prompt_chunks, chunks_per_seq=chunks_per_seq, width=lru_width,
        xb_col_blk=att_width // lru_width, yb_col_blk=att_width // lru_width + 1)
    del nblk

    x2 = _out_proj(att, lru, w_out[0].astype(BF16), xp, xs, tm=1024, tn=512)

    n_routes = N_GROUPS + N_GROUPS * EXPERTS_PER_GROUP
    wr = jnp.concatenate([w_group[0], w_expert_router[0],
                          jnp.zeros((d_model, LANES - n_routes), F32)], axis=1).astype(BF16)
    br = jnp.concatenate([b_group[0], b_expert_router[0],
                          jnp.zeros((LANES - n_routes,), F32)]).reshape(1, LANES)
    route, counts = _router(x2, norm_ffn, wr, br, tm=MOE_TILE)

    n_tiles_max = (n_p + n_s) // MOE_TILE + N_GROUPS
    tabs = _routing_tables(route, counts, tile=MOE_TILE, n_tiles_max=n_tiles_max,
                           n_halves=MOE_DOWN_HALVES)
    xsort, gsort = _dispatch(x2, route, norm_ffn, tabs["src"], tabs["n_used"],
                             tile=MOE_TILE, n_tiles_max=n_tiles_max)
    hid = _moe_up(xsort, gsort, w1[0], w3[0], tabs["up"], tile=MOE_TILE)
    d_expert = w2.shape[2]
    w2g = w2[0].reshape(N_GROUPS, EXPERTS_PER_GROUP * d_expert, d_model)
    ysort = _moe_down(hid, w2g, tabs["down"], tile=MOE_TILE, n_halves=MOE_DOWN_HALVES)

    g_fin = norm_final.reshape(1, d_model)
    y_prompt = _final(x2, ysort, tabs["dest"], g_fin, row0=0, n_rows=n_p,
                      tm=MOE_TILE).reshape(batch, seq, d_model)
    y_sample = _final(x2, ysort, tabs["dest"], g_fin, row0=n_p, n_rows=n_s,
                      tm=MOE_TILE).reshape(dec_batch, dec_seq, d_model)

    win = min(WIN_CHUNKS * CHUNK, seq)

    def prompt_tail(col, width, rows):
        return jnp.stack([lax.slice(z, ((b + 1) * seq - rows, col), ((b + 1) * seq, col + width))
                          for b in range(batch)])

    def sample_rows(col, width):
        return lax.slice(z, (n_p, col), (n_p + n_s, col + width)).reshape(dec_batch, dec_seq, width)

    k_prompt = prompt_tail(k_col, kv_cols, win).reshape(1, batch, win, n_kv, HEAD_DIM)
    v_prompt = prompt_tail(v_col, kv_cols, win).reshape(1, batch, win, n_kv, HEAD_DIM)
    conv_prompt = prompt_tail(att_width, lru_width, CONV_WIDTH - 1)[None]
    h_prompt = h_tiles[:n_prompt_chunks, 0].reshape(batch, chunks_per_seq, lru_width)[:, -1][None]

    ks = sample_rows(k_col, kv_cols).reshape(dec_batch, dec_seq, n_kv, HEAD_DIM)
    vs = sample_rows(v_col, kv_cols).reshape(dec_batch, dec_seq, n_kv, HEAD_DIM)
    k_sample = jnp.concatenate([cache_k[0], ks], axis=1)[:, -cw:][None]
    v_sample = jnp.concatenate([cache_v[0], vs], axis=1)[:, -cw:][None]
    xs_rows = sample_rows(att_width, lru_width)[:, dec_seq - (CONV_WIDTH - 1):]
    conv_sample = jnp.concatenate([state_conv[0], xs_rows], axis=1)[:, -(CONV_WIDTH - 1):][None]
    h_sample = h_tiles[n_prompt_chunks:, 0][None]

    return (y_prompt, y_sample, k_prompt, v_prompt, conv_prompt, h_prompt,
            k_sample, v_sample, conv_sample, h_sample)
```

```python
import functools

import numpy as np
import jax
import jax.numpy as jnp
from jax import lax
from jax.experimental import pallas as pl
from jax.experimental.pallas import tpu as pltpu

F32 = jnp.float32
BF16 = jnp.bfloat16

CHUNK = 64
HEAD_DIM = 64
KV_GROUP = 8
WIN_CHUNKS = 2
ROT_DIM = 16
ROPE_THETA = 500000.0
LRU_BLOCK = 128
CONV_WIDTH = 4
LRU_C = 8.0
N_GROUPS = 8
EXPERTS_PER_GROUP = 4
EPS = 1e-6
NEG = -1e30
LANES = 128
INPROJ_ROW_BLOCK = 128
MOE_TILE = 256
MOE_DOWN_HALVES = 1
VMEM_LIMIT = 56 * 1024 * 1024


def _params(n_axes, vmem=VMEM_LIMIT):
    return pltpu.CompilerParams(
        dimension_semantics=("arbitrary",) * n_axes, vmem_limit_bytes=vmem)


def _rms_scale(x):
    return lax.rsqrt(jnp.mean(x * x, axis=-1, keepdims=True) + EPS)


NORM_ROWS = 16


def _rmsnorm_rows(load_rows, g_ref, o_ref, n_rows, per_trip=None):
    def body(r, carry):
        row0 = r * NORM_ROWS if isinstance(r, int) else pl.multiple_of(r * NORM_ROWS, NORM_ROWS)
        rows = pl.ds(row0, NORM_ROWS)
        x = load_rows(rows)
        o_ref[rows, :] = (x * _rms_scale(x) * g_ref[...]).astype(o_ref.dtype)
        if per_trip is not None:
            per_trip(row0)
        return carry
    if per_trip is not None:
        for r in range(n_rows // NORM_ROWS):
            body(r, 0)
    else:
        lax.fori_loop(0, n_rows // NORM_ROWS, body, 0, unroll=4)


def _rope_block(zb, c, s_hi, s_lo):
    return (zb * c + pltpu.roll(zb, LANES - ROT_DIM // 2, 1) * s_hi
            + pltpu.roll(zb, ROT_DIM // 2, 1) * s_lo)


def _inproj_kernel(xp_hbm, xs_hbm, g_ref, w_ref, c_ref, shi_ref, slo_ref, o_ref, x_buf, h_ref,
                   sem, *, n_prompt_tiles, n_q_tiles, kv_tile):
    i = pl.program_id(0)
    j = pl.program_id(1)
    tm, tn = o_ref.shape

    def x_tile_copy(src_hbm, tile):
        return pltpu.make_async_copy(src_hbm.at[pl.ds(tile * tm, tm)], x_buf, sem)

    def start_x(tile):
        @pl.when(tile < n_prompt_tiles)
        def _():
            x_tile_copy(xp_hbm, tile).start()

        @pl.when(tile >= n_prompt_tiles)
        def _():
            x_tile_copy(xs_hbm, tile - n_prompt_tiles).start()

    @pl.when(jnp.logical_and(i == 0, j == 0))
    def _():
        start_x(0)

    @pl.when(j == 0)
    def _():
        x_tile_copy(xp_hbm, 0).wait()
        _rmsnorm_rows(lambda rows: x_buf[rows, :], g_ref, h_ref, tm)

    @pl.when(jnp.logical_and(j == 1, i + 1 < pl.num_programs(0)))
    def _():
        start_x(i + 1)

    w = w_ref[...].astype(BF16)
    n_blk = tn // LANES
    for r0 in range(0, tm, INPROJ_ROW_BLOCK):
        rows = slice(r0, r0 + INPROJ_ROW_BLOCK)
        z = jnp.dot(h_ref[rows, :], w, preferred_element_type=F32)
        c, shi, slo = c_ref[rows, :], shi_ref[rows, :], slo_ref[rows, :]
        for b in range(n_blk):
            is_rope = j < n_q_tiles
            if b < n_blk // 2:
                is_rope = jnp.logical_or(is_rope, j == kv_tile)
            zb = z[:, b * LANES:(b + 1) * LANES]
            o_ref[rows, b * LANES:(b + 1) * LANES] = _rope_block(
                zb, jnp.where(is_rope, c, 1.0), jnp.where(is_rope, shi, 0.0),
                jnp.where(is_rope, slo, 0.0))


def _in_proj(xp, xs, g, w, c, shi, slo, *, tm, tn, n_q_tiles, kv_tile, prompt_tiles_per_seq):
    n_p, d = xp.shape
    n_s = xs.shape[0]
    t = n_p + n_s
    npt = n_p // tm
    n_cols = w.shape[1]
    kern = functools.partial(_inproj_kernel, n_prompt_tiles=npt, n_q_tiles=n_q_tiles,
                             kv_tile=kv_tile)
    n_tiles = n_cols // tn
    assert kv_tile == n_q_tiles

    def table_tile(i, j):
        return jnp.where(i < npt, i % prompt_tiles_per_seq, prompt_tiles_per_seq), 0

    def out_tile(i, j):
        return i, jnp.where(j < kv_tile, j, jnp.where(j == kv_tile, n_tiles - 1, j - 1))

    return pl.pallas_call(
        kern,
        grid=(t // tm, n_tiles),
        in_specs=[
            pl.BlockSpec(memory_space=pl.ANY),
            pl.BlockSpec(memory_space=pl.ANY),
            pl.BlockSpec((1, d), lambda i, j: (0, 0)),
            pl.BlockSpec((d, tn), lambda i, j: (0, j)),
            pl.BlockSpec((tm, LANES), table_tile),
            pl.BlockSpec((tm, LANES), table_tile),
            pl.BlockSpec((tm, LANES), table_tile),
        ],
        out_specs=pl.BlockSpec((tm, tn), out_tile),
        out_shape=jax.ShapeDtypeStruct((t, n_cols), F32),
        scratch_shapes=[pltpu.VMEM((tm, d), xp.dtype), pltpu.VMEM((tm, d), BF16),
                        pltpu.SemaphoreType.DMA(())],
        compiler_params=_params(2),
        name="in_proj",
    )(xp, xs, g, w, c, shi, slo)


def _attn_kernel(q_ref, k0_ref, k1_ref, k2_ref, v0_ref, v1_ref, v2_ref,
                 kc_ref, vc_ref, sink_ref, g_ref, o_ref, *, n_prompt_chunks, chunks_per_seq, n_kv):
    n = pl.program_id(0)
    is_s = n >= n_prompt_chunks
    c = n % chunks_per_seq
    lo = jnp.where(is_s, 0, jnp.where(c >= 2, 0, jnp.where(c == 1, CHUNK, 2 * CHUNK)))

    kc = kc_ref[0]
    vc = vc_ref[0]
    k_band = jnp.concatenate([
        jnp.where(is_s, kc[:CHUNK], k0_ref[...]),
        jnp.where(is_s, kc[CHUNK:], k1_ref[...]),
        k2_ref[...]], axis=0).astype(BF16)
    v_band = jnp.concatenate([
        jnp.where(is_s, vc[:CHUNK], v0_ref[...]),
        jnp.where(is_s, vc[CHUNK:], v1_ref[...]),
        v2_ref[...]], axis=0).astype(BF16)

    q = (q_ref[...] * (HEAD_DIM ** -0.5)).astype(BF16)
    n_keys = (WIN_CHUNKS + 1) * CHUNK
    key = lax.broadcasted_iota(jnp.int32, (n_keys, KV_GROUP * CHUNK), 0)
    valid = key >= lo

    outs = []
    for gi in range(n_kv):
        kg = k_band[:, gi * HEAD_DIM:(gi + 1) * HEAD_DIM]
        vg = v_band[:, gi * HEAD_DIM:(gi + 1) * HEAD_DIM]
        heads = [gi * KV_GROUP + h for h in range(KV_GROUP)]
        qg = jnp.concatenate([q[:, h * HEAD_DIM:(h + 1) * HEAD_DIM] for h in heads], axis=0)
        st = lax.dot_general(kg, qg, (((1,), (1,)), ((), ())), preferred_element_type=F32)
        st = jnp.where(valid, st, NEG)
        sink = sink_ref[gi:gi + 1, :]
        m = jnp.maximum(jnp.max(st, axis=0, keepdims=True), sink)
        p = jnp.exp(st - m)
        denom = jnp.sum(p, axis=0, keepdims=True) + jnp.exp(sink - m)
        pn = (p * (1.0 / denom)).astype(BF16)
        og = lax.dot_general(pn, vg, (((0,), (0,)), ((), ())), preferred_element_type=F32)
        outs.extend(og[h * CHUNK:(h + 1) * CHUNK] for h in range(KV_GROUP))
    att = jnp.concatenate(outs, axis=1)
    o_ref[...] = (att * _rms_scale(att) * g_ref[...]).astype(o_ref.dtype)


def _attention(z, cache_k, cache_v, sinks, g, *, n_prompt_chunks, chunks_per_seq,
               att_width, n_kv, k_col_blk, v_col_blk):
    t = z.shape[0]
    n_items = t // CHUNK
    kvw = n_kv * HEAD_DIM
    npc = n_prompt_chunks

    def hist(d):
        return lambda n: (n - jnp.minimum(d, n % chunks_per_seq), k_col_blk)

    def histv(d):
        return lambda n: (n - jnp.minimum(d, n % chunks_per_seq), v_col_blk)

    cache_map = lambda n: (jnp.maximum(n - npc, 0), 0, 0)
    kern = functools.partial(_attn_kernel, n_prompt_chunks=npc,
                             chunks_per_seq=chunks_per_seq, n_kv=n_kv)
    sink_rows = jnp.repeat(sinks.reshape(n_kv, KV_GROUP), CHUNK, axis=1)
    return pl.pallas_call(
        kern,
        grid=(n_items,),
        in_specs=[
            pl.BlockSpec((CHUNK, att_width), lambda n: (n, 0)),
            pl.BlockSpec((CHUNK, kvw), hist(2)),
            pl.BlockSpec((CHUNK, kvw), hist(1)),
            pl.BlockSpec((CHUNK, kvw), hist(0)),
            pl.BlockSpec((CHUNK, kvw), histv(2)),
            pl.BlockSpec((CHUNK, kvw), histv(1)),
            pl.BlockSpec((CHUNK, kvw), histv(0)),
            pl.BlockSpec((1, WIN_CHUNKS * CHUNK, kvw), cache_map),
            pl.BlockSpec((1, WIN_CHUNKS * CHUNK, kvw), cache_map),
            pl.BlockSpec((n_kv, KV_GROUP * CHUNK), lambda n: (0, 0)),
            pl.BlockSpec((1, att_width), lambda n: (0, 0)),
        ],
        out_specs=pl.BlockSpec((CHUNK, att_width), lambda n: (n, 0)),
        out_shape=jax.ShapeDtypeStruct((t, att_width), BF16),
        compiler_params=_params(1),
        name="attention",
    )(z, z, z, z, z, z, z, cache_k, cache_v, sink_rows, g)


SUBLANES = 8


def _lru_kernel(xb_ref, yb_ref, sconv_ref, sh_ref, cw_ref, cb_ref, wa_ref, ba_ref,
                wx_ref, bx_ref, lam_ref, g_ref, o_ref, hl_ref, hist_ref, hcar_ref,
                a_sc, b_sc, h_sc, *, n_prompt_chunks, chunks_per_seq):
    n = pl.program_id(0)
    is_s = n >= n_prompt_chunks
    hist_rows = hist_ref.shape[0]

    @pl.when(jnp.logical_and(jnp.logical_not(is_s), n % chunks_per_seq == 0))
    def _():
        hist_ref[...] = jnp.zeros_like(hist_ref)
        hcar_ref[...] = jnp.zeros_like(hcar_ref)

    @pl.when(is_s)
    def _():
        hist_ref[...] = jnp.zeros_like(hist_ref)
        hist_ref[hist_rows - (CONV_WIDTH - 1):, :] = sconv_ref[0]
        hcar_ref[...] = sh_ref[0]

    x = xb_ref[...]
    rows, width = x.shape
    hist = hist_ref[...]
    row8 = lax.broadcasted_iota(jnp.int32, (hist_rows, width), 0)
    xc = cb_ref[...]
    for j in range(CONV_WIDTH):
        s = CONV_WIDTH - 1 - j
        if s == 0:
            xs = x
        else:
            xr = pltpu.roll(x, s, 0)
            head = jnp.where(row8 < s, pltpu.roll(hist, s, 0), xr[:hist_rows])
            xs = jnp.concatenate([head, xr[hist_rows:]], axis=0)
        xc = xc + xs * cw_ref[j:j + 1, :]
    hist_ref[...] = x[rows - hist_rows:]

    xcb = xc.astype(BF16)
    ra, rx = [], []
    for nb in range(width // LRU_BLOCK):
        blk = xcb[:, nb * LRU_BLOCK:(nb + 1) * LRU_BLOCK]
        ra.append(jnp.dot(blk, wa_ref[nb], preferred_element_type=F32))
        rx.append(jnp.dot(blk, wx_ref[nb], preferred_element_type=F32))
    r = jax.nn.sigmoid(jnp.concatenate(ra, axis=1) + ba_ref[...])
    ig = jax.nn.sigmoid(jnp.concatenate(rx, axis=1) + bx_ref[...])
    log_a = (-LRU_C * r) * jax.nn.softplus(-lam_ref[...])
    a = jnp.exp(log_a)
    b = jnp.sqrt(-jnp.tanh(log_a) * (1.0 + a * a)) * ig * xc

    seg_len = rows // SUBLANES
    h_blocks, carries = [], []
    for nb in range(width // LANES):
        lanes = slice(nb * LANES, (nb + 1) * LANES)
        a_sc[nb] = a[:, lanes]
        b_sc[nb] = b[:, lanes]
        hloc, ploc = [], []
        for k in range(seg_len):
            a_k = a_sc[nb, pl.ds(k, SUBLANES, stride=seg_len), :]
            b_k = b_sc[nb, pl.ds(k, SUBLANES, stride=seg_len), :]
            hloc.append(b_k if k == 0 else a_k * hloc[-1] + b_k)
            ploc.append(a_k if k == 0 else a_k * ploc[-1])
        carry = hcar_ref[:, lanes]
        seg_in = []
        for s in range(SUBLANES):
            seg_in.append(carry)
            carry = ploc[-1][s:s + 1] * carry + hloc[-1][s:s + 1]
        seg_in = jnp.concatenate(seg_in, axis=0)
        carries.append(carry)
        for k in range(seg_len):
            h_sc[nb, pl.ds(k, SUBLANES, stride=seg_len), :] = hloc[k] + ploc[k] * seg_in
        h_blocks.append(h_sc[nb])
    h = jnp.concatenate(h_blocks, axis=1)
    h_last = jnp.concatenate(carries, axis=1)
    hcar_ref[...] = h_last
    hl_ref[0] = h_last

    y = jax.nn.gelu(yb_ref[...]) * h
    o_ref[...] = (y * _rms_scale(y) * g_ref[...]).astype(o_ref.dtype)


def _rglru(z, state_conv, state_h, conv_w, conv_b, wa, ba, wx, bx, lam, g,
           *, n_prompt_chunks, chunks_per_seq, width, xb_col_blk, yb_col_blk):
    t = z.shape[0]
    n_items = t // CHUNK
    npc = n_prompt_chunks
    nblk = width // LRU_BLOCK
    state_map = lambda n: (jnp.maximum(n - npc, 0), 0, 0)
    full2 = lambda n: (0, 0)
    full3 = lambda n: (0, 0, 0)
    kern = functools.partial(_lru_kernel, n_prompt_chunks=npc, chunks_per_seq=chunks_per_seq)
    return pl.pallas_call(
        kern,
        grid=(n_items,),
        in_specs=[
            pl.BlockSpec((CHUNK, width), lambda n: (n, xb_col_blk)),
            pl.BlockSpec((CHUNK, width), lambda n: (n, yb_col_blk)),
            pl.BlockSpec((1, CONV_WIDTH - 1, width), state_map),
            pl.BlockSpec((1, 1, width), state_map),
            pl.BlockSpec((CONV_WIDTH, width), full2),
            pl.BlockSpec((1, width), full2),
            pl.BlockSpec((nblk, LRU_BLOCK, LRU_BLOCK), full3),
            pl.BlockSpec((1, width), full2),
            pl.BlockSpec((nblk, LRU_BLOCK, LRU_BLOCK), full3),
            pl.BlockSpec((1, width), full2),
            pl.BlockSpec((1, width), full2),
            pl.BlockSpec((1, width), full2),
        ],
        out_specs=[
            pl.BlockSpec((CHUNK, width), lambda n: (n, 0)),
            pl.BlockSpec((1, 1, width), lambda n: (n, 0, 0)),
        ],
        out_shape=[
            jax.ShapeDtypeStruct((t, width), BF16),
            jax.ShapeDtypeStruct((n_items, 1, width), F32),
        ],
        scratch_shapes=[
            pltpu.VMEM((SUBLANES, width), F32),
            pltpu.VMEM((1, width), F32),
            pltpu.VMEM((width // LANES, CHUNK, LANES), F32),
            pltpu.VMEM((width // LANES, CHUNK, LANES), F32),
            pltpu.VMEM((width // LANES, CHUNK, LANES), F32),
        ],
        compiler_params=_params(1),
        name="rglru",
    )(z, z, state_conv, state_h, conv_w, conv_b, wa, ba, wx, bx, lam, g)


def _outproj_kernel(att_ref, lru_ref, wa_ref, wl_ref, xp_ref, xs_ref, o_ref, *, n_prompt_tiles):
    i = pl.program_id(0)
    acc = jnp.dot(att_ref[...], wa_ref[...].astype(BF16), preferred_element_type=F32)
    acc = acc + jnp.dot(lru_ref[...], wl_ref[...].astype(BF16), preferred_element_type=F32)

    @pl.when(i < n_prompt_tiles)
    def _():
        o_ref[...] = xp_ref[...] + acc

    @pl.when(i >= n_prompt_tiles)
    def _():
        o_ref[...] = xs_ref[...] + acc


def _out_proj(att, lru, w, xp, xs, *, tm, tn):
    t, aw = att.shape
    lw = lru.shape[1]
    d = w.shape[1]
    assert aw == lw and w.shape[0] == aw + lw
    npt = xp.shape[0] // tm
    kern = functools.partial(_outproj_kernel, n_prompt_tiles=npt)
    return pl.pallas_call(
        kern,
        grid=(t // tm, d // tn),
        in_specs=[
            pl.BlockSpec((tm, aw), lambda i, j: (i, 0)),
            pl.BlockSpec((tm, lw), lambda i, j: (i, 0)),
            pl.BlockSpec((aw, tn), lambda i, j: (0, j)),
            pl.BlockSpec((lw, tn), lambda i, j: (1, j)),
            pl.BlockSpec((tm, tn), lambda i, j: (jnp.minimum(i, npt - 1), j)),
            pl.BlockSpec((tm, tn), lambda i, j: (jnp.maximum(i - npt, 0), j)),
        ],
        out_specs=pl.BlockSpec((tm, tn), lambda i, j: (i, j)),
        out_shape=jax.ShapeDtypeStruct((t, d), F32),
        compiler_params=_params(2),
        name="out_proj",
    )(att, lru, w, w, xp, xs)


ROUTE_GROUP_LANE = EXPERTS_PER_GROUP
ROUTE_RANK_LANE = EXPERTS_PER_GROUP + 1


def _router_kernel(x_ref, g_ref, wr_ref, br_ref, route_ref, cnt_ref, h_ref):
    @pl.when(pl.program_id(0) == 0)
    def _():
        cnt_ref[...] = jnp.zeros_like(cnt_ref)

    _rmsnorm_rows(lambda rows: x_ref[rows, :], g_ref, h_ref, x_ref.shape[0])
    logits = jnp.dot(h_ref[...], wr_ref[...], preferred_element_type=F32) + br_ref[...]
    lane = lax.broadcasted_iota(jnp.int32, logits.shape, 1).astype(F32)
    ninf = -jnp.inf

    def first_argmax(v, vmax):
        return jnp.min(jnp.where(v == vmax, lane, float(LANES)), axis=-1, keepdims=True)

    gl = jnp.where(lane < N_GROUPS, logits, ninf)
    gm = jnp.max(gl, axis=-1, keepdims=True)
    g_idx = first_argmax(gl, gm)
    g_w = 1.0 / jnp.sum(jnp.exp(gl - gm), axis=-1, keepdims=True)

    lo = N_GROUPS + EXPERTS_PER_GROUP * g_idx
    el = jnp.where(jnp.logical_and(lane >= lo, lane < lo + EXPERTS_PER_GROUP), logits, ninf)
    v1 = jnp.max(el, axis=-1, keepdims=True)
    i1 = first_argmax(el, v1)
    el2 = jnp.where(lane == i1, ninf, el)
    v2 = jnp.max(el2, axis=-1, keepdims=True)
    i2 = first_argmax(el2, v2)
    e2 = jnp.exp(v2 - v1)
    w1 = (1.0 / (1.0 + e2)) * g_w
    w2 = (e2 / (1.0 + e2)) * g_w
    gates = jnp.where(lane == i1 - lo, w1, 0.0) + jnp.where(lane == i2 - lo, w2, 0.0)

    tm = x_ref.shape[0]
    onehot = jnp.where(lane == g_idx, 1.0, 0.0)
    r_i = lax.broadcasted_iota(jnp.int32, (tm, tm), 0)
    c_i = lax.broadcasted_iota(jnp.int32, (tm, tm), 1)
    tri = jnp.where(c_i < r_i, 1.0, 0.0).astype(BF16)
    before = jnp.dot(tri, onehot.astype(BF16), preferred_element_type=F32) + cnt_ref[...]
    rank = jnp.sum(onehot * before, axis=-1, keepdims=True)
    cnt_ref[...] += jnp.sum(onehot, axis=0, keepdims=True)
    route_ref[...] = (gates + jnp.where(lane == ROUTE_GROUP_LANE, g_idx, 0.0)
                      + jnp.where(lane == ROUTE_RANK_LANE, rank, 0.0))


def _router(x2, g, wr, br, *, tm):
    t, d = x2.shape
    return pl.pallas_call(
        _router_kernel,
        grid=(t // tm,),
        in_specs=[
            pl.BlockSpec((tm, d), lambda i: (i, 0)),
            pl.BlockSpec((1, d), lambda i: (0, 0)),
            pl.BlockSpec((d, LANES), lambda i: (0, 0)),
            pl.BlockSpec((1, LANES), lambda i: (0, 0)),
        ],
        out_specs=[
            pl.BlockSpec((tm, LANES), lambda i: (i, 0)),
            pl.BlockSpec((1, LANES), lambda i: (0, 0)),
        ],
        out_shape=[
            jax.ShapeDtypeStruct((t, LANES), F32),
            jax.ShapeDtypeStruct((1, LANES), F32),
        ],
        scratch_shapes=[pltpu.VMEM((tm, d), BF16)],
        compiler_params=_params(1),
        name="router",
    )(x2, g, wr, br)


def _routing_tables(route, counts, *, tile, n_tiles_max, n_halves):
    t = route.shape[0]
    i32 = jnp.int32
    g = route[:, ROUTE_GROUP_LANE].astype(i32)
    rank = route[:, ROUTE_RANK_LANE].astype(i32)
    cnt = counts[0, :N_GROUPS].astype(i32)
    tiles_g = (cnt + tile - 1) // tile
    gids = jnp.arange(N_GROUPS, dtype=i32)
    tend = jnp.sum(jnp.where(gids[None, :] <= gids[:, None], tiles_g[None, :], 0), axis=1)
    tstart = tend - tiles_g
    n_used = tend[-1]
    dest = tstart[g] * tile + rank
    src = jnp.zeros((n_tiles_max * tile,), i32).at[dest].set(jnp.arange(t, dtype=i32))

    def items(per_tile):
        n_items = per_tile * n_used
        w_all = jnp.arange(per_tile * n_tiles_max, dtype=i32)
        w = jnp.minimum(w_all, n_items - 1)
        gi = jnp.sum((w[:, None] >= per_tile * tend[None, :]).astype(i32), axis=1)
        local = w - per_tile * tstart[gi]
        spare = w_all - n_items
        part = jnp.where(spare < 0, local // tiles_g[gi], spare % per_tile)
        p = jnp.where(spare < 0, tstart[gi] + local % tiles_g[gi], n_used + spare // per_tile)
        valid = spare < 0
        wkey = gi * per_tile + part
        first = valid & ((w_all == 0) | (wkey != jnp.roll(wkey, 1)))
        before = w_all[None, :] <= w_all[:, None]
        run_no = jnp.sum(jnp.where(before & first[None, :], 1, 0), axis=1) - 1
        later_start = (w_all[None, :] > w_all[:, None]) & first[None, :]
        n_all = w_all.shape[0]
        nxt = jnp.min(jnp.where(later_start, w_all[None, :], n_all), axis=1)
        nxt_key = jnp.where(nxt < n_all, wkey[jnp.minimum(nxt, n_all - 1)], -1)
        run = jnp.stack([first.astype(i32), run_no % 2, nxt_key], axis=1).reshape(-1)
        return p, part, wkey, n_items.reshape(1), run

    up_p, up_e, up_key, up_n, up_run = items(EXPERTS_PER_GROUP)
    dn_p, dn_h, dn_key, dn_n, dn_run = items(n_halves)
    return dict(dest=dest, src=src, n_used=n_used.reshape(1),
                up=(up_p, up_e, up_key, up_n, up_run),
                down=(dn_p, dn_h, dn_key, dn_n, dn_run))


def _row_copy(src_hbm, dst_vmem, src_row, dst_row, sem):
    return pltpu.make_async_copy(src_hbm.at[pl.ds(src_row, 1)], dst_vmem.at[pl.ds(dst_row, 1)], sem)


def _start_row_gather(idx_ref, base, n_rows, pairs):
    def body(r, carry):
        i = idx_ref[base + r]
        for src_hbm, dst_vmem, sem in pairs:
            _row_copy(src_hbm, dst_vmem, i, r, sem).start()
        return carry
    lax.fori_loop(0, n_rows, body, 0, unroll=8)


def _start_rows(idx_ref, idx_base, row0, n_rows, pairs):
    for k in range(n_rows):
        i = idx_ref[idx_base + row0 + k]
        for src_hbm, dst_vmem, sem in pairs:
            _row_copy(src_hbm, dst_vmem, i, row0 + k, sem).start()


def _norm_tile_and_prefetch(load_rows, g_ref, o_ref, n_rows, has_next, idx_ref, next_base,
                            next_pairs):
    @pl.when(has_next)
    def _():
        _rmsnorm_rows(load_rows, g_ref, o_ref, n_rows,
                      per_trip=lambda row0: _start_rows(idx_ref, next_base, row0, NORM_ROWS,
                                                        next_pairs))

    @pl.when(jnp.logical_not(has_next))
    def _():
        _rmsnorm_rows(load_rows, g_ref, o_ref, n_rows)


def _wait_row_gather(n_rows, pairs):
    for src_hbm, dst_vmem, sem in pairs:
        pltpu.make_async_copy(src_hbm.at[pl.ds(0, n_rows)], dst_vmem, sem).wait()


def _dispatch_kernel(src_ref, nused_ref, x2_hbm, route_hbm, g_ref, xs_ref, gs_ref,
                     xbuf, gbuf, sem):
    p = pl.program_id(0)
    n_used = nused_ref[0]
    tile = xbuf.shape[1]
    slot = p % 2

    def pairs(s):
        return [(x2_hbm, xbuf.at[s], sem.at[0, s]), (route_hbm, gbuf.at[s], sem.at[1, s])]

    @pl.when(p == 0)
    def _():
        _start_row_gather(src_ref, 0, tile, pairs(0))

    @pl.when(p < n_used)
    def _():
        _wait_row_gather(tile, pairs(slot))
        gs_ref[...] = gbuf[slot]
        _norm_tile_and_prefetch(lambda rows: xbuf[slot, rows, :], g_ref, xs_ref, tile,
                                p + 1 < n_used, src_ref, (p + 1) * tile, pairs(1 - slot))

    @pl.when(p >= n_used)
    def _():
        xs_ref[...] = jnp.zeros_like(xs_ref)
        gs_ref[...] = jnp.zeros_like(gs_ref)


def _dispatch(x2, route, g, src, n_used, *, tile, n_tiles_max):
    t, d = x2.shape
    tile_map = lambda p, src_ref, n_ref: (p, 0)
    grid_spec = pltpu.PrefetchScalarGridSpec(
        num_scalar_prefetch=2,
        grid=(n_tiles_max,),
        in_specs=[
            pl.BlockSpec(memory_space=pl.ANY),
            pl.BlockSpec(memory_space=pl.ANY),
            pl.BlockSpec((1, d), lambda p, s, n: (0, 0)),
        ],
        out_specs=[
            pl.BlockSpec((tile, d), tile_map),
            pl.BlockSpec((tile, LANES), tile_map),
        ],
        scratch_shapes=[
            pltpu.VMEM((2, tile, d), F32),
            pltpu.VMEM((2, tile, LANES), F32),
            pltpu.SemaphoreType.DMA((2, 2)),
        ],
    )
    return pl.pallas_call(
        _dispatch_kernel,
        grid_spec=grid_spec,
        out_shape=[
            jax.ShapeDtypeStruct((n_tiles_max * tile, d), BF16),
            jax.ShapeDtypeStruct((n_tiles_max * tile, LANES), F32),
        ],
        compiler_params=_params(1),
        name="moe_dispatch",
    )(src, n_used, x2, route, g)


def _stream_run_weights(w, n_items, key_ref, run_ref, copies):
    is_first = run_ref[3 * w]
    slot = run_ref[3 * w + 1]
    nxt_key = run_ref[3 * w + 2]
    valid = w < n_items

    @pl.when(jnp.logical_and(valid, w == 0))
    def _():
        for cp in copies(key_ref[0], 0):
            cp.start()

    @pl.when(jnp.logical_and(valid, is_first == 1))
    def _():
        for cp in copies(key_ref[w], slot):
            cp.wait()

        @pl.when(nxt_key >= 0)
        def _():
            for cp in copies(nxt_key, 1 - slot):
                cp.start()

    return slot


def _moe_up_kernel(ip_ref, ie_ref, key_ref, n_ref, run_ref, xs_ref, gs_ref, w1_hbm, w3_hbm,
                   hid_ref, w1_buf, w3_buf, sem):
    w = pl.program_id(0)

    def copies(key, slot):
        return [pltpu.make_async_copy(w1_hbm.at[key], w1_buf.at[slot], sem.at[0, slot]),
                pltpu.make_async_copy(w3_hbm.at[key], w3_buf.at[slot], sem.at[1, slot])]

    slot = _stream_run_weights(w, n_ref[0], key_ref, run_ref, copies)

    @pl.when(w < n_ref[0])
    def _():
        x = xs_ref[...]
        a = jnp.dot(x, w1_buf[slot].astype(BF16), preferred_element_type=F32)
        b = jnp.dot(x, w3_buf[slot].astype(BF16), preferred_element_type=F32)
        lane = lax.broadcasted_iota(jnp.int32, gs_ref.shape, 1)
        gate = jnp.sum(jnp.where(lane == ie_ref[w], gs_ref[...], 0.0), axis=-1, keepdims=True)
        hid_ref[...] = (jax.nn.silu(a) * b * gate).astype(hid_ref.dtype)

    @pl.when(w >= n_ref[0])
    def _():
        hid_ref[...] = jnp.zeros_like(hid_ref)


def _moe_up(xs, gs, w1, w3, items, *, tile):
    rows, d = xs.shape
    f = w1.shape[2]
    ip, ie, key, n_items, run = items
    grid_spec = pltpu.PrefetchScalarGridSpec(
        num_scalar_prefetch=5,
        grid=(ip.shape[0],),
        in_specs=[
            pl.BlockSpec((tile, d), lambda w, ip, ie, key, n, run: (ip[w], 0)),
            pl.BlockSpec((tile, LANES), lambda w, ip, ie, key, n, run: (ip[w], 0)),
            pl.BlockSpec(memory_space=pl.ANY),
            pl.BlockSpec(memory_space=pl.ANY),
        ],
        out_specs=pl.BlockSpec((tile, f), lambda w, ip, ie, key, n, run: (ip[w], ie[w])),
        scratch_shapes=[
            pltpu.VMEM((2, d, f), w1.dtype),
            pltpu.VMEM((2, d, f), w3.dtype),
            pltpu.SemaphoreType.DMA((2, 2)),
        ],
    )
    return pl.pallas_call(
        _moe_up_kernel,
        grid_spec=grid_spec,
        out_shape=jax.ShapeDtypeStruct((rows, EXPERTS_PER_GROUP * f), BF16),
        compiler_params=_params(1),
        name="moe_up",
    )(ip, ie, key, n_items, run, xs, gs, w1, w3)


def _moe_down_kernel(ip_ref, ih_ref, key_ref, n_ref, run_ref, hid_ref, w2_hbm, y_ref,
                     w2_buf, sem, *, n_halves):
    w = pl.program_id(0)
    tn = w2_buf.shape[2]

    def copies(key, slot):
        g = key // n_halves
        col = pl.multiple_of((key % n_halves) * tn, tn)
        return [pltpu.make_async_copy(w2_hbm.at[g, :, pl.ds(col, tn)], w2_buf.at[slot],
                                      sem.at[slot])]

    slot = _stream_run_weights(w, n_ref[0], key_ref, run_ref, copies)

    @pl.when(w < n_ref[0])
    def _():
        y_ref[...] = jnp.dot(hid_ref[...], w2_buf[slot].astype(BF16),
                             preferred_element_type=F32)

    @pl.when(w >= n_ref[0])
    def _():
        y_ref[...] = jnp.zeros_like(y_ref)


def _moe_down(hid, w2g, items, *, tile, n_halves):
    rows, k = hid.shape
    d = w2g.shape[2]
    tn = d // n_halves
    ip, ih, key, n_items, run = items
    grid_spec = pltpu.PrefetchScalarGridSpec(
        num_scalar_prefetch=5,
        grid=(ip.shape[0],),
        in_specs=[
            pl.BlockSpec((tile, k), lambda w, ip, ih, key, n, run: (ip[w], 0)),
            pl.BlockSpec(memory_space=pl.ANY),
        ],
        out_specs=pl.BlockSpec((tile, tn), lambda w, ip, ih, key, n, run: (ip[w], ih[w])),
        scratch_shapes=[
            pltpu.VMEM((2, k, tn), w2g.dtype),
            pltpu.SemaphoreType.DMA((2,)),
        ],
    )
    return pl.pallas_call(
        functools.partial(_moe_down_kernel, n_halves=n_halves),
        grid_spec=grid_spec,
        out_shape=jax.ShapeDtypeStruct((rows, d), F32),
        compiler_params=_params(1),
        name="moe_down",
    )(ip, ih, key, n_items, run, hid, w2g)


def _final_kernel(dest_ref, x_ref, ys_hbm, g_ref, o_ref, ybuf, sem, *, tile0):
    i = pl.program_id(0)
    tm = ybuf.shape[1]
    slot = i % 2

    def pairs(s):
        return [(ys_hbm, ybuf.at[s], sem.at[s])]

    @pl.when(i == 0)
    def _():
        _start_row_gather(dest_ref, tile0 * tm, tm, pairs(0))

    _wait_row_gather(tm, pairs(slot))
    _norm_tile_and_prefetch(lambda rows: x_ref[rows, :] + ybuf[slot, rows, :], g_ref, o_ref, tm,
                            i + 1 < pl.num_programs(0), dest_ref, (tile0 + i + 1) * tm,
                            pairs(1 - slot))


def _final(x2, ys, dest, g, *, row0, n_rows, tm):
    d = x2.shape[1]
    off = row0 // tm
    grid_spec = pltpu.PrefetchScalarGridSpec(
        num_scalar_prefetch=1,
        grid=(n_rows // tm,),
        in_specs=[
            pl.BlockSpec((tm, d), lambda i, dest: (i + off, 0)),
            pl.BlockSpec(memory_space=pl.ANY),
            pl.BlockSpec((1, d), lambda i, dest: (0, 0)),
        ],
        out_specs=pl.BlockSpec((tm, d), lambda i, dest: (i, 0)),
        scratch_shapes=[pltpu.VMEM((2, tm, d), F32), pltpu.SemaphoreType.DMA((2,))],
    )
    return pl.pallas_call(
        functools.partial(_final_kernel, tile0=off),
        grid_spec=grid_spec,
        out_shape=jax.ShapeDtypeStruct((n_rows, d), F32),
        compiler_params=_params(1),
        name="final_norm",
    )(dest, x2, ys, g)


def _rope_tables(pos):
    half = ROT_DIM // 2
    inv = ROPE_THETA ** (-np.arange(half, dtype=np.float64) / half)
    ang = np.asarray(pos, np.float64)[:, None] * inv[None, :]
    cos, sin = np.cos(ang), np.sin(ang)
    ones = np.ones((ang.shape[0], HEAD_DIM - ROT_DIM))
    zeros = np.zeros((ang.shape[0], HEAD_DIM - half))
    c = np.concatenate([cos, cos, ones], axis=1)
    s_hi = np.concatenate([-sin, zeros], axis=1)
    s_lo = np.concatenate([np.zeros_like(sin), sin, 0.0 * ones], axis=1)
    rep = LANES // HEAD_DIM
    return tuple(jnp.asarray(np.tile(t, (1, rep)), F32) for t in (c, s_hi, s_lo))


def kernel(x_prompt, x_sample, cache_k, cache_v, state_conv, state_h, norm_mix, w_in, conv_w,
           conv_b, w_gate_a, b_gate_a, w_gate_x, b_gate_x, lru_lambda, sinks, norm_attn_out,
           norm_lru_out, w_out, norm_ffn, w_group, b_group, w_expert_router, b_expert_router,
           w1, w3, w2, norm_final):
    batch, seq, d_model = x_prompt.shape
    dec_batch, dec_seq, _ = x_sample.shape
    depth = w_in.shape[0]
    assert depth == 1 and dec_seq == CHUNK and seq % CHUNK == 0
    n_heads = sinks.shape[1]
    att_width = n_heads * HEAD_DIM
    n_kv = cache_k.shape[3]
    kv_cols = n_kv * HEAD_DIM
    lru_width = lru_lambda.shape[1]
    past_len = 1024
    cw = cache_k.shape[2]
    assert cw == WIN_CHUNKS * CHUNK
    n_p, n_s = batch * seq, dec_batch * dec_seq
    chunks_per_seq = seq // CHUNK
    n_prompt_chunks = n_p // CHUNK

    xp = x_prompt.reshape(n_p, d_model)
    xs = x_sample.reshape(n_s, d_model)

    w_r = w_in[0]
    tn = 2 * kv_cols
    assert tn == 512 and att_width % tn == 0 and lru_width % tn == 0
    n_q_tiles = att_width // tn
    kv_tile = n_q_tiles

    tm_in = 1024
    assert seq % tm_in == 0 and n_s % tm_in == 0 and tm_in % dec_seq == 0
    pos = np.concatenate([np.arange(seq), np.tile(past_len + np.arange(dec_seq), tm_in // dec_seq)])
    c_tab, shi_tab, slo_tab = _rope_tables(pos)

    z = _in_proj(xp, xs, norm_mix, w_r, c_tab, shi_tab, slo_tab, tm=tm_in, tn=tn,
                 n_q_tiles=n_q_tiles, kv_tile=kv_tile, prompt_tiles_per_seq=seq // tm_in)

    k_col = att_width + 2 * lru_width
    v_col = k_col + kv_cols
    att = _attention(
        z, cache_k[0].reshape(dec_batch, cw, kv_cols), cache_v[0].reshape(dec_batch, cw, kv_cols),
        sinks[0], norm_attn_out,
        n_prompt_chunks=n_prompt_chunks, chunks_per_seq=chunks_per_seq, att_width=att_width,
        n_kv=n_kv, k_col_blk=k_col // kv_cols, v_col_blk=v_col // kv_cols)

    nblk = lru_width // LRU_BLOCK
    lru, h_tiles = _rglru(
        z, state_conv[0], state_h[0].reshape(dec_batch, 1, lru_width), conv_w[0], conv_b,
        w_gate_a[0].astype(BF16), b_gate_a[0].reshape(1, lru_width),
        w_gate_x[0].astype(BF16), b_gate_x[0].reshape(1, lru_width),
        lru_lambda, norm_lru_out,
        n_prompt_chunks=n_prompt_chunks, chunks_per_seq=chunks_per_seq, width=lru_width,
        xb_col_blk=att_width // lru_width, yb_col_blk=att_width // lru_width + 1)
    del nblk

    x2 = _out_proj(att, lru, w_out[0], xp, xs, tm=1024, tn=512)

    n_routes = N_GROUPS + N_GROUPS * EXPERTS_PER_GROUP
    wr = jnp.concatenate([w_group[0], w_expert_router[0],
                          jnp.zeros((d_model, LANES - n_routes), F32)], axis=1).astype(BF16)
    br = jnp.concatenate([b_group[0], b_expert_router[0],
                          jnp.zeros((LANES - n_routes,), F32)]).reshape(1, LANES)
    route, counts = _router(x2, norm_ffn, wr, br, tm=MOE_TILE)

    n_tiles_max = (n_p + n_s) // MOE_TILE + N_GROUPS
    tabs = _routing_tables(route, counts, tile=MOE_TILE, n_tiles_max=n_tiles_max,
                           n_halves=MOE_DOWN_HALVES)
    xsort, gsort = _dispatch(x2, route, norm_ffn, tabs["src"], tabs["n_used"],
                             tile=MOE_TILE, n_tiles_max=n_tiles_max)
    hid = _moe_up(xsort, gsort, w1[0], w3[0], tabs["up"], tile=MOE_TILE)
    d_expert = w2.shape[2]
    w2g = w2[0].reshape(N_GROUPS, EXPERTS_PER_GROUP * d_expert, d_model)
    ysort = _moe_down(hid, w2g, tabs["down"], tile=MOE_TILE, n_halves=MOE_DOWN_HALVES)

    g_fin = norm_final.reshape(1, d_model)
    y_prompt = _final(x2, ysort, tabs["dest"], g_fin, row0=0, n_rows=n_p,
                      tm=MOE_TILE).reshape(batch, seq, d_model)
    y_sample = _final(x2, ysort, tabs["dest"], g_fin, row0=n_p, n_rows=n_s,
                      tm=MOE_TILE).reshape(dec_batch, dec_seq, d_model)

    win = min(WIN_CHUNKS * CHUNK, seq)

    def prompt_tail(col, width, rows):
        return jnp.stack([lax.slice(z, ((b + 1) * seq - rows, col), ((b + 1) * seq, col + width))
                          for b in range(batch)])

    def sample_rows(col, width):
        return lax.slice(z, (n_p, col), (n_p + n_s, col + width)).reshape(dec_batch, dec_seq, width)

    k_prompt = prompt_tail(k_col, kv_cols, win).reshape(1, batch, win, n_kv, HEAD_DIM)
    v_prompt = prompt_tail(v_col, kv_cols, win).reshape(1, batch, win, n_kv, HEAD_DIM)
    conv_prompt = prompt_tail(att_width, lru_width, CONV_WIDTH - 1)[None]
    h_prompt = h_tiles[:n_prompt_chunks, 0].reshape(batch, chunks_per_seq, lru_width)[:, -1][None]

    ks = sample_rows(k_col, kv_cols).reshape(dec_batch, dec_seq, n_kv, HEAD_DIM)
    vs = sample_rows(v_col, kv_cols).reshape(dec_batch, dec_seq, n_kv, HEAD_DIM)
    k_sample = jnp.concatenate([cache_k[0], ks], axis=1)[:, -cw:][None]
    v_sample = jnp.concatenate([cache_v[0], vs], axis=1)[:, -cw:][None]
    xs_rows = sample_rows(att_width, lru_width)[:, dec_seq - (CONV_WIDTH - 1):]
    conv_sample = jnp.concatenate([state_conv[0], xs_rows], axis=1)[:, -(CONV_WIDTH - 1):][None]
    h_sample = h_tiles[n_prompt_chunks:, 0][None]

    return (y_prompt, y_sample, k_prompt, v_prompt, conv_prompt, h_prompt,
            k_sample, v_sample, conv_sample, h_sample)
```

```python
import functools

import numpy as np
import jax
import jax.numpy as jnp
from jax import lax
from jax.experimental import pallas as pl
from jax.experimental.pallas import tpu as pltpu

F32 = jnp.float32
BF16 = jnp.bfloat16

CHUNK = 64
HEAD_DIM = 64
KV_GROUP = 8
WIN_CHUNKS = 2
ROT_DIM = 16
ROPE_THETA = 500000.0
LRU_BLOCK = 128
CONV_WIDTH = 4
LRU_C = 8.0
N_GROUPS = 8
EXPERTS_PER_GROUP = 4
EPS = 1e-6
NEG = -1e30
LANES = 128
INPROJ_ROW_BLOCK = 128
MOE_TILE = 256
MOE_UP_EXPERTS = 2
MOE_DOWN_HALVES = 1
VMEM_LIMIT = 56 * 1024 * 1024


def _params(n_axes, vmem=VMEM_LIMIT):
    return pltpu.CompilerParams(
        dimension_semantics=("arbitrary",) * n_axes, vmem_limit_bytes=vmem)


def _rms_scale(x):
    return lax.rsqrt(jnp.mean(x * x, axis=-1, keepdims=True) + EPS)


NORM_ROWS = 16


def _rmsnorm_rows(load_rows, g_ref, o_ref, n_rows, per_trip=None):
    def body(r, carry):
        row0 = r * NORM_ROWS if isinstance(r, int) else pl.multiple_of(r * NORM_ROWS, NORM_ROWS)
        rows = pl.ds(row0, NORM_ROWS)
        x = load_rows(rows)
        o_ref[rows, :] = (x * _rms_scale(x) * g_ref[...]).astype(o_ref.dtype)
        if per_trip is not None:
            per_trip(row0)
        return carry
    if per_trip is not None:
        for r in range(n_rows // NORM_ROWS):
            body(r, 0)
    else:
        lax.fori_loop(0, n_rows // NORM_ROWS, body, 0, unroll=4)


def _rope_block(zb, c, s_hi, s_lo):
    return (zb * c + pltpu.roll(zb, LANES - ROT_DIM // 2, 1) * s_hi
            + pltpu.roll(zb, ROT_DIM // 2, 1) * s_lo)


def _inproj_kernel(xp_hbm, xs_hbm, g_ref, w_ref, c_ref, shi_ref, slo_ref, o_ref, x_buf, h_ref,
                   sem, *, n_prompt_tiles, n_q_tiles, kv_tile):
    i = pl.program_id(0)
    j = pl.program_id(1)
    tm, tn = o_ref.shape

    def x_tile_copy(src_hbm, tile):
        return pltpu.make_async_copy(src_hbm.at[pl.ds(tile * tm, tm)], x_buf, sem)

    def start_x(tile):
        @pl.when(tile < n_prompt_tiles)
        def _():
            x_tile_copy(xp_hbm, tile).start()

        @pl.when(tile >= n_prompt_tiles)
        def _():
            x_tile_copy(xs_hbm, tile - n_prompt_tiles).start()

    @pl.when(jnp.logical_and(i == 0, j == 0))
    def _():
        start_x(0)

    @pl.when(j == 0)
    def _():
        x_tile_copy(xp_hbm, 0).wait()
        _rmsnorm_rows(lambda rows: x_buf[rows, :], g_ref, h_ref, tm)

    @pl.when(jnp.logical_and(j == 1, i + 1 < pl.num_programs(0)))
    def _():
        start_x(i + 1)

    w = w_ref[...].astype(BF16)
    n_blk = tn // LANES
    for r0 in range(0, tm, INPROJ_ROW_BLOCK):
        rows = slice(r0, r0 + INPROJ_ROW_BLOCK)
        z = jnp.dot(h_ref[rows, :], w, preferred_element_type=F32)
        c, shi, slo = c_ref[rows, :], shi_ref[rows, :], slo_ref[rows, :]
        for b in range(n_blk):
            is_rope = j < n_q_tiles
            if b < n_blk // 2:
                is_rope = jnp.logical_or(is_rope, j == kv_tile)
            zb = z[:, b * LANES:(b + 1) * LANES]
            o_ref[rows, b * LANES:(b + 1) * LANES] = _rope_block(
                zb, jnp.where(is_rope, c, 1.0), jnp.where(is_rope, shi, 0.0),
                jnp.where(is_rope, slo, 0.0))


def _in_proj(xp, xs, g, w, c, shi, slo, *, tm, tn, n_q_tiles, kv_tile, prompt_tiles_per_seq):
    n_p, d = xp.shape
    n_s = xs.shape[0]
    t = n_p + n_s
    npt = n_p // tm
    n_cols = w.shape[1]
    kern = functools.partial(_inproj_kernel, n_prompt_tiles=npt, n_q_tiles=n_q_tiles,
                             kv_tile=kv_tile)
    n_tiles = n_cols // tn
    assert kv_tile == n_q_tiles

    def table_tile(i, j):
        return jnp.where(i < npt, i % prompt_tiles_per_seq, prompt_tiles_per_seq), 0

    def out_tile(i, j):
        return i, jnp.where(j < kv_tile, j, jnp.where(j == kv_tile, n_tiles - 1, j - 1))

    return pl.pallas_call(
        kern,
        grid=(t // tm, n_tiles),
        in_specs=[
            pl.BlockSpec(memory_space=pl.ANY),
            pl.BlockSpec(memory_space=pl.ANY),
            pl.BlockSpec((1, d), lambda i, j: (0, 0)),
            pl.BlockSpec((d, tn), lambda i, j: (0, j)),
            pl.BlockSpec((tm, LANES), table_tile),
            pl.BlockSpec((tm, LANES), table_tile),
            pl.BlockSpec((tm, LANES), table_tile),
        ],
        out_specs=pl.BlockSpec((tm, tn), out_tile),
        out_shape=jax.ShapeDtypeStruct((t, n_cols), F32),
        scratch_shapes=[pltpu.VMEM((tm, d), xp.dtype), pltpu.VMEM((tm, d), BF16),
                        pltpu.SemaphoreType.DMA(())],
        compiler_params=_params(2),
        name="in_proj",
    )(xp, xs, g, w, c, shi, slo)


def _attn_kernel(q_ref, k0_ref, k1_ref, k2_ref, v0_ref, v1_ref, v2_ref,
                 kc_ref, vc_ref, sink_ref, g_ref, o_ref, *, n_prompt_chunks, chunks_per_seq, n_kv):
    n = pl.program_id(0)
    is_s = n >= n_prompt_chunks
    c = n % chunks_per_seq
    lo = jnp.where(is_s, 0, jnp.where(c >= 2, 0, jnp.where(c == 1, CHUNK, 2 * CHUNK)))

    kc = kc_ref[0]
    vc = vc_ref[0]
    k_band = jnp.concatenate([
        jnp.where(is_s, kc[:CHUNK], k0_ref[...]),
        jnp.where(is_s, kc[CHUNK:], k1_ref[...]),
        k2_ref[...]], axis=0).astype(BF16)
    v_band = jnp.concatenate([
        jnp.where(is_s, vc[:CHUNK], v0_ref[...]),
        jnp.where(is_s, vc[CHUNK:], v1_ref[...]),
        v2_ref[...]], axis=0).astype(BF16)

    q = (q_ref[...] * (HEAD_DIM ** -0.5)).astype(BF16)
    n_keys = (WIN_CHUNKS + 1) * CHUNK
    key = lax.broadcasted_iota(jnp.int32, (n_keys, KV_GROUP * CHUNK), 0)
    valid = key >= lo

    outs = []
    for gi in range(n_kv):
        kg = k_band[:, gi * HEAD_DIM:(gi + 1) * HEAD_DIM]
        vg = v_band[:, gi * HEAD_DIM:(gi + 1) * HEAD_DIM]
        heads = [gi * KV_GROUP + h for h in range(KV_GROUP)]
        qg = jnp.concatenate([q[:, h * HEAD_DIM:(h + 1) * HEAD_DIM] for h in heads], axis=0)
        st = lax.dot_general(kg, qg, (((1,), (1,)), ((), ())), preferred_element_type=F32)
        st = jnp.where(valid, st, NEG)
        sink = sink_ref[gi:gi + 1, :]
        m = jnp.maximum(jnp.max(st, axis=0, keepdims=True), sink)
        p = jnp.exp(st - m)
        denom = jnp.sum(p, axis=0, keepdims=True) + jnp.exp(sink - m)
        pn = (p * (1.0 / denom)).astype(BF16)
        og = lax.dot_general(pn, vg, (((0,), (0,)), ((), ())), preferred_element_type=F32)
        outs.extend(og[h * CHUNK:(h + 1) * CHUNK] for h in range(KV_GROUP))
    att = jnp.concatenate(outs, axis=1)
    o_ref[...] = (att * _rms_scale(att) * g_ref[...]).astype(o_ref.dtype)


def _attention(z, cache_k, cache_v, sinks, g, *, n_prompt_chunks, chunks_per_seq,
               att_width, n_kv, k_col_blk, v_col_blk):
    t = z.shape[0]
    n_items = t // CHUNK
    kvw = n_kv * HEAD_DIM
    npc = n_prompt_chunks

    def hist(d):
        return lambda n: (n - jnp.minimum(d, n % chunks_per_seq), k_col_blk)

    def histv(d):
        return lambda n: (n - jnp.minimum(d, n % chunks_per_seq), v_col_blk)

    cache_map = lambda n: (jnp.maximum(n - npc, 0), 0, 0)
    kern = functools.partial(_attn_kernel, n_prompt_chunks=npc,
                             chunks_per_seq=chunks_per_seq, n_kv=n_kv)
    sink_rows = jnp.repeat(sinks.reshape(n_kv, KV_GROUP), CHUNK, axis=1)
    return pl.pallas_call(
        kern,
        grid=(n_items,),
        in_specs=[
            pl.BlockSpec((CHUNK, att_width), lambda n: (n, 0)),
            pl.BlockSpec((CHUNK, kvw), hist(2)),
            pl.BlockSpec((CHUNK, kvw), hist(1)),
            pl.BlockSpec((CHUNK, kvw), hist(0)),
            pl.BlockSpec((CHUNK, kvw), histv(2)),
            pl.BlockSpec((CHUNK, kvw), histv(1)),
            pl.BlockSpec((CHUNK, kvw), histv(0)),
            pl.BlockSpec((1, WIN_CHUNKS * CHUNK, kvw), cache_map),
            pl.BlockSpec((1, WIN_CHUNKS * CHUNK, kvw), cache_map),
            pl.BlockSpec((n_kv, KV_GROUP * CHUNK), lambda n: (0, 0)),
            pl.BlockSpec((1, att_width), lambda n: (0, 0)),
        ],
        out_specs=pl.BlockSpec((CHUNK, att_width), lambda n: (n, 0)),
        out_shape=jax.ShapeDtypeStruct((t, att_width), BF16),
        compiler_params=_params(1),
        name="attention",
    )(z, z, z, z, z, z, z, cache_k, cache_v, sink_rows, g)


SUBLANES = 8


def _lru_kernel(xb_ref, yb_ref, sconv_ref, sh_ref, cw_ref, cb_ref, wa_ref, ba_ref,
                wx_ref, bx_ref, lam_ref, g_ref, o_ref, hl_ref, hist_ref, hcar_ref,
                a_sc, b_sc, h_sc, *, n_prompt_chunks, chunks_per_seq):
    n = pl.program_id(0)
    is_s = n >= n_prompt_chunks
    hist_rows = hist_ref.shape[0]

    @pl.when(jnp.logical_and(jnp.logical_not(is_s), n % chunks_per_seq == 0))
    def _():
        hist_ref[...] = jnp.zeros_like(hist_ref)
        hcar_ref[...] = jnp.zeros_like(hcar_ref)

    @pl.when(is_s)
    def _():
        hist_ref[...] = jnp.zeros_like(hist_ref)
        hist_ref[hist_rows - (CONV_WIDTH - 1):, :] = sconv_ref[0]
        hcar_ref[...] = sh_ref[0]

    x = xb_ref[...]
    rows, width = x.shape
    hist = hist_ref[...]
    row8 = lax.broadcasted_iota(jnp.int32, (hist_rows, width), 0)
    xc = cb_ref[...]
    for j in range(CONV_WIDTH):
        s = CONV_WIDTH - 1 - j
        if s == 0:
            xs = x
        else:
            xr = pltpu.roll(x, s, 0)
            head = jnp.where(row8 < s, pltpu.roll(hist, s, 0), xr[:hist_rows])
            xs = jnp.concatenate([head, xr[hist_rows:]], axis=0)
        xc = xc + xs * cw_ref[j:j + 1, :]
    hist_ref[...] = x[rows - hist_rows:]

    xcb = xc.astype(BF16)
    ra, rx = [], []
    for nb in range(width // LRU_BLOCK):
        blk = xcb[:, nb * LRU_BLOCK:(nb + 1) * LRU_BLOCK]
        ra.append(jnp.dot(blk, wa_ref[nb], preferred_element_type=F32))
        rx.append(jnp.dot(blk, wx_ref[nb], preferred_element_type=F32))
    r = jax.nn.sigmoid(jnp.concatenate(ra, axis=1) + ba_ref[...])
    ig = jax.nn.sigmoid(jnp.concatenate(rx, axis=1) + bx_ref[...])
    log_a = (-LRU_C * r) * jax.nn.softplus(-lam_ref[...])
    a = jnp.exp(log_a)
    b = jnp.sqrt(-jnp.tanh(log_a) * (1.0 + a * a)) * ig * xc

    seg_len = rows // SUBLANES
    h_blocks, carries = [], []
    for nb in range(width // LANES):
        lanes = slice(nb * LANES, (nb + 1) * LANES)
        a_sc[nb] = a[:, lanes]
        b_sc[nb] = b[:, lanes]
        hloc, ploc = [], []
        for k in range(seg_len):
            a_k = a_sc[nb, pl.ds(k, SUBLANES, stride=seg_len), :]
            b_k = b_sc[nb, pl.ds(k, SUBLANES, stride=seg_len), :]
            hloc.append(b_k if k == 0 else a_k * hloc[-1] + b_k)
            ploc.append(a_k if k == 0 else a_k * ploc[-1])
        carry = hcar_ref[:, lanes]
        seg_in = []
        for s in range(SUBLANES):
            seg_in.append(carry)
            carry = ploc[-1][s:s + 1] * carry + hloc[-1][s:s + 1]
        seg_in = jnp.concatenate(seg_in, axis=0)
        carries.append(carry)
        for k in range(seg_len):
            h_sc[nb, pl.ds(k, SUBLANES, stride=seg_len), :] = hloc[k] + ploc[k] * seg_in
        h_blocks.append(h_sc[nb])
    h = jnp.concatenate(h_blocks, axis=1)
    h_last = jnp.concatenate(carries, axis=1)
    hcar_ref[...] = h_last
    hl_ref[0] = h_last

    y = jax.nn.gelu(yb_ref[...]) * h
    o_ref[...] = (y * _rms_scale(y) * g_ref[...]).astype(o_ref.dtype)


def _rglru(z, state_conv, state_h, conv_w, conv_b, wa, ba, wx, bx, lam, g,
           *, n_prompt_chunks, chunks_per_seq, width, xb_col_blk, yb_col_blk):
    t = z.shape[0]
    n_items = t // CHUNK
    npc = n_prompt_chunks
    nblk = width // LRU_BLOCK
    state_map = lambda n: (jnp.maximum(n - npc, 0), 0, 0)
    full2 = lambda n: (0, 0)
    full3 = lambda n: (0, 0, 0)
    kern = functools.partial(_lru_kernel, n_prompt_chunks=npc, chunks_per_seq=chunks_per_seq)
    return pl.pallas_call(
        kern,
        grid=(n_items,),
        in_specs=[
            pl.BlockSpec((CHUNK, width), lambda n: (n, xb_col_blk)),
            pl.BlockSpec((CHUNK, width), lambda n: (n, yb_col_blk)),
            pl.BlockSpec((1, CONV_WIDTH - 1, width), state_map),
            pl.BlockSpec((1, 1, width), state_map),
            pl.BlockSpec((CONV_WIDTH, width), full2),
            pl.BlockSpec((1, width), full2),
            pl.BlockSpec((nblk, LRU_BLOCK, LRU_BLOCK), full3),
            pl.BlockSpec((1, width), full2),
            pl.BlockSpec((nblk, LRU_BLOCK, LRU_BLOCK), full3),
            pl.BlockSpec((1, width), full2),
            pl.BlockSpec((1, width), full2),
            pl.BlockSpec((1, width), full2),
        ],
        out_specs=[
            pl.BlockSpec((CHUNK, width), lambda n: (n, 0)),
            pl.BlockSpec((1, 1, width), lambda n: (n, 0, 0)),
        ],
        out_shape=[
            jax.ShapeDtypeStruct((t, width), BF16),
            jax.ShapeDtypeStruct((n_items, 1, width), F32),
        ],
        scratch_shapes=[
            pltpu.VMEM((SUBLANES, width), F32),
            pltpu.VMEM((1, width), F32),
            pltpu.VMEM((width // LANES, CHUNK, LANES), F32),
            pltpu.VMEM((width // LANES, CHUNK, LANES), F32),
            pltpu.VMEM((width // LANES, CHUNK, LANES), F32),
        ],
        compiler_params=_params(1),
        name="rglru",
    )(z, z, state_conv, state_h, conv_w, conv_b, wa, ba, wx, bx, lam, g)


def _outproj_kernel(att_ref, lru_ref, wa_ref, wl_ref, xp_ref, xs_ref, o_ref, *, n_prompt_tiles):
    i = pl.program_id(0)
    acc = jnp.dot(att_ref[...], wa_ref[...].astype(BF16), preferred_element_type=F32)
    acc = acc + jnp.dot(lru_ref[...], wl_ref[...].astype(BF16), preferred_element_type=F32)

    @pl.when(i < n_prompt_tiles)
    def _():
        o_ref[...] = xp_ref[...] + acc

    @pl.when(i >= n_prompt_tiles)
    def _():
        o_ref[...] = xs_ref[...] + acc


def _out_proj(att, lru, w, xp, xs, *, tm, tn):
    t, aw = att.shape
    lw = lru.shape[1]
    d = w.shape[1]
    assert aw == lw and w.shape[0] == aw + lw
    npt = xp.shape[0] // tm
    kern = functools.partial(_outproj_kernel, n_prompt_tiles=npt)
    return pl.pallas_call(
        kern,
        grid=(t // tm, d // tn),
        in_specs=[
            pl.BlockSpec((tm, aw), lambda i, j: (i, 0)),
            pl.BlockSpec((tm, lw), lambda i, j: (i, 0)),
            pl.BlockSpec((aw, tn), lambda i, j: (0, j)),
            pl.BlockSpec((lw, tn), lambda i, j: (1, j)),
            pl.BlockSpec((tm, tn), lambda i, j: (jnp.minimum(i, npt - 1), j)),
            pl.BlockSpec((tm, tn), lambda i, j: (jnp.maximum(i - npt, 0), j)),
        ],
        out_specs=pl.BlockSpec((tm, tn), lambda i, j: (i, j)),
        out_shape=jax.ShapeDtypeStruct((t, d), F32),
        compiler_params=_params(2),
        name="out_proj",
    )(att, lru, w, w, xp, xs)


ROUTE_GROUP_LANE = EXPERTS_PER_GROUP
ROUTE_RANK_LANE = EXPERTS_PER_GROUP + 1


def _pack_bf16_pairs(h):
    half = h.shape[1] // 2
    bits = pltpu.bitcast(h.astype(F32), jnp.uint32)
    return (bits[:, :half] >> 16) | (bits[:, half:] & jnp.uint32(0xFFFF0000))


def _unpack_bf16_pairs(words):
    lo = pltpu.bitcast(words << 16, F32).astype(BF16)
    hi = pltpu.bitcast(words & jnp.uint32(0xFFFF0000), F32).astype(BF16)
    return jnp.concatenate([lo, hi], axis=1)


def _router_kernel(x_ref, g_ref, wr_ref, br_ref, route_ref, cnt_ref, hp_ref, h_ref):
    @pl.when(pl.program_id(0) == 0)
    def _():
        cnt_ref[...] = jnp.zeros_like(cnt_ref)

    _rmsnorm_rows(lambda rows: x_ref[rows, :], g_ref, h_ref, x_ref.shape[0])

    def pack_rows(r, carry):
        rows = pl.ds(pl.multiple_of(r * NORM_ROWS, NORM_ROWS), NORM_ROWS)
        hp_ref[rows, :] = _pack_bf16_pairs(h_ref[rows, :])
        return carry
    lax.fori_loop(0, x_ref.shape[0] // NORM_ROWS, pack_rows, 0, unroll=4)

    logits = jnp.dot(h_ref[...], wr_ref[...], preferred_element_type=F32) + br_ref[...]
    lane = lax.broadcasted_iota(jnp.int32, logits.shape, 1).astype(F32)
    ninf = -jnp.inf

    def first_argmax(v, vmax):
        return jnp.min(jnp.where(v == vmax, lane, float(LANES)), axis=-1, keepdims=True)

    gl = jnp.where(lane < N_GROUPS, logits, ninf)
    gm = jnp.max(gl, axis=-1, keepdims=True)
    g_idx = first_argmax(gl, gm)
    g_w = 1.0 / jnp.sum(jnp.exp(gl - gm), axis=-1, keepdims=True)

    lo = N_GROUPS + EXPERTS_PER_GROUP * g_idx
    el = jnp.where(jnp.logical_and(lane >= lo, lane < lo + EXPERTS_PER_GROUP), logits, ninf)
    v1 = jnp.max(el, axis=-1, keepdims=True)
    i1 = first_argmax(el, v1)
    el2 = jnp.where(lane == i1, ninf, el)
    v2 = jnp.max(el2, axis=-1, keepdims=True)
    i2 = first_argmax(el2, v2)
    e2 = jnp.exp(v2 - v1)
    w1 = (1.0 / (1.0 + e2)) * g_w
    w2 = (e2 / (1.0 + e2)) * g_w
    gates = jnp.where(lane == i1 - lo, w1, 0.0) + jnp.where(lane == i2 - lo, w2, 0.0)

    tm = x_ref.shape[0]
    onehot = jnp.where(lane == g_idx, 1.0, 0.0)
    r_i = lax.broadcasted_iota(jnp.int32, (tm, tm), 0)
    c_i = lax.broadcasted_iota(jnp.int32, (tm, tm), 1)
    tri = jnp.where(c_i < r_i, 1.0, 0.0).astype(BF16)
    before = jnp.dot(tri, onehot.astype(BF16), preferred_element_type=F32) + cnt_ref[...]
    rank = jnp.sum(onehot * before, axis=-1, keepdims=True)
    cnt_ref[...] += jnp.sum(onehot, axis=0, keepdims=True)
    route_ref[...] = (gates + jnp.where(lane == ROUTE_GROUP_LANE, g_idx, 0.0)
                      + jnp.where(lane == ROUTE_RANK_LANE, rank, 0.0))


def _router(x2, g, wr, br, *, tm):
    t, d = x2.shape
    return pl.pallas_call(
        _router_kernel,
        grid=(t // tm,),
        in_specs=[
            pl.BlockSpec((tm, d), lambda i: (i, 0)),
            pl.BlockSpec((1, d), lambda i: (0, 0)),
            pl.BlockSpec((d, LANES), lambda i: (0, 0)),
            pl.BlockSpec((1, LANES), lambda i: (0, 0)),
        ],
        out_specs=[
            pl.BlockSpec((tm, LANES), lambda i: (i, 0)),
            pl.BlockSpec((1, LANES), lambda i: (0, 0)),
            pl.BlockSpec((tm, d // 2), lambda i: (i, 0)),
        ],
        out_shape=[
            jax.ShapeDtypeStruct((t, LANES), F32),
            jax.ShapeDtypeStruct((1, LANES), F32),
            jax.ShapeDtypeStruct((t, d // 2), jnp.uint32),
        ],
        scratch_shapes=[pltpu.VMEM((tm, d), BF16)],
        compiler_params=_params(1),
        name="router",
    )(x2, g, wr, br)


def _routing_tables(route, counts, *, tile, n_tiles_max, n_halves):
    t = route.shape[0]
    i32 = jnp.int32
    g = route[:, ROUTE_GROUP_LANE].astype(i32)
    rank = route[:, ROUTE_RANK_LANE].astype(i32)
    cnt = counts[0, :N_GROUPS].astype(i32)
    tiles_g = (cnt + tile - 1) // tile
    gids = jnp.arange(N_GROUPS, dtype=i32)
    tend = jnp.sum(jnp.where(gids[None, :] <= gids[:, None], tiles_g[None, :], 0), axis=1)
    tstart = tend - tiles_g
    n_used = tend[-1]
    dest = tstart[g] * tile + rank
    src = jnp.zeros((n_tiles_max * tile,), i32).at[dest].set(jnp.arange(t, dtype=i32))

    def items(per_tile):
        n_items = per_tile * n_used
        w_all = jnp.arange(per_tile * n_tiles_max, dtype=i32)
        w = jnp.minimum(w_all, n_items - 1)
        gi = jnp.sum((w[:, None] >= per_tile * tend[None, :]).astype(i32), axis=1)
        local = w - per_tile * tstart[gi]
        spare = w_all - n_items
        part = jnp.where(spare < 0, local // tiles_g[gi], spare % per_tile)
        p = jnp.where(spare < 0, tstart[gi] + local % tiles_g[gi], n_used + spare // per_tile)
        valid = spare < 0
        wkey = gi * per_tile + part
        first = valid & ((w_all == 0) | (wkey != jnp.roll(wkey, 1)))
        before = w_all[None, :] <= w_all[:, None]
        run_no = jnp.sum(jnp.where(before & first[None, :], 1, 0), axis=1) - 1
        later_start = (w_all[None, :] > w_all[:, None]) & first[None, :]
        n_all = w_all.shape[0]
        nxt = jnp.min(jnp.where(later_start, w_all[None, :], n_all), axis=1)
        nxt_key = jnp.where(nxt < n_all, wkey[jnp.minimum(nxt, n_all - 1)], -1)
        run = jnp.stack([first.astype(i32), run_no % 2, nxt_key], axis=1).reshape(-1)
        return p, part, wkey, n_items.reshape(1), run

    up_p, up_e, up_key, up_n, up_run = items(EXPERTS_PER_GROUP // MOE_UP_EXPERTS)
    dn_p, dn_h, dn_key, dn_n, dn_run = items(n_halves)
    return dict(dest=dest, src=src, n_used=n_used.reshape(1),
                up=(up_p, up_e, up_key, up_n, up_run),
                down=(dn_p, dn_h, dn_key, dn_n, dn_run))


def _row_copy(src_hbm, dst_vmem, src_row, dst_row, sem):
    return pltpu.make_async_copy(src_hbm.at[pl.ds(src_row, 1)], dst_vmem.at[pl.ds(dst_row, 1)], sem)


def _start_row_gather(idx_ref, base, n_rows, pairs):
    def body(r, carry):
        i = idx_ref[base + r]
        for src_hbm, dst_vmem, sem in pairs:
            _row_copy(src_hbm, dst_vmem, i, r, sem).start()
        return carry
    lax.fori_loop(0, n_rows, body, 0, unroll=8)


def _start_rows(idx_ref, idx_base, row0, n_rows, pairs):
    for k in range(n_rows):
        i = idx_ref[idx_base + row0 + k]
        for src_hbm, dst_vmem, sem in pairs:
            _row_copy(src_hbm, dst_vmem, i, row0 + k, sem).start()


def _norm_tile_and_prefetch(load_rows, g_ref, o_ref, n_rows, has_next, idx_ref, next_base,
                            next_pairs):
    @pl.when(has_next)
    def _():
        _rmsnorm_rows(load_rows, g_ref, o_ref, n_rows,
                      per_trip=lambda row0: _start_rows(idx_ref, next_base, row0, NORM_ROWS,
                                                        next_pairs))

    @pl.when(jnp.logical_not(has_next))
    def _():
        _rmsnorm_rows(load_rows, g_ref, o_ref, n_rows)


def _wait_row_gather(n_rows, pairs):
    for src_hbm, dst_vmem, sem in pairs:
        pltpu.make_async_copy(src_hbm.at[pl.ds(0, n_rows)], dst_vmem, sem).wait()


def _dispatch_kernel(src_ref, h_hbm, route_hbm, xs_hbm, gs_hbm, sem, *, tile):
    p = pl.program_id(0)
    slot = p % 2

    def pairs(s):
        return [(h_hbm, xs_hbm, sem.at[0, s]), (route_hbm, gs_hbm, sem.at[1, s])]

    def wait_tile(s):
        for src_hbm, dst_hbm, sm in pairs(s):
            pltpu.make_async_copy(src_hbm.at[pl.ds(0, tile)], dst_hbm.at[pl.ds(0, tile)],
                                  sm).wait()

    def body(r, carry):
        i = src_ref[p * tile + r]
        for src_hbm, dst_hbm, sm in pairs(slot):
            _row_copy(src_hbm, dst_hbm, i, p * tile + r, sm).start()
        return carry
    lax.fori_loop(0, tile, body, 0, unroll=8)

    @pl.when(p > 0)
    def _():
        wait_tile(1 - slot)

    @pl.when(p == pl.num_programs(0) - 1)
    def _():
        wait_tile(slot)


def _dispatch(h2p, route, src, *, tile, n_tiles_max):
    rows = n_tiles_max * tile
    grid_spec = pltpu.PrefetchScalarGridSpec(
        num_scalar_prefetch=1,
        grid=(n_tiles_max,),
        in_specs=[pl.BlockSpec(memory_space=pl.ANY), pl.BlockSpec(memory_space=pl.ANY)],
        out_specs=[pl.BlockSpec(memory_space=pl.ANY), pl.BlockSpec(memory_space=pl.ANY)],
        scratch_shapes=[pltpu.SemaphoreType.DMA((2, 2))],
    )
    return pl.pallas_call(
        functools.partial(_dispatch_kernel, tile=tile),
        grid_spec=grid_spec,
        out_shape=[
            jax.ShapeDtypeStruct((rows, h2p.shape[1]), h2p.dtype),
            jax.ShapeDtypeStruct((rows, route.shape[1]), route.dtype),
        ],
        compiler_params=_params(1),
        name="moe_dispatch",
    )(src, h2p, route)


def _stream_run_weights(w, n_items, key_ref, run_ref, copies):
    is_first = run_ref[3 * w]
    slot = run_ref[3 * w + 1]
    nxt_key = run_ref[3 * w + 2]
    valid = w < n_items

    @pl.when(jnp.logical_and(valid, w == 0))
    def _():
        for cp in copies(key_ref[0], 0):
            cp.start()

    @pl.when(jnp.logical_and(valid, is_first == 1))
    def _():
        for cp in copies(key_ref[w], slot):
            cp.wait()

        @pl.when(nxt_key >= 0)
        def _():
            for cp in copies(nxt_key, 1 - slot):
                cp.start()

    return slot


def _moe_up_kernel(ip_ref, ie_ref, key_ref, n_ref, run_ref, xs_ref, gs_ref, w1_hbm, w3_hbm,
                   hid_ref, w1_buf, w3_buf, sem):
    w = pl.program_id(0)

    n_e = w1_buf.shape[1]
    f = w1_buf.shape[3]

    def copies(key, slot):
        experts = pl.ds(key * n_e, n_e)
        return [pltpu.make_async_copy(w1_hbm.at[experts], w1_buf.at[slot], sem.at[0, slot]),
                pltpu.make_async_copy(w3_hbm.at[experts], w3_buf.at[slot], sem.at[1, slot])]

    slot = _stream_run_weights(w, n_ref[0], key_ref, run_ref, copies)

    @pl.when(w < n_ref[0])
    def _():
        x = _unpack_bf16_pairs(xs_ref[...])
        lane = lax.broadcasted_iota(jnp.int32, gs_ref.shape, 1)
        for e in range(n_e):
            a = jnp.dot(x, w1_buf[slot, e].astype(BF16), preferred_element_type=F32)
            b = jnp.dot(x, w3_buf[slot, e].astype(BF16), preferred_element_type=F32)
            in_group = ie_ref[w] * n_e + e
            gate = jnp.sum(jnp.where(lane == in_group, gs_ref[...], 0.0), axis=-1, keepdims=True)
            hid_ref[:, e * f:(e + 1) * f] = (jax.nn.silu(a) * b * gate).astype(hid_ref.dtype)

    @pl.when(w >= n_ref[0])
    def _():
        hid_ref[...] = jnp.zeros_like(hid_ref)


def _moe_up(xs, gs, w1, w3, items, *, tile):
    rows, d_words = xs.shape
    d, f = w1.shape[1:]
    assert d == 2 * d_words
    ip, ie, key, n_items, run = items
    grid_spec = pltpu.PrefetchScalarGridSpec(
        num_scalar_prefetch=5,
        grid=(ip.shape[0],),
        in_specs=[
            pl.BlockSpec((tile, d_words), lambda w, ip, ie, key, n, run: (ip[w], 0)),
            pl.BlockSpec((tile, LANES), lambda w, ip, ie, key, n, run: (ip[w], 0)),
            pl.BlockSpec(memory_space=pl.ANY),
            pl.BlockSpec(memory_space=pl.ANY),
        ],
        out_specs=pl.BlockSpec((tile, MOE_UP_EXPERTS * f),
                               lambda w, ip, ie, key, n, run: (ip[w], ie[w])),
        scratch_shapes=[
            pltpu.VMEM((2, MOE_UP_EXPERTS, d, f), w1.dtype),
            pltpu.VMEM((2, MOE_UP_EXPERTS, d, f), w3.dtype),
            pltpu.SemaphoreType.DMA((2, 2)),
        ],
    )
    return pl.pallas_call(
        _moe_up_kernel,
        grid_spec=grid_spec,
        out_shape=jax.ShapeDtypeStruct((rows, EXPERTS_PER_GROUP * f), BF16),
        compiler_params=_params(1),
        name="moe_up",
    )(ip, ie, key, n_items, run, xs, gs, w1, w3)


def _moe_down_kernel(ip_ref, ih_ref, key_ref, n_ref, run_ref, hid_ref, w2_hbm, y_ref,
                     w2_buf, sem, *, n_halves):
    w = pl.program_id(0)
    tn = w2_buf.shape[2]

    def copies(key, slot):
        g = key // n_halves
        col = pl.multiple_of((key % n_halves) * tn, tn)
        return [pltpu.make_async_copy(w2_hbm.at[g, :, pl.ds(col, tn)], w2_buf.at[slot],
                                      sem.at[slot])]

    slot = _stream_run_weights(w, n_ref[0], key_ref, run_ref, copies)

    @pl.when(w < n_ref[0])
    def _():
        y_ref[...] = jnp.dot(hid_ref[...], w2_buf[slot].astype(BF16),
                             preferred_element_type=F32)

    @pl.when(w >= n_ref[0])
    def _():
        y_ref[...] = jnp.zeros_like(y_ref)


def _moe_down(hid, w2g, items, *, tile, n_halves):
    rows, k = hid.shape
    d = w2g.shape[2]
    tn = d // n_halves
    ip, ih, key, n_items, run = items
    grid_spec = pltpu.PrefetchScalarGridSpec(
        num_scalar_prefetch=5,
        grid=(ip.shape[0],),
        in_specs=[
            pl.BlockSpec((tile, k), lambda w, ip, ih, key, n, run: (ip[w], 0)),
            pl.BlockSpec(memory_space=pl.ANY),
        ],
        out_specs=pl.BlockSpec((tile, tn), lambda w, ip, ih, key, n, run: (ip[w], ih[w])),
        scratch_shapes=[
            pltpu.VMEM((2, k, tn), w2g.dtype),
            pltpu.SemaphoreType.DMA((2,)),
        ],
    )
    return pl.pallas_call(
        functools.partial(_moe_down_kernel, n_halves=n_halves),
        grid_spec=grid_spec,
        out_shape=jax.ShapeDtypeStruct((rows, d), F32),
        compiler_params=_params(1),
        name="moe_down",
    )(ip, ih, key, n_items, run, hid, w2g)


def _final_kernel(dest_ref, x_ref, ys_hbm, g_ref, o_ref, ybuf, sem, *, tile0):
    i = pl.program_id(0)
    tm = ybuf.shape[1]
    slot = i % 2

    def pairs(s):
        return [(ys_hbm, ybuf.at[s], sem.at[s])]

    @pl.when(i == 0)
    def _():
        _start_row_gather(dest_ref, tile0 * tm, tm, pairs(0))

    _wait_row_gather(tm, pairs(slot))
    _norm_tile_and_prefetch(lambda rows: x_ref[rows, :] + ybuf[slot, rows, :], g_ref, o_ref, tm,
                            i + 1 < pl.num_programs(0), dest_ref, (tile0 + i + 1) * tm,
                            pairs(1 - slot))


def _final(x2, ys, dest, g, *, row0, n_rows, tm):
    d = x2.shape[1]
    off = row0 // tm
    grid_spec = pltpu.PrefetchScalarGridSpec(
        num_scalar_prefetch=1,
        grid=(n_rows // tm,),
        in_specs=[
            pl.BlockSpec((tm, d), lambda i, dest: (i + off, 0)),
            pl.BlockSpec(memory_space=pl.ANY),
            pl.BlockSpec((1, d), lambda i, dest: (0, 0)),
        ],
        out_specs=pl.BlockSpec((tm, d), lambda i, dest: (i, 0)),
        scratch_shapes=[pltpu.VMEM((2, tm, d), F32), pltpu.SemaphoreType.DMA((2,))],
    )
    return pl.pallas_call(
        functools.partial(_final_kernel, tile0=off),
        grid_spec=grid_spec,
        out_shape=jax.ShapeDtypeStruct((n_rows, d), F32),
        compiler_params=_params(1),
        name="final_norm",
    )(dest, x2, ys, g)


def _rope_tables(pos):
    half = ROT_DIM // 2
    inv = ROPE_THETA ** (-np.arange(half, dtype=np.float64) / half)
    ang = np.asarray(pos, np.float64)[:, None] * inv[None, :]
    cos, sin = np.cos(ang), np.sin(ang)
    ones = np.ones((ang.shape[0], HEAD_DIM - ROT_DIM))
    zeros = np.zeros((ang.shape[0], HEAD_DIM - half))
    c = np.concatenate([cos, cos, ones], axis=1)
    s_hi = np.concatenate([-sin, zeros], axis=1)
    s_lo = np.concatenate([np.zeros_like(sin), sin, 0.0 * ones], axis=1)
    rep = LANES // HEAD_DIM
    return tuple(jnp.asarray(np.tile(t, (1, rep)), F32) for t in (c, s_hi, s_lo))


def kernel(x_prompt, x_sample, cache_k, cache_v, state_conv, state_h, norm_mix, w_in, conv_w,
           conv_b, w_gate_a, b_gate_a, w_gate_x, b_gate_x, lru_lambda, sinks, norm_attn_out,
           norm_lru_out, w_out, norm_ffn, w_group, b_group, w_expert_router, b_expert_router,
           w1, w3, w2, norm_final):
    batch, seq, d_model = x_prompt.shape
    dec_batch, dec_seq, _ = x_sample.shape
    depth = w_in.shape[0]
    assert depth == 1 and dec_seq == CHUNK and seq % CHUNK == 0
    n_heads = sinks.shape[1]
    att_width = n_heads * HEAD_DIM
    n_kv = cache_k.shape[3]
    kv_cols = n_kv * HEAD_DIM
    lru_width = lru_lambda.shape[1]
    past_len = 1024
    cw = cache_k.shape[2]
    assert cw == WIN_CHUNKS * CHUNK
    n_p, n_s = batch * seq, dec_batch * dec_seq
    chunks_per_seq = seq // CHUNK
    n_prompt_chunks = n_p // CHUNK

    xp = x_prompt.reshape(n_p, d_model)
    xs = x_sample.reshape(n_s, d_model)

    w_r = w_in[0]
    tn = 2 * kv_cols
    assert tn == 512 and att_width % tn == 0 and lru_width % tn == 0
    n_q_tiles = att_width // tn
    kv_tile = n_q_tiles

    tm_in = 1024
    assert seq % tm_in == 0 and n_s % tm_in == 0 and tm_in % dec_seq == 0
    pos = np.concatenate([np.arange(seq), np.tile(past_len + np.arange(dec_seq), tm_in // dec_seq)])
    c_tab, shi_tab, slo_tab = _rope_tables(pos)

    z = _in_proj(xp, xs, norm_mix, w_r, c_tab, shi_tab, slo_tab, tm=tm_in, tn=tn,
                 n_q_tiles=n_q_tiles, kv_tile=kv_tile, prompt_tiles_per_seq=seq // tm_in)

    k_col = att_width + 2 * lru_width
    v_col = k_col + kv_cols
    att = _attention(
        z, cache_k[0].reshape(dec_batch, cw, kv_cols), cache_v[0].reshape(dec_batch, cw, kv_cols),
        sinks[0], norm_attn_out,
        n_prompt_chunks=n_prompt_chunks, chunks_per_seq=chunks_per_seq, att_width=att_width,
        n_kv=n_kv, k_col_blk=k_col // kv_cols, v_col_blk=v_col // kv_cols)

    nblk = lru_width // LRU_BLOCK
    lru, h_tiles = _rglru(
        z, state_conv[0], state_h[0].reshape(dec_batch, 1, lru_width), conv_w[0], conv_b,
        w_gate_a[0].astype(BF16), b_gate_a[0].reshape(1, lru_width),
        w_gate_x[0].astype(BF16), b_gate_x[0].reshape(1, lru_width),
        lru_lambda, norm_lru_out,
        n_prompt_chunks=n_prompt_chunks, chunks_per_seq=chunks_per_seq, width=lru_width,
        xb_col_blk=att_width // lru_width, yb_col_blk=att_width // lru_width + 1)
    del nblk

    x2 = _out_proj(att, lru, w_out[0], xp, xs, tm=1024, tn=512)

    n_routes = N_GROUPS + N_GROUPS * EXPERTS_PER_GROUP
    wr = jnp.concatenate([w_group[0], w_expert_router[0],
                          jnp.zeros((d_model, LANES - n_routes), F32)], axis=1).astype(BF16)
    br = jnp.concatenate([b_group[0], b_expert_router[0],
                          jnp.zeros((LANES - n_routes,), F32)]).reshape(1, LANES)
    route, counts, h2p = _router(x2, norm_ffn, wr, br, tm=MOE_TILE)

    n_tiles_max = (n_p + n_s) // MOE_TILE + N_GROUPS
    tabs = _routing_tables(route, counts, tile=MOE_TILE, n_tiles_max=n_tiles_max,
                           n_halves=MOE_DOWN_HALVES)
    xsort, gsort = _dispatch(h2p, route, tabs["src"], tile=MOE_TILE, n_tiles_max=n_tiles_max)
    hid = _moe_up(xsort, gsort, w1[0], w3[0], tabs["up"], tile=MOE_TILE)
    d_expert = w2.shape[2]
    w2g = w2[0].reshape(N_GROUPS, EXPERTS_PER_GROUP * d_expert, d_model)
    ysort = _moe_down(hid, w2g, tabs["down"], tile=MOE_TILE, n_halves=MOE_DOWN_HALVES)

    g_fin = norm_final.reshape(1, d_model)
    y_prompt = _final(x2, ysort, tabs["dest"], g_fin, row0=0, n_rows=n_p,
                      tm=MOE_TILE).reshape(batch, seq, d_model)
    y_sample = _final(x2, ysort, tabs["dest"], g_fin, row0=n_p, n_rows=n_s,
                      tm=MOE_TILE).reshape(dec_batch, dec_seq, d_model)

    win = min(WIN_CHUNKS * CHUNK, seq)

    def prompt_tail(col, width, rows):
        return jnp.stack([lax.slice(z, ((b + 1) * seq - rows, col), ((b + 1) * seq, col + width))
                          for b in range(batch)])

    def sample_rows(col, width):
        return lax.slice(z, (n_p, col), (n_p + n_s, col + width)).reshape(dec_batch, dec_seq, width)

    k_prompt = prompt_tail(k_col, kv_cols, win).reshape(1, batch, win, n_kv, HEAD_DIM)
    v_prompt = prompt_tail(v_col, kv_cols, win).reshape(1, batch, win, n_kv, HEAD_DIM)
    conv_prompt = prompt_tail(att_width, lru_width, CONV_WIDTH - 1)[None]
    h_prompt = h_tiles[:n_prompt_chunks, 0].reshape(batch, chunks_per_seq, lru_width)[:, -1][None]

    ks = sample_rows(k_col, kv_cols).reshape(dec_batch, dec_seq, n_kv, HEAD_DIM)
    vs = sample_rows(v_col, kv_cols).reshape(dec_batch, dec_seq, n_kv, HEAD_DIM)
    k_sample = jnp.concatenate([cache_k[0], ks], axis=1)[:, -cw:][None]
    v_sample = jnp.concatenate([cache_v[0], vs], axis=1)[:, -cw:][None]
    xs_rows = sample_rows(att_width, lru_width)[:, dec_seq - (CONV_WIDTH - 1):]
    conv_sample = jnp.concatenate([state_conv[0], xs_rows], axis=1)[:, -(CONV_WIDTH - 1):][None]
    h_sample = h_tiles[n_prompt_chunks:, 0][None]

    return (y_prompt, y_sample, k_prompt, v_prompt, conv_prompt, h_prompt,
            k_sample, v_sample, conv_sample, h_sample)
```

```python
import functools

import numpy as np
import jax
import jax.numpy as jnp
from jax import lax
from jax.experimental import pallas as pl
from jax.experimental.pallas import tpu as pltpu

F32 = jnp.float32
BF16 = jnp.bfloat16

CHUNK = 64
HEAD_DIM = 64
KV_GROUP = 8
WIN_CHUNKS = 2
ROT_DIM = 16
ROPE_THETA = 500000.0
LRU_BLOCK = 128
CONV_WIDTH = 4
LRU_C = 8.0
N_GROUPS = 8
EXPERTS_PER_GROUP = 4
EPS = 1e-6
NEG = -1e30
LANES = 128
INPROJ_ROW_BLOCK = 128
MOE_TILE = 256
MOE_UP_EXPERTS = 2
MOE_DOWN_HALVES = 1
VMEM_LIMIT = 56 * 1024 * 1024


def _params(n_axes, vmem=VMEM_LIMIT):
    return pltpu.CompilerParams(
        dimension_semantics=("arbitrary",) * n_axes, vmem_limit_bytes=vmem)


def _rms_scale(x):
    return lax.rsqrt(jnp.mean(x * x, axis=-1, keepdims=True) + EPS)


NORM_ROWS = 16


def _rmsnorm_rows(load_rows, g_ref, o_ref, n_rows, per_trip=None):
    def body(r, carry):
        row0 = r * NORM_ROWS if isinstance(r, int) else pl.multiple_of(r * NORM_ROWS, NORM_ROWS)
        rows = pl.ds(row0, NORM_ROWS)
        x = load_rows(rows)
        o_ref[rows, :] = (x * _rms_scale(x) * g_ref[...]).astype(o_ref.dtype)
        if per_trip is not None:
            per_trip(row0)
        return carry
    if per_trip is not None:
        for r in range(n_rows // NORM_ROWS):
            body(r, 0)
    else:
        lax.fori_loop(0, n_rows // NORM_ROWS, body, 0, unroll=4)


def _rope_block(zb, c, s_hi, s_lo):
    return (zb * c + pltpu.roll(zb, LANES - ROT_DIM // 2, 1) * s_hi
            + pltpu.roll(zb, ROT_DIM // 2, 1) * s_lo)


def _inproj_kernel(xp_hbm, xs_hbm, g_ref, w_ref, c_ref, shi_ref, slo_ref, o_ref, x_buf, h_ref,
                   sem, *, n_prompt_tiles, n_q_tiles, kv_tile):
    i = pl.program_id(0)
    j = pl.program_id(1)
    tm, tn = o_ref.shape

    def x_tile_copy(src_hbm, tile):
        return pltpu.make_async_copy(src_hbm.at[pl.ds(tile * tm, tm)], x_buf, sem)

    def start_x(tile):
        @pl.when(tile < n_prompt_tiles)
        def _():
            x_tile_copy(xp_hbm, tile).start()

        @pl.when(tile >= n_prompt_tiles)
        def _():
            x_tile_copy(xs_hbm, tile - n_prompt_tiles).start()

    @pl.when(jnp.logical_and(i == 0, j == 0))
    def _():
        start_x(0)

    @pl.when(j == 0)
    def _():
        x_tile_copy(xp_hbm, 0).wait()
        _rmsnorm_rows(lambda rows: x_buf[rows, :], g_ref, h_ref, tm)

    @pl.when(jnp.logical_and(j == 1, i + 1 < pl.num_programs(0)))
    def _():
        start_x(i + 1)

    w = w_ref[...].astype(BF16)
    n_blk = tn // LANES
    for r0 in range(0, tm, INPROJ_ROW_BLOCK):
        rows = slice(r0, r0 + INPROJ_ROW_BLOCK)
        z = jnp.dot(h_ref[rows, :], w, preferred_element_type=F32)
        c, shi, slo = c_ref[rows, :], shi_ref[rows, :], slo_ref[rows, :]
        for b in range(n_blk):
            is_rope = j < n_q_tiles
            if b < n_blk // 2:
                is_rope = jnp.logical_or(is_rope, j == kv_tile)
            zb = z[:, b * LANES:(b + 1) * LANES]
            o_ref[rows, b * LANES:(b + 1) * LANES] = _rope_block(
                zb, jnp.where(is_rope, c, 1.0), jnp.where(is_rope, shi, 0.0),
                jnp.where(is_rope, slo, 0.0))


def _in_proj(xp, xs, g, w, c, shi, slo, *, tm, tn, n_q_tiles, kv_tile, prompt_tiles_per_seq):
    n_p, d = xp.shape
    n_s = xs.shape[0]
    t = n_p + n_s
    npt = n_p // tm
    n_cols = w.shape[1]
    kern = functools.partial(_inproj_kernel, n_prompt_tiles=npt, n_q_tiles=n_q_tiles,
                             kv_tile=kv_tile)
    n_tiles = n_cols // tn
    assert kv_tile == n_q_tiles

    def table_tile(i, j):
        return jnp.where(i < npt, i % prompt_tiles_per_seq, prompt_tiles_per_seq), 0

    def out_tile(i, j):
        return i, jnp.where(j < kv_tile, j, jnp.where(j == kv_tile, n_tiles - 1, j - 1))

    return pl.pallas_call(
        kern,
        grid=(t // tm, n_tiles),
        in_specs=[
            pl.BlockSpec(memory_space=pl.ANY),
            pl.BlockSpec(memory_space=pl.ANY),
            pl.BlockSpec((1, d), lambda i, j: (0, 0)),
            pl.BlockSpec((d, tn), lambda i, j: (0, j)),
            pl.BlockSpec((tm, LANES), table_tile),
            pl.BlockSpec((tm, LANES), table_tile),
            pl.BlockSpec((tm, LANES), table_tile),
        ],
        out_specs=pl.BlockSpec((tm, tn), out_tile),
        out_shape=jax.ShapeDtypeStruct((t, n_cols), F32),
        scratch_shapes=[pltpu.VMEM((tm, d), xp.dtype), pltpu.VMEM((tm, d), BF16),
                        pltpu.SemaphoreType.DMA(())],
        compiler_params=_params(2),
        name="in_proj",
    )(xp, xs, g, w, c, shi, slo)


def _attn_kernel(q_ref, k0_ref, k1_ref, k2_ref, v0_ref, v1_ref, v2_ref,
                 kc_ref, vc_ref, sink_ref, g_ref, o_ref, *, n_prompt_chunks, chunks_per_seq, n_kv):
    n = pl.program_id(0)
    is_s = n >= n_prompt_chunks
    c = n % chunks_per_seq
    lo = jnp.where(is_s, 0, jnp.where(c >= 2, 0, jnp.where(c == 1, CHUNK, 2 * CHUNK)))

    kc = kc_ref[0]
    vc = vc_ref[0]
    k_band = jnp.concatenate([
        jnp.where(is_s, kc[:CHUNK], k0_ref[...]),
        jnp.where(is_s, kc[CHUNK:], k1_ref[...]),
        k2_ref[...]], axis=0).astype(BF16)
    v_band = jnp.concatenate([
        jnp.where(is_s, vc[:CHUNK], v0_ref[...]),
        jnp.where(is_s, vc[CHUNK:], v1_ref[...]),
        v2_ref[...]], axis=0).astype(BF16)

    q = (q_ref[...] * (HEAD_DIM ** -0.5)).astype(BF16)
    n_keys = (WIN_CHUNKS + 1) * CHUNK
    key = lax.broadcasted_iota(jnp.int32, (n_keys, KV_GROUP * CHUNK), 0)
    valid = key >= lo

    outs = []
    for gi in range(n_kv):
        kg = k_band[:, gi * HEAD_DIM:(gi + 1) * HEAD_DIM]
        vg = v_band[:, gi * HEAD_DIM:(gi + 1) * HEAD_DIM]
        heads = [gi * KV_GROUP + h for h in range(KV_GROUP)]
        qg = jnp.concatenate([q[:, h * HEAD_DIM:(h + 1) * HEAD_DIM] for h in heads], axis=0)
        st = lax.dot_general(kg, qg, (((1,), (1,)), ((), ())), preferred_element_type=F32)
        st = jnp.where(valid, st, NEG)
        sink = sink_ref[gi:gi + 1, :]
        m = jnp.maximum(jnp.max(st, axis=0, keepdims=True), sink)
        p = jnp.exp(st - m)
        denom = jnp.sum(p, axis=0, keepdims=True) + jnp.exp(sink - m)
        pn = (p * (1.0 / denom)).astype(BF16)
        og = lax.dot_general(pn, vg, (((0,), (0,)), ((), ())), preferred_element_type=F32)
        outs.extend(og[h * CHUNK:(h + 1) * CHUNK] for h in range(KV_GROUP))
    att = jnp.concatenate(outs, axis=1)
    o_ref[...] = (att * _rms_scale(att) * g_ref[...]).astype(o_ref.dtype)


def _attention(z, cache_k, cache_v, sinks, g, *, n_prompt_chunks, chunks_per_seq,
               att_width, n_kv, k_col_blk, v_col_blk):
    t = z.shape[0]
    n_items = t // CHUNK
    kvw = n_kv * HEAD_DIM
    npc = n_prompt_chunks

    def hist(d):
        return lambda n: (n - jnp.minimum(d, n % chunks_per_seq), k_col_blk)

    def histv(d):
        return lambda n: (n - jnp.minimum(d, n % chunks_per_seq), v_col_blk)

    cache_map = lambda n: (jnp.maximum(n - npc, 0), 0, 0)
    kern = functools.partial(_attn_kernel, n_prompt_chunks=npc,
                             chunks_per_seq=chunks_per_seq, n_kv=n_kv)
    sink_rows = jnp.repeat(sinks.reshape(n_kv, KV_GROUP), CHUNK, axis=1)
    return pl.pallas_call(
        kern,
        grid=(n_items,),
        in_specs=[
            pl.BlockSpec((CHUNK, att_width), lambda n: (n, 0)),
            pl.BlockSpec((CHUNK, kvw), hist(2)),
            pl.BlockSpec((CHUNK, kvw), hist(1)),
            pl.BlockSpec((CHUNK, kvw), hist(0)),
            pl.BlockSpec((CHUNK, kvw), histv(2)),
            pl.BlockSpec((CHUNK, kvw), histv(1)),
            pl.BlockSpec((CHUNK, kvw), histv(0)),
            pl.BlockSpec((1, WIN_CHUNKS * CHUNK, kvw), cache_map),
            pl.BlockSpec((1, WIN_CHUNKS * CHUNK, kvw), cache_map),
            pl.BlockSpec((n_kv, KV_GROUP * CHUNK), lambda n: (0, 0)),
            pl.BlockSpec((1, att_width), lambda n: (0, 0)),
        ],
        out_specs=pl.BlockSpec((CHUNK, att_width), lambda n: (n, 0)),
        out_shape=jax.ShapeDtypeStruct((t, att_width), BF16),
        compiler_params=_params(1),
        name="attention",
    )(z, z, z, z, z, z, z, cache_k, cache_v, sink_rows, g)


SUBLANES = 8


def _lru_kernel(xb_ref, yb_ref, sconv_ref, sh_ref, cw_ref, cb_ref, wa_ref, ba_ref,
                wx_ref, bx_ref, lam_ref, g_ref, o_ref, hl_ref, hist_ref, hcar_ref,
                a_sc, b_sc, h_sc, *, n_prompt_chunks, chunks_per_seq):
    n = pl.program_id(0)
    is_s = n >= n_prompt_chunks
    hist_rows = hist_ref.shape[0]

    @pl.when(jnp.logical_and(jnp.logical_not(is_s), n % chunks_per_seq == 0))
    def _():
        hist_ref[...] = jnp.zeros_like(hist_ref)
        hcar_ref[...] = jnp.zeros_like(hcar_ref)

    @pl.when(is_s)
    def _():
        hist_ref[...] = jnp.zeros_like(hist_ref)
        hist_ref[hist_rows - (CONV_WIDTH - 1):, :] = sconv_ref[0]
        hcar_ref[...] = sh_ref[0]

    x = xb_ref[...]
    rows, width = x.shape
    hist = hist_ref[...]
    row8 = lax.broadcasted_iota(jnp.int32, (hist_rows, width), 0)
    xc = cb_ref[...]
    for j in range(CONV_WIDTH):
        s = CONV_WIDTH - 1 - j
        if s == 0:
            xs = x
        else:
            xr = pltpu.roll(x, s, 0)
            head = jnp.where(row8 < s, pltpu.roll(hist, s, 0), xr[:hist_rows])
            xs = jnp.concatenate([head, xr[hist_rows:]], axis=0)
        xc = xc + xs * cw_ref[j:j + 1, :]
    hist_ref[...] = x[rows - hist_rows:]

    xcb = xc.astype(BF16)
    ra, rx = [], []
    for nb in range(width // LRU_BLOCK):
        blk = xcb[:, nb * LRU_BLOCK:(nb + 1) * LRU_BLOCK]
        ra.append(jnp.dot(blk, wa_ref[nb], preferred_element_type=F32))
        rx.append(jnp.dot(blk, wx_ref[nb], preferred_element_type=F32))
    r = jax.nn.sigmoid(jnp.concatenate(ra, axis=1) + ba_ref[...])
    ig = jax.nn.sigmoid(jnp.concatenate(rx, axis=1) + bx_ref[...])
    log_a = (-LRU_C * r) * jax.nn.softplus(-lam_ref[...])
    a = jnp.exp(log_a)
    b = jnp.sqrt(-jnp.tanh(log_a) * (1.0 + a * a)) * ig * xc

    seg_len = rows // SUBLANES
    h_blocks, carries = [], []
    for nb in range(width // LANES):
        lanes = slice(nb * LANES, (nb + 1) * LANES)
        a_sc[nb] = a[:, lanes]
        b_sc[nb] = b[:, lanes]
        hloc, ploc = [], []
        for k in range(seg_len):
            a_k = a_sc[nb, pl.ds(k, SUBLANES, stride=seg_len), :]
            b_k = b_sc[nb, pl.ds(k, SUBLANES, stride=seg_len), :]
            hloc.append(b_k if k == 0 else a_k * hloc[-1] + b_k)
            ploc.append(a_k if k == 0 else a_k * ploc[-1])
        carry = hcar_ref[:, lanes]
        seg_in = []
        for s in range(SUBLANES):
            seg_in.append(carry)
            carry = ploc[-1][s:s + 1] * carry + hloc[-1][s:s + 1]
        seg_in = jnp.concatenate(seg_in, axis=0)
        carries.append(carry)
        for k in range(seg_len):
            h_sc[nb, pl.ds(k, SUBLANES, stride=seg_len), :] = hloc[k] + ploc[k] * seg_in
        h_blocks.append(h_sc[nb])
    h = jnp.concatenate(h_blocks, axis=1)
    h_last = jnp.concatenate(carries, axis=1)
    hcar_ref[...] = h_last
    hl_ref[0] = h_last

    y = jax.nn.gelu(yb_ref[...]) * h
    o_ref[...] = (y * _rms_scale(y) * g_ref[...]).astype(o_ref.dtype)


def _rglru(z, state_conv, state_h, conv_w, conv_b, wa, ba, wx, bx, lam, g,
           *, n_prompt_chunks, chunks_per_seq, width, xb_col_blk, yb_col_blk):
    t = z.shape[0]
    n_items = t // CHUNK
    npc = n_prompt_chunks
    nblk = width // LRU_BLOCK
    state_map = lambda n: (jnp.maximum(n - npc, 0), 0, 0)
    full2 = lambda n: (0, 0)
    full3 = lambda n: (0, 0, 0)
    kern = functools.partial(_lru_kernel, n_prompt_chunks=npc, chunks_per_seq=chunks_per_seq)
    return pl.pallas_call(
        kern,
        grid=(n_items,),
        in_specs=[
            pl.BlockSpec((CHUNK, width), lambda n: (n, xb_col_blk)),
            pl.BlockSpec((CHUNK, width), lambda n: (n, yb_col_blk)),
            pl.BlockSpec((1, CONV_WIDTH - 1, width), state_map),
            pl.BlockSpec((1, 1, width), state_map),
            pl.BlockSpec((CONV_WIDTH, width), full2),
            pl.BlockSpec((1, width), full2),
            pl.BlockSpec((nblk, LRU_BLOCK, LRU_BLOCK), full3),
            pl.BlockSpec((1, width), full2),
            pl.BlockSpec((nblk, LRU_BLOCK, LRU_BLOCK), full3),
            pl.BlockSpec((1, width), full2),
            pl.BlockSpec((1, width), full2),
            pl.BlockSpec((1, width), full2),
        ],
        out_specs=[
            pl.BlockSpec((CHUNK, width), lambda n: (n, 0)),
            pl.BlockSpec((1, 1, width), lambda n: (n, 0, 0)),
        ],
        out_shape=[
            jax.ShapeDtypeStruct((t, width), BF16),
            jax.ShapeDtypeStruct((n_items, 1, width), F32),
        ],
        scratch_shapes=[
            pltpu.VMEM((SUBLANES, width), F32),
            pltpu.VMEM((1, width), F32),
            pltpu.VMEM((width // LANES, CHUNK, LANES), F32),
            pltpu.VMEM((width // LANES, CHUNK, LANES), F32),
            pltpu.VMEM((width // LANES, CHUNK, LANES), F32),
        ],
        compiler_params=_params(1),
        name="rglru",
    )(z, z, state_conv, state_h, conv_w, conv_b, wa, ba, wx, bx, lam, g)


def _outproj_kernel(att_ref, lru_ref, wa_ref, wl_ref, xp_ref, xs_ref, g_ref, wr_ref, br_ref,
                    o_ref, route_ref, cnt_ref, ssq_ref, lg_ref, *, n_prompt_tiles, d_model):
    i = pl.program_id(0)
    j = pl.program_id(1)
    acc = jnp.dot(att_ref[...], wa_ref[...].astype(BF16), preferred_element_type=F32)
    acc = acc + jnp.dot(lru_ref[...], wl_ref[...].astype(BF16), preferred_element_type=F32)
    x2 = jnp.where(i < n_prompt_tiles, xp_ref[...], xs_ref[...]) + acc
    o_ref[...] = x2

    @pl.when(jnp.logical_and(i == 0, j == 0))
    def _():
        cnt_ref[...] = jnp.zeros_like(cnt_ref)

    @pl.when(j == 0)
    def _():
        ssq_ref[...] = jnp.zeros_like(ssq_ref)
        lg_ref[...] = jnp.zeros_like(lg_ref)

    ssq_ref[...] += jnp.sum(x2 * x2, axis=-1, keepdims=True)
    lg_ref[...] += jnp.dot((x2 * g_ref[...]).astype(BF16), wr_ref[...],
                           preferred_element_type=F32)

    @pl.when(j == pl.num_programs(1) - 1)
    def _():
        scale = lax.rsqrt(ssq_ref[...] * (1.0 / d_model) + EPS)
        logits = lg_ref[...] * scale + br_ref[...]
        for r0 in range(0, logits.shape[0], ROUTE_ROWS):
            route_ref[r0:r0 + ROUTE_ROWS, :] = _route_rows(logits[r0:r0 + ROUTE_ROWS], cnt_ref)


def _out_proj(att, lru, w, xp, xs, g, wr, br, *, tm, tn):
    t, aw = att.shape
    lw = lru.shape[1]
    d = w.shape[1]
    assert aw == lw and w.shape[0] == aw + lw
    npt = xp.shape[0] // tm
    kern = functools.partial(_outproj_kernel, n_prompt_tiles=npt, d_model=d)
    return pl.pallas_call(
        kern,
        grid=(t // tm, d // tn),
        in_specs=[
            pl.BlockSpec((tm, aw), lambda i, j: (i, 0)),
            pl.BlockSpec((tm, lw), lambda i, j: (i, 0)),
            pl.BlockSpec((aw, tn), lambda i, j: (0, j)),
            pl.BlockSpec((lw, tn), lambda i, j: (1, j)),
            pl.BlockSpec((tm, tn), lambda i, j: (jnp.minimum(i, npt - 1), j)),
            pl.BlockSpec((tm, tn), lambda i, j: (jnp.maximum(i - npt, 0), j)),
            pl.BlockSpec((1, tn), lambda i, j: (0, j)),
            pl.BlockSpec((tn, LANES), lambda i, j: (j, 0)),
            pl.BlockSpec((1, LANES), lambda i, j: (0, 0)),
        ],
        out_specs=[
            pl.BlockSpec((tm, tn), lambda i, j: (i, j)),
            pl.BlockSpec((tm, LANES), lambda i, j: (i, 0)),
            pl.BlockSpec((1, LANES), lambda i, j: (0, 0)),
        ],
        out_shape=[
            jax.ShapeDtypeStruct((t, d), F32),
            jax.ShapeDtypeStruct((t, LANES), F32),
            jax.ShapeDtypeStruct((1, LANES), F32),
        ],
        scratch_shapes=[pltpu.VMEM((tm, 1), F32), pltpu.VMEM((tm, LANES), F32)],
        compiler_params=_params(2),
        name="out_proj",
    )(att, lru, w, w, xp, xs, g, wr, br)


ROUTE_GROUP_LANE = EXPERTS_PER_GROUP
ROUTE_RANK_LANE = EXPERTS_PER_GROUP + 1
ROUTE_ROWS = 256


def _route_rows(logits, cnt_ref):
    lane = lax.broadcasted_iota(jnp.int32, logits.shape, 1).astype(F32)
    ninf = -jnp.inf

    def first_argmax(v, vmax):
        return jnp.min(jnp.where(v == vmax, lane, float(LANES)), axis=-1, keepdims=True)

    gl = jnp.where(lane < N_GROUPS, logits, ninf)
    gm = jnp.max(gl, axis=-1, keepdims=True)
    g_idx = first_argmax(gl, gm)
    g_w = 1.0 / jnp.sum(jnp.exp(gl - gm), axis=-1, keepdims=True)

    lo = N_GROUPS + EXPERTS_PER_GROUP * g_idx
    el = jnp.where(jnp.logical_and(lane >= lo, lane < lo + EXPERTS_PER_GROUP), logits, ninf)
    v1 = jnp.max(el, axis=-1, keepdims=True)
    i1 = first_argmax(el, v1)
    el2 = jnp.where(lane == i1, ninf, el)
    v2 = jnp.max(el2, axis=-1, keepdims=True)
    i2 = first_argmax(el2, v2)
    e2 = jnp.exp(v2 - v1)
    w1 = (1.0 / (1.0 + e2)) * g_w
    w2 = (e2 / (1.0 + e2)) * g_w
    gates = jnp.where(lane == i1 - lo, w1, 0.0) + jnp.where(lane == i2 - lo, w2, 0.0)

    rows = logits.shape[0]
    onehot = jnp.where(lane == g_idx, 1.0, 0.0)
    r_i = lax.broadcasted_iota(jnp.int32, (rows, rows), 0)
    c_i = lax.broadcasted_iota(jnp.int32, (rows, rows), 1)
    tri = jnp.where(c_i < r_i, 1.0, 0.0).astype(BF16)
    before = jnp.dot(tri, onehot.astype(BF16), preferred_element_type=F32) + cnt_ref[...]
    rank = jnp.sum(onehot * before, axis=-1, keepdims=True)
    cnt_ref[...] += jnp.sum(onehot, axis=0, keepdims=True)
    return (gates + jnp.where(lane == ROUTE_GROUP_LANE, g_idx, 0.0)
            + jnp.where(lane == ROUTE_RANK_LANE, rank, 0.0))


def _routing_tables(route, counts, *, tile, n_tiles_max, n_halves):
    t = route.shape[0]
    i32 = jnp.int32
    g = route[:, ROUTE_GROUP_LANE].astype(i32)
    rank = route[:, ROUTE_RANK_LANE].astype(i32)
    cnt = counts[0, :N_GROUPS].astype(i32)
    tiles_g = (cnt + tile - 1) // tile
    gids = jnp.arange(N_GROUPS, dtype=i32)
    tend = jnp.sum(jnp.where(gids[None, :] <= gids[:, None], tiles_g[None, :], 0), axis=1)
    tstart = tend - tiles_g
    n_used = tend[-1]
    dest = tstart[g] * tile + rank
    src = jnp.zeros((n_tiles_max * tile,), i32).at[dest].set(
        jnp.arange(t, dtype=i32), unique_indices=True, mode="promise_in_bounds")
    owns = tiles_g > 0
    later = gids[None, :] > gids[:, None]
    groups_before = jnp.sum(jnp.where(later.T & owns[None, :], 1, 0), axis=1)
    next_group = jnp.min(jnp.where(later & owns[None, :], gids[None, :], N_GROUPS), axis=1)
    next_group = jnp.where(next_group < N_GROUPS, next_group, -1)

    def items(per_tile):
        n_items = per_tile * n_used
        w_all = jnp.arange(per_tile * n_tiles_max, dtype=i32)
        w = jnp.minimum(w_all, n_items - 1)
        gi = jnp.sum((w[:, None] >= per_tile * tend[None, :]).astype(i32), axis=1)
        local = w - per_tile * tstart[gi]
        spare = w_all - n_items
        part = jnp.where(spare < 0, local // tiles_g[gi], spare % per_tile)
        p = jnp.where(spare < 0, tstart[gi] + local % tiles_g[gi], n_used + spare // per_tile)
        valid = spare < 0
        wkey = gi * per_tile + part
        first = valid & (local % tiles_g[gi] == 0)
        run_no = per_tile * groups_before[gi] + part
        nxt_key = jnp.where(part + 1 < per_tile, wkey + 1,
                            jnp.where(next_group[gi] >= 0, next_group[gi] * per_tile, -1))
        run = jnp.stack([first.astype(i32), run_no % 2, nxt_key], axis=1).reshape(-1)
        return p, part, wkey, n_items.reshape(1), run

    up_p, up_e, up_key, up_n, up_run = items(EXPERTS_PER_GROUP // MOE_UP_EXPERTS)
    dn_p, dn_h, dn_key, dn_n, dn_run = items(n_halves)
    return dict(dest=dest, src=src, n_used=n_used.reshape(1),
                up=(up_p, up_e, up_key, up_n, up_run),
                down=(dn_p, dn_h, dn_key, dn_n, dn_run))


def _row_copy(src_hbm, dst_vmem, src_row, dst_row, sem):
    return pltpu.make_async_copy(src_hbm.at[pl.ds(src_row, 1)], dst_vmem.at[pl.ds(dst_row, 1)], sem)


def _start_row_gather(idx_ref, base, n_rows, pairs):
    def body(r, carry):
        i = idx_ref[base + r]
        for src_hbm, dst_vmem, sem in pairs:
            _row_copy(src_hbm, dst_vmem, i, r, sem).start()
        return carry
    lax.fori_loop(0, n_rows, body, 0, unroll=8)


def _start_rows(idx_ref, idx_base, row0, n_rows, pairs):
    for k in range(n_rows):
        i = idx_ref[idx_base + row0 + k]
        for src_hbm, dst_vmem, sem in pairs:
            _row_copy(src_hbm, dst_vmem, i, row0 + k, sem).start()


def _norm_tile_and_prefetch(load_rows, g_ref, o_ref, n_rows, has_next, idx_ref, next_base,
                            next_pairs):
    @pl.when(has_next)
    def _():
        _rmsnorm_rows(load_rows, g_ref, o_ref, n_rows,
                      per_trip=lambda row0: _start_rows(idx_ref, next_base, row0, NORM_ROWS,
                                                        next_pairs))

    @pl.when(jnp.logical_not(has_next))
    def _():
        _rmsnorm_rows(load_rows, g_ref, o_ref, n_rows)


def _wait_row_gather(n_rows, pairs):
    for src_hbm, dst_vmem, sem in pairs:
        pltpu.make_async_copy(src_hbm.at[pl.ds(0, n_rows)], dst_vmem, sem).wait()


def _dispatch_kernel(src_ref, nused_ref, x2_hbm, route_hbm, g_ref, xs_ref, gs_ref,
                     xbuf, gbuf, sem):
    p = pl.program_id(0)
    n_used = nused_ref[0]
    tile = xbuf.shape[1]
    slot = p % 2

    def pairs(s):
        return [(x2_hbm, xbuf.at[s], sem.at[0, s]), (route_hbm, gbuf.at[s], sem.at[1, s])]

    @pl.when(p == 0)
    def _():
        _start_row_gather(src_ref, 0, tile, pairs(0))

    @pl.when(p < n_used)
    def _():
        _wait_row_gather(tile, pairs(slot))
        gs_ref[...] = gbuf[slot]
        _norm_tile_and_prefetch(lambda rows: xbuf[slot, rows, :], g_ref, xs_ref, tile,
                                p + 1 < n_used, src_ref, (p + 1) * tile, pairs(1 - slot))

    @pl.when(p >= n_used)
    def _():
        xs_ref[...] = jnp.zeros_like(xs_ref)
        gs_ref[...] = jnp.zeros_like(gs_ref)


def _dispatch(x2, route, g, src, n_used, *, tile, n_tiles_max):
    t, d = x2.shape
    tile_map = lambda p, src_ref, n_ref: (p, 0)
    grid_spec = pltpu.PrefetchScalarGridSpec(
        num_scalar_prefetch=2,
        grid=(n_tiles_max,),
        in_specs=[
            pl.BlockSpec(memory_space=pl.ANY),
            pl.BlockSpec(memory_space=pl.ANY),
            pl.BlockSpec((1, d), lambda p, s, n: (0, 0)),
        ],
        out_specs=[
            pl.BlockSpec((tile, d), tile_map),
            pl.BlockSpec((tile, LANES), tile_map),
        ],
        scratch_shapes=[
            pltpu.VMEM((2, tile, d), F32),
            pltpu.VMEM((2, tile, LANES), F32),
            pltpu.SemaphoreType.DMA((2, 2)),
        ],
    )
    return pl.pallas_call(
        _dispatch_kernel,
        grid_spec=grid_spec,
        out_shape=[
            jax.ShapeDtypeStruct((n_tiles_max * tile, d), BF16),
            jax.ShapeDtypeStruct((n_tiles_max * tile, LANES), F32),
        ],
        compiler_params=_params(1),
        name="moe_dispatch",
    )(src, n_used, x2, route, g)


def _stream_run_weights(w, n_items, key_ref, run_ref, copies):
    is_first = run_ref[3 * w]
    slot = run_ref[3 * w + 1]
    nxt_key = run_ref[3 * w + 2]
    valid = w < n_items

    @pl.when(jnp.logical_and(valid, w == 0))
    def _():
        for cp in copies(key_ref[0], 0):
            cp.start()

    @pl.when(jnp.logical_and(valid, is_first == 1))
    def _():
        for cp in copies(key_ref[w], slot):
            cp.wait()

        @pl.when(nxt_key >= 0)
        def _():
            for cp in copies(nxt_key, 1 - slot):
                cp.start()

    return slot


def _moe_up_kernel(ip_ref, ie_ref, key_ref, n_ref, run_ref, xs_ref, gs_ref, w1_hbm, w3_hbm,
                   hid_ref, w1_buf, w3_buf, sem):
    w = pl.program_id(0)

    n_e = w1_buf.shape[1]
    f = w1_buf.shape[3]

    def copies(key, slot):
        experts = pl.ds(key * n_e, n_e)
        return [pltpu.make_async_copy(w1_hbm.at[experts], w1_buf.at[slot], sem.at[0, slot]),
                pltpu.make_async_copy(w3_hbm.at[experts], w3_buf.at[slot], sem.at[1, slot])]

    slot = _stream_run_weights(w, n_ref[0], key_ref, run_ref, copies)

    @pl.when(w < n_ref[0])
    def _():
        x = xs_ref[...]
        lane = lax.broadcasted_iota(jnp.int32, gs_ref.shape, 1)
        for e in range(n_e):
            a = jnp.dot(x, w1_buf[slot, e].astype(BF16), preferred_element_type=F32)
            b = jnp.dot(x, w3_buf[slot, e].astype(BF16), preferred_element_type=F32)
            in_group = ie_ref[w] * n_e + e
            gate = jnp.sum(jnp.where(lane == in_group, gs_ref[...], 0.0), axis=-1, keepdims=True)
            hid_ref[:, e * f:(e + 1) * f] = (jax.nn.silu(a) * b * gate).astype(hid_ref.dtype)

    @pl.when(w >= n_ref[0])
    def _():
        hid_ref[...] = jnp.zeros_like(hid_ref)


def _moe_up(xs, gs, w1, w3, items, *, tile):
    rows, d = xs.shape
    f = w1.shape[2]
    ip, ie, key, n_items, run = items
    grid_spec = pltpu.PrefetchScalarGridSpec(
        num_scalar_prefetch=5,
        grid=(ip.shape[0],),
        in_specs=[
            pl.BlockSpec((tile, d), lambda w, ip, ie, key, n, run: (ip[w], 0)),
            pl.BlockSpec((tile, LANES), lambda w, ip, ie, key, n, run: (ip[w], 0)),
            pl.BlockSpec(memory_space=pl.ANY),
            pl.BlockSpec(memory_space=pl.ANY),
        ],
        out_specs=pl.BlockSpec((tile, MOE_UP_EXPERTS * f),
                               lambda w, ip, ie, key, n, run: (ip[w], ie[w])),
        scratch_shapes=[
            pltpu.VMEM((2, MOE_UP_EXPERTS, d, f), w1.dtype),
            pltpu.VMEM((2, MOE_UP_EXPERTS, d, f), w3.dtype),
            pltpu.SemaphoreType.DMA((2, 2)),
        ],
    )
    return pl.pallas_call(
        _moe_up_kernel,
        grid_spec=grid_spec,
        out_shape=jax.ShapeDtypeStruct((rows, EXPERTS_PER_GROUP * f), BF16),
        compiler_params=_params(1),
        name="moe_up",
    )(ip, ie, key, n_items, run, xs, gs, w1, w3)


def _moe_down_kernel(ip_ref, ih_ref, key_ref, n_ref, run_ref, hid_ref, w2_hbm, y_ref,
                     w2_buf, sem, *, n_halves):
    w = pl.program_id(0)
    tn = w2_buf.shape[2]

    def copies(key, slot):
        g = key // n_halves
        col = pl.multiple_of((key % n_halves) * tn, tn)
        return [pltpu.make_async_copy(w2_hbm.at[g, :, pl.ds(col, tn)], w2_buf.at[slot],
                                      sem.at[slot])]

    slot = _stream_run_weights(w, n_ref[0], key_ref, run_ref, copies)

    @pl.when(w < n_ref[0])
    def _():
        y_ref[...] = jnp.dot(hid_ref[...], w2_buf[slot].astype(BF16),
                             preferred_element_type=F32)

    @pl.when(w >= n_ref[0])
    def _():
        y_ref[...] = jnp.zeros_like(y_ref)


def _moe_down(hid, w2g, items, *, tile, n_halves):
    rows, k = hid.shape
    d = w2g.shape[2]
    tn = d // n_halves
    ip, ih, key, n_items, run = items
    grid_spec = pltpu.PrefetchScalarGridSpec(
        num_scalar_prefetch=5,
        grid=(ip.shape[0],),
        in_specs=[
            pl.BlockSpec((tile, k), lambda w, ip, ih, key, n, run: (ip[w], 0)),
            pl.BlockSpec(memory_space=pl.ANY),
        ],
        out_specs=pl.BlockSpec((tile, tn), lambda w, ip, ih, key, n, run: (ip[w], ih[w])),
        scratch_shapes=[
            pltpu.VMEM((2, k, tn), w2g.dtype),
            pltpu.SemaphoreType.DMA((2,)),
        ],
    )
    return pl.pallas_call(
        functools.partial(_moe_down_kernel, n_halves=n_halves),
        grid_spec=grid_spec,
        out_shape=jax.ShapeDtypeStruct((rows, d), F32),
        compiler_params=_params(1),
        name="moe_down",
    )(ip, ih, key, n_items, run, hid, w2g)


def _final_kernel(dest_ref, x_ref, ys_hbm, g_ref, o_ref, ybuf, sem, *, tile0):
    i = pl.program_id(0)
    tm = ybuf.shape[1]
    slot = i % 2

    def pairs(s):
        return [(ys_hbm, ybuf.at[s], sem.at[s])]

    @pl.when(i == 0)
    def _():
        _start_row_gather(dest_ref, tile0 * tm, tm, pairs(0))

    _wait_row_gather(tm, pairs(slot))
    _norm_tile_and_prefetch(lambda rows: x_ref[rows, :] + ybuf[slot, rows, :], g_ref, o_ref, tm,
                            i + 1 < pl.num_programs(0), dest_ref, (tile0 + i + 1) * tm,
                            pairs(1 - slot))


def _final(x2, ys, dest, g, *, row0, n_rows, tm):
    d = x2.shape[1]
    off = row0 // tm
    grid_spec = pltpu.PrefetchScalarGridSpec(
        num_scalar_prefetch=1,
        grid=(n_rows // tm,),
        in_specs=[
            pl.BlockSpec((tm, d), lambda i, dest: (i + off, 0)),
            pl.BlockSpec(memory_space=pl.ANY),
            pl.BlockSpec((1, d), lambda i, dest: (0, 0)),
        ],
        out_specs=pl.BlockSpec((tm, d), lambda i, dest: (i, 0)),
        scratch_shapes=[pltpu.VMEM((2, tm, d), F32), pltpu.SemaphoreType.DMA((2,))],
    )
    return pl.pallas_call(
        functools.partial(_final_kernel, tile0=off),
        grid_spec=grid_spec,
        out_shape=jax.ShapeDtypeStruct((n_rows, d), F32),
        compiler_params=_params(1),
        name="final_norm",
    )(dest, x2, ys, g)


def _rope_tables(pos):
    half = ROT_DIM // 2
    inv = ROPE_THETA ** (-np.arange(half, dtype=np.float64) / half)
    ang = np.asarray(pos, np.float64)[:, None] * inv[None, :]
    cos, sin = np.cos(ang), np.sin(ang)
    ones = np.ones((ang.shape[0], HEAD_DIM - ROT_DIM))
    zeros = np.zeros((ang.shape[0], HEAD_DIM - half))
    c = np.concatenate([cos, cos, ones], axis=1)
    s_hi = np.concatenate([-sin, zeros], axis=1)
    s_lo = np.concatenate([np.zeros_like(sin), sin, 0.0 * ones], axis=1)
    rep = LANES // HEAD_DIM
    return tuple(jnp.asarray(np.tile(t, (1, rep)), F32) for t in (c, s_hi, s_lo))


def kernel(x_prompt, x_sample, cache_k, cache_v, state_conv, state_h, norm_mix, w_in, conv_w,
           conv_b, w_gate_a, b_gate_a, w_gate_x, b_gate_x, lru_lambda, sinks, norm_attn_out,
           norm_lru_out, w_out, norm_ffn, w_group, b_group, w_expert_router, b_expert_router,
           w1, w3, w2, norm_final):
    batch, seq, d_model = x_prompt.shape
    dec_batch, dec_seq, _ = x_sample.shape
    depth = w_in.shape[0]
    assert depth == 1 and dec_seq == CHUNK and seq % CHUNK == 0
    n_heads = sinks.shape[1]
    att_width = n_heads * HEAD_DIM
    n_kv = cache_k.shape[3]
    kv_cols = n_kv * HEAD_DIM
    lru_width = lru_lambda.shape[1]
    past_len = 1024
    cw = cache_k.shape[2]
    assert cw == WIN_CHUNKS * CHUNK
    n_p, n_s = batch * seq, dec_batch * dec_seq
    chunks_per_seq = seq // CHUNK
    n_prompt_chunks = n_p // CHUNK

    xp = x_prompt.reshape(n_p, d_model)
    xs = x_sample.reshape(n_s, d_model)

    w_r = w_in[0]
    tn = 2 * kv_cols
    assert tn == 512 and att_width % tn == 0 and lru_width % tn == 0
    n_q_tiles = att_width // tn
    kv_tile = n_q_tiles

    tm_in = 1024
    assert seq % tm_in == 0 and n_s % tm_in == 0 and tm_in % dec_seq == 0
    pos = np.concatenate([np.arange(seq), np.tile(past_len + np.arange(dec_seq), tm_in // dec_seq)])
    c_tab, shi_tab, slo_tab = _rope_tables(pos)

    z = _in_proj(xp, xs, norm_mix, w_r, c_tab, shi_tab, slo_tab, tm=tm_in, tn=tn,
                 n_q_tiles=n_q_tiles, kv_tile=kv_tile, prompt_tiles_per_seq=seq // tm_in)

    k_col = att_width + 2 * lru_width
    v_col = k_col + kv_cols
    att = _attention(
        z, cache_k[0].reshape(dec_batch, cw, kv_cols), cache_v[0].reshape(dec_batch, cw, kv_cols),
        sinks[0], norm_attn_out,
        n_prompt_chunks=n_prompt_chunks, chunks_per_seq=chunks_per_seq, att_width=att_width,
        n_kv=n_kv, k_col_blk=k_col // kv_cols, v_col_blk=v_col // kv_cols)

    nblk = lru_width // LRU_BLOCK
    lru, h_tiles = _rglru(
        z, state_conv[0], state_h[0].reshape(dec_batch, 1, lru_width), conv_w[0], conv_b,
        w_gate_a[0].astype(BF16), b_gate_a[0].reshape(1, lru_width),
        w_gate_x[0].astype(BF16), b_gate_x[0].reshape(1, lru_width),
        lru_lambda, norm_lru_out,
        n_prompt_chunks=n_prompt_chunks, chunks_per_seq=chunks_per_seq, width=lru_width,
        xb_col_blk=att_width // lru_width, yb_col_blk=att_width // lru_width + 1)
    del nblk

    n_routes = N_GROUPS + N_GROUPS * EXPERTS_PER_GROUP
    wr = jnp.concatenate([w_group[0], w_expert_router[0],
                          jnp.zeros((d_model, LANES - n_routes), F32)], axis=1).astype(BF16)
    br = jnp.concatenate([b_group[0], b_expert_router[0],
                          jnp.zeros((LANES - n_routes,), F32)]).reshape(1, LANES)
    x2, route, counts = _out_proj(att, lru, w_out[0], xp, xs, norm_ffn, wr, br, tm=1024, tn=512)

    n_tiles_max = (n_p + n_s) // MOE_TILE + N_GROUPS
    tabs = _routing_tables(route, counts, tile=MOE_TILE, n_tiles_max=n_tiles_max,
                           n_halves=MOE_DOWN_HALVES)
    xsort, gsort = _dispatch(x2, route, norm_ffn, tabs["src"], tabs["n_used"],
                             tile=MOE_TILE, n_tiles_max=n_tiles_max)
    hid = _moe_up(xsort, gsort, w1[0], w3[0], tabs["up"], tile=MOE_TILE)
    d_expert = w2.shape[2]
    w2g = w2[0].reshape(N_GROUPS, EXPERTS_PER_GROUP * d_expert, d_model)
    ysort = _moe_down(hid, w2g, tabs["down"], tile=MOE_TILE, n_halves=MOE_DOWN_HALVES)

    g_fin = norm_final.reshape(1, d_model)
    y_prompt = _final(x2, ysort, tabs["dest"], g_fin, row0=0, n_rows=n_p,
                      tm=MOE_TILE).reshape(batch, seq, d_model)
    y_sample = _final(x2, ysort, tabs["dest"], g_fin, row0=n_p, n_rows=n_s,
                      tm=MOE_TILE).reshape(dec_batch, dec_seq, d_model)

    win = min(WIN_CHUNKS * CHUNK, seq)

    def prompt_tail(col, width, rows):
        return jnp.stack([lax.slice(z, ((b + 1) * seq - rows, col), ((b + 1) * seq, col + width))
                          for b in range(batch)])

    def sample_rows(col, width):
        return lax.slice(z, (n_p, col), (n_p + n_s, col + width)).reshape(dec_batch, dec_seq, width)

    k_prompt = prompt_tail(k_col, kv_cols, win).reshape(1, batch, win, n_kv, HEAD_DIM)
    v_prompt = prompt_tail(v_col, kv_cols, win).reshape(1, batch, win, n_kv, HEAD_DIM)
    conv_prompt = prompt_tail(att_width, lru_width, CONV_WIDTH - 1)[None]
    h_prompt = h_tiles[:n_prompt_chunks, 0].reshape(batch, chunks_per_seq, lru_width)[:, -1][None]

    ks = sample_rows(k_col, kv_cols).reshape(dec_batch, dec_seq, n_kv, HEAD_DIM)
    vs = sample_rows(v_col, kv_cols).reshape(dec_batch, dec_seq, n_kv, HEAD_DIM)
    k_sample = jnp.concatenate([cache_k[0], ks], axis=1)[:, -cw:][None]
    v_sample = jnp.concatenate([cache_v[0], vs], axis=1)[:, -cw:][None]
    xs_rows = sample_rows(att_width, lru_width)[:, dec_seq - (CONV_WIDTH - 1):]
    conv_sample = jnp.concatenate([state_conv[0], xs_rows], axis=1)[:, -(CONV_WIDTH - 1):][None]
    h_sample = h_tiles[n_prompt_chunks:, 0][None]

    return (y_prompt, y_sample, k_prompt, v_prompt, conv_prompt, h_prompt,
            k_sample, v_sample, conv_sample, h_sample)
```

```python
import functools

import numpy as np
import jax
import jax.numpy as jnp
from jax import lax
from jax.experimental import pallas as pl
from jax.experimental.pallas import tpu as pltpu

F32 = jnp.float32
BF16 = jnp.bfloat16

CHUNK = 64
HEAD_DIM = 64
KV_GROUP = 8
WIN_CHUNKS = 2
ROT_DIM = 16
ROPE_THETA = 500000.0
LRU_BLOCK = 128
CONV_WIDTH = 4
LRU_C = 8.0
N_GROUPS = 8
EXPERTS_PER_GROUP = 4
EPS = 1e-6
NEG = -1e30
LANES = 128
INPROJ_ROW_BLOCK = 128
MOE_TILE = 256
MOE_UP_EXPERTS = 2
MOE_DOWN_HALVES = 1
VMEM_LIMIT = 56 * 1024 * 1024


def _params(n_axes, vmem=VMEM_LIMIT):
    return pltpu.CompilerParams(
        dimension_semantics=("arbitrary",) * n_axes, vmem_limit_bytes=vmem)


def _rms_scale(x):
    return lax.rsqrt(jnp.mean(x * x, axis=-1, keepdims=True) + EPS)


NORM_ROWS = 16


def _rmsnorm_rows(load_rows, g_ref, o_ref, n_rows, per_trip=None):
    def body(r, carry):
        row0 = r * NORM_ROWS if isinstance(r, int) else pl.multiple_of(r * NORM_ROWS, NORM_ROWS)
        rows = pl.ds(row0, NORM_ROWS)
        x = load_rows(rows)
        o_ref[rows, :] = (x * _rms_scale(x) * g_ref[...]).astype(o_ref.dtype)
        if per_trip is not None:
            per_trip(row0)
        return carry
    if per_trip is not None:
        for r in range(n_rows // NORM_ROWS):
            body(r, 0)
    else:
        lax.fori_loop(0, n_rows // NORM_ROWS, body, 0, unroll=4)


def _rope_block(zb, c, s_hi, s_lo):
    return (zb * c + pltpu.roll(zb, LANES - ROT_DIM // 2, 1) * s_hi
            + pltpu.roll(zb, ROT_DIM // 2, 1) * s_lo)


def _inproj_kernel(xp_hbm, xs_hbm, g_ref, w_ref, c_ref, shi_ref, slo_ref, o_ref, x_buf, h_ref,
                   sem, *, n_prompt_tiles, n_q_tiles, kv_tile):
    i = pl.program_id(0)
    j = pl.program_id(1)
    tm, tn = o_ref.shape

    def x_tile_copy(src_hbm, tile):
        return pltpu.make_async_copy(src_hbm.at[pl.ds(tile * tm, tm)], x_buf, sem)

    def start_x(tile):
        @pl.when(tile < n_prompt_tiles)
        def _():
            x_tile_copy(xp_hbm, tile).start()

        @pl.when(tile >= n_prompt_tiles)
        def _():
            x_tile_copy(xs_hbm, tile - n_prompt_tiles).start()

    @pl.when(jnp.logical_and(i == 0, j == 0))
    def _():
        start_x(0)

    @pl.when(j == 0)
    def _():
        x_tile_copy(xp_hbm, 0).wait()
        _rmsnorm_rows(lambda rows: x_buf[rows, :], g_ref, h_ref, tm)

    @pl.when(jnp.logical_and(j == 1, i + 1 < pl.num_programs(0)))
    def _():
        start_x(i + 1)

    w = w_ref[...].astype(BF16)
    n_blk = tn // LANES
    for r0 in range(0, tm, INPROJ_ROW_BLOCK):
        rows = slice(r0, r0 + INPROJ_ROW_BLOCK)
        z = jnp.dot(h_ref[rows, :], w, preferred_element_type=F32)
        c, shi, slo = c_ref[rows, :], shi_ref[rows, :], slo_ref[rows, :]
        for b in range(n_blk):
            is_rope = j < n_q_tiles
            if b < n_blk // 2:
                is_rope = jnp.logical_or(is_rope, j == kv_tile)
            zb = z[:, b * LANES:(b + 1) * LANES]
            o_ref[rows, b * LANES:(b + 1) * LANES] = _rope_block(
                zb, jnp.where(is_rope, c, 1.0), jnp.where(is_rope, shi, 0.0),
                jnp.where(is_rope, slo, 0.0))


def _in_proj(xp, xs, g, w, c, shi, slo, *, tm, tn, n_q_tiles, kv_tile, prompt_tiles_per_seq):
    n_p, d = xp.shape
    n_s = xs.shape[0]
    t = n_p + n_s
    npt = n_p // tm
    n_cols = w.shape[1]
    kern = functools.partial(_inproj_kernel, n_prompt_tiles=npt, n_q_tiles=n_q_tiles,
                             kv_tile=kv_tile)
    n_tiles = n_cols // tn
    assert kv_tile == n_q_tiles

    def table_tile(i, j):
        return jnp.where(i < npt, i % prompt_tiles_per_seq, prompt_tiles_per_seq), 0

    def out_tile(i, j):
        return i, jnp.where(j < kv_tile, j, jnp.where(j == kv_tile, n_tiles - 1, j - 1))

    return pl.pallas_call(
        kern,
        grid=(t // tm, n_tiles),
        in_specs=[
            pl.BlockSpec(memory_space=pl.ANY),
            pl.BlockSpec(memory_space=pl.ANY),
            pl.BlockSpec((1, d), lambda i, j: (0, 0)),
            pl.BlockSpec((d, tn), lambda i, j: (0, j)),
            pl.BlockSpec((tm, LANES), table_tile),
            pl.BlockSpec((tm, LANES), table_tile),
            pl.BlockSpec((tm, LANES), table_tile),
        ],
        out_specs=pl.BlockSpec((tm, tn), out_tile),
        out_shape=jax.ShapeDtypeStruct((t, n_cols), F32),
        scratch_shapes=[pltpu.VMEM((tm, d), xp.dtype), pltpu.VMEM((tm, d), BF16),
                        pltpu.SemaphoreType.DMA(())],
        compiler_params=_params(2),
        name="in_proj",
    )(xp, xs, g, w, c, shi, slo)


ATTN_CHUNKS = 2


def _attn_chunk(q, k_band, v_band, lo, sink_ref, n_kv):
    n_keys = k_band.shape[0]
    key = lax.broadcasted_iota(jnp.int32, (n_keys, KV_GROUP * CHUNK), 0)
    valid = key >= lo
    outs = []
    for gi in range(n_kv):
        kg = k_band[:, gi * HEAD_DIM:(gi + 1) * HEAD_DIM]
        vg = v_band[:, gi * HEAD_DIM:(gi + 1) * HEAD_DIM]
        heads = [gi * KV_GROUP + h for h in range(KV_GROUP)]
        qg = jnp.concatenate([q[:, h * HEAD_DIM:(h + 1) * HEAD_DIM] for h in heads], axis=0)
        st = lax.dot_general(kg, qg, (((1,), (1,)), ((), ())), preferred_element_type=F32)
        st = jnp.where(valid, st, NEG)
        sink = sink_ref[gi:gi + 1, :]
        m = jnp.maximum(jnp.max(st, axis=0, keepdims=True), sink)
        p = jnp.exp(st - m)
        denom = jnp.sum(p, axis=0, keepdims=True) + jnp.exp(sink - m)
        pn = (p * (1.0 / denom)).astype(BF16)
        og = lax.dot_general(pn, vg, (((0,), (0,)), ((), ())), preferred_element_type=F32)
        outs.extend(og[h * CHUNK:(h + 1) * CHUNK] for h in range(KV_GROUP))
    return jnp.concatenate(outs, axis=1)


def _attn_kernel(q_ref, kp_ref, k_ref, vp_ref, v_ref, kc_ref, vc_ref, sink_ref, g_ref, o_ref,
                 *, n_prompt_tiles, tiles_per_seq, n_kv):
    n = pl.program_id(0)
    is_s = n >= n_prompt_tiles
    has_prev = jnp.logical_or(is_s, n % tiles_per_seq > 0)

    def band(prev_ref, own_ref, cache_ref, c):
        prev, own, cache = prev_ref[...], own_ref[...], cache_ref[c]
        hist = jnp.concatenate([prev, own], axis=0)[c * CHUNK:(c + WIN_CHUNKS) * CHUNK]
        hist = jnp.where(is_s, cache, hist)
        return jnp.concatenate([hist, own[c * CHUNK:(c + 1) * CHUNK]], axis=0).astype(BF16)

    q = (q_ref[...] * (HEAD_DIM ** -0.5)).astype(BF16)
    for c in range(ATTN_CHUNKS):
        lo = jnp.where(has_prev, 0, (WIN_CHUNKS - c) * CHUNK)
        att = _attn_chunk(q[c * CHUNK:(c + 1) * CHUNK], band(kp_ref, k_ref, kc_ref, c),
                          band(vp_ref, v_ref, vc_ref, c), lo, sink_ref, n_kv)
        o_ref[c * CHUNK:(c + 1) * CHUNK, :] = (
            att * _rms_scale(att) * g_ref[...]).astype(o_ref.dtype)


def _attention(z, cache_k, cache_v, sinks, g, *, n_prompt_chunks, chunks_per_seq,
               att_width, n_kv, k_col_blk, v_col_blk):
    t = z.shape[0]
    rows = ATTN_CHUNKS * CHUNK
    assert ATTN_CHUNKS == WIN_CHUNKS and chunks_per_seq % ATTN_CHUNKS == 0
    assert cache_k.shape[0] % ATTN_CHUNKS == 0
    n_tiles = t // rows
    npt = n_prompt_chunks // ATTN_CHUNKS
    tps = chunks_per_seq // ATTN_CHUNKS
    kvw = n_kv * HEAD_DIM

    def prev(col):
        return lambda n: (n - jnp.minimum(1, n % tps), col)

    cache_map = lambda n: (jnp.maximum(n - npt, 0), 0, 0)
    kern = functools.partial(_attn_kernel, n_prompt_tiles=npt, tiles_per_seq=tps, n_kv=n_kv)
    sink_rows = jnp.repeat(sinks.reshape(n_kv, KV_GROUP), CHUNK, axis=1)
    return pl.pallas_call(
        kern,
        grid=(n_tiles,),
        in_specs=[
            pl.BlockSpec((rows, att_width), lambda n: (n, 0)),
            pl.BlockSpec((rows, kvw), prev(k_col_blk)),
            pl.BlockSpec((rows, kvw), lambda n: (n, k_col_blk)),
            pl.BlockSpec((rows, kvw), prev(v_col_blk)),
            pl.BlockSpec((rows, kvw), lambda n: (n, v_col_blk)),
            pl.BlockSpec((ATTN_CHUNKS, WIN_CHUNKS * CHUNK, kvw), cache_map),
            pl.BlockSpec((ATTN_CHUNKS, WIN_CHUNKS * CHUNK, kvw), cache_map),
            pl.BlockSpec((n_kv, KV_GROUP * CHUNK), lambda n: (0, 0)),
            pl.BlockSpec((1, att_width), lambda n: (0, 0)),
        ],
        out_specs=pl.BlockSpec((rows, att_width), lambda n: (n, 0)),
        out_shape=jax.ShapeDtypeStruct((t, att_width), BF16),
        compiler_params=_params(1),
        name="attention",
    )(z, z, z, z, z, cache_k, cache_v, sink_rows, g)


SUBLANES = 8


def _lru_kernel(xb_ref, yb_ref, sconv_ref, sh_ref, cw_ref, cb_ref, wa_ref, ba_ref,
                wx_ref, bx_ref, lam_ref, g_ref, o_ref, hl_ref, hist_ref, hcar_ref,
                a_sc, b_sc, h_sc, *, n_prompt_chunks, chunks_per_seq):
    n = pl.program_id(0)
    is_s = n >= n_prompt_chunks
    hist_rows = hist_ref.shape[0]

    @pl.when(jnp.logical_and(jnp.logical_not(is_s), n % chunks_per_seq == 0))
    def _():
        hist_ref[...] = jnp.zeros_like(hist_ref)
        hcar_ref[...] = jnp.zeros_like(hcar_ref)

    @pl.when(is_s)
    def _():
        hist_ref[...] = jnp.zeros_like(hist_ref)
        hist_ref[hist_rows - (CONV_WIDTH - 1):, :] = sconv_ref[0]
        hcar_ref[...] = sh_ref[0]

    x = xb_ref[...]
    rows, width = x.shape
    hist = hist_ref[...]
    row8 = lax.broadcasted_iota(jnp.int32, (hist_rows, width), 0)
    xc = cb_ref[...]
    for j in range(CONV_WIDTH):
        s = CONV_WIDTH - 1 - j
        if s == 0:
            xs = x
        else:
            xr = pltpu.roll(x, s, 0)
            head = jnp.where(row8 < s, pltpu.roll(hist, s, 0), xr[:hist_rows])
            xs = jnp.concatenate([head, xr[hist_rows:]], axis=0)
        xc = xc + xs * cw_ref[j:j + 1, :]
    hist_ref[...] = x[rows - hist_rows:]

    xcb = xc.astype(BF16)
    ra, rx = [], []
    for nb in range(width // LRU_BLOCK):
        blk = xcb[:, nb * LRU_BLOCK:(nb + 1) * LRU_BLOCK]
        ra.append(jnp.dot(blk, wa_ref[nb], preferred_element_type=F32))
        rx.append(jnp.dot(blk, wx_ref[nb], preferred_element_type=F32))
    r = jax.nn.sigmoid(jnp.concatenate(ra, axis=1) + ba_ref[...])
    ig = jax.nn.sigmoid(jnp.concatenate(rx, axis=1) + bx_ref[...])
    log_a = (-LRU_C * r) * jax.nn.softplus(-lam_ref[...])
    a = jnp.exp(log_a)
    b = jnp.sqrt(-jnp.tanh(log_a) * (1.0 + a * a)) * ig * xc

    seg_len = rows // SUBLANES
    h_blocks, carries = [], []
    for nb in range(width // LANES):
        lanes = slice(nb * LANES, (nb + 1) * LANES)
        a_sc[nb] = a[:, lanes]
        b_sc[nb] = b[:, lanes]
        hloc, ploc = [], []
        for k in range(seg_len):
            a_k = a_sc[nb, pl.ds(k, SUBLANES, stride=seg_len), :]
            b_k = b_sc[nb, pl.ds(k, SUBLANES, stride=seg_len), :]
            hloc.append(b_k if k == 0 else a_k * hloc[-1] + b_k)
            ploc.append(a_k if k == 0 else a_k * ploc[-1])
        carry = hcar_ref[:, lanes]
        seg_in = []
        for s in range(SUBLANES):
            seg_in.append(carry)
            carry = ploc[-1][s:s + 1] * carry + hloc[-1][s:s + 1]
        seg_in = jnp.concatenate(seg_in, axis=0)
        carries.append(carry)
        for k in range(seg_len):
            h_sc[nb, pl.ds(k, SUBLANES, stride=seg_len), :] = hloc[k] + ploc[k] * seg_in
        h_blocks.append(h_sc[nb])
    h = jnp.concatenate(h_blocks, axis=1)
    h_last = jnp.concatenate(carries, axis=1)
    hcar_ref[...] = h_last
    hl_ref[0] = h_last

    y = jax.nn.gelu(yb_ref[...]) * h
    o_ref[...] = (y * _rms_scale(y) * g_ref[...]).astype(o_ref.dtype)


def _rglru(z, state_conv, state_h, conv_w, conv_b, wa, ba, wx, bx, lam, g,
           *, n_prompt_chunks, chunks_per_seq, width, xb_col_blk, yb_col_blk):
    t = z.shape[0]
    n_items = t // CHUNK
    npc = n_prompt_chunks
    nblk = width // LRU_BLOCK
    state_map = lambda n: (jnp.maximum(n - npc, 0), 0, 0)
    full2 = lambda n: (0, 0)
    full3 = lambda n: (0, 0, 0)
    kern = functools.partial(_lru_kernel, n_prompt_chunks=npc, chunks_per_seq=chunks_per_seq)
    return pl.pallas_call(
        kern,
        grid=(n_items,),
        in_specs=[
            pl.BlockSpec((CHUNK, width), lambda n: (n, xb_col_blk)),
            pl.BlockSpec((CHUNK, width), lambda n: (n, yb_col_blk)),
            pl.BlockSpec((1, CONV_WIDTH - 1, width), state_map),
            pl.BlockSpec((1, 1, width), state_map),
            pl.BlockSpec((CONV_WIDTH, width), full2),
            pl.BlockSpec((1, width), full2),
            pl.BlockSpec((nblk, LRU_BLOCK, LRU_BLOCK), full3),
            pl.BlockSpec((1, width), full2),
            pl.BlockSpec((nblk, LRU_BLOCK, LRU_BLOCK), full3),
            pl.BlockSpec((1, width), full2),
            pl.BlockSpec((1, width), full2),
            pl.BlockSpec((1, width), full2),
        ],
        out_specs=[
            pl.BlockSpec((CHUNK, width), lambda n: (n, 0)),
            pl.BlockSpec((1, 1, width), lambda n: (n, 0, 0)),
        ],
        out_shape=[
            jax.ShapeDtypeStruct((t, width), BF16),
            jax.ShapeDtypeStruct((n_items, 1, width), F32),
        ],
        scratch_shapes=[
            pltpu.VMEM((SUBLANES, width), F32),
            pltpu.VMEM((1, width), F32),
            pltpu.VMEM((width // LANES, CHUNK, LANES), F32),
            pltpu.VMEM((width // LANES, CHUNK, LANES), F32),
            pltpu.VMEM((width // LANES, CHUNK, LANES), F32),
        ],
        compiler_params=_params(1),
        name="rglru",
    )(z, z, state_conv, state_h, conv_w, conv_b, wa, ba, wx, bx, lam, g)


def _outproj_kernel(att_ref, lru_ref, wa_ref, wl_ref, xp_ref, xs_ref, g_ref, wr_ref, br_ref,
                    o_ref, route_ref, cnt_ref, meta_ref, ssq_ref, lg_ref,
                    *, n_prompt_tiles, d_model):
    i = pl.program_id(0)
    j = pl.program_id(1)
    acc = jnp.dot(att_ref[...], wa_ref[...].astype(BF16), preferred_element_type=F32)
    acc = acc + jnp.dot(lru_ref[...], wl_ref[...].astype(BF16), preferred_element_type=F32)
    x2 = jnp.where(i < n_prompt_tiles, xp_ref[...], xs_ref[...]) + acc
    o_ref[...] = x2

    @pl.when(jnp.logical_and(i == 0, j == 0))
    def _():
        cnt_ref[...] = jnp.zeros_like(cnt_ref)

    @pl.when(j == 0)
    def _():
        ssq_ref[...] = jnp.zeros_like(ssq_ref)
        lg_ref[...] = jnp.zeros_like(lg_ref)

    ssq_ref[...] += jnp.sum(x2 * x2, axis=-1, keepdims=True)
    lg_ref[...] += jnp.dot((x2 * g_ref[...]).astype(BF16), wr_ref[...],
                           preferred_element_type=F32)

    @pl.when(j == pl.num_programs(1) - 1)
    def _():
        scale = lax.rsqrt(ssq_ref[...] * (1.0 / d_model) + EPS)
        logits = lg_ref[...] * scale + br_ref[...]
        for r0 in range(0, logits.shape[0], ROUTE_ROWS):
            route = _route_rows(logits[r0:r0 + ROUTE_ROWS], cnt_ref)
            route_ref[r0:r0 + ROUTE_ROWS, :] = route
            meta_ref[:, r0:r0 + ROUTE_ROWS] = route.T[ROUTE_GROUP_LANE:ROUTE_RANK_LANE + 1, :]


def _out_proj(att, lru, w, xp, xs, g, wr, br, *, tm, tn):
    t, aw = att.shape
    lw = lru.shape[1]
    d = w.shape[1]
    assert aw == lw and w.shape[0] == aw + lw
    npt = xp.shape[0] // tm
    kern = functools.partial(_outproj_kernel, n_prompt_tiles=npt, d_model=d)
    return pl.pallas_call(
        kern,
        grid=(t // tm, d // tn),
        in_specs=[
            pl.BlockSpec((tm, aw), lambda i, j: (i, 0)),
            pl.BlockSpec((tm, lw), lambda i, j: (i, 0)),
            pl.BlockSpec((aw, tn), lambda i, j: (0, j)),
            pl.BlockSpec((lw, tn), lambda i, j: (1, j)),
            pl.BlockSpec((tm, tn), lambda i, j: (jnp.minimum(i, npt - 1), j)),
            pl.BlockSpec((tm, tn), lambda i, j: (jnp.maximum(i - npt, 0), j)),
            pl.BlockSpec((1, tn), lambda i, j: (0, j)),
            pl.BlockSpec((tn, LANES), lambda i, j: (j, 0)),
            pl.BlockSpec((1, LANES), lambda i, j: (0, 0)),
        ],
        out_specs=[
            pl.BlockSpec((tm, tn), lambda i, j: (i, j)),
            pl.BlockSpec((tm, LANES), lambda i, j: (i, 0)),
            pl.BlockSpec((1, LANES), lambda i, j: (0, 0)),
            pl.BlockSpec((2, tm), lambda i, j: (0, i)),
        ],
        out_shape=[
            jax.ShapeDtypeStruct((t, d), F32),
            jax.ShapeDtypeStruct((t, LANES), F32),
            jax.ShapeDtypeStruct((1, LANES), F32),
            jax.ShapeDtypeStruct((2, t), F32),
        ],
        scratch_shapes=[pltpu.VMEM((tm, 1), F32), pltpu.VMEM((tm, LANES), F32)],
        compiler_params=_params(2),
        name="out_proj",
    )(att, lru, w, w, xp, xs, g, wr, br)


ROUTE_GROUP_LANE = EXPERTS_PER_GROUP
ROUTE_RANK_LANE = EXPERTS_PER_GROUP + 1
ROUTE_ROWS = 256


def _route_rows(logits, cnt_ref):
    lane = lax.broadcasted_iota(jnp.int32, logits.shape, 1).astype(F32)
    ninf = -jnp.inf

    def first_argmax(v, vmax):
        return jnp.min(jnp.where(v == vmax, lane, float(LANES)), axis=-1, keepdims=True)

    gl = jnp.where(lane < N_GROUPS, logits, ninf)
    gm = jnp.max(gl, axis=-1, keepdims=True)
    g_idx = first_argmax(gl, gm)
    g_w = 1.0 / jnp.sum(jnp.exp(gl - gm), axis=-1, keepdims=True)

    lo = N_GROUPS + EXPERTS_PER_GROUP * g_idx
    el = jnp.where(jnp.logical_and(lane >= lo, lane < lo + EXPERTS_PER_GROUP), logits, ninf)
    v1 = jnp.max(el, axis=-1, keepdims=True)
    i1 = first_argmax(el, v1)
    el2 = jnp.where(lane == i1, ninf, el)
    v2 = jnp.max(el2, axis=-1, keepdims=True)
    i2 = first_argmax(el2, v2)
    e2 = jnp.exp(v2 - v1)
    w1 = (1.0 / (1.0 + e2)) * g_w
    w2 = (e2 / (1.0 + e2)) * g_w
    gates = jnp.where(lane == i1 - lo, w1, 0.0) + jnp.where(lane == i2 - lo, w2, 0.0)

    rows = logits.shape[0]
    onehot = jnp.where(lane == g_idx, 1.0, 0.0)
    r_i = lax.broadcasted_iota(jnp.int32, (rows, rows), 0)
    c_i = lax.broadcasted_iota(jnp.int32, (rows, rows), 1)
    tri = jnp.where(c_i < r_i, 1.0, 0.0).astype(BF16)
    before = jnp.dot(tri, onehot.astype(BF16), preferred_element_type=F32) + cnt_ref[...]
    rank = jnp.sum(onehot * before, axis=-1, keepdims=True)
    cnt_ref[...] += jnp.sum(onehot, axis=0, keepdims=True)
    return (gates + jnp.where(lane == ROUTE_GROUP_LANE, g_idx, 0.0)
            + jnp.where(lane == ROUTE_RANK_LANE, rank, 0.0))


def _routing_tables(meta, counts, *, tile, n_tiles_max, n_halves):
    t = meta.shape[1]
    i32 = jnp.int32
    g = meta[0].astype(i32)
    rank = meta[1].astype(i32)
    cnt = counts[0, :N_GROUPS].astype(i32)
    tiles_g = (cnt + tile - 1) // tile
    gids = jnp.arange(N_GROUPS, dtype=i32)
    tend = jnp.sum(jnp.where(gids[None, :] <= gids[:, None], tiles_g[None, :], 0), axis=1)
    tstart = tend - tiles_g
    n_used = tend[-1]
    dest = tstart[g] * tile + rank
    src = jnp.zeros((n_tiles_max * tile,), i32).at[dest].set(
        jnp.arange(t, dtype=i32), unique_indices=True, mode="promise_in_bounds")
    owns = tiles_g > 0
    later = gids[None, :] > gids[:, None]
    groups_before = jnp.sum(jnp.where(later.T & owns[None, :], 1, 0), axis=1)
    next_group = jnp.min(jnp.where(later & owns[None, :], gids[None, :], N_GROUPS), axis=1)
    next_group = jnp.where(next_group < N_GROUPS, next_group, -1)

    def items(per_tile):
        n_items = per_tile * n_used
        w_all = jnp.arange(per_tile * n_tiles_max, dtype=i32)
        w = jnp.minimum(w_all, n_items - 1)
        gi = jnp.sum((w[:, None] >= per_tile * tend[None, :]).astype(i32), axis=1)
        local = w - per_tile * tstart[gi]
        spare = w_all - n_items
        part = jnp.where(spare < 0, local // tiles_g[gi], spare % per_tile)
        p = jnp.where(spare < 0, tstart[gi] + local % tiles_g[gi], n_used + spare // per_tile)
        valid = spare < 0
        wkey = gi * per_tile + part
        first = valid & (local % tiles_g[gi] == 0)
        run_no = per_tile * groups_before[gi] + part
        nxt_key = jnp.where(part + 1 < per_tile, wkey + 1,
                            jnp.where(next_group[gi] >= 0, next_group[gi] * per_tile, -1))
        run = jnp.stack([first.astype(i32), run_no % 2, nxt_key], axis=1).reshape(-1)
        return p, part, wkey, n_items.reshape(1), run

    up_p, up_e, up_key, up_n, up_run = items(EXPERTS_PER_GROUP // MOE_UP_EXPERTS)
    dn_p, dn_h, dn_key, dn_n, dn_run = items(n_halves)
    return dict(dest=dest, src=src, n_used=n_used.reshape(1),
                up=(up_p, up_e, up_key, up_n, up_run),
                down=(dn_p, dn_h, dn_key, dn_n, dn_run))


def _row_copy(src_hbm, dst_vmem, src_row, dst_row, sem):
    return pltpu.make_async_copy(src_hbm.at[pl.ds(src_row, 1)], dst_vmem.at[pl.ds(dst_row, 1)], sem)


def _start_row_gather(idx_ref, base, n_rows, pairs):
    def body(r, carry):
        i = idx_ref[base + r]
        for src_hbm, dst_vmem, sem in pairs:
            _row_copy(src_hbm, dst_vmem, i, r, sem).start()
        return carry
    lax.fori_loop(0, n_rows, body, 0, unroll=8)


def _start_rows(idx_ref, idx_base, row0, n_rows, pairs):
    for k in range(n_rows):
        i = idx_ref[idx_base + row0 + k]
        for src_hbm, dst_vmem, sem in pairs:
            _row_copy(src_hbm, dst_vmem, i, row0 + k, sem).start()


def _norm_tile_and_prefetch(load_rows, g_ref, o_ref, n_rows, has_next, idx_ref, next_base,
                            next_pairs):
    @pl.when(has_next)
    def _():
        _rmsnorm_rows(load_rows, g_ref, o_ref, n_rows,
                      per_trip=lambda row0: _start_rows(idx_ref, next_base, row0, NORM_ROWS,
                                                        next_pairs))

    @pl.when(jnp.logical_not(has_next))
    def _():
        _rmsnorm_rows(load_rows, g_ref, o_ref, n_rows)


def _wait_row_gather(n_rows, pairs):
    for src_hbm, dst_vmem, sem in pairs:
        pltpu.make_async_copy(src_hbm.at[pl.ds(0, n_rows)], dst_vmem, sem).wait()


def _dispatch_kernel(src_ref, nused_ref, x2_hbm, route_hbm, g_ref, xs_ref, gs_ref,
                     xbuf, gbuf, sem):
    p = pl.program_id(0)
    n_used = nused_ref[0]
    tile = xbuf.shape[1]
    slot = p % 2

    def pairs(s):
        return [(x2_hbm, xbuf.at[s], sem.at[0, s]), (route_hbm, gbuf.at[s], sem.at[1, s])]

    @pl.when(p == 0)
    def _():
        _start_row_gather(src_ref, 0, tile, pairs(0))

    @pl.when(p < n_used)
    def _():
        _wait_row_gather(tile, pairs(slot))
        gs_ref[...] = gbuf[slot]
        _norm_tile_and_prefetch(lambda rows: xbuf[slot, rows, :], g_ref, xs_ref, tile,
                                p + 1 < n_used, src_ref, (p + 1) * tile, pairs(1 - slot))

    @pl.when(p >= n_used)
    def _():
        xs_ref[...] = jnp.zeros_like(xs_ref)
        gs_ref[...] = jnp.zeros_like(gs_ref)


def _dispatch(x2, route, g, src, n_used, *, tile, n_tiles_max):
    t, d = x2.shape
    tile_map = lambda p, src_ref, n_ref: (p, 0)
    grid_spec = pltpu.PrefetchScalarGridSpec(
        num_scalar_prefetch=2,
        grid=(n_tiles_max,),
        in_specs=[
            pl.BlockSpec(memory_space=pl.ANY),
            pl.BlockSpec(memory_space=pl.ANY),
            pl.BlockSpec((1, d), lambda p, s, n: (0, 0)),
        ],
        out_specs=[
            pl.BlockSpec((tile, d), tile_map),
            pl.BlockSpec((tile, LANES), tile_map),
        ],
        scratch_shapes=[
            pltpu.VMEM((2, tile, d), F32),
            pltpu.VMEM((2, tile, LANES), F32),
            pltpu.SemaphoreType.DMA((2, 2)),
        ],
    )
    return pl.pallas_call(
        _dispatch_kernel,
        grid_spec=grid_spec,
        out_shape=[
            jax.ShapeDtypeStruct((n_tiles_max * tile, d), BF16),
            jax.ShapeDtypeStruct((n_tiles_max * tile, LANES), F32),
        ],
        compiler_params=_params(1),
        name="moe_dispatch",
    )(src, n_used, x2, route, g)


def _stream_run_weights(w, n_items, key_ref, run_ref, copies):
    is_first = run_ref[3 * w]
    slot = run_ref[3 * w + 1]
    nxt_key = run_ref[3 * w + 2]
    valid = w < n_items

    @pl.when(jnp.logical_and(valid, w == 0))
    def _():
        for cp in copies(key_ref[0], 0):
            cp.start()

    @pl.when(jnp.logical_and(valid, is_first == 1))
    def _():
        for cp in copies(key_ref[w], slot):
            cp.wait()

        @pl.when(nxt_key >= 0)
        def _():
            for cp in copies(nxt_key, 1 - slot):
                cp.start()

    return slot


def _moe_up_kernel(ip_ref, ie_ref, key_ref, n_ref, run_ref, xs_ref, gs_ref, w1_hbm, w3_hbm,
                   hid_ref, w1_buf, w3_buf, sem):
    w = pl.program_id(0)

    n_e = w1_buf.shape[1]
    f = w1_buf.shape[3]

    def copies(key, slot):
        experts = pl.ds(key * n_e, n_e)
        return [pltpu.make_async_copy(w1_hbm.at[experts], w1_buf.at[slot], sem.at[0, slot]),
                pltpu.make_async_copy(w3_hbm.at[experts], w3_buf.at[slot], sem.at[1, slot])]

    slot = _stream_run_weights(w, n_ref[0], key_ref, run_ref, copies)

    @pl.when(w < n_ref[0])
    def _():
        x = xs_ref[...]
        lane = lax.broadcasted_iota(jnp.int32, gs_ref.shape, 1)
        for e in range(n_e):
            a = jnp.dot(x, w1_buf[slot, e].astype(BF16), preferred_element_type=F32)
            b = jnp.dot(x, w3_buf[slot, e].astype(BF16), preferred_element_type=F32)
            in_group = ie_ref[w] * n_e + e
            gate = jnp.sum(jnp.where(lane == in_group, gs_ref[...], 0.0), axis=-1, keepdims=True)
            hid_ref[:, e * f:(e + 1) * f] = (jax.nn.silu(a) * b * gate).astype(hid_ref.dtype)

    @pl.when(w >= n_ref[0])
    def _():
        hid_ref[...] = jnp.zeros_like(hid_ref)


def _moe_up(xs, gs, w1, w3, items, *, tile):
    rows, d = xs.shape
    f = w1.shape[2]
    ip, ie, key, n_items, run = items
    grid_spec = pltpu.PrefetchScalarGridSpec(
        num_scalar_prefetch=5,
        grid=(ip.shape[0],),
        in_specs=[
            pl.BlockSpec((tile, d), lambda w, ip, ie, key, n, run: (ip[w], 0)),
            pl.BlockSpec((tile, LANES), lambda w, ip, ie, key, n, run: (ip[w], 0)),
            pl.BlockSpec(memory_space=pl.ANY),
            pl.BlockSpec(memory_space=pl.ANY),
        ],
        out_specs=pl.BlockSpec((tile, MOE_UP_EXPERTS * f),
                               lambda w, ip, ie, key, n, run: (ip[w], ie[w])),
        scratch_shapes=[
            pltpu.VMEM((2, MOE_UP_EXPERTS, d, f), w1.dtype),
            pltpu.VMEM((2, MOE_UP_EXPERTS, d, f), w3.dtype),
            pltpu.SemaphoreType.DMA((2, 2)),
        ],
    )
    return pl.pallas_call(
        _moe_up_kernel,
        grid_spec=grid_spec,
        out_shape=jax.ShapeDtypeStruct((rows, EXPERTS_PER_GROUP * f), BF16),
        compiler_params=_params(1),
        name="moe_up",
    )(ip, ie, key, n_items, run, xs, gs, w1, w3)


def _moe_down_kernel(ip_ref, ih_ref, key_ref, n_ref, run_ref, hid_ref, w2_hbm, y_ref,
                     w2_buf, sem, *, n_halves):
    w = pl.program_id(0)
    tn = w2_buf.shape[2]

    def copies(key, slot):
        g = key // n_halves
        col = pl.multiple_of((key % n_halves) * tn, tn)
        return [pltpu.make_async_copy(w2_hbm.at[g, :, pl.ds(col, tn)], w2_buf.at[slot],
                                      sem.at[slot])]

    slot = _stream_run_weights(w, n_ref[0], key_ref, run_ref, copies)

    @pl.when(w < n_ref[0])
    def _():
        y_ref[...] = jnp.dot(hid_ref[...], w2_buf[slot].astype(BF16),
                             preferred_element_type=F32)

    @pl.when(w >= n_ref[0])
    def _():
        y_ref[...] = jnp.zeros_like(y_ref)


def _moe_down(hid, w2g, items, *, tile, n_halves):
    rows, k = hid.shape
    d = w2g.shape[2]
    tn = d // n_halves
    ip, ih, key, n_items, run = items
    grid_spec = pltpu.PrefetchScalarGridSpec(
        num_scalar_prefetch=5,
        grid=(ip.shape[0],),
        in_specs=[
            pl.BlockSpec((tile, k), lambda w, ip, ih, key, n, run: (ip[w], 0)),
            pl.BlockSpec(memory_space=pl.ANY),
        ],
        out_specs=pl.BlockSpec((tile, tn), lambda w, ip, ih, key, n, run: (ip[w], ih[w])),
        scratch_shapes=[
            pltpu.VMEM((2, k, tn), w2g.dtype),
            pltpu.SemaphoreType.DMA((2,)),
        ],
    )
    return pl.pallas_call(
        functools.partial(_moe_down_kernel, n_halves=n_halves),
        grid_spec=grid_spec,
        out_shape=jax.ShapeDtypeStruct((rows, d), F32),
        compiler_params=_params(1),
        name="moe_down",
    )(ip, ih, key, n_items, run, hid, w2g)


def _final_kernel(dest_ref, x_ref, ys_hbm, g_ref, o_ref, ybuf, sem, *, tile0):
    i = pl.program_id(0)
    tm = ybuf.shape[1]
    slot = i % 2

    def pairs(s):
        return [(ys_hbm, ybuf.at[s], sem.at[s])]

    @pl.when(i == 0)
    def _():
        _start_row_gather(dest_ref, tile0 * tm, tm, pairs(0))

    _wait_row_gather(tm, pairs(slot))
    _norm_tile_and_prefetch(lambda rows: x_ref[rows, :] + ybuf[slot, rows, :], g_ref, o_ref, tm,
                            i + 1 < pl.num_programs(0), dest_ref, (tile0 + i + 1) * tm,
                            pairs(1 - slot))


def _final(x2, ys, dest, g, *, row0, n_rows, tm):
    d = x2.shape[1]
    off = row0 // tm
    grid_spec = pltpu.PrefetchScalarGridSpec(
        num_scalar_prefetch=1,
        grid=(n_rows // tm,),
        in_specs=[
            pl.BlockSpec((tm, d), lambda i, dest: (i + off, 0)),
            pl.BlockSpec(memory_space=pl.ANY),
            pl.BlockSpec((1, d), lambda i, dest: (0, 0)),
        ],
        out_specs=pl.BlockSpec((tm, d), lambda i, dest: (i, 0)),
        scratch_shapes=[pltpu.VMEM((2, tm, d), F32), pltpu.SemaphoreType.DMA((2,))],
    )
    return pl.pallas_call(
        functools.partial(_final_kernel, tile0=off),
        grid_spec=grid_spec,
        out_shape=jax.ShapeDtypeStruct((n_rows, d), F32),
        compiler_params=_params(1),
        name="final_norm",
    )(dest, x2, ys, g)


def _rope_tables(pos):
    half = ROT_DIM // 2
    inv = ROPE_THETA ** (-np.arange(half, dtype=np.float64) / half)
    ang = np.asarray(pos, np.float64)[:, None] * inv[None, :]
    cos, sin = np.cos(ang), np.sin(ang)
    ones = np.ones((ang.shape[0], HEAD_DIM - ROT_DIM))
    zeros = np.zeros((ang.shape[0], HEAD_DIM - half))
    c = np.concatenate([cos, cos, ones], axis=1)
    s_hi = np.concatenate([-sin, zeros], axis=1)
    s_lo = np.concatenate([np.zeros_like(sin), sin, 0.0 * ones], axis=1)
    rep = LANES // HEAD_DIM
    return tuple(jnp.asarray(np.tile(t, (1, rep)), F32) for t in (c, s_hi, s_lo))


def kernel(x_prompt, x_sample, cache_k, cache_v, state_conv, state_h, norm_mix, w_in, conv_w,
           conv_b, w_gate_a, b_gate_a, w_gate_x, b_gate_x, lru_lambda, sinks, norm_attn_out,
           norm_lru_out, w_out, norm_ffn, w_group, b_group, w_expert_router, b_expert_router,
           w1, w3, w2, norm_final):
    batch, seq, d_model = x_prompt.shape
    dec_batch, dec_seq, _ = x_sample.shape
    depth = w_in.shape[0]
    assert depth == 1 and dec_seq == CHUNK and seq % CHUNK == 0
    n_heads = sinks.shape[1]
    att_width = n_heads * HEAD_DIM
    n_kv = cache_k.shape[3]
    kv_cols = n_kv * HEAD_DIM
    lru_width = lru_lambda.shape[1]
    past_len = 1024
    cw = cache_k.shape[2]
    assert cw == WIN_CHUNKS * CHUNK
    n_p, n_s = batch * seq, dec_batch * dec_seq
    chunks_per_seq = seq // CHUNK
    n_prompt_chunks = n_p // CHUNK

    xp = x_prompt.reshape(n_p, d_model)
    xs = x_sample.reshape(n_s, d_model)

    w_r = w_in[0]
    tn = 2 * kv_cols
    assert tn == 512 and att_width % tn == 0 and lru_width % tn == 0
    n_q_tiles = att_width // tn
    kv_tile = n_q_tiles

    tm_in = 1024
    assert seq % tm_in == 0 and n_s % tm_in == 0 and tm_in % dec_seq == 0
    pos = np.concatenate([np.arange(seq), np.tile(past_len + np.arange(dec_seq), tm_in // dec_seq)])
    c_tab, shi_tab, slo_tab = _rope_tables(pos)

    z = _in_proj(xp, xs, norm_mix, w_r, c_tab, shi_tab, slo_tab, tm=tm_in, tn=tn,
                 n_q_tiles=n_q_tiles, kv_tile=kv_tile, prompt_tiles_per_seq=seq // tm_in)

    k_col = att_width + 2 * lru_width
    v_col = k_col + kv_cols
    att = _attention(
        z, cache_k[0].reshape(dec_batch, cw, kv_cols), cache_v[0].reshape(dec_batch, cw, kv_cols),
        sinks[0], norm_attn_out,
        n_prompt_chunks=n_prompt_chunks, chunks_per_seq=chunks_per_seq, att_width=att_width,
        n_kv=n_kv, k_col_blk=k_col // kv_cols, v_col_blk=v_col // kv_cols)

    nblk = lru_width // LRU_BLOCK
    lru, h_tiles = _rglru(
        z, state_conv[0], state_h[0].reshape(dec_batch, 1, lru_width), conv_w[0], conv_b,
        w_gate_a[0].astype(BF16), b_gate_a[0].reshape(1, lru_width),
        w_gate_x[0].astype(BF16), b_gate_x[0].reshape(1, lru_width),
        lru_lambda, norm_lru_out,
        n_prompt_chunks=n_prompt_chunks, chunks_per_seq=chunks_per_seq, width=lru_width,
        xb_col_blk=att_width // lru_width, yb_col_blk=att_width // lru_width + 1)
    del nblk

    n_routes = N_GROUPS + N_GROUPS * EXPERTS_PER_GROUP
    wr = jnp.concatenate([w_group[0], w_expert_router[0],
                          jnp.zeros((d_model, LANES - n_routes), F32)], axis=1).astype(BF16)
    br = jnp.concatenate([b_group[0], b_expert_router[0],
                          jnp.zeros((LANES - n_routes,), F32)]).reshape(1, LANES)
    x2, route, counts, meta = _out_proj(att, lru, w_out[0], xp, xs, norm_ffn, wr, br, tm=1024, tn=512)

    n_tiles_max = (n_p + n_s) // MOE_TILE + N_GROUPS
    tabs = _routing_tables(meta, counts, tile=MOE_TILE, n_tiles_max=n_tiles_max,
                           n_halves=MOE_DOWN_HALVES)
    xsort, gsort = _dispatch(x2, route, norm_ffn, tabs["src"], tabs["n_used"],
                             tile=MOE_TILE, n_tiles_max=n_tiles_max)
    hid = _moe_up(xsort, gsort, w1[0], w3[0], tabs["up"], tile=MOE_TILE)
    d_expert = w2.shape[2]
    w2g = w2[0].reshape(N_GROUPS, EXPERTS_PER_GROUP * d_expert, d_model)
    ysort = _moe_down(hid, w2g, tabs["down"], tile=MOE_TILE, n_halves=MOE_DOWN_HALVES)

    g_fin = norm_final.reshape(1, d_model)
    y_prompt = _final(x2, ysort, tabs["dest"], g_fin, row0=0, n_rows=n_p,
                      tm=MOE_TILE).reshape(batch, seq, d_model)
    y_sample = _final(x2, ysort, tabs["dest"], g_fin, row0=n_p, n_rows=n_s,
                      tm=MOE_TILE).reshape(dec_batch, dec_seq, d_model)

    win = min(WIN_CHUNKS * CHUNK, seq)

    def prompt_tail(col, width, rows):
        return jnp.stack([lax.slice(z, ((b + 1) * seq - rows, col), ((b + 1) * seq, col + width))
                          for b in range(batch)])

    def sample_rows(col, width):
        return lax.slice(z, (n_p, col), (n_p + n_s, col + width)).reshape(dec_batch, dec_seq, width)

    k_prompt = prompt_tail(k_col, kv_cols, win).reshape(1, batch, win, n_kv, HEAD_DIM)
    v_prompt = prompt_tail(v_col, kv_cols, win).reshape(1, batch, win, n_kv, HEAD_DIM)
    conv_prompt = prompt_tail(att_width, lru_width, CONV_WIDTH - 1)[None]
    h_prompt = h_tiles[:n_prompt_chunks, 0].reshape(batch, chunks_per_seq, lru_width)[:, -1][None]

    ks = sample_rows(k_col, kv_cols).reshape(dec_batch, dec_seq, n_kv, HEAD_DIM)
    vs = sample_rows(v_col, kv_cols).reshape(dec_batch, dec_seq, n_kv, HEAD_DIM)
    k_sample = jnp.concatenate([cache_k[0], ks], axis=1)[:, -cw:][None]
    v_sample = jnp.concatenate([cache_v[0], vs], axis=1)[:, -cw:][None]
    xs_rows = sample_rows(att_width, lru_width)[:, dec_seq - (CONV_WIDTH - 1):]
    conv_sample = jnp.concatenate([state_conv[0], xs_rows], axis=1)[:, -(CONV_WIDTH - 1):][None]
    h_sample = h_tiles[n_prompt_chunks:, 0][None]

    return (y_prompt, y_sample, k_prompt, v_prompt, conv_prompt, h_prompt,
            k_sample, v_sample, conv_sample, h_sample)
```

```python
import functools

import numpy as np
import jax
import jax.numpy as jnp
from jax import lax
from jax.experimental import pallas as pl
from jax.experimental.pallas import tpu as pltpu

F32 = jnp.float32
BF16 = jnp.bfloat16

CHUNK = 64
HEAD_DIM = 64
KV_GROUP = 8
WIN_CHUNKS = 2
ROT_DIM = 16
ROPE_THETA = 500000.0
LRU_BLOCK = 128
CONV_WIDTH = 4
LRU_C = 8.0
N_GROUPS = 8
EXPERTS_PER_GROUP = 4
EPS = 1e-6
NEG = -1e30
LANES = 128
INPROJ_ROW_BLOCK = 128
MOE_TILE = 256
MOE_UP_EXPERTS = 2
MOE_DOWN_HALVES = 1
VMEM_LIMIT = 56 * 1024 * 1024


def _params(n_axes, vmem=VMEM_LIMIT):
    return pltpu.CompilerParams(
        dimension_semantics=("arbitrary",) * n_axes, vmem_limit_bytes=vmem)


def _rms_scale(x):
    return lax.rsqrt(jnp.mean(x * x, axis=-1, keepdims=True) + EPS)


NORM_ROWS = 16


def _rmsnorm_rows(load_rows, g_ref, o_ref, n_rows, per_trip=None):
    def body(r, carry):
        row0 = r * NORM_ROWS if isinstance(r, int) else pl.multiple_of(r * NORM_ROWS, NORM_ROWS)
        rows = pl.ds(row0, NORM_ROWS)
        x = load_rows(rows)
        o_ref[rows, :] = (x * _rms_scale(x) * g_ref[...]).astype(o_ref.dtype)
        if per_trip is not None:
            per_trip(row0)
        return carry
    if per_trip is not None:
        for r in range(n_rows // NORM_ROWS):
            body(r, 0)
    else:
        lax.fori_loop(0, n_rows // NORM_ROWS, body, 0, unroll=4)


def _rope_block(zb, c, s_hi, s_lo):
    return (zb * c + pltpu.roll(zb, LANES - ROT_DIM // 2, 1) * s_hi
            + pltpu.roll(zb, ROT_DIM // 2, 1) * s_lo)


def _inproj_kernel(xp_hbm, xs_hbm, g_ref, w_ref, c_ref, shi_ref, slo_ref, o_ref, x_buf, h_ref,
                   sem, *, n_prompt_tiles, n_q_tiles, kv_tile):
    i = pl.program_id(0)
    j = pl.program_id(1)
    tm, tn = o_ref.shape

    def x_tile_copy(src_hbm, tile):
        return pltpu.make_async_copy(src_hbm.at[pl.ds(tile * tm, tm)], x_buf, sem)

    def start_x(tile):
        @pl.when(tile < n_prompt_tiles)
        def _():
            x_tile_copy(xp_hbm, tile).start()

        @pl.when(tile >= n_prompt_tiles)
        def _():
            x_tile_copy(xs_hbm, tile - n_prompt_tiles).start()

    @pl.when(jnp.logical_and(i == 0, j == 0))
    def _():
        start_x(0)

    @pl.when(j == 0)
    def _():
        x_tile_copy(xp_hbm, 0).wait()
        _rmsnorm_rows(lambda rows: x_buf[rows, :], g_ref, h_ref, tm)

    @pl.when(jnp.logical_and(j == 1, i + 1 < pl.num_programs(0)))
    def _():
        start_x(i + 1)

    w = w_ref[...].astype(BF16)
    n_blk = tn // LANES
    for r0 in range(0, tm, INPROJ_ROW_BLOCK):
        rows = slice(r0, r0 + INPROJ_ROW_BLOCK)
        z = jnp.dot(h_ref[rows, :], w, preferred_element_type=F32)
        c, shi, slo = c_ref[rows, :], shi_ref[rows, :], slo_ref[rows, :]
        for b in range(n_blk):
            is_rope = j < n_q_tiles
            if b < n_blk // 2:
                is_rope = jnp.logical_or(is_rope, j == kv_tile)
            zb = z[:, b * LANES:(b + 1) * LANES]
            o_ref[rows, b * LANES:(b + 1) * LANES] = _rope_block(
                zb, jnp.where(is_rope, c, 1.0), jnp.where(is_rope, shi, 0.0),
                jnp.where(is_rope, slo, 0.0))


def _in_proj(xp, xs, g, w, c, shi, slo, *, tm, tn, n_q_tiles, kv_tile, prompt_tiles_per_seq):
    n_p, d = xp.shape
    n_s = xs.shape[0]
    t = n_p + n_s
    npt = n_p // tm
    n_cols = w.shape[1]
    kern = functools.partial(_inproj_kernel, n_prompt_tiles=npt, n_q_tiles=n_q_tiles,
                             kv_tile=kv_tile)
    n_tiles = n_cols // tn
    assert kv_tile == n_q_tiles

    def table_tile(i, j):
        return jnp.where(i < npt, i % prompt_tiles_per_seq, prompt_tiles_per_seq), 0

    def out_tile(i, j):
        return i, jnp.where(j < kv_tile, j, jnp.where(j == kv_tile, n_tiles - 1, j - 1))

    return pl.pallas_call(
        kern,
        grid=(t // tm, n_tiles),
        in_specs=[
            pl.BlockSpec(memory_space=pl.ANY),
            pl.BlockSpec(memory_space=pl.ANY),
            pl.BlockSpec((1, d), lambda i, j: (0, 0)),
            pl.BlockSpec((d, tn), lambda i, j: (0, j)),
            pl.BlockSpec((tm, LANES), table_tile),
            pl.BlockSpec((tm, LANES), table_tile),
            pl.BlockSpec((tm, LANES), table_tile),
        ],
        out_specs=pl.BlockSpec((tm, tn), out_tile),
        out_shape=jax.ShapeDtypeStruct((t, n_cols), F32),
        scratch_shapes=[pltpu.VMEM((tm, d), xp.dtype), pltpu.VMEM((tm, d), BF16),
                        pltpu.SemaphoreType.DMA(())],
        compiler_params=_params(2),
        name="in_proj",
    )(xp, xs, g, w, c, shi, slo)


ATTN_CHUNKS = 2


def _attn_chunk(q, k_band, v_band, lo, sink_ref, n_kv):
    n_keys = k_band.shape[0]
    key = lax.broadcasted_iota(jnp.int32, (n_keys, KV_GROUP * CHUNK), 0)
    valid = key >= lo
    outs = []
    for gi in range(n_kv):
        kg = k_band[:, gi * HEAD_DIM:(gi + 1) * HEAD_DIM]
        vg = v_band[:, gi * HEAD_DIM:(gi + 1) * HEAD_DIM]
        heads = [gi * KV_GROUP + h for h in range(KV_GROUP)]
        qg = jnp.concatenate([q[:, h * HEAD_DIM:(h + 1) * HEAD_DIM] for h in heads], axis=0)
        st = lax.dot_general(kg, qg, (((1,), (1,)), ((), ())), preferred_element_type=F32)
        st = jnp.where(valid, st, NEG)
        sink = sink_ref[gi:gi + 1, :]
        m = jnp.maximum(jnp.max(st, axis=0, keepdims=True), sink)
        p = jnp.exp(st - m)
        denom = jnp.sum(p, axis=0, keepdims=True) + jnp.exp(sink - m)
        pn = (p * (1.0 / denom)).astype(BF16)
        og = lax.dot_general(pn, vg, (((0,), (0,)), ((), ())), preferred_element_type=F32)
        outs.extend(og[h * CHUNK:(h + 1) * CHUNK] for h in range(KV_GROUP))
    return jnp.concatenate(outs, axis=1)


def _attn_kernel(q_ref, kp_ref, k_ref, vp_ref, v_ref, kc_ref, vc_ref, sink_ref, g_ref, o_ref,
                 *, n_prompt_tiles, tiles_per_seq, n_kv):
    n = pl.program_id(0)
    is_s = n >= n_prompt_tiles
    has_prev = jnp.logical_or(is_s, n % tiles_per_seq > 0)

    def band(prev_ref, own_ref, cache_ref, c):
        prev, own, cache = prev_ref[...], own_ref[...], cache_ref[c]
        hist = jnp.concatenate([prev, own], axis=0)[c * CHUNK:(c + WIN_CHUNKS) * CHUNK]
        hist = jnp.where(is_s, cache, hist)
        return jnp.concatenate([hist, own[c * CHUNK:(c + 1) * CHUNK]], axis=0).astype(BF16)

    q = (q_ref[...] * (HEAD_DIM ** -0.5)).astype(BF16)
    for c in range(ATTN_CHUNKS):
        lo = jnp.where(has_prev, 0, (WIN_CHUNKS - c) * CHUNK)
        att = _attn_chunk(q[c * CHUNK:(c + 1) * CHUNK], band(kp_ref, k_ref, kc_ref, c),
                          band(vp_ref, v_ref, vc_ref, c), lo, sink_ref, n_kv)
        o_ref[c * CHUNK:(c + 1) * CHUNK, :] = (
            att * _rms_scale(att) * g_ref[...]).astype(o_ref.dtype)


def _attention(z, cache_k, cache_v, sinks, g, *, n_prompt_chunks, chunks_per_seq,
               att_width, n_kv, k_col_blk, v_col_blk):
    t = z.shape[0]
    rows = ATTN_CHUNKS * CHUNK
    assert ATTN_CHUNKS == WIN_CHUNKS and chunks_per_seq % ATTN_CHUNKS == 0
    assert cache_k.shape[0] % ATTN_CHUNKS == 0
    n_tiles = t // rows
    npt = n_prompt_chunks // ATTN_CHUNKS
    tps = chunks_per_seq // ATTN_CHUNKS
    kvw = n_kv * HEAD_DIM

    def prev(col):
        return lambda n: (n - jnp.minimum(1, n % tps), col)

    cache_map = lambda n: (jnp.maximum(n - npt, 0), 0, 0)
    kern = functools.partial(_attn_kernel, n_prompt_tiles=npt, tiles_per_seq=tps, n_kv=n_kv)
    sink_rows = jnp.repeat(sinks.reshape(n_kv, KV_GROUP), CHUNK, axis=1)
    return pl.pallas_call(
        kern,
        grid=(n_tiles,),
        in_specs=[
            pl.BlockSpec((rows, att_width), lambda n: (n, 0)),
            pl.BlockSpec((rows, kvw), prev(k_col_blk)),
            pl.BlockSpec((rows, kvw), lambda n: (n, k_col_blk)),
            pl.BlockSpec((rows, kvw), prev(v_col_blk)),
            pl.BlockSpec((rows, kvw), lambda n: (n, v_col_blk)),
            pl.BlockSpec((ATTN_CHUNKS, WIN_CHUNKS * CHUNK, kvw), cache_map),
            pl.BlockSpec((ATTN_CHUNKS, WIN_CHUNKS * CHUNK, kvw), cache_map),
            pl.BlockSpec((n_kv, KV_GROUP * CHUNK), lambda n: (0, 0)),
            pl.BlockSpec((1, att_width), lambda n: (0, 0)),
        ],
        out_specs=pl.BlockSpec((rows, att_width), lambda n: (n, 0)),
        out_shape=jax.ShapeDtypeStruct((t, att_width), BF16),
        compiler_params=_params(1),
        name="attention",
    )(z, z, z, z, z, cache_k, cache_v, sink_rows, g)


SUBLANES = 8


def _lru_kernel(xb_ref, yb_ref, sconv_ref, sh_ref, cw_ref, cb_ref, wa_ref, ba_ref,
                wx_ref, bx_ref, lam_ref, g_ref, o_ref, hl_ref, hist_ref, hcar_ref,
                a_sc, b_sc, h_sc, *, n_prompt_chunks, chunks_per_seq):
    n = pl.program_id(0)
    is_s = n >= n_prompt_chunks
    hist_rows = hist_ref.shape[0]

    @pl.when(jnp.logical_and(jnp.logical_not(is_s), n % chunks_per_seq == 0))
    def _():
        hist_ref[...] = jnp.zeros_like(hist_ref)
        hcar_ref[...] = jnp.zeros_like(hcar_ref)

    @pl.when(is_s)
    def _():
        hist_ref[...] = jnp.zeros_like(hist_ref)
        hist_ref[hist_rows - (CONV_WIDTH - 1):, :] = sconv_ref[0]
        hcar_ref[...] = sh_ref[0]

    x = xb_ref[...]
    rows, width = x.shape
    hist = hist_ref[...]
    row8 = lax.broadcasted_iota(jnp.int32, (hist_rows, width), 0)
    xc = cb_ref[...]
    for j in range(CONV_WIDTH):
        s = CONV_WIDTH - 1 - j
        if s == 0:
            xs = x
        else:
            xr = pltpu.roll(x, s, 0)
            head = jnp.where(row8 < s, pltpu.roll(hist, s, 0), xr[:hist_rows])
            xs = jnp.concatenate([head, xr[hist_rows:]], axis=0)
        xc = xc + xs * cw_ref[j:j + 1, :]
    hist_ref[...] = x[rows - hist_rows:]

    xcb = xc.astype(BF16)
    ra, rx = [], []
    for nb in range(width // LRU_BLOCK):
        blk = xcb[:, nb * LRU_BLOCK:(nb + 1) * LRU_BLOCK]
        ra.append(jnp.dot(blk, wa_ref[nb], preferred_element_type=F32))
        rx.append(jnp.dot(blk, wx_ref[nb], preferred_element_type=F32))
    r = jax.nn.sigmoid(jnp.concatenate(ra, axis=1) + ba_ref[...])
    ig = jax.nn.sigmoid(jnp.concatenate(rx, axis=1) + bx_ref[...])
    log_a = (-LRU_C * r) * jax.nn.softplus(-lam_ref[...])
    a = jnp.exp(log_a)
    b = jnp.sqrt(-jnp.tanh(log_a) * (1.0 + a * a)) * ig * xc

    seg_len = rows // SUBLANES
    h_blocks, carries = [], []
    for nb in range(width // LANES):
        lanes = slice(nb * LANES, (nb + 1) * LANES)
        a_sc[nb] = a[:, lanes]
        b_sc[nb] = b[:, lanes]
        hloc, ploc = [], []
        for k in range(seg_len):
            a_k = a_sc[nb, pl.ds(k, SUBLANES, stride=seg_len), :]
            b_k = b_sc[nb, pl.ds(k, SUBLANES, stride=seg_len), :]
            hloc.append(b_k if k == 0 else a_k * hloc[-1] + b_k)
            ploc.append(a_k if k == 0 else a_k * ploc[-1])
        carry = hcar_ref[:, lanes]
        seg_in = []
        for s in range(SUBLANES):
            seg_in.append(carry)
            carry = ploc[-1][s:s + 1] * carry + hloc[-1][s:s + 1]
        seg_in = jnp.concatenate(seg_in, axis=0)
        carries.append(carry)
        for k in range(seg_len):
            h_sc[nb, pl.ds(k, SUBLANES, stride=seg_len), :] = hloc[k] + ploc[k] * seg_in
        h_blocks.append(h_sc[nb])
    h = jnp.concatenate(h_blocks, axis=1)
    h_last = jnp.concatenate(carries, axis=1)
    hcar_ref[...] = h_last
    hl_ref[0] = h_last

    y = jax.nn.gelu(yb_ref[...]) * h
    o_ref[...] = (y * _rms_scale(y) * g_ref[...]).astype(o_ref.dtype)


def _rglru(z, state_conv, state_h, conv_w, conv_b, wa, ba, wx, bx, lam, g,
           *, n_prompt_chunks, chunks_per_seq, width, xb_col_blk, yb_col_blk):
    t = z.shape[0]
    n_items = t // CHUNK
    npc = n_prompt_chunks
    nblk = width // LRU_BLOCK
    state_map = lambda n: (jnp.maximum(n - npc, 0), 0, 0)
    full2 = lambda n: (0, 0)
    full3 = lambda n: (0, 0, 0)
    kern = functools.partial(_lru_kernel, n_prompt_chunks=npc, chunks_per_seq=chunks_per_seq)
    return pl.pallas_call(
        kern,
        grid=(n_items,),
        in_specs=[
            pl.BlockSpec((CHUNK, width), lambda n: (n, xb_col_blk)),
            pl.BlockSpec((CHUNK, width), lambda n: (n, yb_col_blk)),
            pl.BlockSpec((1, CONV_WIDTH - 1, width), state_map),
            pl.BlockSpec((1, 1, width), state_map),
            pl.BlockSpec((CONV_WIDTH, width), full2),
            pl.BlockSpec((1, width), full2),
            pl.BlockSpec((nblk, LRU_BLOCK, LRU_BLOCK), full3),
            pl.BlockSpec((1, width), full2),
            pl.BlockSpec((nblk, LRU_BLOCK, LRU_BLOCK), full3),
            pl.BlockSpec((1, width), full2),
            pl.BlockSpec((1, width), full2),
            pl.BlockSpec((1, width), full2),
        ],
        out_specs=[
            pl.BlockSpec((CHUNK, width), lambda n: (n, 0)),
            pl.BlockSpec((1, 1, width), lambda n: (n, 0, 0)),
        ],
        out_shape=[
            jax.ShapeDtypeStruct((t, width), BF16),
            jax.ShapeDtypeStruct((n_items, 1, width), F32),
        ],
        scratch_shapes=[
            pltpu.VMEM((SUBLANES, width), F32),
            pltpu.VMEM((1, width), F32),
            pltpu.VMEM((width // LANES, CHUNK, LANES), F32),
            pltpu.VMEM((width // LANES, CHUNK, LANES), F32),
            pltpu.VMEM((width // LANES, CHUNK, LANES), F32),
        ],
        compiler_params=_params(1),
        name="rglru",
    )(z, z, state_conv, state_h, conv_w, conv_b, wa, ba, wx, bx, lam, g)


def _outproj_kernel(att_ref, lru_ref, wa_ref, wl_ref, xp_ref, xs_ref, g_ref, wr_ref, br_ref,
                    o_ref, route_ref, cnt_ref, meta_ref, ssq_ref, lg_ref,
                    *, n_prompt_tiles, d_model):
    i = pl.program_id(0)
    j = pl.program_id(1)
    acc = jnp.dot(att_ref[...], wa_ref[...].astype(BF16), preferred_element_type=F32)
    acc = acc + jnp.dot(lru_ref[...], wl_ref[...].astype(BF16), preferred_element_type=F32)
    x2 = jnp.where(i < n_prompt_tiles, xp_ref[...], xs_ref[...]) + acc
    o_ref[...] = x2

    @pl.when(jnp.logical_and(i == 0, j == 0))
    def _():
        cnt_ref[...] = jnp.zeros_like(cnt_ref)

    @pl.when(j == 0)
    def _():
        ssq_ref[...] = jnp.zeros_like(ssq_ref)
        lg_ref[...] = jnp.zeros_like(lg_ref)

    ssq_ref[...] += jnp.sum(x2 * x2, axis=-1, keepdims=True)
    lg_ref[...] += jnp.dot((x2 * g_ref[...]).astype(BF16), wr_ref[...],
                           preferred_element_type=F32)

    @pl.when(j == pl.num_programs(1) - 1)
    def _():
        scale = lax.rsqrt(ssq_ref[...] * (1.0 / d_model) + EPS)
        logits = lg_ref[...] * scale + br_ref[...]
        for r0 in range(0, logits.shape[0], ROUTE_ROWS):
            route = _route_rows(logits[r0:r0 + ROUTE_ROWS], cnt_ref)
            route_ref[r0:r0 + ROUTE_ROWS, :] = route
            meta_ref[:, r0:r0 + ROUTE_ROWS] = route.T[
                ROUTE_GROUP_LANE:ROUTE_RANK_LANE + 1, :].astype(jnp.int32)


def _out_proj(att, lru, w, xp, xs, g, wr, br, *, tm, tn):
    t, aw = att.shape
    lw = lru.shape[1]
    d = w.shape[1]
    assert aw == lw and w.shape[0] == aw + lw
    npt = xp.shape[0] // tm
    kern = functools.partial(_outproj_kernel, n_prompt_tiles=npt, d_model=d)
    return pl.pallas_call(
        kern,
        grid=(t // tm, d // tn),
        in_specs=[
            pl.BlockSpec((tm, aw), lambda i, j: (i, 0)),
            pl.BlockSpec((tm, lw), lambda i, j: (i, 0)),
            pl.BlockSpec((aw, tn), lambda i, j: (0, j)),
            pl.BlockSpec((lw, tn), lambda i, j: (1, j)),
            pl.BlockSpec((tm, tn), lambda i, j: (jnp.minimum(i, npt - 1), j)),
            pl.BlockSpec((tm, tn), lambda i, j: (jnp.maximum(i - npt, 0), j)),
            pl.BlockSpec((1, tn), lambda i, j: (0, j)),
            pl.BlockSpec((tn, LANES), lambda i, j: (j, 0)),
            pl.BlockSpec((1, LANES), lambda i, j: (0, 0)),
        ],
        out_specs=[
            pl.BlockSpec((tm, tn), lambda i, j: (i, j)),
            pl.BlockSpec((tm, LANES), lambda i, j: (i, 0)),
            pl.BlockSpec((1, LANES), lambda i, j: (0, 0)),
            pl.BlockSpec((2, tm), lambda i, j: (0, i)),
        ],
        out_shape=[
            jax.ShapeDtypeStruct((t, d), F32),
            jax.ShapeDtypeStruct((t, LANES), F32),
            jax.ShapeDtypeStruct((1, LANES), F32),
            jax.ShapeDtypeStruct((2, t), jnp.int32),
        ],
        scratch_shapes=[pltpu.VMEM((tm, 1), F32), pltpu.VMEM((tm, LANES), F32)],
        compiler_params=_params(2),
        name="out_proj",
    )(att, lru, w, w, xp, xs, g, wr, br)


ROUTE_GROUP_LANE = EXPERTS_PER_GROUP
ROUTE_RANK_LANE = EXPERTS_PER_GROUP + 1
ROUTE_ROWS = 256


def _route_rows(logits, cnt_ref):
    lane = lax.broadcasted_iota(jnp.int32, logits.shape, 1).astype(F32)
    ninf = -jnp.inf

    def first_argmax(v, vmax):
        return jnp.min(jnp.where(v == vmax, lane, float(LANES)), axis=-1, keepdims=True)

    gl = jnp.where(lane < N_GROUPS, logits, ninf)
    gm = jnp.max(gl, axis=-1, keepdims=True)
    g_idx = first_argmax(gl, gm)
    g_w = 1.0 / jnp.sum(jnp.exp(gl - gm), axis=-1, keepdims=True)

    lo = N_GROUPS + EXPERTS_PER_GROUP * g_idx
    el = jnp.where(jnp.logical_and(lane >= lo, lane < lo + EXPERTS_PER_GROUP), logits, ninf)
    v1 = jnp.max(el, axis=-1, keepdims=True)
    i1 = first_argmax(el, v1)
    el2 = jnp.where(lane == i1, ninf, el)
    v2 = jnp.max(el2, axis=-1, keepdims=True)
    i2 = first_argmax(el2, v2)
    e2 = jnp.exp(v2 - v1)
    w1 = (1.0 / (1.0 + e2)) * g_w
    w2 = (e2 / (1.0 + e2)) * g_w
    gates = jnp.where(lane == i1 - lo, w1, 0.0) + jnp.where(lane == i2 - lo, w2, 0.0)

    rows = logits.shape[0]
    onehot = jnp.where(lane == g_idx, 1.0, 0.0)
    r_i = lax.broadcasted_iota(jnp.int32, (rows, rows), 0)
    c_i = lax.broadcasted_iota(jnp.int32, (rows, rows), 1)
    tri = jnp.where(c_i < r_i, 1.0, 0.0).astype(BF16)
    before = jnp.dot(tri, onehot.astype(BF16), preferred_element_type=F32) + cnt_ref[...]
    rank = jnp.sum(onehot * before, axis=-1, keepdims=True)
    cnt_ref[...] += jnp.sum(onehot, axis=0, keepdims=True)
    return (gates + jnp.where(lane == ROUTE_GROUP_LANE, g_idx, 0.0)
            + jnp.where(lane == ROUTE_RANK_LANE, rank, 0.0))


PLAN_CHUNK = 1024


def _plan_items(per, n_used, n_items_max, grp, p_ref, e_ref, key_ref, n_ref, run_ref):
    tstart_s, tend_s, tiles_s, before_s, next_s = grp
    n_items = per * n_used
    n_ref[0] = n_items

    def item(w, carry):
        valid = w < n_items
        wc = jnp.minimum(w, n_items - 1)
        gi = jnp.int32(0)
        for g in range(N_GROUPS):
            gi = gi + (wc >= per * tend_s[g]).astype(jnp.int32)
        local = wc - per * tstart_s[gi]
        tiles_w = tiles_s[gi]
        lpart = jnp.int32(0)
        for k in range(1, per):
            lpart = lpart + (local >= k * tiles_w).astype(jnp.int32)
        ltile = local - lpart * tiles_w
        spare = jnp.maximum(w - n_items, 0)
        part = jnp.where(valid, lpart, spare % per)
        p_ref[w] = jnp.where(valid, tstart_s[gi] + ltile, n_used + spare // per)
        e_ref[w] = part
        key = gi * per + part
        key_ref[w] = key
        nxt_g = next_s[gi]
        run_ref[3 * w] = jnp.logical_and(valid, ltile == 0).astype(jnp.int32)
        run_ref[3 * w + 1] = (per * before_s[gi] + part) % 2
        run_ref[3 * w + 2] = jnp.where(part + 1 < per, key + 1,
                                       jnp.where(nxt_g >= 0, nxt_g * per, -1))
        return carry
    lax.fori_loop(0, n_items_max, item, 0)


def _plan_kernel(meta_ref, cnt_ref, dest_ref, src_ref, nused_ref,
                 up_p, up_e, up_key, up_n, up_run, dn_p, dn_e, dn_key, dn_n, dn_run,
                 tstart_s, tend_s, tiles_s, before_s, next_s,
                 *, tile, n_tiles_max, up_parts, dn_parts):
    s = pl.program_id(0)

    @pl.when(s == 0)
    def _():
        start = jnp.int32(0)
        owned = jnp.int32(0)
        for g in range(N_GROUPS):
            tiles = (cnt_ref[g] + tile - 1) // tile
            tstart_s[g] = start
            tiles_s[g] = tiles
            before_s[g] = owned
            start = start + tiles
            tend_s[g] = start
            owned = owned + (tiles > 0).astype(jnp.int32)
        nxt = jnp.int32(-1)
        for g in reversed(range(N_GROUPS)):
            next_s[g] = nxt
            nxt = jnp.where(tiles_s[g] > 0, g, nxt)
        nused_ref[0] = start

        def clear(r, carry):
            src_ref[r] = 0
            return carry
        lax.fori_loop(0, n_tiles_max * tile, clear, 0, unroll=8)

        grp = (tstart_s, tend_s, tiles_s, before_s, next_s)
        _plan_items(up_parts, start, up_parts * n_tiles_max, grp, up_p, up_e, up_key, up_n, up_run)
        _plan_items(dn_parts, start, dn_parts * n_tiles_max, grp, dn_p, dn_e, dn_key, dn_n, dn_run)

    def place(i, carry):
        t = s * PLAN_CHUNK + i
        d = tstart_s[meta_ref[0, i]] * tile + meta_ref[1, i]
        dest_ref[t] = d
        src_ref[d] = t
        return carry
    lax.fori_loop(0, PLAN_CHUNK, place, 0, unroll=8)


def _moe_plan(meta, counts, *, tile, n_tiles_max, up_parts, dn_parts):
    t = meta.shape[1]
    i32 = jnp.int32
    smem = functools.partial(pl.BlockSpec, memory_space=pltpu.SMEM)
    n_up, n_dn = up_parts * n_tiles_max, dn_parts * n_tiles_max
    shapes = [(t,), (n_tiles_max * tile,), (1,),
              (n_up,), (n_up,), (n_up,), (1,), (3 * n_up,),
              (n_dn,), (n_dn,), (n_dn,), (1,), (3 * n_dn,)]
    outs = pl.pallas_call(
        functools.partial(_plan_kernel, tile=tile, n_tiles_max=n_tiles_max,
                          up_parts=up_parts, dn_parts=dn_parts),
        grid=(t // PLAN_CHUNK,),
        in_specs=[smem((2, PLAN_CHUNK), lambda s: (0, s)), smem()],
        out_specs=[smem() for _ in shapes],
        out_shape=[jax.ShapeDtypeStruct(sh, i32) for sh in shapes],
        scratch_shapes=[pltpu.SMEM((N_GROUPS,), i32) for _ in range(5)],
        compiler_params=_params(1),
        name="moe_plan",
    )(meta, counts)
    dest, src, n_used = outs[:3]
    return dict(dest=dest, src=src, n_used=n_used, up=tuple(outs[3:8]), down=tuple(outs[8:13]))


def _row_copy(src_hbm, dst_vmem, src_row, dst_row, sem):
    return pltpu.make_async_copy(src_hbm.at[pl.ds(src_row, 1)], dst_vmem.at[pl.ds(dst_row, 1)], sem)


def _start_row_gather(idx_ref, base, n_rows, pairs):
    def body(r, carry):
        i = idx_ref[base + r]
        for src_hbm, dst_vmem, sem in pairs:
            _row_copy(src_hbm, dst_vmem, i, r, sem).start()
        return carry
    lax.fori_loop(0, n_rows, body, 0, unroll=8)


def _start_rows(idx_ref, idx_base, row0, n_rows, pairs):
    for k in range(n_rows):
        i = idx_ref[idx_base + row0 + k]
        for src_hbm, dst_vmem, sem in pairs:
            _row_copy(src_hbm, dst_vmem, i, row0 + k, sem).start()


def _norm_tile_and_prefetch(load_rows, g_ref, o_ref, n_rows, has_next, idx_ref, next_base,
                            next_pairs):
    @pl.when(has_next)
    def _():
        _rmsnorm_rows(load_rows, g_ref, o_ref, n_rows,
                      per_trip=lambda row0: _start_rows(idx_ref, next_base, row0, NORM_ROWS,
                                                        next_pairs))

    @pl.when(jnp.logical_not(has_next))
    def _():
        _rmsnorm_rows(load_rows, g_ref, o_ref, n_rows)


def _wait_row_gather(n_rows, pairs):
    for src_hbm, dst_vmem, sem in pairs:
        pltpu.make_async_copy(src_hbm.at[pl.ds(0, n_rows)], dst_vmem, sem).wait()


def _dispatch_kernel(src_ref, nused_ref, x2_hbm, route_hbm, g_ref, xs_ref, gs_ref,
                     xbuf, gbuf, sem):
    p = pl.program_id(0)
    n_used = nused_ref[0]
    tile = xbuf.shape[1]
    slot = p % 2

    def pairs(s):
        return [(x2_hbm, xbuf.at[s], sem.at[0, s]), (route_hbm, gbuf.at[s], sem.at[1, s])]

    @pl.when(p == 0)
    def _():
        _start_row_gather(src_ref, 0, tile, pairs(0))

    @pl.when(p < n_used)
    def _():
        _wait_row_gather(tile, pairs(slot))
        gs_ref[...] = gbuf[slot]
        _norm_tile_and_prefetch(lambda rows: xbuf[slot, rows, :], g_ref, xs_ref, tile,
                                p + 1 < n_used, src_ref, (p + 1) * tile, pairs(1 - slot))

    @pl.when(p >= n_used)
    def _():
        xs_ref[...] = jnp.zeros_like(xs_ref)
        gs_ref[...] = jnp.zeros_like(gs_ref)


def _dispatch(x2, route, g, src, n_used, *, tile, n_tiles_max):
    t, d = x2.shape
    tile_map = lambda p, src_ref, n_ref: (p, 0)
    grid_spec = pltpu.PrefetchScalarGridSpec(
        num_scalar_prefetch=2,
        grid=(n_tiles_max,),
        in_specs=[
            pl.BlockSpec(memory_space=pl.ANY),
            pl.BlockSpec(memory_space=pl.ANY),
            pl.BlockSpec((1, d), lambda p, s, n: (0, 0)),
        ],
        out_specs=[
            pl.BlockSpec((tile, d), tile_map),
            pl.BlockSpec((tile, LANES), tile_map),
        ],
        scratch_shapes=[
            pltpu.VMEM((2, tile, d), F32),
            pltpu.VMEM((2, tile, LANES), F32),
            pltpu.SemaphoreType.DMA((2, 2)),
        ],
    )
    return pl.pallas_call(
        _dispatch_kernel,
        grid_spec=grid_spec,
        out_shape=[
            jax.ShapeDtypeStruct((n_tiles_max * tile, d), BF16),
            jax.ShapeDtypeStruct((n_tiles_max * tile, LANES), F32),
        ],
        compiler_params=_params(1),
        name="moe_dispatch",
    )(src, n_used, x2, route, g)


def _stream_run_weights(w, n_items, key_ref, run_ref, copies):
    is_first = run_ref[3 * w]
    slot = run_ref[3 * w + 1]
    nxt_key = run_ref[3 * w + 2]
    valid = w < n_items

    @pl.when(jnp.logical_and(valid, w == 0))
    def _():
        for cp in copies(key_ref[0], 0):
            cp.start()

    @pl.when(jnp.logical_and(valid, is_first == 1))
    def _():
        for cp in copies(key_ref[w], slot):
            cp.wait()

        @pl.when(nxt_key >= 0)
        def _():
            for cp in copies(nxt_key, 1 - slot):
                cp.start()

    return slot


def _moe_up_kernel(ip_ref, ie_ref, key_ref, n_ref, run_ref, xs_ref, gs_ref, w1_hbm, w3_hbm,
                   hid_ref, w1_buf, w3_buf, sem):
    w = pl.program_id(0)

    n_e = w1_buf.shape[1]
    f = w1_buf.shape[3]

    def copies(key, slot):
        experts = pl.ds(key * n_e, n_e)
        return [pltpu.make_async_copy(w1_hbm.at[experts], w1_buf.at[slot], sem.at[0, slot]),
                pltpu.make_async_copy(w3_hbm.at[experts], w3_buf.at[slot], sem.at[1, slot])]

    slot = _stream_run_weights(w, n_ref[0], key_ref, run_ref, copies)

    @pl.when(w < n_ref[0])
    def _():
        x = xs_ref[...]
        lane = lax.broadcasted_iota(jnp.int32, gs_ref.shape, 1)
        for e in range(n_e):
            a = jnp.dot(x, w1_buf[slot, e].astype(BF16), preferred_element_type=F32)
            b = jnp.dot(x, w3_buf[slot, e].astype(BF16), preferred_element_type=F32)
            in_group = ie_ref[w] * n_e + e
            gate = jnp.sum(jnp.where(lane == in_group, gs_ref[...], 0.0), axis=-1, keepdims=True)
            hid_ref[:, e * f:(e + 1) * f] = (jax.nn.silu(a) * b * gate).astype(hid_ref.dtype)

    @pl.when(w >= n_ref[0])
    def _():
        hid_ref[...] = jnp.zeros_like(hid_ref)


def _moe_up(xs, gs, w1, w3, items, *, tile):
    rows, d = xs.shape
    f = w1.shape[2]
    ip, ie, key, n_items, run = items
    grid_spec = pltpu.PrefetchScalarGridSpec(
        num_scalar_prefetch=5,
        grid=(ip.shape[0],),
        in_specs=[
            pl.BlockSpec((tile, d), lambda w, ip, ie, key, n, run: (ip[w], 0)),
            pl.BlockSpec((tile, LANES), lambda w, ip, ie, key, n, run: (ip[w], 0)),
            pl.BlockSpec(memory_space=pl.ANY),
            pl.BlockSpec(memory_space=pl.ANY),
        ],
        out_specs=pl.BlockSpec((tile, MOE_UP_EXPERTS * f),
                               lambda w, ip, ie, key, n, run: (ip[w], ie[w])),
        scratch_shapes=[
            pltpu.VMEM((2, MOE_UP_EXPERTS, d, f), w1.dtype),
            pltpu.VMEM((2, MOE_UP_EXPERTS, d, f), w3.dtype),
            pltpu.SemaphoreType.DMA((2, 2)),
        ],
    )
    return pl.pallas_call(
        _moe_up_kernel,
        grid_spec=grid_spec,
        out_shape=jax.ShapeDtypeStruct((rows, EXPERTS_PER_GROUP * f), BF16),
        compiler_params=_params(1),
        name="moe_up",
    )(ip, ie, key, n_items, run, xs, gs, w1, w3)


def _moe_down_kernel(ip_ref, ih_ref, key_ref, n_ref, run_ref, hid_ref, w2_hbm, y_ref,
                     w2_buf, sem, *, n_halves):
    w = pl.program_id(0)
    tn = w2_buf.shape[2]

    def copies(key, slot):
        g = key // n_halves
        col = pl.multiple_of((key % n_halves) * tn, tn)
        return [pltpu.make_async_copy(w2_hbm.at[g, :, pl.ds(col, tn)], w2_buf.at[slot],
                                      sem.at[slot])]

    slot = _stream_run_weights(w, n_ref[0], key_ref, run_ref, copies)

    @pl.when(w < n_ref[0])
    def _():
        y_ref[...] = jnp.dot(hid_ref[...], w2_buf[slot].astype(BF16),
                             preferred_element_type=F32)

    @pl.when(w >= n_ref[0])
    def _():
        y_ref[...] = jnp.zeros_like(y_ref)


def _moe_down(hid, w2g, items, *, tile, n_halves):
    rows, k = hid.shape
    d = w2g.shape[2]
    tn = d // n_halves
    ip, ih, key, n_items, run = items
    grid_spec = pltpu.PrefetchScalarGridSpec(
        num_scalar_prefetch=5,
        grid=(ip.shape[0],),
        in_specs=[
            pl.BlockSpec((tile, k), lambda w, ip, ih, key, n, run: (ip[w], 0)),
            pl.BlockSpec(memory_space=pl.ANY),
        ],
        out_specs=pl.BlockSpec((tile, tn), lambda w, ip, ih, key, n, run: (ip[w], ih[w])),
        scratch_shapes=[
            pltpu.VMEM((2, k, tn), w2g.dtype),
            pltpu.SemaphoreType.DMA((2,)),
        ],
    )
    return pl.pallas_call(
        functools.partial(_moe_down_kernel, n_halves=n_halves),
        grid_spec=grid_spec,
        out_shape=jax.ShapeDtypeStruct((rows, d), F32),
        compiler_params=_params(1),
        name="moe_down",
    )(ip, ih, key, n_items, run, hid, w2g)


def _final_kernel(dest_ref, x_ref, ys_hbm, g_ref, o_ref, ybuf, sem, *, tile0):
    i = pl.program_id(0)
    tm = ybuf.shape[1]
    slot = i % 2

    def pairs(s):
        return [(ys_hbm, ybuf.at[s], sem.at[s])]

    @pl.when(i == 0)
    def _():
        _start_row_gather(dest_ref, tile0 * tm, tm, pairs(0))

    _wait_row_gather(tm, pairs(slot))
    _norm_tile_and_prefetch(lambda rows: x_ref[rows, :] + ybuf[slot, rows, :], g_ref, o_ref, tm,
                            i + 1 < pl.num_programs(0), dest_ref, (tile0 + i + 1) * tm,
                            pairs(1 - slot))


def _final(x2, ys, dest, g, *, row0, n_rows, tm):
    d = x2.shape[1]
    off = row0 // tm
    grid_spec = pltpu.PrefetchScalarGridSpec(
        num_scalar_prefetch=1,
        grid=(n_rows // tm,),
        in_specs=[
            pl.BlockSpec((tm, d), lambda i, dest: (i + off, 0)),
            pl.BlockSpec(memory_space=pl.ANY),
            pl.BlockSpec((1, d), lambda i, dest: (0, 0)),
        ],
        out_specs=pl.BlockSpec((tm, d), lambda i, dest: (i, 0)),
        scratch_shapes=[pltpu.VMEM((2, tm, d), F32), pltpu.SemaphoreType.DMA((2,))],
    )
    return pl.pallas_call(
        functools.partial(_final_kernel, tile0=off),
        grid_spec=grid_spec,
        out_shape=jax.ShapeDtypeStruct((n_rows, d), F32),
        compiler_params=_params(1),
        name="final_norm",
    )(dest, x2, ys, g)


def _rope_tables(pos):
    half = ROT_DIM // 2
    inv = ROPE_THETA ** (-np.arange(half, dtype=np.float64) / half)
    ang = np.asarray(pos, np.float64)[:, None] * inv[None, :]
    cos, sin = np.cos(ang), np.sin(ang)
    ones = np.ones((ang.shape[0], HEAD_DIM - ROT_DIM))
    zeros = np.zeros((ang.shape[0], HEAD_DIM - half))
    c = np.concatenate([cos, cos, ones], axis=1)
    s_hi = np.concatenate([-sin, zeros], axis=1)
    s_lo = np.concatenate([np.zeros_like(sin), sin, 0.0 * ones], axis=1)
    rep = LANES // HEAD_DIM
    return tuple(jnp.asarray(np.tile(t, (1, rep)), F32) for t in (c, s_hi, s_lo))


def kernel(x_prompt, x_sample, cache_k, cache_v, state_conv, state_h, norm_mix, w_in, conv_w,
           conv_b, w_gate_a, b_gate_a, w_gate_x, b_gate_x, lru_lambda, sinks, norm_attn_out,
           norm_lru_out, w_out, norm_ffn, w_group, b_group, w_expert_router, b_expert_router,
           w1, w3, w2, norm_final):
    batch, seq, d_model = x_prompt.shape
    dec_batch, dec_seq, _ = x_sample.shape
    depth = w_in.shape[0]
    assert depth == 1 and dec_seq == CHUNK and seq % CHUNK == 0
    n_heads = sinks.shape[1]
    att_width = n_heads * HEAD_DIM
    n_kv = cache_k.shape[3]
    kv_cols = n_kv * HEAD_DIM
    lru_width = lru_lambda.shape[1]
    past_len = 1024
    cw = cache_k.shape[2]
    assert cw == WIN_CHUNKS * CHUNK
    n_p, n_s = batch * seq, dec_batch * dec_seq
    chunks_per_seq = seq // CHUNK
    n_prompt_chunks = n_p // CHUNK

    xp = x_prompt.reshape(n_p, d_model)
    xs = x_sample.reshape(n_s, d_model)

    w_r = w_in[0]
    tn = 2 * kv_cols
    assert tn == 512 and att_width % tn == 0 and lru_width % tn == 0
    n_q_tiles = att_width // tn
    kv_tile = n_q_tiles

    tm_in = 1024
    assert seq % tm_in == 0 and n_s % tm_in == 0 and tm_in % dec_seq == 0
    pos = np.concatenate([np.arange(seq), np.tile(past_len + np.arange(dec_seq), tm_in // dec_seq)])
    c_tab, shi_tab, slo_tab = _rope_tables(pos)

    z = _in_proj(xp, xs, norm_mix, w_r, c_tab, shi_tab, slo_tab, tm=tm_in, tn=tn,
                 n_q_tiles=n_q_tiles, kv_tile=kv_tile, prompt_tiles_per_seq=seq // tm_in)

    k_col = att_width + 2 * lru_width
    v_col = k_col + kv_cols
    att = _attention(
        z, cache_k[0].reshape(dec_batch, cw, kv_cols), cache_v[0].reshape(dec_batch, cw, kv_cols),
        sinks[0], norm_attn_out,
        n_prompt_chunks=n_prompt_chunks, chunks_per_seq=chunks_per_seq, att_width=att_width,
        n_kv=n_kv, k_col_blk=k_col // kv_cols, v_col_blk=v_col // kv_cols)

    nblk = lru_width // LRU_BLOCK
    lru, h_tiles = _rglru(
        z, state_conv[0], state_h[0].reshape(dec_batch, 1, lru_width), conv_w[0], conv_b,
        w_gate_a[0].astype(BF16), b_gate_a[0].reshape(1, lru_width),
        w_gate_x[0].astype(BF16), b_gate_x[0].reshape(1, lru_width),
        lru_lambda, norm_lru_out,
        n_prompt_chunks=n_prompt_chunks, chunks_per_seq=chunks_per_seq, width=lru_width,
        xb_col_blk=att_width // lru_width, yb_col_blk=att_width // lru_width + 1)
    del nblk

    n_routes = N_GROUPS + N_GROUPS * EXPERTS_PER_GROUP
    wr = jnp.concatenate([w_group[0], w_expert_router[0],
                          jnp.zeros((d_model, LANES - n_routes), F32)], axis=1).astype(BF16)
    br = jnp.concatenate([b_group[0], b_expert_router[0],
                          jnp.zeros((LANES - n_routes,), F32)]).reshape(1, LANES)
    x2, route, counts, meta = _out_proj(att, lru, w_out[0], xp, xs, norm_ffn, wr, br, tm=1024, tn=512)

    n_tiles_max = (n_p + n_s) // MOE_TILE + N_GROUPS
    tabs = _moe_plan(meta, counts[0, :N_GROUPS].astype(jnp.int32), tile=MOE_TILE,
                     n_tiles_max=n_tiles_max, up_parts=EXPERTS_PER_GROUP // MOE_UP_EXPERTS,
                     dn_parts=MOE_DOWN_HALVES)
    xsort, gsort = _dispatch(x2, route, norm_ffn, tabs["src"], tabs["n_used"],
                             tile=MOE_TILE, n_tiles_max=n_tiles_max)
    hid = _moe_up(xsort, gsort, w1[0], w3[0], tabs["up"], tile=MOE_TILE)
    d_expert = w2.shape[2]
    w2g = w2[0].reshape(N_GROUPS, EXPERTS_PER_GROUP * d_expert, d_model)
    ysort = _moe_down(hid, w2g, tabs["down"], tile=MOE_TILE, n_halves=MOE_DOWN_HALVES)

    g_fin = norm_final.reshape(1, d_model)
    y_prompt = _final(x2, ysort, tabs["dest"], g_fin, row0=0, n_rows=n_p,
                      tm=MOE_TILE).reshape(batch, seq, d_model)
    y_sample = _final(x2, ysort, tabs["dest"], g_fin, row0=n_p, n_rows=n_s,
                      tm=MOE_TILE).reshape(dec_batch, dec_seq, d_model)

    win = min(WIN_CHUNKS * CHUNK, seq)

    def prompt_tail(col, width, rows):
        return jnp.stack([lax.slice(z, ((b + 1) * seq - rows, col), ((b + 1) * seq, col + width))
                          for b in range(batch)])

    def sample_rows(col, width):
        return lax.slice(z, (n_p, col), (n_p + n_s, col + width)).reshape(dec_batch, dec_seq, width)

    k_prompt = prompt_tail(k_col, kv_cols, win).reshape(1, batch, win, n_kv, HEAD_DIM)
    v_prompt = prompt_tail(v_col, kv_cols, win).reshape(1, batch, win, n_kv, HEAD_DIM)
    conv_prompt = prompt_tail(att_width, lru_width, CONV_WIDTH - 1)[None]
    h_prompt = h_tiles[:n_prompt_chunks, 0].reshape(batch, chunks_per_seq, lru_width)[:, -1][None]

    ks = sample_rows(k_col, kv_cols).reshape(dec_batch, dec_seq, n_kv, HEAD_DIM)
    vs = sample_rows(v_col, kv_cols).reshape(dec_batch, dec_seq, n_kv, HEAD_DIM)
    k_sample = jnp.concatenate([cache_k[0], ks], axis=1)[:, -cw:][None]
    v_sample = jnp.concatenate([cache_v[0], vs], axis=1)[:, -cw:][None]
    xs_rows = sample_rows(att_width, lru_width)[:, dec_seq - (CONV_WIDTH - 1):]
    conv_sample = jnp.concatenate([state_conv[0], xs_rows], axis=1)[:, -(CONV_WIDTH - 1):][None]
    h_sample = h_tiles[n_prompt_chunks:, 0][None]

    return (y_prompt, y_sample, k_prompt, v_prompt, conv_prompt, h_prompt,
            k_sample, v_sample, conv_sample, h_sample)
```

```python
import functools

import numpy as np
import jax
import jax.numpy as jnp
from jax import lax
from jax.experimental import pallas as pl
from jax.experimental.pallas import tpu as pltpu

F32 = jnp.float32
BF16 = jnp.bfloat16

CHUNK = 64
HEAD_DIM = 64
KV_GROUP = 8
WIN_CHUNKS = 2
ROT_DIM = 16
ROPE_THETA = 500000.0
LRU_BLOCK = 128
CONV_WIDTH = 4
LRU_C = 8.0
N_GROUPS = 8
EXPERTS_PER_GROUP = 4
EPS = 1e-6
NEG = -1e30
LANES = 128
INPROJ_ROW_BLOCK = 128
MOE_TILE = 256
MOE_UP_EXPERTS = 2
MOE_DOWN_HALVES = 1
VMEM_LIMIT = 56 * 1024 * 1024


def _params(n_axes, vmem=VMEM_LIMIT):
    return pltpu.CompilerParams(
        dimension_semantics=("arbitrary",) * n_axes, vmem_limit_bytes=vmem)


def _rms_scale(x):
    return lax.rsqrt(jnp.mean(x * x, axis=-1, keepdims=True) + EPS)


NORM_ROWS = 16


def _rmsnorm_rows(load_rows, g_ref, o_ref, n_rows, per_trip=None):
    def body(r, carry):
        row0 = r * NORM_ROWS if isinstance(r, int) else pl.multiple_of(r * NORM_ROWS, NORM_ROWS)
        rows = pl.ds(row0, NORM_ROWS)
        x = load_rows(rows)
        o_ref[rows, :] = (x * _rms_scale(x) * g_ref[...]).astype(o_ref.dtype)
        if per_trip is not None:
            per_trip(row0)
        return carry
    if per_trip is not None:
        for r in range(n_rows // NORM_ROWS):
            body(r, 0)
    else:
        lax.fori_loop(0, n_rows // NORM_ROWS, body, 0, unroll=4)


def _rope_block(zb, c, s_hi, s_lo):
    return (zb * c + pltpu.roll(zb, LANES - ROT_DIM // 2, 1) * s_hi
            + pltpu.roll(zb, ROT_DIM // 2, 1) * s_lo)


def _inproj_kernel(xp_hbm, xs_hbm, g_ref, w_ref, c_ref, shi_ref, slo_ref, o_ref, x_buf, h_ref,
                   sem, *, n_prompt_tiles, n_q_tiles, kv_tile):
    i = pl.program_id(0)
    j = pl.program_id(1)
    tm, tn = o_ref.shape

    def x_tile_copy(src_hbm, tile):
        return pltpu.make_async_copy(src_hbm.at[pl.ds(tile * tm, tm)], x_buf, sem)

    def start_x(tile):
        @pl.when(tile < n_prompt_tiles)
        def _():
            x_tile_copy(xp_hbm, tile).start()

        @pl.when(tile >= n_prompt_tiles)
        def _():
            x_tile_copy(xs_hbm, tile - n_prompt_tiles).start()

    @pl.when(jnp.logical_and(i == 0, j == 0))
    def _():
        start_x(0)

    @pl.when(j == 0)
    def _():
        x_tile_copy(xp_hbm, 0).wait()
        _rmsnorm_rows(lambda rows: x_buf[rows, :], g_ref, h_ref, tm)

    @pl.when(jnp.logical_and(j == 1, i + 1 < pl.num_programs(0)))
    def _():
        start_x(i + 1)

    w = w_ref[...].astype(BF16)
    n_blk = tn // LANES
    for r0 in range(0, tm, INPROJ_ROW_BLOCK):
        rows = slice(r0, r0 + INPROJ_ROW_BLOCK)
        z = jnp.dot(h_ref[rows, :], w, preferred_element_type=F32)
        c, shi, slo = c_ref[rows, :], shi_ref[rows, :], slo_ref[rows, :]
        for b in range(n_blk):
            is_rope = j < n_q_tiles
            if b < n_blk // 2:
                is_rope = jnp.logical_or(is_rope, j == kv_tile)
            zb = z[:, b * LANES:(b + 1) * LANES]
            o_ref[rows, b * LANES:(b + 1) * LANES] = _rope_block(
                zb, jnp.where(is_rope, c, 1.0), jnp.where(is_rope, shi, 0.0),
                jnp.where(is_rope, slo, 0.0))


def _in_proj(xp, xs, g, w, c, shi, slo, *, tm, tn, n_q_tiles, kv_tile, prompt_tiles_per_seq):
    n_p, d = xp.shape
    n_s = xs.shape[0]
    t = n_p + n_s
    npt = n_p // tm
    n_cols = w.shape[1]
    kern = functools.partial(_inproj_kernel, n_prompt_tiles=npt, n_q_tiles=n_q_tiles,
                             kv_tile=kv_tile)
    n_tiles = n_cols // tn
    assert kv_tile == n_q_tiles

    def table_tile(i, j):
        return jnp.where(i < npt, i % prompt_tiles_per_seq, prompt_tiles_per_seq), 0

    def out_tile(i, j):
        return i, jnp.where(j < kv_tile, j, jnp.where(j == kv_tile, n_tiles - 1, j - 1))

    return pl.pallas_call(
        kern,
        grid=(t // tm, n_tiles),
        in_specs=[
            pl.BlockSpec(memory_space=pl.ANY),
            pl.BlockSpec(memory_space=pl.ANY),
            pl.BlockSpec((1, d), lambda i, j: (0, 0)),
            pl.BlockSpec((d, tn), lambda i, j: (0, j)),
            pl.BlockSpec((tm, LANES), table_tile),
            pl.BlockSpec((tm, LANES), table_tile),
            pl.BlockSpec((tm, LANES), table_tile),
        ],
        out_specs=pl.BlockSpec((tm, tn), out_tile),
        out_shape=jax.ShapeDtypeStruct((t, n_cols), F32),
        scratch_shapes=[pltpu.VMEM((tm, d), xp.dtype), pltpu.VMEM((tm, d), BF16),
                        pltpu.SemaphoreType.DMA(())],
        compiler_params=_params(2),
        name="in_proj",
    )(xp, xs, g, w, c, shi, slo)


ATTN_CHUNKS = 4


def _attn_chunk(q, k_band, v_band, lo, sink_ref, n_kv):
    n_keys = k_band.shape[0]
    key = lax.broadcasted_iota(jnp.int32, (n_keys, KV_GROUP * CHUNK), 0)
    valid = key >= lo
    outs = []
    for gi in range(n_kv):
        kg = k_band[:, gi * HEAD_DIM:(gi + 1) * HEAD_DIM]
        vg = v_band[:, gi * HEAD_DIM:(gi + 1) * HEAD_DIM]
        heads = [gi * KV_GROUP + h for h in range(KV_GROUP)]
        qg = jnp.concatenate([q[:, h * HEAD_DIM:(h + 1) * HEAD_DIM] for h in heads], axis=0)
        st = lax.dot_general(kg, qg, (((1,), (1,)), ((), ())), preferred_element_type=F32)
        st = jnp.where(valid, st, NEG)
        sink = sink_ref[gi:gi + 1, :]
        m = jnp.maximum(jnp.max(st, axis=0, keepdims=True), sink)
        p = jnp.exp(st - m)
        denom = jnp.sum(p, axis=0, keepdims=True) + jnp.exp(sink - m)
        pn = (p * (1.0 / denom)).astype(BF16)
        og = lax.dot_general(pn, vg, (((0,), (0,)), ((), ())), preferred_element_type=F32)
        outs.extend(og[h * CHUNK:(h + 1) * CHUNK] for h in range(KV_GROUP))
    return jnp.concatenate(outs, axis=1)


def _attn_kernel(q_ref, kp_ref, k_ref, vp_ref, v_ref, kc_ref, vc_ref, sink_ref, g_ref, o_ref,
                 *, n_prompt_tiles, tiles_per_seq, n_kv):
    n = pl.program_id(0)
    is_s = n >= n_prompt_tiles
    has_prev = jnp.logical_or(is_s, n % tiles_per_seq > 0)

    def band(prev_ref, own_ref, cache_ref, c):
        prev, own, cache = prev_ref[...], own_ref[...], cache_ref[c]
        hist = jnp.concatenate([prev, own], axis=0)[c * CHUNK:(c + WIN_CHUNKS) * CHUNK]
        hist = jnp.where(is_s, cache, hist)
        return jnp.concatenate([hist, own[c * CHUNK:(c + 1) * CHUNK]], axis=0).astype(BF16)

    q = (q_ref[...] * (HEAD_DIM ** -0.5)).astype(BF16)
    for c in range(ATTN_CHUNKS):
        lo = jnp.where(has_prev, 0, max(WIN_CHUNKS - c, 0) * CHUNK)
        att = _attn_chunk(q[c * CHUNK:(c + 1) * CHUNK], band(kp_ref, k_ref, kc_ref, c),
                          band(vp_ref, v_ref, vc_ref, c), lo, sink_ref, n_kv)
        o_ref[c * CHUNK:(c + 1) * CHUNK, :] = (
            att * _rms_scale(att) * g_ref[...]).astype(o_ref.dtype)


def _attention(z, cache_k, cache_v, sinks, g, *, n_prompt_chunks, chunks_per_seq,
               att_width, n_kv, k_col_blk, v_col_blk):
    t = z.shape[0]
    rows = ATTN_CHUNKS * CHUNK
    assert ATTN_CHUNKS % WIN_CHUNKS == 0 and chunks_per_seq % ATTN_CHUNKS == 0
    hist_rows = WIN_CHUNKS * CHUNK
    hist_per_tile = rows // hist_rows
    assert cache_k.shape[0] % ATTN_CHUNKS == 0
    n_tiles = t // rows
    npt = n_prompt_chunks // ATTN_CHUNKS
    tps = chunks_per_seq // ATTN_CHUNKS
    kvw = n_kv * HEAD_DIM

    def prev(col):
        return lambda n: (n * hist_per_tile - jnp.minimum(1, n % tps), col)

    cache_map = lambda n: (jnp.maximum(n - npt, 0), 0, 0)
    kern = functools.partial(_attn_kernel, n_prompt_tiles=npt, tiles_per_seq=tps, n_kv=n_kv)
    sink_rows = jnp.repeat(sinks.reshape(n_kv, KV_GROUP), CHUNK, axis=1)
    return pl.pallas_call(
        kern,
        grid=(n_tiles,),
        in_specs=[
            pl.BlockSpec((rows, att_width), lambda n: (n, 0)),
            pl.BlockSpec((hist_rows, kvw), prev(k_col_blk)),
            pl.BlockSpec((rows, kvw), lambda n: (n, k_col_blk)),
            pl.BlockSpec((hist_rows, kvw), prev(v_col_blk)),
            pl.BlockSpec((rows, kvw), lambda n: (n, v_col_blk)),
            pl.BlockSpec((ATTN_CHUNKS, WIN_CHUNKS * CHUNK, kvw), cache_map),
            pl.BlockSpec((ATTN_CHUNKS, WIN_CHUNKS * CHUNK, kvw), cache_map),
            pl.BlockSpec((n_kv, KV_GROUP * CHUNK), lambda n: (0, 0)),
            pl.BlockSpec((1, att_width), lambda n: (0, 0)),
        ],
        out_specs=pl.BlockSpec((rows, att_width), lambda n: (n, 0)),
        out_shape=jax.ShapeDtypeStruct((t, att_width), BF16),
        compiler_params=_params(1),
        name="attention",
    )(z, z, z, z, z, cache_k, cache_v, sink_rows, g)


SUBLANES = 8


def _lru_kernel(xb_ref, yb_ref, sconv_ref, sh_ref, cw_ref, cb_ref, wa_ref, ba_ref,
                wx_ref, bx_ref, lam_ref, g_ref, o_ref, hl_ref, hist_ref, hcar_ref,
                a_sc, b_sc, h_sc, *, n_prompt_chunks, chunks_per_seq):
    n = pl.program_id(0)
    is_s = n >= n_prompt_chunks
    hist_rows = hist_ref.shape[0]

    @pl.when(jnp.logical_and(jnp.logical_not(is_s), n % chunks_per_seq == 0))
    def _():
        hist_ref[...] = jnp.zeros_like(hist_ref)
        hcar_ref[...] = jnp.zeros_like(hcar_ref)

    @pl.when(is_s)
    def _():
        hist_ref[...] = jnp.zeros_like(hist_ref)
        hist_ref[hist_rows - (CONV_WIDTH - 1):, :] = sconv_ref[0]
        hcar_ref[...] = sh_ref[0]

    x = xb_ref[...]
    rows, width = x.shape
    hist = hist_ref[...]
    row8 = lax.broadcasted_iota(jnp.int32, (hist_rows, width), 0)
    xc = cb_ref[...]
    for j in range(CONV_WIDTH):
        s = CONV_WIDTH - 1 - j
        if s == 0:
            xs = x
        else:
            xr = pltpu.roll(x, s, 0)
            head = jnp.where(row8 < s, pltpu.roll(hist, s, 0), xr[:hist_rows])
            xs = jnp.concatenate([head, xr[hist_rows:]], axis=0)
        xc = xc + xs * cw_ref[j:j + 1, :]
    hist_ref[...] = x[rows - hist_rows:]

    xcb = xc.astype(BF16)
    ra, rx = [], []
    for nb in range(width // LRU_BLOCK):
        blk = xcb[:, nb * LRU_BLOCK:(nb + 1) * LRU_BLOCK]
        ra.append(jnp.dot(blk, wa_ref[nb], preferred_element_type=F32))
        rx.append(jnp.dot(blk, wx_ref[nb], preferred_element_type=F32))
    r = jax.nn.sigmoid(jnp.concatenate(ra, axis=1) + ba_ref[...])
    ig = jax.nn.sigmoid(jnp.concatenate(rx, axis=1) + bx_ref[...])
    log_a = (-LRU_C * r) * jax.nn.softplus(-lam_ref[...])
    a = jnp.exp(log_a)
    b = jnp.sqrt(-jnp.tanh(log_a) * (1.0 + a * a)) * ig * xc

    seg_len = rows // SUBLANES
    h_blocks, carries = [], []
    for nb in range(width // LANES):
        lanes = slice(nb * LANES, (nb + 1) * LANES)
        a_sc[nb] = a[:, lanes]
        b_sc[nb] = b[:, lanes]
        hloc, ploc = [], []
        for k in range(seg_len):
            a_k = a_sc[nb, pl.ds(k, SUBLANES, stride=seg_len), :]
            b_k = b_sc[nb, pl.ds(k, SUBLANES, stride=seg_len), :]
            hloc.append(b_k if k == 0 else a_k * hloc[-1] + b_k)
            ploc.append(a_k if k == 0 else a_k * ploc[-1])
        carry = hcar_ref[:, lanes]
        seg_in = []
        for s in range(SUBLANES):
            seg_in.append(carry)
            carry = ploc[-1][s:s + 1] * carry + hloc[-1][s:s + 1]
        seg_in = jnp.concatenate(seg_in, axis=0)
        carries.append(carry)
        for k in range(seg_len):
            h_sc[nb, pl.ds(k, SUBLANES, stride=seg_len), :] = hloc[k] + ploc[k] * seg_in
        h_blocks.append(h_sc[nb])
    h = jnp.concatenate(h_blocks, axis=1)
    h_last = jnp.concatenate(carries, axis=1)
    hcar_ref[...] = h_last
    hl_ref[0] = h_last

    y = jax.nn.gelu(yb_ref[...]) * h
    o_ref[...] = (y * _rms_scale(y) * g_ref[...]).astype(o_ref.dtype)


def _rglru(z, state_conv, state_h, conv_w, conv_b, wa, ba, wx, bx, lam, g,
           *, n_prompt_chunks, chunks_per_seq, width, xb_col_blk, yb_col_blk):
    t = z.shape[0]
    n_items = t // CHUNK
    npc = n_prompt_chunks
    nblk = width // LRU_BLOCK
    state_map = lambda n: (jnp.maximum(n - npc, 0), 0, 0)
    full2 = lambda n: (0, 0)
    full3 = lambda n: (0, 0, 0)
    kern = functools.partial(_lru_kernel, n_prompt_chunks=npc, chunks_per_seq=chunks_per_seq)
    return pl.pallas_call(
        kern,
        grid=(n_items,),
        in_specs=[
            pl.BlockSpec((CHUNK, width), lambda n: (n, xb_col_blk)),
            pl.BlockSpec((CHUNK, width), lambda n: (n, yb_col_blk)),
            pl.BlockSpec((1, CONV_WIDTH - 1, width), state_map),
            pl.BlockSpec((1, 1, width), state_map),
            pl.BlockSpec((CONV_WIDTH, width), full2),
            pl.BlockSpec((1, width), full2),
            pl.BlockSpec((nblk, LRU_BLOCK, LRU_BLOCK), full3),
            pl.BlockSpec((1, width), full2),
            pl.BlockSpec((nblk, LRU_BLOCK, LRU_BLOCK), full3),
            pl.BlockSpec((1, width), full2),
            pl.BlockSpec((1, width), full2),
            pl.BlockSpec((1, width), full2),
        ],
        out_specs=[
            pl.BlockSpec((CHUNK, width), lambda n: (n, 0)),
            pl.BlockSpec((1, 1, width), lambda n: (n, 0, 0)),
        ],
        out_shape=[
            jax.ShapeDtypeStruct((t, width), BF16),
            jax.ShapeDtypeStruct((n_items, 1, width), F32),
        ],
        scratch_shapes=[
            pltpu.VMEM((SUBLANES, width), F32),
            pltpu.VMEM((1, width), F32),
            pltpu.VMEM((width // LANES, CHUNK, LANES), F32),
            pltpu.VMEM((width // LANES, CHUNK, LANES), F32),
            pltpu.VMEM((width // LANES, CHUNK, LANES), F32),
        ],
        compiler_params=_params(1),
        name="rglru",
    )(z, z, state_conv, state_h, conv_w, conv_b, wa, ba, wx, bx, lam, g)


def _outproj_kernel(att_ref, lru_ref, wa_ref, wl_ref, xp_ref, xs_ref, g_ref, wr_ref, br_ref,
                    o_ref, route_ref, cnt_ref, meta_ref, ssq_ref, lg_ref,
                    *, n_prompt_tiles, d_model):
    i = pl.program_id(0)
    j = pl.program_id(1)
    acc = jnp.dot(att_ref[...], wa_ref[...].astype(BF16), preferred_element_type=F32)
    acc = acc + jnp.dot(lru_ref[...], wl_ref[...].astype(BF16), preferred_element_type=F32)
    x2 = jnp.where(i < n_prompt_tiles, xp_ref[...], xs_ref[...]) + acc
    o_ref[...] = x2

    @pl.when(jnp.logical_and(i == 0, j == 0))
    def _():
        cnt_ref[...] = jnp.zeros_like(cnt_ref)

    @pl.when(j == 0)
    def _():
        ssq_ref[...] = jnp.zeros_like(ssq_ref)
        lg_ref[...] = jnp.zeros_like(lg_ref)

    ssq_ref[...] += jnp.sum(x2 * x2, axis=-1, keepdims=True)
    lg_ref[...] += jnp.dot((x2 * g_ref[...]).astype(BF16), wr_ref[...],
                           preferred_element_type=F32)

    @pl.when(j == pl.num_programs(1) - 1)
    def _():
        scale = lax.rsqrt(ssq_ref[...] * (1.0 / d_model) + EPS)
        logits = lg_ref[...] * scale + br_ref[...]
        for r0 in range(0, logits.shape[0], ROUTE_ROWS):
            route = _route_rows(logits[r0:r0 + ROUTE_ROWS], cnt_ref)
            route_ref[r0:r0 + ROUTE_ROWS, :] = route
            meta_ref[:, r0:r0 + ROUTE_ROWS] = route.T[
                ROUTE_GROUP_LANE:ROUTE_RANK_LANE + 1, :].astype(jnp.int32)


def _out_proj(att, lru, w, xp, xs, g, wr, br, *, tm, tn):
    t, aw = att.shape
    lw = lru.shape[1]
    d = w.shape[1]
    assert aw == lw and w.shape[0] == aw + lw
    npt = xp.shape[0] // tm
    kern = functools.partial(_outproj_kernel, n_prompt_tiles=npt, d_model=d)
    return pl.pallas_call(
        kern,
        grid=(t // tm, d // tn),
        in_specs=[
            pl.BlockSpec((tm, aw), lambda i, j: (i, 0)),
            pl.BlockSpec((tm, lw), lambda i, j: (i, 0)),
            pl.BlockSpec((aw, tn), lambda i, j: (0, j)),
            pl.BlockSpec((lw, tn), lambda i, j: (1, j)),
            pl.BlockSpec((tm, tn), lambda i, j: (jnp.minimum(i, npt - 1), j)),
            pl.BlockSpec((tm, tn), lambda i, j: (jnp.maximum(i - npt, 0), j)),
            pl.BlockSpec((1, tn), lambda i, j: (0, j)),
            pl.BlockSpec((tn, LANES), lambda i, j: (j, 0)),
            pl.BlockSpec((1, LANES), lambda i, j: (0, 0)),
        ],
        out_specs=[
            pl.BlockSpec((tm, tn), lambda i, j: (i, j)),
            pl.BlockSpec((tm, LANES), lambda i, j: (i, 0)),
            pl.BlockSpec((1, LANES), lambda i, j: (0, 0)),
            pl.BlockSpec((2, tm), lambda i, j: (0, i)),
        ],
        out_shape=[
            jax.ShapeDtypeStruct((t, d), F32),
            jax.ShapeDtypeStruct((t, LANES), F32),
            jax.ShapeDtypeStruct((1, LANES), F32),
            jax.ShapeDtypeStruct((2, t), jnp.int32),
        ],
        scratch_shapes=[pltpu.VMEM((tm, 1), F32), pltpu.VMEM((tm, LANES), F32)],
        compiler_params=_params(2),
        name="out_proj",
    )(att, lru, w, w, xp, xs, g, wr, br)


ROUTE_GROUP_LANE = EXPERTS_PER_GROUP
ROUTE_RANK_LANE = EXPERTS_PER_GROUP + 1
ROUTE_ROWS = 256


def _route_rows(logits, cnt_ref):
    lane = lax.broadcasted_iota(jnp.int32, logits.shape, 1).astype(F32)
    ninf = -jnp.inf

    def first_argmax(v, vmax):
        return jnp.min(jnp.where(v == vmax, lane, float(LANES)), axis=-1, keepdims=True)

    gl = jnp.where(lane < N_GROUPS, logits, ninf)
    gm = jnp.max(gl, axis=-1, keepdims=True)
    g_idx = first_argmax(gl, gm)
    g_w = 1.0 / jnp.sum(jnp.exp(gl - gm), axis=-1, keepdims=True)

    lo = N_GROUPS + EXPERTS_PER_GROUP * g_idx
    el = jnp.where(jnp.logical_and(lane >= lo, lane < lo + EXPERTS_PER_GROUP), logits, ninf)
    v1 = jnp.max(el, axis=-1, keepdims=True)
    i1 = first_argmax(el, v1)
    el2 = jnp.where(lane == i1, ninf, el)
    v2 = jnp.max(el2, axis=-1, keepdims=True)
    i2 = first_argmax(el2, v2)
    e2 = jnp.exp(v2 - v1)
    w1 = (1.0 / (1.0 + e2)) * g_w
    w2 = (e2 / (1.0 + e2)) * g_w
    gates = jnp.where(lane == i1 - lo, w1, 0.0) + jnp.where(lane == i2 - lo, w2, 0.0)

    rows = logits.shape[0]
    onehot = jnp.where(lane == g_idx, 1.0, 0.0)
    r_i = lax.broadcasted_iota(jnp.int32, (rows, rows), 0)
    c_i = lax.broadcasted_iota(jnp.int32, (rows, rows), 1)
    tri = jnp.where(c_i < r_i, 1.0, 0.0).astype(BF16)
    before = jnp.dot(tri, onehot.astype(BF16), preferred_element_type=F32) + cnt_ref[...]
    rank = jnp.sum(onehot * before, axis=-1, keepdims=True)
    cnt_ref[...] += jnp.sum(onehot, axis=0, keepdims=True)
    return (gates + jnp.where(lane == ROUTE_GROUP_LANE, g_idx, 0.0)
            + jnp.where(lane == ROUTE_RANK_LANE, rank, 0.0))


PLAN_CHUNK = 1024


def _plan_items(per, n_used, n_items_max, grp, p_ref, e_ref, key_ref, n_ref, run_ref):
    tstart_s, tend_s, tiles_s, before_s, next_s = grp
    n_items = per * n_used
    n_ref[0] = n_items

    def item(w, carry):
        valid = w < n_items
        wc = jnp.minimum(w, n_items - 1)
        gi = jnp.int32(0)
        for g in range(N_GROUPS):
            gi = gi + (wc >= per * tend_s[g]).astype(jnp.int32)
        local = wc - per * tstart_s[gi]
        tiles_w = tiles_s[gi]
        lpart = jnp.int32(0)
        for k in range(1, per):
            lpart = lpart + (local >= k * tiles_w).astype(jnp.int32)
        ltile = local - lpart * tiles_w
        spare = jnp.maximum(w - n_items, 0)
        part = jnp.where(valid, lpart, spare % per)
        p_ref[w] = jnp.where(valid, tstart_s[gi] + ltile, n_used + spare // per)
        e_ref[w] = part
        key = gi * per + part
        key_ref[w] = key
        nxt_g = next_s[gi]
        run_ref[3 * w] = jnp.logical_and(valid, ltile == 0).astype(jnp.int32)
        run_ref[3 * w + 1] = (per * before_s[gi] + part) % 2
        run_ref[3 * w + 2] = jnp.where(part + 1 < per, key + 1,
                                       jnp.where(nxt_g >= 0, nxt_g * per, -1))
        return carry
    lax.fori_loop(0, n_items_max, item, 0)


def _plan_kernel(group_ref, rank_ref, cnt_ref, dest_ref, src_ref, nused_ref,
                 up_p, up_e, up_key, up_n, up_run, dn_p, dn_e, dn_key, dn_n, dn_run,
                 tstart_s, tend_s, tiles_s, before_s, next_s, row0_s,
                 *, tile, n_tiles_max, up_parts, dn_parts):
    s = pl.program_id(0)

    @pl.when(s == 0)
    def _():
        start = jnp.int32(0)
        owned = jnp.int32(0)
        for g in range(N_GROUPS):
            tiles = (cnt_ref[g] + tile - 1) // tile
            tstart_s[g] = start
            row0_s[g] = start * tile
            tiles_s[g] = tiles
            before_s[g] = owned
            start = start + tiles
            tend_s[g] = start
            owned = owned + (tiles > 0).astype(jnp.int32)
        nxt = jnp.int32(-1)
        for g in reversed(range(N_GROUPS)):
            next_s[g] = nxt
            nxt = jnp.where(tiles_s[g] > 0, g, nxt)
        nused_ref[0] = start

        def clear(r, carry):
            src_ref[r] = 0
            return carry
        lax.fori_loop(0, n_tiles_max * tile, clear, 0, unroll=8)

        grp = (tstart_s, tend_s, tiles_s, before_s, next_s)
        _plan_items(up_parts, start, up_parts * n_tiles_max, grp, up_p, up_e, up_key, up_n, up_run)
        _plan_items(dn_parts, start, dn_parts * n_tiles_max, grp, dn_p, dn_e, dn_key, dn_n, dn_run)

    def place(i, carry):
        t = s * PLAN_CHUNK + i
        d = row0_s[group_ref[i]] + rank_ref[i]
        dest_ref[t] = d
        src_ref[d] = t
        return carry
    lax.fori_loop(0, PLAN_CHUNK, place, 0, unroll=8)


def _moe_plan(group, rank, counts, *, tile, n_tiles_max, up_parts, dn_parts):
    t = group.shape[0]
    i32 = jnp.int32
    smem = functools.partial(pl.BlockSpec, memory_space=pltpu.SMEM)
    n_up, n_dn = up_parts * n_tiles_max, dn_parts * n_tiles_max
    shapes = [(t,), (n_tiles_max * tile,), (1,),
              (n_up,), (n_up,), (n_up,), (1,), (3 * n_up,),
              (n_dn,), (n_dn,), (n_dn,), (1,), (3 * n_dn,)]
    outs = pl.pallas_call(
        functools.partial(_plan_kernel, tile=tile, n_tiles_max=n_tiles_max,
                          up_parts=up_parts, dn_parts=dn_parts),
        grid=(t // PLAN_CHUNK,),
        in_specs=[smem((PLAN_CHUNK,), lambda s: (s,)), smem((PLAN_CHUNK,), lambda s: (s,)),
                  smem()],
        out_specs=[smem() for _ in shapes],
        out_shape=[jax.ShapeDtypeStruct(sh, i32) for sh in shapes],
        scratch_shapes=[pltpu.SMEM((N_GROUPS,), i32) for _ in range(6)],
        compiler_params=_params(1),
        name="moe_plan",
    )(group, rank, counts)
    dest, src, n_used = outs[:3]
    return dict(dest=dest, src=src, n_used=n_used, up=tuple(outs[3:8]), down=tuple(outs[8:13]))


def _row_copy(src_hbm, dst_vmem, src_row, dst_row, sem):
    return pltpu.make_async_copy(src_hbm.at[pl.ds(src_row, 1)], dst_vmem.at[pl.ds(dst_row, 1)], sem)


def _start_row_gather(idx_ref, base, n_rows, pairs):
    def body(r, carry):
        i = idx_ref[base + r]
        for src_hbm, dst_vmem, sem in pairs:
            _row_copy(src_hbm, dst_vmem, i, r, sem).start()
        return carry
    lax.fori_loop(0, n_rows, body, 0, unroll=8)


def _start_rows(idx_ref, idx_base, row0, n_rows, pairs):
    for k in range(n_rows):
        i = idx_ref[idx_base + row0 + k]
        for src_hbm, dst_vmem, sem in pairs:
            _row_copy(src_hbm, dst_vmem, i, row0 + k, sem).start()


def _norm_tile_and_prefetch(load_rows, g_ref, o_ref, n_rows, has_next, idx_ref, next_base,
                            next_pairs):
    @pl.when(has_next)
    def _():
        _rmsnorm_rows(load_rows, g_ref, o_ref, n_rows,
                      per_trip=lambda row0: _start_rows(idx_ref, next_base, row0, NORM_ROWS,
                                                        next_pairs))

    @pl.when(jnp.logical_not(has_next))
    def _():
        _rmsnorm_rows(load_rows, g_ref, o_ref, n_rows)


def _wait_row_gather(n_rows, pairs):
    for src_hbm, dst_vmem, sem in pairs:
        pltpu.make_async_copy(src_hbm.at[pl.ds(0, n_rows)], dst_vmem, sem).wait()


def _dispatch_kernel(src_ref, nused_ref, x2_hbm, route_hbm, g_ref, xs_ref, gs_ref,
                     xbuf, gbuf, sem):
    p = pl.program_id(0)
    n_used = nused_ref[0]
    tile = xbuf.shape[1]
    slot = p % 2

    def pairs(s):
        return [(x2_hbm, xbuf.at[s], sem.at[0, s]), (route_hbm, gbuf.at[s], sem.at[1, s])]

    @pl.when(p == 0)
    def _():
        _start_row_gather(src_ref, 0, tile, pairs(0))

    @pl.when(p < n_used)
    def _():
        _wait_row_gather(tile, pairs(slot))
        gs_ref[...] = gbuf[slot]
        _norm_tile_and_prefetch(lambda rows: xbuf[slot, rows, :], g_ref, xs_ref, tile,
                                p + 1 < n_used, src_ref, (p + 1) * tile, pairs(1 - slot))

    @pl.when(p >= n_used)
    def _():
        xs_ref[...] = jnp.zeros_like(xs_ref)
        gs_ref[...] = jnp.zeros_like(gs_ref)


def _dispatch(x2, route, g, src, n_used, *, tile, n_tiles_max):
    t, d = x2.shape
    tile_map = lambda p, src_ref, n_ref: (p, 0)
    grid_spec = pltpu.PrefetchScalarGridSpec(
        num_scalar_prefetch=2,
        grid=(n_tiles_max,),
        in_specs=[
            pl.BlockSpec(memory_space=pl.ANY),
            pl.BlockSpec(memory_space=pl.ANY),
            pl.BlockSpec((1, d), lambda p, s, n: (0, 0)),
        ],
        out_specs=[
            pl.BlockSpec((tile, d), tile_map),
            pl.BlockSpec((tile, LANES), tile_map),
        ],
        scratch_shapes=[
            pltpu.VMEM((2, tile, d), F32),
            pltpu.VMEM((2, tile, LANES), F32),
            pltpu.SemaphoreType.DMA((2, 2)),
        ],
    )
    return pl.pallas_call(
        _dispatch_kernel,
        grid_spec=grid_spec,
        out_shape=[
            jax.ShapeDtypeStruct((n_tiles_max * tile, d), BF16),
            jax.ShapeDtypeStruct((n_tiles_max * tile, LANES), F32),
        ],
        compiler_params=_params(1),
        name="moe_dispatch",
    )(src, n_used, x2, route, g)


def _stream_run_weights(w, n_items, key_ref, run_ref, copies):
    is_first = run_ref[3 * w]
    slot = run_ref[3 * w + 1]
    nxt_key = run_ref[3 * w + 2]
    valid = w < n_items

    @pl.when(jnp.logical_and(valid, w == 0))
    def _():
        for cp in copies(key_ref[0], 0):
            cp.start()

    @pl.when(jnp.logical_and(valid, is_first == 1))
    def _():
        for cp in copies(key_ref[w], slot):
            cp.wait()

        @pl.when(nxt_key >= 0)
        def _():
            for cp in copies(nxt_key, 1 - slot):
                cp.start()

    return slot


def _moe_up_kernel(ip_ref, ie_ref, key_ref, n_ref, run_ref, xs_ref, gs_ref, w1_hbm, w3_hbm,
                   hid_ref, w1_buf, w3_buf, sem):
    w = pl.program_id(0)

    n_e = w1_buf.shape[1]
    f = w1_buf.shape[3]

    def copies(key, slot):
        experts = pl.ds(key * n_e, n_e)
        return [pltpu.make_async_copy(w1_hbm.at[experts], w1_buf.at[slot], sem.at[0, slot]),
                pltpu.make_async_copy(w3_hbm.at[experts], w3_buf.at[slot], sem.at[1, slot])]

    slot = _stream_run_weights(w, n_ref[0], key_ref, run_ref, copies)

    @pl.when(w < n_ref[0])
    def _():
        x = xs_ref[...]
        lane = lax.broadcasted_iota(jnp.int32, gs_ref.shape, 1)
        for e in range(n_e):
            a = jnp.dot(x, w1_buf[slot, e].astype(BF16), preferred_element_type=F32)
            b = jnp.dot(x, w3_buf[slot, e].astype(BF16), preferred_element_type=F32)
            in_group = ie_ref[w] * n_e + e
            gate = jnp.sum(jnp.where(lane == in_group, gs_ref[...], 0.0), axis=-1, keepdims=True)
            hid_ref[:, e * f:(e + 1) * f] = (jax.nn.silu(a) * b * gate).astype(hid_ref.dtype)

    @pl.when(w >= n_ref[0])
    def _():
        hid_ref[...] = jnp.zeros_like(hid_ref)


def _moe_up(xs, gs, w1, w3, items, *, tile):
    rows, d = xs.shape
    f = w1.shape[2]
    ip, ie, key, n_items, run = items
    grid_spec = pltpu.PrefetchScalarGridSpec(
        num_scalar_prefetch=5,
        grid=(ip.shape[0],),
        in_specs=[
            pl.BlockSpec((tile, d), lambda w, ip, ie, key, n, run: (ip[w], 0)),
            pl.BlockSpec((tile, LANES), lambda w, ip, ie, key, n, run: (ip[w], 0)),
            pl.BlockSpec(memory_space=pl.ANY),
            pl.BlockSpec(memory_space=pl.ANY),
        ],
        out_specs=pl.BlockSpec((tile, MOE_UP_EXPERTS * f),
                               lambda w, ip, ie, key, n, run: (ip[w], ie[w])),
        scratch_shapes=[
            pltpu.VMEM((2, MOE_UP_EXPERTS, d, f), w1.dtype),
            pltpu.VMEM((2, MOE_UP_EXPERTS, d, f), w3.dtype),
            pltpu.SemaphoreType.DMA((2, 2)),
        ],
    )
    return pl.pallas_call(
        _moe_up_kernel,
        grid_spec=grid_spec,
        out_shape=jax.ShapeDtypeStruct((rows, EXPERTS_PER_GROUP * f), BF16),
        compiler_params=_params(1),
        name="moe_up",
    )(ip, ie, key, n_items, run, xs, gs, w1, w3)


def _moe_down_kernel(ip_ref, ih_ref, key_ref, n_ref, run_ref, hid_ref, w2_hbm, y_ref,
                     w2_buf, sem, *, n_halves):
    w = pl.program_id(0)
    tn = w2_buf.shape[2]

    def copies(key, slot):
        g = key // n_halves
        col = pl.multiple_of((key % n_halves) * tn, tn)
        return [pltpu.make_async_copy(w2_hbm.at[g, :, pl.ds(col, tn)], w2_buf.at[slot],
                                      sem.at[slot])]

    slot = _stream_run_weights(w, n_ref[0], key_ref, run_ref, copies)

    @pl.when(w < n_ref[0])
    def _():
        y_ref[...] = jnp.dot(hid_ref[...], w2_buf[slot].astype(BF16),
                             preferred_element_type=F32)

    @pl.when(w >= n_ref[0])
    def _():
        y_ref[...] = jnp.zeros_like(y_ref)


def _moe_down(hid, w2g, items, *, tile, n_halves):
    rows, k = hid.shape
    d = w2g.shape[2]
    tn = d // n_halves
    ip, ih, key, n_items, run = items
    grid_spec = pltpu.PrefetchScalarGridSpec(
        num_scalar_prefetch=5,
        grid=(ip.shape[0],),
        in_specs=[
            pl.BlockSpec((tile, k), lambda w, ip, ih, key, n, run: (ip[w], 0)),
            pl.BlockSpec(memory_space=pl.ANY),
        ],
        out_specs=pl.BlockSpec((tile, tn), lambda w, ip, ih, key, n, run: (ip[w], ih[w])),
        scratch_shapes=[
            pltpu.VMEM((2, k, tn), w2g.dtype),
            pltpu.SemaphoreType.DMA((2,)),
        ],
    )
    return pl.pallas_call(
        functools.partial(_moe_down_kernel, n_halves=n_halves),
        grid_spec=grid_spec,
        out_shape=jax.ShapeDtypeStruct((rows, d), F32),
        compiler_params=_params(1),
        name="moe_down",
    )(ip, ih, key, n_items, run, hid, w2g)


def _final_kernel(dest_ref, x_ref, ys_hbm, g_ref, o_ref, ybuf, sem, *, tile0):
    i = pl.program_id(0)
    tm = ybuf.shape[1]
    slot = i % 2

    def pairs(s):
        return [(ys_hbm, ybuf.at[s], sem.at[s])]

    @pl.when(i == 0)
    def _():
        _start_row_gather(dest_ref, tile0 * tm, tm, pairs(0))

    _wait_row_gather(tm, pairs(slot))
    _norm_tile_and_prefetch(lambda rows: x_ref[rows, :] + ybuf[slot, rows, :], g_ref, o_ref, tm,
                            i + 1 < pl.num_programs(0), dest_ref, (tile0 + i + 1) * tm,
                            pairs(1 - slot))


def _final(x2, ys, dest, g, *, row0, n_rows, tm):
    d = x2.shape[1]
    off = row0 // tm
    grid_spec = pltpu.PrefetchScalarGridSpec(
        num_scalar_prefetch=1,
        grid=(n_rows // tm,),
        in_specs=[
            pl.BlockSpec((tm, d), lambda i, dest: (i + off, 0)),
            pl.BlockSpec(memory_space=pl.ANY),
            pl.BlockSpec((1, d), lambda i, dest: (0, 0)),
        ],
        out_specs=pl.BlockSpec((tm, d), lambda i, dest: (i, 0)),
        scratch_shapes=[pltpu.VMEM((2, tm, d), F32), pltpu.SemaphoreType.DMA((2,))],
    )
    return pl.pallas_call(
        functools.partial(_final_kernel, tile0=off),
        grid_spec=grid_spec,
        out_shape=jax.ShapeDtypeStruct((n_rows, d), F32),
        compiler_params=_params(1),
        name="final_norm",
    )(dest, x2, ys, g)


def _rope_tables(pos):
    half = ROT_DIM // 2
    inv = ROPE_THETA ** (-np.arange(half, dtype=np.float64) / half)
    ang = np.asarray(pos, np.float64)[:, None] * inv[None, :]
    cos, sin = np.cos(ang), np.sin(ang)
    ones = np.ones((ang.shape[0], HEAD_DIM - ROT_DIM))
    zeros = np.zeros((ang.shape[0], HEAD_DIM - half))
    c = np.concatenate([cos, cos, ones], axis=1)
    s_hi = np.concatenate([-sin, zeros], axis=1)
    s_lo = np.concatenate([np.zeros_like(sin), sin, 0.0 * ones], axis=1)
    rep = LANES // HEAD_DIM
    return tuple(jnp.asarray(np.tile(t, (1, rep)), F32) for t in (c, s_hi, s_lo))


def kernel(x_prompt, x_sample, cache_k, cache_v, state_conv, state_h, norm_mix, w_in, conv_w,
           conv_b, w_gate_a, b_gate_a, w_gate_x, b_gate_x, lru_lambda, sinks, norm_attn_out,
           norm_lru_out, w_out, norm_ffn, w_group, b_group, w_expert_router, b_expert_router,
           w1, w3, w2, norm_final):
    batch, seq, d_model = x_prompt.shape
    dec_batch, dec_seq, _ = x_sample.shape
    depth = w_in.shape[0]
    assert depth == 1 and dec_seq == CHUNK and seq % CHUNK == 0
    n_heads = sinks.shape[1]
    att_width = n_heads * HEAD_DIM
    n_kv = cache_k.shape[3]
    kv_cols = n_kv * HEAD_DIM
    lru_width = lru_lambda.shape[1]
    past_len = 1024
    cw = cache_k.shape[2]
    assert cw == WIN_CHUNKS * CHUNK
    n_p, n_s = batch * seq, dec_batch * dec_seq
    chunks_per_seq = seq // CHUNK
    n_prompt_chunks = n_p // CHUNK

    xp = x_prompt.reshape(n_p, d_model)
    xs = x_sample.reshape(n_s, d_model)

    w_r = w_in[0]
    tn = 2 * kv_cols
    assert tn == 512 and att_width % tn == 0 and lru_width % tn == 0
    n_q_tiles = att_width // tn
    kv_tile = n_q_tiles

    tm_in = 1024
    assert seq % tm_in == 0 and n_s % tm_in == 0 and tm_in % dec_seq == 0
    pos = np.concatenate([np.arange(seq), np.tile(past_len + np.arange(dec_seq), tm_in // dec_seq)])
    c_tab, shi_tab, slo_tab = _rope_tables(pos)

    z = _in_proj(xp, xs, norm_mix, w_r, c_tab, shi_tab, slo_tab, tm=tm_in, tn=tn,
                 n_q_tiles=n_q_tiles, kv_tile=kv_tile, prompt_tiles_per_seq=seq // tm_in)

    k_col = att_width + 2 * lru_width
    v_col = k_col + kv_cols
    att = _attention(
        z, cache_k[0].reshape(dec_batch, cw, kv_cols), cache_v[0].reshape(dec_batch, cw, kv_cols),
        sinks[0], norm_attn_out,
        n_prompt_chunks=n_prompt_chunks, chunks_per_seq=chunks_per_seq, att_width=att_width,
        n_kv=n_kv, k_col_blk=k_col // kv_cols, v_col_blk=v_col // kv_cols)

    nblk = lru_width // LRU_BLOCK
    lru, h_tiles = _rglru(
        z, state_conv[0], state_h[0].reshape(dec_batch, 1, lru_width), conv_w[0], conv_b,
        w_gate_a[0].astype(BF16), b_gate_a[0].reshape(1, lru_width),
        w_gate_x[0].astype(BF16), b_gate_x[0].reshape(1, lru_width),
        lru_lambda, norm_lru_out,
        n_prompt_chunks=n_prompt_chunks, chunks_per_seq=chunks_per_seq, width=lru_width,
        xb_col_blk=att_width // lru_width, yb_col_blk=att_width // lru_width + 1)
    del nblk

    n_routes = N_GROUPS + N_GROUPS * EXPERTS_PER_GROUP
    wr = jnp.concatenate([w_group[0], w_expert_router[0],
                          jnp.zeros((d_model, LANES - n_routes), F32)], axis=1).astype(BF16)
    br = jnp.concatenate([b_group[0], b_expert_router[0],
                          jnp.zeros((LANES - n_routes,), F32)]).reshape(1, LANES)
    x2, route, counts, meta = _out_proj(att, lru, w_out[0], xp, xs, norm_ffn, wr, br, tm=1024, tn=512)

    n_tiles_max = (n_p + n_s) // MOE_TILE + N_GROUPS
    tabs = _moe_plan(meta[0], meta[1], counts[0, :N_GROUPS].astype(jnp.int32), tile=MOE_TILE,
                     n_tiles_max=n_tiles_max, up_parts=EXPERTS_PER_GROUP // MOE_UP_EXPERTS,
                     dn_parts=MOE_DOWN_HALVES)
    xsort, gsort = _dispatch(x2, route, norm_ffn, tabs["src"], tabs["n_used"],
                             tile=MOE_TILE, n_tiles_max=n_tiles_max)
    hid = _moe_up(xsort, gsort, w1[0], w3[0], tabs["up"], tile=MOE_TILE)
    d_expert = w2.shape[2]
    w2g = w2[0].reshape(N_GROUPS, EXPERTS_PER_GROUP * d_expert, d_model)
    ysort = _moe_down(hid, w2g, tabs["down"], tile=MOE_TILE, n_halves=MOE_DOWN_HALVES)

    g_fin = norm_final.reshape(1, d_model)
    y_prompt = _final(x2, ysort, tabs["dest"], g_fin, row0=0, n_rows=n_p,
                      tm=MOE_TILE).reshape(batch, seq, d_model)
    y_sample = _final(x2, ysort, tabs["dest"], g_fin, row0=n_p, n_rows=n_s,
                      tm=MOE_TILE).reshape(dec_batch, dec_seq, d_model)

    win = min(WIN_CHUNKS * CHUNK, seq)

    def prompt_tail(col, width, rows):
        return jnp.stack([lax.slice(z, ((b + 1) * seq - rows, col), ((b + 1) * seq, col + width))
                          for b in range(batch)])

    def sample_rows(col, width):
        return lax.slice(z, (n_p, col), (n_p + n_s, col + width)).reshape(dec_batch, dec_seq, width)

    k_prompt = prompt_tail(k_col, kv_cols, win).reshape(1, batch, win, n_kv, HEAD_DIM)
    v_prompt = prompt_tail(v_col, kv_cols, win).reshape(1, batch, win, n_kv, HEAD_DIM)
    conv_prompt = prompt_tail(att_width, lru_width, CONV_WIDTH - 1)[None]
    h_prompt = h_tiles[:n_prompt_chunks, 0].reshape(batch, chunks_per_seq, lru_width)[:, -1][None]

    ks = sample_rows(k_col, kv_cols).reshape(dec_batch, dec_seq, n_kv, HEAD_DIM)
    vs = sample_rows(v_col, kv_cols).reshape(dec_batch, dec_seq, n_kv, HEAD_DIM)
    k_sample = jnp.concatenate([cache_k[0], ks], axis=1)[:, -cw:][None]
    v_sample = jnp.concatenate([cache_v[0], vs], axis=1)[:, -cw:][None]
    xs_rows = sample_rows(att_width, lru_width)[:, dec_seq - (CONV_WIDTH - 1):]
    conv_sample = jnp.concatenate([state_conv[0], xs_rows], axis=1)[:, -(CONV_WIDTH - 1):][None]
    h_sample = h_tiles[n_prompt_chunks:, 0][None]

    return (y_prompt, y_sample, k_prompt, v_prompt, conv_prompt, h_prompt,
            k_sample, v_sample, conv_sample, h_sample)
```

```python
import functools

import numpy as np
import jax
import jax.numpy as jnp
from jax import lax
from jax.experimental import pallas as pl
from jax.experimental.pallas import tpu as pltpu

F32 = jnp.float32
BF16 = jnp.bfloat16

CHUNK = 64
HEAD_DIM = 64
KV_GROUP = 8
WIN_CHUNKS = 2
ROT_DIM = 16
ROPE_THETA = 500000.0
LRU_BLOCK = 128
CONV_WIDTH = 4
LRU_C = 8.0
N_GROUPS = 8
EXPERTS_PER_GROUP = 4
EPS = 1e-6
NEG = -1e30
PAST_LEN = 1024
LANES = 128
PROJ_ROWS = 1024
OUT_PROJ_COLS = 512
INPROJ_ROW_BLOCK = 128
MOE_TILE = 256
MOE_UP_EXPERTS = 2
MOE_DOWN_HALVES = 1
VMEM_LIMIT = 56 * 1024 * 1024


def _params(n_axes, vmem=VMEM_LIMIT):
    return pltpu.CompilerParams(
        dimension_semantics=("arbitrary",) * n_axes, vmem_limit_bytes=vmem)


def _rms_scale(x):
    return lax.rsqrt(jnp.mean(x * x, axis=-1, keepdims=True) + EPS)


NORM_ROWS = 16


def _rmsnorm_rows(load_rows, g_ref, o_ref, n_rows, per_trip=None):
    def body(r, carry):
        row0 = r * NORM_ROWS if isinstance(r, int) else pl.multiple_of(r * NORM_ROWS, NORM_ROWS)
        rows = pl.ds(row0, NORM_ROWS)
        x = load_rows(rows)
        o_ref[rows, :] = (x * _rms_scale(x) * g_ref[...]).astype(o_ref.dtype)
        if per_trip is not None:
            per_trip(row0)
        return carry
    if per_trip is not None:
        for r in range(n_rows // NORM_ROWS):
            body(r, 0)
    else:
        lax.fori_loop(0, n_rows // NORM_ROWS, body, 0, unroll=4)


def _rope_block(zb, c, s_hi, s_lo):
    return (zb * c + pltpu.roll(zb, LANES - ROT_DIM // 2, 1) * s_hi
            + pltpu.roll(zb, ROT_DIM // 2, 1) * s_lo)


def _inproj_kernel(xp_hbm, xs_hbm, g_ref, w_ref, c_ref, shi_ref, slo_ref, o_ref, x_buf, h_ref,
                   sem, *, n_prompt_tiles, n_q_tiles, kv_tile):
    i = pl.program_id(0)
    j = pl.program_id(1)
    tm, tn = o_ref.shape

    def x_tile_copy(src_hbm, tile):
        return pltpu.make_async_copy(src_hbm.at[pl.ds(tile * tm, tm)], x_buf, sem)

    def start_x(tile):
        @pl.when(tile < n_prompt_tiles)
        def _():
            x_tile_copy(xp_hbm, tile).start()

        @pl.when(tile >= n_prompt_tiles)
        def _():
            x_tile_copy(xs_hbm, tile - n_prompt_tiles).start()

    @pl.when(jnp.logical_and(i == 0, j == 0))
    def _():
        start_x(0)

    @pl.when(j == 0)
    def _():
        x_tile_copy(xp_hbm, 0).wait()
        _rmsnorm_rows(lambda rows: x_buf[rows, :], g_ref, h_ref, tm)

    @pl.when(jnp.logical_and(j == 1, i + 1 < pl.num_programs(0)))
    def _():
        start_x(i + 1)

    w = w_ref[...].astype(BF16)
    n_blk = tn // LANES
    for r0 in range(0, tm, INPROJ_ROW_BLOCK):
        rows = slice(r0, r0 + INPROJ_ROW_BLOCK)
        z = jnp.dot(h_ref[rows, :], w, preferred_element_type=F32)
        c, shi, slo = c_ref[rows, :], shi_ref[rows, :], slo_ref[rows, :]
        for b in range(n_blk):
            is_rope = j < n_q_tiles
            if b < n_blk // 2:
                is_rope = jnp.logical_or(is_rope, j == kv_tile)
            zb = z[:, b * LANES:(b + 1) * LANES]
            o_ref[rows, b * LANES:(b + 1) * LANES] = _rope_block(
                zb, jnp.where(is_rope, c, 1.0), jnp.where(is_rope, shi, 0.0),
                jnp.where(is_rope, slo, 0.0))


def _in_proj(xp, xs, g, w, c, shi, slo, *, tm, tn, n_q_tiles, kv_tile, prompt_tiles_per_seq):
    n_p, d = xp.shape
    n_s = xs.shape[0]
    t = n_p + n_s
    npt = n_p // tm
    n_cols = w.shape[1]
    kern = functools.partial(_inproj_kernel, n_prompt_tiles=npt, n_q_tiles=n_q_tiles,
                             kv_tile=kv_tile)
    n_tiles = n_cols // tn
    assert kv_tile == n_q_tiles

    def table_tile(i, j):
        return jnp.where(i < npt, i % prompt_tiles_per_seq, prompt_tiles_per_seq), 0

    def out_tile(i, j):
        return i, jnp.where(j < kv_tile, j, jnp.where(j == kv_tile, n_tiles - 1, j - 1))

    return pl.pallas_call(
        kern,
        grid=(t // tm, n_tiles),
        in_specs=[
            pl.BlockSpec(memory_space=pl.ANY),
            pl.BlockSpec(memory_space=pl.ANY),
            pl.BlockSpec((1, d), lambda i, j: (0, 0)),
            pl.BlockSpec((d, tn), lambda i, j: (0, j)),
            pl.BlockSpec((tm, LANES), table_tile),
            pl.BlockSpec((tm, LANES), table_tile),
            pl.BlockSpec((tm, LANES), table_tile),
        ],
        out_specs=pl.BlockSpec((tm, tn), out_tile),
        out_shape=jax.ShapeDtypeStruct((t, n_cols), F32),
        scratch_shapes=[pltpu.VMEM((tm, d), xp.dtype), pltpu.VMEM((tm, d), BF16),
                        pltpu.SemaphoreType.DMA(())],
        compiler_params=_params(2),
        name="in_proj",
    )(xp, xs, g, w, c, shi, slo)


ATTN_CHUNKS = 4


def _attn_chunk(q, k_band, v_band, lo, sink_ref, n_kv):
    n_keys = k_band.shape[0]
    key = lax.broadcasted_iota(jnp.int32, (n_keys, KV_GROUP * CHUNK), 0)
    valid = key >= lo
    outs = []
    for gi in range(n_kv):
        kg = k_band[:, gi * HEAD_DIM:(gi + 1) * HEAD_DIM]
        vg = v_band[:, gi * HEAD_DIM:(gi + 1) * HEAD_DIM]
        heads = [gi * KV_GROUP + h for h in range(KV_GROUP)]
        qg = jnp.concatenate([q[:, h * HEAD_DIM:(h + 1) * HEAD_DIM] for h in heads], axis=0)
        st = lax.dot_general(kg, qg, (((1,), (1,)), ((), ())), preferred_element_type=F32)
        st = jnp.where(valid, st, NEG)
        sink = sink_ref[gi:gi + 1, :]
        m = jnp.maximum(jnp.max(st, axis=0, keepdims=True), sink)
        p = jnp.exp(st - m)
        denom = jnp.sum(p, axis=0, keepdims=True) + jnp.exp(sink - m)
        pn = (p * (1.0 / denom)).astype(BF16)
        og = lax.dot_general(pn, vg, (((0,), (0,)), ((), ())), preferred_element_type=F32)
        outs.extend(og[h * CHUNK:(h + 1) * CHUNK] for h in range(KV_GROUP))
    return jnp.concatenate(outs, axis=1)


def _attn_kernel(q_ref, kp_ref, k_ref, vp_ref, v_ref, kc_ref, vc_ref, sink_ref, g_ref, o_ref,
                 *, n_prompt_tiles, tiles_per_seq, n_kv):
    n = pl.program_id(0)
    is_s = n >= n_prompt_tiles
    has_prev = jnp.logical_or(is_s, n % tiles_per_seq > 0)

    def band(prev_ref, own_ref, cache_ref, c):
        prev, own, cache = prev_ref[...], own_ref[...], cache_ref[c]
        hist = jnp.concatenate([prev, own], axis=0)[c * CHUNK:(c + WIN_CHUNKS) * CHUNK]
        hist = jnp.where(is_s, cache, hist)
        return jnp.concatenate([hist, own[c * CHUNK:(c + 1) * CHUNK]], axis=0).astype(BF16)

    q = (q_ref[...] * (HEAD_DIM ** -0.5)).astype(BF16)
    for c in range(ATTN_CHUNKS):
        lo = jnp.where(has_prev, 0, max(WIN_CHUNKS - c, 0) * CHUNK)
        att = _attn_chunk(q[c * CHUNK:(c + 1) * CHUNK], band(kp_ref, k_ref, kc_ref, c),
                          band(vp_ref, v_ref, vc_ref, c), lo, sink_ref, n_kv)
        o_ref[c * CHUNK:(c + 1) * CHUNK, :] = (
            att * _rms_scale(att) * g_ref[...]).astype(o_ref.dtype)


def _attention(z, cache_k, cache_v, sinks, g, *, n_prompt_chunks, chunks_per_seq,
               att_width, n_kv, k_col_blk, v_col_blk):
    t = z.shape[0]
    rows = ATTN_CHUNKS * CHUNK
    assert ATTN_CHUNKS % WIN_CHUNKS == 0 and chunks_per_seq % ATTN_CHUNKS == 0
    hist_rows = WIN_CHUNKS * CHUNK
    hist_per_tile = rows // hist_rows
    assert cache_k.shape[0] % ATTN_CHUNKS == 0
    n_tiles = t // rows
    npt = n_prompt_chunks // ATTN_CHUNKS
    tps = chunks_per_seq // ATTN_CHUNKS
    kvw = n_kv * HEAD_DIM

    def prev(col):
        return lambda n: (n * hist_per_tile - jnp.minimum(1, n % tps), col)

    cache_map = lambda n: (jnp.maximum(n - npt, 0), 0, 0)
    kern = functools.partial(_attn_kernel, n_prompt_tiles=npt, tiles_per_seq=tps, n_kv=n_kv)
    sink_rows = jnp.repeat(sinks.reshape(n_kv, KV_GROUP), CHUNK, axis=1)
    return pl.pallas_call(
        kern,
        grid=(n_tiles,),
        in_specs=[
            pl.BlockSpec((rows, att_width), lambda n: (n, 0)),
            pl.BlockSpec((hist_rows, kvw), prev(k_col_blk)),
            pl.BlockSpec((rows, kvw), lambda n: (n, k_col_blk)),
            pl.BlockSpec((hist_rows, kvw), prev(v_col_blk)),
            pl.BlockSpec((rows, kvw), lambda n: (n, v_col_blk)),
            pl.BlockSpec((ATTN_CHUNKS, WIN_CHUNKS * CHUNK, kvw), cache_map),
            pl.BlockSpec((ATTN_CHUNKS, WIN_CHUNKS * CHUNK, kvw), cache_map),
            pl.BlockSpec((n_kv, KV_GROUP * CHUNK), lambda n: (0, 0)),
            pl.BlockSpec((1, att_width), lambda n: (0, 0)),
        ],
        out_specs=pl.BlockSpec((rows, att_width), lambda n: (n, 0)),
        out_shape=jax.ShapeDtypeStruct((t, att_width), BF16),
        compiler_params=_params(1),
        name="attention",
    )(z, z, z, z, z, cache_k, cache_v, sink_rows, g)


SUBLANES = 8


def _lru_kernel(xb_ref, yb_ref, sconv_ref, sh_ref, cw_ref, cb_ref, wa_ref, ba_ref,
                wx_ref, bx_ref, lam_ref, g_ref, o_ref, hl_ref, hist_ref, hcar_ref,
                a_sc, b_sc, h_sc, *, n_prompt_chunks, chunks_per_seq):
    n = pl.program_id(0)
    is_s = n >= n_prompt_chunks
    hist_rows = hist_ref.shape[0]

    @pl.when(jnp.logical_and(jnp.logical_not(is_s), n % chunks_per_seq == 0))
    def _():
        hist_ref[...] = jnp.zeros_like(hist_ref)
        hcar_ref[...] = jnp.zeros_like(hcar_ref)

    @pl.when(is_s)
    def _():
        hist_ref[...] = jnp.zeros_like(hist_ref)
        hist_ref[hist_rows - (CONV_WIDTH - 1):, :] = sconv_ref[0]
        hcar_ref[...] = sh_ref[0]

    x = xb_ref[...]
    rows, width = x.shape
    hist = hist_ref[...]
    row8 = lax.broadcasted_iota(jnp.int32, (hist_rows, width), 0)
    xc = cb_ref[...]
    for j in range(CONV_WIDTH):
        s = CONV_WIDTH - 1 - j
        if s == 0:
            xs = x
        else:
            xr = pltpu.roll(x, s, 0)
            head = jnp.where(row8 < s, pltpu.roll(hist, s, 0), xr[:hist_rows])
            xs = jnp.concatenate([head, xr[hist_rows:]], axis=0)
        xc = xc + xs * cw_ref[j:j + 1, :]
    hist_ref[...] = x[rows - hist_rows:]

    xcb = xc.astype(BF16)
    ra, rx = [], []
    for nb in range(width // LRU_BLOCK):
        blk = xcb[:, nb * LRU_BLOCK:(nb + 1) * LRU_BLOCK]
        ra.append(jnp.dot(blk, wa_ref[nb], preferred_element_type=F32))
        rx.append(jnp.dot(blk, wx_ref[nb], preferred_element_type=F32))
    r = jax.nn.sigmoid(jnp.concatenate(ra, axis=1) + ba_ref[...])
    ig = jax.nn.sigmoid(jnp.concatenate(rx, axis=1) + bx_ref[...])
    log_a = (-LRU_C * r) * jax.nn.softplus(-lam_ref[...])
    a = jnp.exp(log_a)
    b = jnp.sqrt(-jnp.tanh(log_a) * (1.0 + a * a)) * ig * xc

    seg_len = rows // SUBLANES
    h_blocks, carries = [], []
    for nb in range(width // LANES):
        lanes = slice(nb * LANES, (nb + 1) * LANES)
        a_sc[nb] = a[:, lanes]
        b_sc[nb] = b[:, lanes]
        hloc, ploc = [], []
        for k in range(seg_len):
            a_k = a_sc[nb, pl.ds(k, SUBLANES, stride=seg_len), :]
            b_k = b_sc[nb, pl.ds(k, SUBLANES, stride=seg_len), :]
            hloc.append(b_k if k == 0 else a_k * hloc[-1] + b_k)
            ploc.append(a_k if k == 0 else a_k * ploc[-1])
        carry = hcar_ref[:, lanes]
        seg_in = []
        for s in range(SUBLANES):
            seg_in.append(carry)
            carry = ploc[-1][s:s + 1] * carry + hloc[-1][s:s + 1]
        seg_in = jnp.concatenate(seg_in, axis=0)
        carries.append(carry)
        for k in range(seg_len):
            h_sc[nb, pl.ds(k, SUBLANES, stride=seg_len), :] = hloc[k] + ploc[k] * seg_in
        h_blocks.append(h_sc[nb])
    h = jnp.concatenate(h_blocks, axis=1)
    h_last = jnp.concatenate(carries, axis=1)
    hcar_ref[...] = h_last
    hl_ref[0] = h_last

    y = jax.nn.gelu(yb_ref[...]) * h
    o_ref[...] = (y * _rms_scale(y) * g_ref[...]).astype(o_ref.dtype)


def _rglru(z, state_conv, state_h, conv_w, conv_b, wa, ba, wx, bx, lam, g,
           *, n_prompt_chunks, chunks_per_seq, width, xb_col_blk, yb_col_blk):
    t = z.shape[0]
    n_items = t // CHUNK
    npc = n_prompt_chunks
    nblk = width // LRU_BLOCK
    state_map = lambda n: (jnp.maximum(n - npc, 0), 0, 0)
    full2 = lambda n: (0, 0)
    full3 = lambda n: (0, 0, 0)
    kern = functools.partial(_lru_kernel, n_prompt_chunks=npc, chunks_per_seq=chunks_per_seq)
    return pl.pallas_call(
        kern,
        grid=(n_items,),
        in_specs=[
            pl.BlockSpec((CHUNK, width), lambda n: (n, xb_col_blk)),
            pl.BlockSpec((CHUNK, width), lambda n: (n, yb_col_blk)),
            pl.BlockSpec((1, CONV_WIDTH - 1, width), state_map),
            pl.BlockSpec((1, 1, width), state_map),
            pl.BlockSpec((CONV_WIDTH, width), full2),
            pl.BlockSpec((1, width), full2),
            pl.BlockSpec((nblk, LRU_BLOCK, LRU_BLOCK), full3),
            pl.BlockSpec((1, width), full2),
            pl.BlockSpec((nblk, LRU_BLOCK, LRU_BLOCK), full3),
            pl.BlockSpec((1, width), full2),
            pl.BlockSpec((1, width), full2),
            pl.BlockSpec((1, width), full2),
        ],
        out_specs=[
            pl.BlockSpec((CHUNK, width), lambda n: (n, 0)),
            pl.BlockSpec((1, 1, width), lambda n: (n, 0, 0)),
        ],
        out_shape=[
            jax.ShapeDtypeStruct((t, width), BF16),
            jax.ShapeDtypeStruct((n_items, 1, width), F32),
        ],
        scratch_shapes=[
            pltpu.VMEM((SUBLANES, width), F32),
            pltpu.VMEM((1, width), F32),
            pltpu.VMEM((width // LANES, CHUNK, LANES), F32),
            pltpu.VMEM((width // LANES, CHUNK, LANES), F32),
            pltpu.VMEM((width // LANES, CHUNK, LANES), F32),
        ],
        compiler_params=_params(1),
        name="rglru",
    )(z, z, state_conv, state_h, conv_w, conv_b, wa, ba, wx, bx, lam, g)


def _outproj_kernel(att_ref, lru_ref, wa_ref, wl_ref, xp_ref, xs_ref, g_ref, wr_ref, br_ref,
                    o_ref, route_ref, cnt_ref, meta_ref, ssq_ref, lg_ref,
                    *, n_prompt_tiles, d_model):
    i = pl.program_id(0)
    j = pl.program_id(1)
    acc = jnp.dot(att_ref[...], wa_ref[...].astype(BF16), preferred_element_type=F32)
    acc = acc + jnp.dot(lru_ref[...], wl_ref[...].astype(BF16), preferred_element_type=F32)
    x2 = jnp.where(i < n_prompt_tiles, xp_ref[...], xs_ref[...]) + acc
    o_ref[...] = x2

    @pl.when(jnp.logical_and(i == 0, j == 0))
    def _():
        cnt_ref[...] = jnp.zeros_like(cnt_ref)

    @pl.when(j == 0)
    def _():
        ssq_ref[...] = jnp.zeros_like(ssq_ref)
        lg_ref[...] = jnp.zeros_like(lg_ref)

    ssq_ref[...] += jnp.sum(x2 * x2, axis=-1, keepdims=True)
    lg_ref[...] += jnp.dot((x2 * g_ref[...]).astype(BF16), wr_ref[...],
                           preferred_element_type=F32)

    @pl.when(j == pl.num_programs(1) - 1)
    def _():
        scale = lax.rsqrt(ssq_ref[...] * (1.0 / d_model) + EPS)
        logits = lg_ref[...] * scale + br_ref[...]
        for r0 in range(0, logits.shape[0], ROUTE_ROWS):
            route = _route_rows(logits[r0:r0 + ROUTE_ROWS], cnt_ref)
            route_ref[r0:r0 + ROUTE_ROWS, :] = route
            meta_ref[:, r0:r0 + ROUTE_ROWS] = route.T[
                ROUTE_GROUP_LANE:ROUTE_RANK_LANE + 1, :].astype(jnp.int32)


def _out_proj(att, lru, w, xp, xs, g, wr, br, *, tm, tn):
    t, aw = att.shape
    lw = lru.shape[1]
    d = w.shape[1]
    assert aw == lw and w.shape[0] == aw + lw
    npt = xp.shape[0] // tm
    kern = functools.partial(_outproj_kernel, n_prompt_tiles=npt, d_model=d)
    return pl.pallas_call(
        kern,
        grid=(t // tm, d // tn),
        in_specs=[
            pl.BlockSpec((tm, aw), lambda i, j: (i, 0)),
            pl.BlockSpec((tm, lw), lambda i, j: (i, 0)),
            pl.BlockSpec((aw, tn), lambda i, j: (0, j)),
            pl.BlockSpec((lw, tn), lambda i, j: (1, j)),
            pl.BlockSpec((tm, tn), lambda i, j: (jnp.minimum(i, npt - 1), j)),
            pl.BlockSpec((tm, tn), lambda i, j: (jnp.maximum(i - npt, 0), j)),
            pl.BlockSpec((1, tn), lambda i, j: (0, j)),
            pl.BlockSpec((tn, LANES), lambda i, j: (j, 0)),
            pl.BlockSpec((1, LANES), lambda i, j: (0, 0)),
        ],
        out_specs=[
            pl.BlockSpec((tm, tn), lambda i, j: (i, j)),
            pl.BlockSpec((tm, LANES), lambda i, j: (i, 0)),
            pl.BlockSpec((1, LANES), lambda i, j: (0, 0)),
            pl.BlockSpec((2, tm), lambda i, j: (0, i)),
        ],
        out_shape=[
            jax.ShapeDtypeStruct((t, d), F32),
            jax.ShapeDtypeStruct((t, LANES), F32),
            jax.ShapeDtypeStruct((1, LANES), F32),
            jax.ShapeDtypeStruct((2, t), jnp.int32),
        ],
        scratch_shapes=[pltpu.VMEM((tm, 1), F32), pltpu.VMEM((tm, LANES), F32)],
        compiler_params=_params(2),
        name="out_proj",
    )(att, lru, w, w, xp, xs, g, wr, br)


ROUTE_GROUP_LANE = EXPERTS_PER_GROUP
ROUTE_RANK_LANE = EXPERTS_PER_GROUP + 1
ROUTE_ROWS = 256


def _route_rows(logits, cnt_ref):
    lane = lax.broadcasted_iota(jnp.int32, logits.shape, 1).astype(F32)
    ninf = -jnp.inf

    def first_argmax(v, vmax):
        return jnp.min(jnp.where(v == vmax, lane, float(LANES)), axis=-1, keepdims=True)

    gl = jnp.where(lane < N_GROUPS, logits, ninf)
    gm = jnp.max(gl, axis=-1, keepdims=True)
    g_idx = first_argmax(gl, gm)
    g_w = 1.0 / jnp.sum(jnp.exp(gl - gm), axis=-1, keepdims=True)

    lo = N_GROUPS + EXPERTS_PER_GROUP * g_idx
    el = jnp.where(jnp.logical_and(lane >= lo, lane < lo + EXPERTS_PER_GROUP), logits, ninf)
    v1 = jnp.max(el, axis=-1, keepdims=True)
    i1 = first_argmax(el, v1)
    el2 = jnp.where(lane == i1, ninf, el)
    v2 = jnp.max(el2, axis=-1, keepdims=True)
    i2 = first_argmax(el2, v2)
    e2 = jnp.exp(v2 - v1)
    w1 = (1.0 / (1.0 + e2)) * g_w
    w2 = (e2 / (1.0 + e2)) * g_w
    gates = jnp.where(lane == i1 - lo, w1, 0.0) + jnp.where(lane == i2 - lo, w2, 0.0)

    rows = logits.shape[0]
    onehot = jnp.where(lane == g_idx, 1.0, 0.0)
    r_i = lax.broadcasted_iota(jnp.int32, (rows, rows), 0)
    c_i = lax.broadcasted_iota(jnp.int32, (rows, rows), 1)
    tri = jnp.where(c_i < r_i, 1.0, 0.0).astype(BF16)
    before = jnp.dot(tri, onehot.astype(BF16), preferred_element_type=F32) + cnt_ref[...]
    rank = jnp.sum(onehot * before, axis=-1, keepdims=True)
    cnt_ref[...] += jnp.sum(onehot, axis=0, keepdims=True)
    return (gates + jnp.where(lane == ROUTE_GROUP_LANE, g_idx, 0.0)
            + jnp.where(lane == ROUTE_RANK_LANE, rank, 0.0))


PLAN_CHUNK = 1024


def _plan_items(per, n_used, n_items_max, grp, p_ref, e_ref, key_ref, n_ref, run_ref):
    tstart_s, tend_s, tiles_s, before_s, next_s = grp
    n_items = per * n_used
    n_ref[0] = n_items

    def item(w, carry):
        valid = w < n_items
        wc = jnp.minimum(w, n_items - 1)
        gi = jnp.int32(0)
        for g in range(N_GROUPS):
            gi = gi + (wc >= per * tend_s[g]).astype(jnp.int32)
        local = wc - per * tstart_s[gi]
        tiles_w = tiles_s[gi]
        lpart = jnp.int32(0)
        for k in range(1, per):
            lpart = lpart + (local >= k * tiles_w).astype(jnp.int32)
        ltile = local - lpart * tiles_w
        spare = jnp.maximum(w - n_items, 0)
        part = jnp.where(valid, lpart, spare % per)
        p_ref[w] = jnp.where(valid, tstart_s[gi] + ltile, n_used + spare // per)
        e_ref[w] = part
        key = gi * per + part
        key_ref[w] = key
        nxt_g = next_s[gi]
        run_ref[3 * w] = jnp.logical_and(valid, ltile == 0).astype(jnp.int32)
        run_ref[3 * w + 1] = (per * before_s[gi] + part) % 2
        run_ref[3 * w + 2] = jnp.where(part + 1 < per, key + 1,
                                       jnp.where(nxt_g >= 0, nxt_g * per, -1))
        return carry
    lax.fori_loop(0, n_items_max, item, 0)


def _plan_kernel(group_ref, rank_ref, cnt_ref, dest_ref, src_ref, nused_ref,
                 up_p, up_e, up_key, up_n, up_run, dn_p, dn_e, dn_key, dn_n, dn_run,
                 tstart_s, tend_s, tiles_s, before_s, next_s, row0_s,
                 *, tile, n_tiles_max, up_parts, dn_parts):
    s = pl.program_id(0)

    @pl.when(s == 0)
    def _():
        start = jnp.int32(0)
        owned = jnp.int32(0)
        for g in range(N_GROUPS):
            tiles = (cnt_ref[g] + tile - 1) // tile
            tstart_s[g] = start
            row0_s[g] = start * tile
            tiles_s[g] = tiles
            before_s[g] = owned
            start = start + tiles
            tend_s[g] = start
            owned = owned + (tiles > 0).astype(jnp.int32)
        nxt = jnp.int32(-1)
        for g in reversed(range(N_GROUPS)):
            next_s[g] = nxt
            nxt = jnp.where(tiles_s[g] > 0, g, nxt)
        nused_ref[0] = start

        def clear(r, carry):
            src_ref[r] = 0
            return carry
        lax.fori_loop(0, n_tiles_max * tile, clear, 0, unroll=8)

        grp = (tstart_s, tend_s, tiles_s, before_s, next_s)
        _plan_items(up_parts, start, up_parts * n_tiles_max, grp, up_p, up_e, up_key, up_n, up_run)
        _plan_items(dn_parts, start, dn_parts * n_tiles_max, grp, dn_p, dn_e, dn_key, dn_n, dn_run)

    def place(i, carry):
        t = s * PLAN_CHUNK + i
        d = row0_s[group_ref[i]] + rank_ref[i]
        dest_ref[t] = d
        src_ref[d] = t
        return carry
    lax.fori_loop(0, PLAN_CHUNK, place, 0, unroll=8)


def _moe_plan(group, rank, counts, *, tile, n_tiles_max, up_parts, dn_parts):
    t = group.shape[0]
    i32 = jnp.int32
    smem = functools.partial(pl.BlockSpec, memory_space=pltpu.SMEM)
    n_up, n_dn = up_parts * n_tiles_max, dn_parts * n_tiles_max
    shapes = [(t,), (n_tiles_max * tile,), (1,),
              (n_up,), (n_up,), (n_up,), (1,), (3 * n_up,),
              (n_dn,), (n_dn,), (n_dn,), (1,), (3 * n_dn,)]
    outs = pl.pallas_call(
        functools.partial(_plan_kernel, tile=tile, n_tiles_max=n_tiles_max,
                          up_parts=up_parts, dn_parts=dn_parts),
        grid=(t // PLAN_CHUNK,),
        in_specs=[smem((PLAN_CHUNK,), lambda s: (s,)), smem((PLAN_CHUNK,), lambda s: (s,)),
                  smem()],
        out_specs=[smem() for _ in shapes],
        out_shape=[jax.ShapeDtypeStruct(sh, i32) for sh in shapes],
        scratch_shapes=[pltpu.SMEM((N_GROUPS,), i32) for _ in range(6)],
        compiler_params=_params(1),
        name="moe_plan",
    )(group, rank, counts)
    dest, src, n_used = outs[:3]
    return dict(dest=dest, src=src, n_used=n_used, up=tuple(outs[3:8]), down=tuple(outs[8:13]))


def _row_copy(src_hbm, dst_vmem, src_row, dst_row, sem):
    return pltpu.make_async_copy(src_hbm.at[pl.ds(src_row, 1)], dst_vmem.at[pl.ds(dst_row, 1)], sem)


def _start_row_gather(idx_ref, base, n_rows, pairs):
    def body(r, carry):
        i = idx_ref[base + r]
        for src_hbm, dst_vmem, sem in pairs:
            _row_copy(src_hbm, dst_vmem, i, r, sem).start()
        return carry
    lax.fori_loop(0, n_rows, body, 0, unroll=8)


def _start_rows(idx_ref, idx_base, row0, n_rows, pairs):
    for k in range(n_rows):
        i = idx_ref[idx_base + row0 + k]
        for src_hbm, dst_vmem, sem in pairs:
            _row_copy(src_hbm, dst_vmem, i, row0 + k, sem).start()


def _norm_tile_and_prefetch(load_rows, g_ref, o_ref, n_rows, has_next, idx_ref, next_base,
                            next_pairs):
    @pl.when(has_next)
    def _():
        _rmsnorm_rows(load_rows, g_ref, o_ref, n_rows,
                      per_trip=lambda row0: _start_rows(idx_ref, next_base, row0, NORM_ROWS,
                                                        next_pairs))

    @pl.when(jnp.logical_not(has_next))
    def _():
        _rmsnorm_rows(load_rows, g_ref, o_ref, n_rows)


def _wait_row_gather(n_rows, pairs):
    for src_hbm, dst_vmem, sem in pairs:
        pltpu.make_async_copy(src_hbm.at[pl.ds(0, n_rows)], dst_vmem, sem).wait()


def _dispatch_kernel(src_ref, nused_ref, x2_hbm, route_hbm, g_ref, xs_ref, gs_ref,
                     xbuf, gbuf, sem):
    p = pl.program_id(0)
    n_used = nused_ref[0]
    tile = xbuf.shape[1]
    slot = p % 2

    def pairs(s):
        return [(x2_hbm, xbuf.at[s], sem.at[0, s]), (route_hbm, gbuf.at[s], sem.at[1, s])]

    @pl.when(p == 0)
    def _():
        _start_row_gather(src_ref, 0, tile, pairs(0))

    @pl.when(p < n_used)
    def _():
        _wait_row_gather(tile, pairs(slot))
        gs_ref[...] = gbuf[slot]
        _norm_tile_and_prefetch(lambda rows: xbuf[slot, rows, :], g_ref, xs_ref, tile,
                                p + 1 < n_used, src_ref, (p + 1) * tile, pairs(1 - slot))

    @pl.when(p >= n_used)
    def _():
        xs_ref[...] = jnp.zeros_like(xs_ref)
        gs_ref[...] = jnp.zeros_like(gs_ref)


def _dispatch(x2, route, g, src, n_used, *, tile, n_tiles_max):
    t, d = x2.shape
    tile_map = lambda p, src_ref, n_ref: (p, 0)
    grid_spec = pltpu.PrefetchScalarGridSpec(
        num_scalar_prefetch=2,
        grid=(n_tiles_max,),
        in_specs=[
            pl.BlockSpec(memory_space=pl.ANY),
            pl.BlockSpec(memory_space=pl.ANY),
            pl.BlockSpec((1, d), lambda p, s, n: (0, 0)),
        ],
        out_specs=[
            pl.BlockSpec((tile, d), tile_map),
            pl.BlockSpec((tile, LANES), tile_map),
        ],
        scratch_shapes=[
            pltpu.VMEM((2, tile, d), F32),
            pltpu.VMEM((2, tile, LANES), F32),
            pltpu.SemaphoreType.DMA((2, 2)),
        ],
    )
    return pl.pallas_call(
        _dispatch_kernel,
        grid_spec=grid_spec,
        out_shape=[
            jax.ShapeDtypeStruct((n_tiles_max * tile, d), BF16),
            jax.ShapeDtypeStruct((n_tiles_max * tile, LANES), F32),
        ],
        compiler_params=_params(1),
        name="moe_dispatch",
    )(src, n_used, x2, route, g)


def _stream_run_weights(w, n_items, key_ref, run_ref, copies):
    is_first = run_ref[3 * w]
    slot = run_ref[3 * w + 1]
    nxt_key = run_ref[3 * w + 2]
    valid = w < n_items

    @pl.when(jnp.logical_and(valid, w == 0))
    def _():
        for cp in copies(key_ref[0], 0):
            cp.start()

    @pl.when(jnp.logical_and(valid, is_first == 1))
    def _():
        for cp in copies(key_ref[w], slot):
            cp.wait()

        @pl.when(nxt_key >= 0)
        def _():
            for cp in copies(nxt_key, 1 - slot):
                cp.start()

    return slot


def _moe_up_kernel(ip_ref, ie_ref, key_ref, n_ref, run_ref, xs_ref, gs_ref, w1_hbm, w3_hbm,
                   hid_ref, w1_buf, w3_buf, sem):
    w = pl.program_id(0)

    n_e = w1_buf.shape[1]
    f = w1_buf.shape[3]

    def copies(key, slot):
        experts = pl.ds(key * n_e, n_e)
        return [pltpu.make_async_copy(w1_hbm.at[experts], w1_buf.at[slot], sem.at[0, slot]),
                pltpu.make_async_copy(w3_hbm.at[experts], w3_buf.at[slot], sem.at[1, slot])]

    slot = _stream_run_weights(w, n_ref[0], key_ref, run_ref, copies)

    @pl.when(w < n_ref[0])
    def _():
        x = xs_ref[...]
        lane = lax.broadcasted_iota(jnp.int32, gs_ref.shape, 1)
        for e in range(n_e):
            a = jnp.dot(x, w1_buf[slot, e].astype(BF16), preferred_element_type=F32)
            b = jnp.dot(x, w3_buf[slot, e].astype(BF16), preferred_element_type=F32)
            in_group = ie_ref[w] * n_e + e
            gate = jnp.sum(jnp.where(lane == in_group, gs_ref[...], 0.0), axis=-1, keepdims=True)
            hid_ref[:, e * f:(e + 1) * f] = (jax.nn.silu(a) * b * gate).astype(hid_ref.dtype)

    @pl.when(w >= n_ref[0])
    def _():
        hid_ref[...] = jnp.zeros_like(hid_ref)


def _moe_up(xs, gs, w1, w3, items, *, tile):
    rows, d = xs.shape
    f = w1.shape[2]
    ip, ie, key, n_items, run = items
    grid_spec = pltpu.PrefetchScalarGridSpec(
        num_scalar_prefetch=5,
        grid=(ip.shape[0],),
        in_specs=[
            pl.BlockSpec((tile, d), lambda w, ip, ie, key, n, run: (ip[w], 0)),
            pl.BlockSpec((tile, LANES), lambda w, ip, ie, key, n, run: (ip[w], 0)),
            pl.BlockSpec(memory_space=pl.ANY),
            pl.BlockSpec(memory_space=pl.ANY),
        ],
        out_specs=pl.BlockSpec((tile, MOE_UP_EXPERTS * f),
                               lambda w, ip, ie, key, n, run: (ip[w], ie[w])),
        scratch_shapes=[
            pltpu.VMEM((2, MOE_UP_EXPERTS, d, f), w1.dtype),
            pltpu.VMEM((2, MOE_UP_EXPERTS, d, f), w3.dtype),
            pltpu.SemaphoreType.DMA((2, 2)),
        ],
    )
    return pl.pallas_call(
        _moe_up_kernel,
        grid_spec=grid_spec,
        out_shape=jax.ShapeDtypeStruct((rows, EXPERTS_PER_GROUP * f), BF16),
        compiler_params=_params(1),
        name="moe_up",
    )(ip, ie, key, n_items, run, xs, gs, w1, w3)


def _moe_down_kernel(ip_ref, ih_ref, key_ref, n_ref, run_ref, hid_ref, w2_hbm, y_ref,
                     w2_buf, sem, *, n_halves):
    w = pl.program_id(0)
    tn = w2_buf.shape[2]

    def copies(key, slot):
        g = key // n_halves
        col = pl.multiple_of((key % n_halves) * tn, tn)
        return [pltpu.make_async_copy(w2_hbm.at[g, :, pl.ds(col, tn)], w2_buf.at[slot],
                                      sem.at[slot])]

    slot = _stream_run_weights(w, n_ref[0], key_ref, run_ref, copies)

    @pl.when(w < n_ref[0])
    def _():
        y_ref[...] = jnp.dot(hid_ref[...], w2_buf[slot].astype(BF16),
                             preferred_element_type=F32)

    @pl.when(w >= n_ref[0])
    def _():
        y_ref[...] = jnp.zeros_like(y_ref)


def _moe_down(hid, w2g, items, *, tile, n_halves):
    rows, k = hid.shape
    d = w2g.shape[2]
    tn = d // n_halves
    ip, ih, key, n_items, run = items
    grid_spec = pltpu.PrefetchScalarGridSpec(
        num_scalar_prefetch=5,
        grid=(ip.shape[0],),
        in_specs=[
            pl.BlockSpec((tile, k), lambda w, ip, ih, key, n, run: (ip[w], 0)),
            pl.BlockSpec(memory_space=pl.ANY),
        ],
        out_specs=pl.BlockSpec((tile, tn), lambda w, ip, ih, key, n, run: (ip[w], ih[w])),
        scratch_shapes=[
            pltpu.VMEM((2, k, tn), w2g.dtype),
            pltpu.SemaphoreType.DMA((2,)),
        ],
    )
    return pl.pallas_call(
        functools.partial(_moe_down_kernel, n_halves=n_halves),
        grid_spec=grid_spec,
        out_shape=jax.ShapeDtypeStruct((rows, d), F32),
        compiler_params=_params(1),
        name="moe_down",
    )(ip, ih, key, n_items, run, hid, w2g)


def _final_kernel(dest_ref, x_ref, ys_hbm, g_ref, o_ref, ybuf, sem, *, tile0):
    i = pl.program_id(0)
    tm = ybuf.shape[1]
    slot = i % 2

    def pairs(s):
        return [(ys_hbm, ybuf.at[s], sem.at[s])]

    @pl.when(i == 0)
    def _():
        _start_row_gather(dest_ref, tile0 * tm, tm, pairs(0))

    _wait_row_gather(tm, pairs(slot))
    _norm_tile_and_prefetch(lambda rows: x_ref[rows, :] + ybuf[slot, rows, :], g_ref, o_ref, tm,
                            i + 1 < pl.num_programs(0), dest_ref, (tile0 + i + 1) * tm,
                            pairs(1 - slot))


def _final(x2, ys, dest, g, *, row0, n_rows, tm):
    d = x2.shape[1]
    off = row0 // tm
    grid_spec = pltpu.PrefetchScalarGridSpec(
        num_scalar_prefetch=1,
        grid=(n_rows // tm,),
        in_specs=[
            pl.BlockSpec((tm, d), lambda i, dest: (i + off, 0)),
            pl.BlockSpec(memory_space=pl.ANY),
            pl.BlockSpec((1, d), lambda i, dest: (0, 0)),
        ],
        out_specs=pl.BlockSpec((tm, d), lambda i, dest: (i, 0)),
        scratch_shapes=[pltpu.VMEM((2, tm, d), F32), pltpu.SemaphoreType.DMA((2,))],
    )
    return pl.pallas_call(
        functools.partial(_final_kernel, tile0=off),
        grid_spec=grid_spec,
        out_shape=jax.ShapeDtypeStruct((n_rows, d), F32),
        compiler_params=_params(1),
        name="final_norm",
    )(dest, x2, ys, g)


def _rope_tables(pos):
    half = ROT_DIM // 2
    inv = ROPE_THETA ** (-np.arange(half, dtype=np.float64) / half)
    ang = np.asarray(pos, np.float64)[:, None] * inv[None, :]
    cos, sin = np.cos(ang), np.sin(ang)
    ones = np.ones((ang.shape[0], HEAD_DIM - ROT_DIM))
    zeros = np.zeros((ang.shape[0], HEAD_DIM - half))
    c = np.concatenate([cos, cos, ones], axis=1)
    s_hi = np.concatenate([-sin, zeros], axis=1)
    s_lo = np.concatenate([np.zeros_like(sin), sin, 0.0 * ones], axis=1)
    rep = LANES // HEAD_DIM
    return tuple(jnp.asarray(np.tile(t, (1, rep)), F32) for t in (c, s_hi, s_lo))


def kernel(x_prompt, x_sample, cache_k, cache_v, state_conv, state_h, norm_mix, w_in, conv_w,
           conv_b, w_gate_a, b_gate_a, w_gate_x, b_gate_x, lru_lambda, sinks, norm_attn_out,
           norm_lru_out, w_out, norm_ffn, w_group, b_group, w_expert_router, b_expert_router,
           w1, w3, w2, norm_final):
    batch, seq, d_model = x_prompt.shape
    dec_batch, dec_seq, _ = x_sample.shape
    depth = w_in.shape[0]
    assert depth == 1 and dec_seq == CHUNK and seq % CHUNK == 0
    n_heads = sinks.shape[1]
    att_width = n_heads * HEAD_DIM
    n_kv = cache_k.shape[3]
    kv_cols = n_kv * HEAD_DIM
    lru_width = lru_lambda.shape[1]
    cw = cache_k.shape[2]
    assert cw == WIN_CHUNKS * CHUNK
    n_p, n_s = batch * seq, dec_batch * dec_seq
    chunks_per_seq = seq // CHUNK
    n_prompt_chunks = n_p // CHUNK

    xp = x_prompt.reshape(n_p, d_model)
    xs = x_sample.reshape(n_s, d_model)

    w_r = w_in[0]
    tn = 2 * kv_cols
    assert tn == 512 and att_width % tn == 0 and lru_width % tn == 0
    n_q_tiles = att_width // tn
    kv_tile = n_q_tiles

    tm_in = PROJ_ROWS
    assert seq % tm_in == 0 and n_s % tm_in == 0 and tm_in % dec_seq == 0
    pos = np.concatenate([np.arange(seq), np.tile(PAST_LEN + np.arange(dec_seq), tm_in // dec_seq)])
    c_tab, shi_tab, slo_tab = _rope_tables(pos)

    z = _in_proj(xp, xs, norm_mix, w_r, c_tab, shi_tab, slo_tab, tm=tm_in, tn=tn,
                 n_q_tiles=n_q_tiles, kv_tile=kv_tile, prompt_tiles_per_seq=seq // tm_in)

    k_col = att_width + 2 * lru_width
    v_col = k_col + kv_cols
    att = _attention(
        z, cache_k[0].reshape(dec_batch, cw, kv_cols), cache_v[0].reshape(dec_batch, cw, kv_cols),
        sinks[0], norm_attn_out,
        n_prompt_chunks=n_prompt_chunks, chunks_per_seq=chunks_per_seq, att_width=att_width,
        n_kv=n_kv, k_col_blk=k_col // kv_cols, v_col_blk=v_col // kv_cols)

    lru, h_tiles = _rglru(
        z, state_conv[0], state_h[0].reshape(dec_batch, 1, lru_width), conv_w[0], conv_b,
        w_gate_a[0].astype(BF16), b_gate_a[0].reshape(1, lru_width),
        w_gate_x[0].astype(BF16), b_gate_x[0].reshape(1, lru_width),
        lru_lambda, norm_lru_out,
        n_prompt_chunks=n_prompt_chunks, chunks_per_seq=chunks_per_seq, width=lru_width,
        xb_col_blk=att_width // lru_width, yb_col_blk=att_width // lru_width + 1)

    n_routes = N_GROUPS + N_GROUPS * EXPERTS_PER_GROUP
    wr = jnp.concatenate([w_group[0], w_expert_router[0],
                          jnp.zeros((d_model, LANES - n_routes), F32)], axis=1).astype(BF16)
    br = jnp.concatenate([b_group[0], b_expert_router[0],
                          jnp.zeros((LANES - n_routes,), F32)]).reshape(1, LANES)
    x2, route, counts, meta = _out_proj(att, lru, w_out[0], xp, xs, norm_ffn, wr, br,
                                        tm=PROJ_ROWS, tn=OUT_PROJ_COLS)

    n_tiles_max = (n_p + n_s) // MOE_TILE + N_GROUPS
    tabs = _moe_plan(meta[0], meta[1], counts[0, :N_GROUPS].astype(jnp.int32), tile=MOE_TILE,
                     n_tiles_max=n_tiles_max, up_parts=EXPERTS_PER_GROUP // MOE_UP_EXPERTS,
                     dn_parts=MOE_DOWN_HALVES)
    xsort, gsort = _dispatch(x2, route, norm_ffn, tabs["src"], tabs["n_used"],
                             tile=MOE_TILE, n_tiles_max=n_tiles_max)
    hid = _moe_up(xsort, gsort, w1[0], w3[0], tabs["up"], tile=MOE_TILE)
    d_expert = w2.shape[2]
    w2g = w2[0].reshape(N_GROUPS, EXPERTS_PER_GROUP * d_expert, d_model)
    ysort = _moe_down(hid, w2g, tabs["down"], tile=MOE_TILE, n_halves=MOE_DOWN_HALVES)

    g_fin = norm_final.reshape(1, d_model)
    y_prompt = _final(x2, ysort, tabs["dest"], g_fin, row0=0, n_rows=n_p,
                      tm=MOE_TILE).reshape(batch, seq, d_model)
    y_sample = _final(x2, ysort, tabs["dest"], g_fin, row0=n_p, n_rows=n_s,
                      tm=MOE_TILE).reshape(dec_batch, dec_seq, d_model)

    win = min(WIN_CHUNKS * CHUNK, seq)

    def prompt_tail(col, width, rows):
        return jnp.stack([lax.slice(z, ((b + 1) * seq - rows, col), ((b + 1) * seq, col + width))
                          for b in range(batch)])

    def sample_rows(col, width):
        return lax.slice(z, (n_p, col), (n_p + n_s, col + width)).reshape(dec_batch, dec_seq, width)

    k_prompt = prompt_tail(k_col, kv_cols, win).reshape(1, batch, win, n_kv, HEAD_DIM)
    v_prompt = prompt_tail(v_col, kv_cols, win).reshape(1, batch, win, n_kv, HEAD_DIM)
    conv_prompt = prompt_tail(att_width, lru_width, CONV_WIDTH - 1)[None]
    h_prompt = h_tiles[:n_prompt_chunks, 0].reshape(batch, chunks_per_seq, lru_width)[:, -1][None]

    ks = sample_rows(k_col, kv_cols).reshape(dec_batch, dec_seq, n_kv, HEAD_DIM)
    vs = sample_rows(v_col, kv_cols).reshape(dec_batch, dec_seq, n_kv, HEAD_DIM)
    k_sample = jnp.concatenate([cache_k[0], ks], axis=1)[:, -cw:][None]
    v_sample = jnp.concatenate([cache_v[0], vs], axis=1)[:, -cw:][None]
    xs_rows = sample_rows(att_width, lru_width)[:, dec_seq - (CONV_WIDTH - 1):]
    conv_sample = jnp.concatenate([state_conv[0], xs_rows], axis=1)[:, -(CONV_WIDTH - 1):][None]
    h_sample = h_tiles[n_prompt_chunks:, 0][None]

    return (y_prompt, y_sample, k_prompt, v_prompt, conv_prompt, h_prompt,
            k_sample, v_sample, conv_sample, h_sample)
```

```python
import functools

import numpy as np
import jax
import jax.numpy as jnp
from jax import lax
from jax.experimental import pallas as pl
from jax.experimental.pallas import tpu as pltpu

F32 = jnp.float32
BF16 = jnp.bfloat16

CHUNK = 64
HEAD_DIM = 64
KV_GROUP = 8
WIN_CHUNKS = 2
ROT_DIM = 16
ROPE_THETA = 500000.0
LRU_BLOCK = 128
CONV_WIDTH = 4
LRU_C = 8.0
N_GROUPS = 8
EXPERTS_PER_GROUP = 4
EPS = 1e-6
NEG = -1e30
PAST_LEN = 1024
LANES = 128
PROJ_ROWS = 1024
OUT_PROJ_COLS = 512
INPROJ_ROW_BLOCK = 128
MOE_TILE = 256
MOE_UP_EXPERTS = 2
MOE_DOWN_HALVES = 1
VMEM_LIMIT = 56 * 1024 * 1024


def _params(n_axes, vmem=VMEM_LIMIT):
    return pltpu.CompilerParams(
        dimension_semantics=("arbitrary",) * n_axes, vmem_limit_bytes=vmem)


def _rms_scale(x):
    return lax.rsqrt(jnp.mean(x * x, axis=-1, keepdims=True) + EPS)


NORM_ROWS = 16


def _rmsnorm_rows(load_rows, g_ref, o_ref, n_rows, per_trip=None):
    def body(r, carry):
        row0 = r * NORM_ROWS if isinstance(r, int) else pl.multiple_of(r * NORM_ROWS, NORM_ROWS)
        rows = pl.ds(row0, NORM_ROWS)
        x = load_rows(rows)
        o_ref[rows, :] = (x * _rms_scale(x) * g_ref[...]).astype(o_ref.dtype)
        if per_trip is not None:
            per_trip(row0)
        return carry
    if per_trip is not None:
        for r in range(n_rows // NORM_ROWS):
            body(r, 0)
    else:
        lax.fori_loop(0, n_rows // NORM_ROWS, body, 0, unroll=4)


def _rope_block(zb, c, s_hi, s_lo):
    return (zb * c + pltpu.roll(zb, LANES - ROT_DIM // 2, 1) * s_hi
            + pltpu.roll(zb, ROT_DIM // 2, 1) * s_lo)


def _inproj_kernel(xp_hbm, xs_hbm, g_ref, w_ref, c_ref, shi_ref, slo_ref, o_ref, x_buf, h_ref,
                   sem, *, n_prompt_tiles, n_q_tiles, kv_tile):
    i = pl.program_id(0)
    j = pl.program_id(1)
    tm, tn = o_ref.shape

    def x_tile_copy(src_hbm, tile):
        return pltpu.make_async_copy(src_hbm.at[pl.ds(tile * tm, tm)], x_buf, sem)

    def start_x(tile):
        @pl.when(tile < n_prompt_tiles)
        def _():
            x_tile_copy(xp_hbm, tile).start()

        @pl.when(tile >= n_prompt_tiles)
        def _():
            x_tile_copy(xs_hbm, tile - n_prompt_tiles).start()

    @pl.when(jnp.logical_and(i == 0, j == 0))
    def _():
        start_x(0)

    @pl.when(j == 0)
    def _():
        x_tile_copy(xp_hbm, 0).wait()
        _rmsnorm_rows(lambda rows: x_buf[rows, :], g_ref, h_ref, tm)

    @pl.when(jnp.logical_and(j == 1, i + 1 < pl.num_programs(0)))
    def _():
        start_x(i + 1)

    w = w_ref[...].astype(BF16)
    n_blk = tn // LANES
    for r0 in range(0, tm, INPROJ_ROW_BLOCK):
        rows = slice(r0, r0 + INPROJ_ROW_BLOCK)
        z = jnp.dot(h_ref[rows, :], w, preferred_element_type=F32)
        c, shi, slo = c_ref[rows, :], shi_ref[rows, :], slo_ref[rows, :]
        for b in range(n_blk):
            is_rope = j < n_q_tiles
            if b < n_blk // 2:
                is_rope = jnp.logical_or(is_rope, j == kv_tile)
            zb = z[:, b * LANES:(b + 1) * LANES]
            o_ref[rows, b * LANES:(b + 1) * LANES] = _rope_block(
                zb, jnp.where(is_rope, c, 1.0), jnp.where(is_rope, shi, 0.0),
                jnp.where(is_rope, slo, 0.0))


def _in_proj(xp, xs, g, w, c, shi, slo, *, tm, tn, n_q_tiles, kv_tile, prompt_tiles_per_seq):
    n_p, d = xp.shape
    n_s = xs.shape[0]
    t = n_p + n_s
    npt = n_p // tm
    n_cols = w.shape[1]
    kern = functools.partial(_inproj_kernel, n_prompt_tiles=npt, n_q_tiles=n_q_tiles,
                             kv_tile=kv_tile)
    n_tiles = n_cols // tn
    assert kv_tile == n_q_tiles

    def table_tile(i, j):
        return jnp.where(i < npt, i % prompt_tiles_per_seq, prompt_tiles_per_seq), 0

    def out_tile(i, j):
        return i, jnp.where(j < kv_tile, j, jnp.where(j == kv_tile, n_tiles - 1, j - 1))

    return pl.pallas_call(
        kern,
        grid=(t // tm, n_tiles),
        in_specs=[
            pl.BlockSpec(memory_space=pl.ANY),
            pl.BlockSpec(memory_space=pl.ANY),
            pl.BlockSpec((1, d), lambda i, j: (0, 0)),
            pl.BlockSpec((d, tn), lambda i, j: (0, j)),
            pl.BlockSpec((tm, LANES), table_tile),
            pl.BlockSpec((tm, LANES), table_tile),
            pl.BlockSpec((tm, LANES), table_tile),
        ],
        out_specs=pl.BlockSpec((tm, tn), out_tile),
        out_shape=jax.ShapeDtypeStruct((t, n_cols), F32),
        scratch_shapes=[pltpu.VMEM((tm, d), xp.dtype), pltpu.VMEM((tm, d), BF16),
                        pltpu.SemaphoreType.DMA(())],
        compiler_params=_params(2),
        name="in_proj",
    )(xp, xs, g, w, c, shi, slo)


ATTN_CHUNKS = 4


def _attn_chunk(q, k_band, v_band, lo, sink_ref, n_kv):
    n_keys = k_band.shape[0]
    key = lax.broadcasted_iota(jnp.int32, (n_keys, KV_GROUP * CHUNK), 0)
    valid = key >= lo
    outs = []
    for gi in range(n_kv):
        kg = k_band[:, gi * HEAD_DIM:(gi + 1) * HEAD_DIM]
        vg = v_band[:, gi * HEAD_DIM:(gi + 1) * HEAD_DIM]
        heads = [gi * KV_GROUP + h for h in range(KV_GROUP)]
        qg = jnp.concatenate([q[:, h * HEAD_DIM:(h + 1) * HEAD_DIM] for h in heads], axis=0)
        st = lax.dot_general(kg, qg, (((1,), (1,)), ((), ())), preferred_element_type=F32)
        st = jnp.where(valid, st, NEG)
        sink = sink_ref[gi:gi + 1, :]
        m = jnp.maximum(jnp.max(st, axis=0, keepdims=True), sink)
        p = jnp.exp(st - m)
        denom = jnp.sum(p, axis=0, keepdims=True) + jnp.exp(sink - m)
        pn = (p * (1.0 / denom)).astype(BF16)
        og = lax.dot_general(pn, vg, (((0,), (0,)), ((), ())), preferred_element_type=F32)
        outs.extend(og[h * CHUNK:(h + 1) * CHUNK] for h in range(KV_GROUP))
    return jnp.concatenate(outs, axis=1)


def _attn_kernel(q_ref, kp_ref, k_ref, vp_ref, v_ref, kc_ref, vc_ref, sink_ref, g_ref, o_ref,
                 *, n_prompt_tiles, tiles_per_seq, n_kv):
    n = pl.program_id(0)
    is_s = n >= n_prompt_tiles
    has_prev = jnp.logical_or(is_s, n % tiles_per_seq > 0)

    def band(prev_ref, own_ref, cache_ref, c):
        prev, own, cache = prev_ref[...], own_ref[...], cache_ref[c]
        hist = jnp.concatenate([prev, own], axis=0)[c * CHUNK:(c + WIN_CHUNKS) * CHUNK]
        hist = jnp.where(is_s, cache, hist)
        return jnp.concatenate([hist, own[c * CHUNK:(c + 1) * CHUNK]], axis=0).astype(BF16)

    q = (q_ref[...] * (HEAD_DIM ** -0.5)).astype(BF16)
    for c in range(ATTN_CHUNKS):
        lo = jnp.where(has_prev, 0, max(WIN_CHUNKS - c, 0) * CHUNK)
        att = _attn_chunk(q[c * CHUNK:(c + 1) * CHUNK], band(kp_ref, k_ref, kc_ref, c),
                          band(vp_ref, v_ref, vc_ref, c), lo, sink_ref, n_kv)
        o_ref[c * CHUNK:(c + 1) * CHUNK, :] = (
            att * _rms_scale(att) * g_ref[...]).astype(o_ref.dtype)


def _attention(z, cache_k, cache_v, sinks, g, *, n_prompt_chunks, chunks_per_seq,
               att_width, n_kv, k_col_blk, v_col_blk):
    t = z.shape[0]
    rows = ATTN_CHUNKS * CHUNK
    assert ATTN_CHUNKS % WIN_CHUNKS == 0 and chunks_per_seq % ATTN_CHUNKS == 0
    hist_rows = WIN_CHUNKS * CHUNK
    hist_per_tile = rows // hist_rows
    assert cache_k.shape[0] % ATTN_CHUNKS == 0
    n_tiles = t // rows
    npt = n_prompt_chunks // ATTN_CHUNKS
    tps = chunks_per_seq // ATTN_CHUNKS
    kvw = n_kv * HEAD_DIM

    def prev(col):
        return lambda n: (n * hist_per_tile - jnp.minimum(1, n % tps), col)

    cache_map = lambda n: (jnp.maximum(n - npt, 0), 0, 0)
    kern = functools.partial(_attn_kernel, n_prompt_tiles=npt, tiles_per_seq=tps, n_kv=n_kv)
    sink_rows = jnp.repeat(sinks.reshape(n_kv, KV_GROUP), CHUNK, axis=1)
    return pl.pallas_call(
        kern,
        grid=(n_tiles,),
        in_specs=[
            pl.BlockSpec((rows, att_width), lambda n: (n, 0)),
            pl.BlockSpec((hist_rows, kvw), prev(k_col_blk)),
            pl.BlockSpec((rows, kvw), lambda n: (n, k_col_blk)),
            pl.BlockSpec((hist_rows, kvw), prev(v_col_blk)),
            pl.BlockSpec((rows, kvw), lambda n: (n, v_col_blk)),
            pl.BlockSpec((ATTN_CHUNKS, WIN_CHUNKS * CHUNK, kvw), cache_map),
            pl.BlockSpec((ATTN_CHUNKS, WIN_CHUNKS * CHUNK, kvw), cache_map),
            pl.BlockSpec((n_kv, KV_GROUP * CHUNK), lambda n: (0, 0)),
            pl.BlockSpec((1, att_width), lambda n: (0, 0)),
        ],
        out_specs=pl.BlockSpec((rows, att_width), lambda n: (n, 0)),
        out_shape=jax.ShapeDtypeStruct((t, att_width), BF16),
        compiler_params=_params(1),
        name="attention",
    )(z, z, z, z, z, cache_k, cache_v, sink_rows, g)


SUBLANES = 8


def _lru_kernel(xb_ref, yb_ref, sconv_ref, sh_ref, cw_ref, cb_ref, wa_ref, ba_ref,
                wx_ref, bx_ref, lam_ref, g_ref, o_ref, hl_ref, hist_ref, hcar_ref,
                a_sc, b_sc, h_sc, *, n_prompt_chunks, chunks_per_seq):
    n = pl.program_id(0)
    is_s = n >= n_prompt_chunks
    hist_rows = hist_ref.shape[0]

    @pl.when(jnp.logical_and(jnp.logical_not(is_s), n % chunks_per_seq == 0))
    def _():
        hist_ref[...] = jnp.zeros_like(hist_ref)
        hcar_ref[...] = jnp.zeros_like(hcar_ref)

    @pl.when(is_s)
    def _():
        hist_ref[...] = jnp.zeros_like(hist_ref)
        hist_ref[hist_rows - (CONV_WIDTH - 1):, :] = sconv_ref[0]
        hcar_ref[...] = sh_ref[0]

    x = xb_ref[...]
    rows, width = x.shape
    hist = hist_ref[...]
    row8 = lax.broadcasted_iota(jnp.int32, (hist_rows, width), 0)
    xc = cb_ref[...]
    for j in range(CONV_WIDTH):
        s = CONV_WIDTH - 1 - j
        if s == 0:
            xs = x
        else:
            xr = pltpu.roll(x, s, 0)
            head = jnp.where(row8 < s, pltpu.roll(hist, s, 0), xr[:hist_rows])
            xs = jnp.concatenate([head, xr[hist_rows:]], axis=0)
        xc = xc + xs * cw_ref[j:j + 1, :]
    hist_ref[...] = x[rows - hist_rows:]

    xcb = xc.astype(BF16)
    ra, rx = [], []
    for nb in range(width // LRU_BLOCK):
        blk = xcb[:, nb * LRU_BLOCK:(nb + 1) * LRU_BLOCK]
        ra.append(jnp.dot(blk, wa_ref[nb], preferred_element_type=F32))
        rx.append(jnp.dot(blk, wx_ref[nb], preferred_element_type=F32))
    r = jax.nn.sigmoid(jnp.concatenate(ra, axis=1) + ba_ref[...])
    ig = jax.nn.sigmoid(jnp.concatenate(rx, axis=1) + bx_ref[...])
    log_a = (-LRU_C * r) * jax.nn.softplus(-lam_ref[...])
    a = jnp.exp(log_a)
    b = jnp.sqrt(-jnp.tanh(log_a) * (1.0 + a * a)) * ig * xc

    seg_len = rows // SUBLANES
    h_blocks, carries = [], []
    for nb in range(width // LANES):
        lanes = slice(nb * LANES, (nb + 1) * LANES)
        a_sc[nb] = a[:, lanes]
        b_sc[nb] = b[:, lanes]
        hloc, ploc = [], []
        for k in range(seg_len):
            a_k = a_sc[nb, pl.ds(k, SUBLANES, stride=seg_len), :]
            b_k = b_sc[nb, pl.ds(k, SUBLANES, stride=seg_len), :]
            hloc.append(b_k if k == 0 else a_k * hloc[-1] + b_k)
            ploc.append(a_k if k == 0 else a_k * ploc[-1])
        carry = hcar_ref[:, lanes]
        seg_in = []
        for s in range(SUBLANES):
            seg_in.append(carry)
            carry = ploc[-1][s:s + 1] * carry + hloc[-1][s:s + 1]
        seg_in = jnp.concatenate(seg_in, axis=0)
        carries.append(carry)
        for k in range(seg_len):
            h_sc[nb, pl.ds(k, SUBLANES, stride=seg_len), :] = hloc[k] + ploc[k] * seg_in
        h_blocks.append(h_sc[nb])
    h = jnp.concatenate(h_blocks, axis=1)
    h_last = jnp.concatenate(carries, axis=1)
    hcar_ref[...] = h_last
    hl_ref[0] = h_last

    y = jax.nn.gelu(yb_ref[...]) * h
    o_ref[...] = (y * _rms_scale(y) * g_ref[...]).astype(o_ref.dtype)


def _rglru(z, state_conv, state_h, conv_w, conv_b, wa, ba, wx, bx, lam, g,
           *, n_prompt_chunks, chunks_per_seq, width, xb_col_blk, yb_col_blk):
    t = z.shape[0]
    n_items = t // CHUNK
    npc = n_prompt_chunks
    nblk = width // LRU_BLOCK
    state_map = lambda n: (jnp.maximum(n - npc, 0), 0, 0)
    full2 = lambda n: (0, 0)
    full3 = lambda n: (0, 0, 0)
    kern = functools.partial(_lru_kernel, n_prompt_chunks=npc, chunks_per_seq=chunks_per_seq)
    return pl.pallas_call(
        kern,
        grid=(n_items,),
        in_specs=[
            pl.BlockSpec((CHUNK, width), lambda n: (n, xb_col_blk)),
            pl.BlockSpec((CHUNK, width), lambda n: (n, yb_col_blk)),
            pl.BlockSpec((1, CONV_WIDTH - 1, width), state_map),
            pl.BlockSpec((1, 1, width), state_map),
            pl.BlockSpec((CONV_WIDTH, width), full2),
            pl.BlockSpec((1, width), full2),
            pl.BlockSpec((nblk, LRU_BLOCK, LRU_BLOCK), full3),
            pl.BlockSpec((1, width), full2),
            pl.BlockSpec((nblk, LRU_BLOCK, LRU_BLOCK), full3),
            pl.BlockSpec((1, width), full2),
            pl.BlockSpec((1, width), full2),
            pl.BlockSpec((1, width), full2),
        ],
        out_specs=[
            pl.BlockSpec((CHUNK, width), lambda n: (n, 0)),
            pl.BlockSpec((1, 1, width), lambda n: (n, 0, 0)),
        ],
        out_shape=[
            jax.ShapeDtypeStruct((t, width), BF16),
            jax.ShapeDtypeStruct((n_items, 1, width), F32),
        ],
        scratch_shapes=[
            pltpu.VMEM((SUBLANES, width), F32),
            pltpu.VMEM((1, width), F32),
            pltpu.VMEM((width // LANES, CHUNK, LANES), F32),
            pltpu.VMEM((width // LANES, CHUNK, LANES), F32),
            pltpu.VMEM((width // LANES, CHUNK, LANES), F32),
        ],
        compiler_params=_params(1),
        name="rglru",
    )(z, z, state_conv, state_h, conv_w, conv_b, wa, ba, wx, bx, lam, g)


def _outproj_kernel(att_ref, lru_ref, wa_ref, wl_ref, xp_ref, xs_ref, g_ref, wr_ref, br_ref,
                    o_ref, route_ref, cnt_ref, meta_ref, ssq_ref, lg_ref,
                    *, n_prompt_tiles, d_model):
    i = pl.program_id(0)
    j = pl.program_id(1)
    acc = jnp.dot(att_ref[...], wa_ref[...].astype(BF16), preferred_element_type=F32)
    acc = acc + jnp.dot(lru_ref[...], wl_ref[...].astype(BF16), preferred_element_type=F32)
    x2 = jnp.where(i < n_prompt_tiles, xp_ref[...], xs_ref[...]) + acc
    o_ref[...] = x2

    @pl.when(jnp.logical_and(i == 0, j == 0))
    def _():
        cnt_ref[...] = jnp.zeros_like(cnt_ref)

    @pl.when(j == 0)
    def _():
        ssq_ref[...] = jnp.zeros_like(ssq_ref)
        lg_ref[...] = jnp.zeros_like(lg_ref)

    ssq_ref[...] += jnp.sum(x2 * x2, axis=-1, keepdims=True)
    lg_ref[...] += jnp.dot((x2 * g_ref[...]).astype(BF16), wr_ref[...],
                           preferred_element_type=F32)

    @pl.when(j == pl.num_programs(1) - 1)
    def _():
        scale = lax.rsqrt(ssq_ref[...] * (1.0 / d_model) + EPS)
        logits = lg_ref[...] * scale + br_ref[...]
        for r0 in range(0, logits.shape[0], ROUTE_ROWS):
            route = _route_rows(logits[r0:r0 + ROUTE_ROWS], cnt_ref)
            route_ref[r0:r0 + ROUTE_ROWS, :] = route
            meta_ref[:, r0:r0 + ROUTE_ROWS] = route.T[
                ROUTE_GROUP_LANE:ROUTE_RANK_LANE + 1, :].astype(jnp.int32)


def _out_proj(att, lru, w, xp, xs, g, wr, br, *, tm, tn):
    t, aw = att.shape
    lw = lru.shape[1]
    d = w.shape[1]
    assert aw == lw and w.shape[0] == aw + lw
    npt = xp.shape[0] // tm
    kern = functools.partial(_outproj_kernel, n_prompt_tiles=npt, d_model=d)
    return pl.pallas_call(
        kern,
        grid=(t // tm, d // tn),
        in_specs=[
            pl.BlockSpec((tm, aw), lambda i, j: (i, 0)),
            pl.BlockSpec((tm, lw), lambda i, j: (i, 0)),
            pl.BlockSpec((aw, tn), lambda i, j: (0, j)),
            pl.BlockSpec((lw, tn), lambda i, j: (1, j)),
            pl.BlockSpec((tm, tn), lambda i, j: (jnp.minimum(i, npt - 1),
                                                 jnp.where(i < npt, j, 0))),
            pl.BlockSpec((tm, tn), lambda i, j: (jnp.maximum(i - npt, 0),
                                                 jnp.where(i < npt, 0, j))),
            pl.BlockSpec((1, tn), lambda i, j: (0, j)),
            pl.BlockSpec((tn, LANES), lambda i, j: (j, 0)),
            pl.BlockSpec((1, LANES), lambda i, j: (0, 0)),
        ],
        out_specs=[
            pl.BlockSpec((tm, tn), lambda i, j: (i, j)),
            pl.BlockSpec((tm, LANES), lambda i, j: (i, 0)),
            pl.BlockSpec((1, LANES), lambda i, j: (0, 0)),
            pl.BlockSpec((2, tm), lambda i, j: (0, i)),
        ],
        out_shape=[
            jax.ShapeDtypeStruct((t, d), F32),
            jax.ShapeDtypeStruct((t, LANES), F32),
            jax.ShapeDtypeStruct((1, LANES), F32),
            jax.ShapeDtypeStruct((2, t), jnp.int32),
        ],
        scratch_shapes=[pltpu.VMEM((tm, 1), F32), pltpu.VMEM((tm, LANES), F32)],
        compiler_params=_params(2),
        name="out_proj",
    )(att, lru, w, w, xp, xs, g, wr, br)


ROUTE_GROUP_LANE = EXPERTS_PER_GROUP
ROUTE_RANK_LANE = EXPERTS_PER_GROUP + 1
ROUTE_ROWS = 256


def _route_rows(logits, cnt_ref):
    lane = lax.broadcasted_iota(jnp.int32, logits.shape, 1).astype(F32)
    ninf = -jnp.inf

    def first_argmax(v, vmax):
        return jnp.min(jnp.where(v == vmax, lane, float(LANES)), axis=-1, keepdims=True)

    gl = jnp.where(lane < N_GROUPS, logits, ninf)
    gm = jnp.max(gl, axis=-1, keepdims=True)
    g_idx = first_argmax(gl, gm)
    g_w = 1.0 / jnp.sum(jnp.exp(gl - gm), axis=-1, keepdims=True)

    lo = N_GROUPS + EXPERTS_PER_GROUP * g_idx
    el = jnp.where(jnp.logical_and(lane >= lo, lane < lo + EXPERTS_PER_GROUP), logits, ninf)
    v1 = jnp.max(el, axis=-1, keepdims=True)
    i1 = first_argmax(el, v1)
    el2 = jnp.where(lane == i1, ninf, el)
    v2 = jnp.max(el2, axis=-1, keepdims=True)
    i2 = first_argmax(el2, v2)
    e2 = jnp.exp(v2 - v1)
    w1 = (1.0 / (1.0 + e2)) * g_w
    w2 = (e2 / (1.0 + e2)) * g_w
    gates = jnp.where(lane == i1 - lo, w1, 0.0) + jnp.where(lane == i2 - lo, w2, 0.0)

    rows = logits.shape[0]
    onehot = jnp.where(lane == g_idx, 1.0, 0.0)
    r_i = lax.broadcasted_iota(jnp.int32, (rows, rows), 0)
    c_i = lax.broadcasted_iota(jnp.int32, (rows, rows), 1)
    tri = jnp.where(c_i < r_i, 1.0, 0.0).astype(BF16)
    before = jnp.dot(tri, onehot.astype(BF16), preferred_element_type=F32) + cnt_ref[...]
    rank = jnp.sum(onehot * before, axis=-1, keepdims=True)
    cnt_ref[...] += jnp.sum(onehot, axis=0, keepdims=True)
    return (gates + jnp.where(lane == ROUTE_GROUP_LANE, g_idx, 0.0)
            + jnp.where(lane == ROUTE_RANK_LANE, rank, 0.0))


PLAN_CHUNK = 1024


def _plan_items(per, n_used, n_items_max, grp, p_ref, e_ref, key_ref, n_ref, run_ref):
    tstart_s, tend_s, tiles_s, before_s, next_s = grp
    n_items = per * n_used
    n_ref[0] = n_items

    def item(w, carry):
        valid = w < n_items
        wc = jnp.minimum(w, n_items - 1)
        gi = jnp.int32(0)
        for g in range(N_GROUPS):
            gi = gi + (wc >= per * tend_s[g]).astype(jnp.int32)
        local = wc - per * tstart_s[gi]
        tiles_w = tiles_s[gi]
        lpart = jnp.int32(0)
        for k in range(1, per):
            lpart = lpart + (local >= k * tiles_w).astype(jnp.int32)
        ltile = local - lpart * tiles_w
        spare = jnp.maximum(w - n_items, 0)
        part = jnp.where(valid, lpart, spare % per)
        p_ref[w] = jnp.where(valid, tstart_s[gi] + ltile, n_used + spare // per)
        e_ref[w] = part
        key = gi * per + part
        key_ref[w] = key
        nxt_g = next_s[gi]
        run_ref[3 * w] = jnp.logical_and(valid, ltile == 0).astype(jnp.int32)
        run_ref[3 * w + 1] = (per * before_s[gi] + part) % 2
        run_ref[3 * w + 2] = jnp.where(part + 1 < per, key + 1,
                                       jnp.where(nxt_g >= 0, nxt_g * per, -1))
        return carry
    lax.fori_loop(0, n_items_max, item, 0)


def _plan_kernel(group_ref, rank_ref, cnt_ref, dest_ref, src_ref, nused_ref,
                 up_p, up_e, up_key, up_n, up_run, dn_p, dn_e, dn_key, dn_n, dn_run,
                 tstart_s, tend_s, tiles_s, before_s, next_s, row0_s,
                 *, tile, n_tiles_max, up_parts, dn_parts):
    s = pl.program_id(0)

    @pl.when(s == 0)
    def _():
        start = jnp.int32(0)
        owned = jnp.int32(0)
        for g in range(N_GROUPS):
            tiles = (cnt_ref[g] + tile - 1) // tile
            tstart_s[g] = start
            row0_s[g] = start * tile
            tiles_s[g] = tiles
            before_s[g] = owned
            start = start + tiles
            tend_s[g] = start
            owned = owned + (tiles > 0).astype(jnp.int32)
        nxt = jnp.int32(-1)
        for g in reversed(range(N_GROUPS)):
            next_s[g] = nxt
            nxt = jnp.where(tiles_s[g] > 0, g, nxt)
        nused_ref[0] = start

        def clear(r, carry):
            src_ref[r] = 0
            return carry
        lax.fori_loop(0, n_tiles_max * tile, clear, 0, unroll=8)

        grp = (tstart_s, tend_s, tiles_s, before_s, next_s)
        _plan_items(up_parts, start, up_parts * n_tiles_max, grp, up_p, up_e, up_key, up_n, up_run)
        _plan_items(dn_parts, start, dn_parts * n_tiles_max, grp, dn_p, dn_e, dn_key, dn_n, dn_run)

    def place(i, carry):
        t = s * PLAN_CHUNK + i
        d = row0_s[group_ref[i]] + rank_ref[i]
        dest_ref[t] = d
        src_ref[d] = t
        return carry
    lax.fori_loop(0, PLAN_CHUNK, place, 0, unroll=8)


def _moe_plan(group, rank, counts, *, tile, n_tiles_max, up_parts, dn_parts):
    t = group.shape[0]
    i32 = jnp.int32
    smem = functools.partial(pl.BlockSpec, memory_space=pltpu.SMEM)
    n_up, n_dn = up_parts * n_tiles_max, dn_parts * n_tiles_max
    shapes = [(t,), (n_tiles_max * tile,), (1,),
              (n_up,), (n_up,), (n_up,), (1,), (3 * n_up,),
              (n_dn,), (n_dn,), (n_dn,), (1,), (3 * n_dn,)]
    outs = pl.pallas_call(
        functools.partial(_plan_kernel, tile=tile, n_tiles_max=n_tiles_max,
                          up_parts=up_parts, dn_parts=dn_parts),
        grid=(t // PLAN_CHUNK,),
        in_specs=[smem((PLAN_CHUNK,), lambda s: (s,)), smem((PLAN_CHUNK,), lambda s: (s,)),
                  smem()],
        out_specs=[smem() for _ in shapes],
        out_shape=[jax.ShapeDtypeStruct(sh, i32) for sh in shapes],
        scratch_shapes=[pltpu.SMEM((N_GROUPS,), i32) for _ in range(6)],
        compiler_params=_params(1),
        name="moe_plan",
    )(group, rank, counts)
    dest, src, n_used = outs[:3]
    return dict(dest=dest, src=src, n_used=n_used, up=tuple(outs[3:8]), down=tuple(outs[8:13]))


def _row_copy(src_hbm, dst_vmem, src_row, dst_row, sem):
    return pltpu.make_async_copy(src_hbm.at[pl.ds(src_row, 1)], dst_vmem.at[pl.ds(dst_row, 1)], sem)


def _start_row_gather(idx_ref, base, n_rows, pairs):
    def body(r, carry):
        i = idx_ref[base + r]
        for src_hbm, dst_vmem, sem in pairs:
            _row_copy(src_hbm, dst_vmem, i, r, sem).start()
        return carry
    lax.fori_loop(0, n_rows, body, 0, unroll=8)


def _start_rows(idx_ref, idx_base, row0, n_rows, pairs):
    for k in range(n_rows):
        i = idx_ref[idx_base + row0 + k]
        for src_hbm, dst_vmem, sem in pairs:
            _row_copy(src_hbm, dst_vmem, i, row0 + k, sem).start()


def _norm_tile_and_prefetch(load_rows, g_ref, o_ref, n_rows, has_next, idx_ref, next_base,
                            next_pairs):
    @pl.when(has_next)
    def _():
        _rmsnorm_rows(load_rows, g_ref, o_ref, n_rows,
                      per_trip=lambda row0: _start_rows(idx_ref, next_base, row0, NORM_ROWS,
                                                        next_pairs))

    @pl.when(jnp.logical_not(has_next))
    def _():
        _rmsnorm_rows(load_rows, g_ref, o_ref, n_rows)


def _wait_row_gather(n_rows, pairs):
    for src_hbm, dst_vmem, sem in pairs:
        pltpu.make_async_copy(src_hbm.at[pl.ds(0, n_rows)], dst_vmem, sem).wait()


def _dispatch_kernel(src_ref, nused_ref, x2_hbm, route_hbm, g_ref, xs_ref, gs_ref,
                     xbuf, gbuf, sem):
    p = pl.program_id(0)
    n_used = nused_ref[0]
    tile = xbuf.shape[1]
    slot = p % 2

    def pairs(s):
        return [(x2_hbm, xbuf.at[s], sem.at[0, s]), (route_hbm, gbuf.at[s], sem.at[1, s])]

    @pl.when(p == 0)
    def _():
        _start_row_gather(src_ref, 0, tile, pairs(0))

    @pl.when(p < n_used)
    def _():
        _wait_row_gather(tile, pairs(slot))
        gs_ref[...] = gbuf[slot]
        _norm_tile_and_prefetch(lambda rows: xbuf[slot, rows, :], g_ref, xs_ref, tile,
                                p + 1 < n_used, src_ref, (p + 1) * tile, pairs(1 - slot))

    @pl.when(p >= n_used)
    def _():
        xs_ref[...] = jnp.zeros_like(xs_ref)
        gs_ref[...] = jnp.zeros_like(gs_ref)


def _dispatch(x2, route, g, src, n_used, *, tile, n_tiles_max):
    t, d = x2.shape
    tile_map = lambda p, src_ref, n_ref: (p, 0)
    grid_spec = pltpu.PrefetchScalarGridSpec(
        num_scalar_prefetch=2,
        grid=(n_tiles_max,),
        in_specs=[
            pl.BlockSpec(memory_space=pl.ANY),
            pl.BlockSpec(memory_space=pl.ANY),
            pl.BlockSpec((1, d), lambda p, s, n: (0, 0)),
        ],
        out_specs=[
            pl.BlockSpec((tile, d), tile_map),
            pl.BlockSpec((tile, LANES), tile_map),
        ],
        scratch_shapes=[
            pltpu.VMEM((2, tile, d), F32),
            pltpu.VMEM((2, tile, LANES), F32),
            pltpu.SemaphoreType.DMA((2, 2)),
        ],
    )
    return pl.pallas_call(
        _dispatch_kernel,
        grid_spec=grid_spec,
        out_shape=[
            jax.ShapeDtypeStruct((n_tiles_max * tile, d), BF16),
            jax.ShapeDtypeStruct((n_tiles_max * tile, LANES), F32),
        ],
        compiler_params=_params(1),
        name="moe_dispatch",
    )(src, n_used, x2, route, g)


def _stream_run_weights(w, n_items, key_ref, run_ref, copies):
    is_first = run_ref[3 * w]
    slot = run_ref[3 * w + 1]
    nxt_key = run_ref[3 * w + 2]
    valid = w < n_items

    @pl.when(jnp.logical_and(valid, w == 0))
    def _():
        for cp in copies(key_ref[0], 0):
            cp.start()

    @pl.when(jnp.logical_and(valid, is_first == 1))
    def _():
        for cp in copies(key_ref[w], slot):
            cp.wait()

        @pl.when(nxt_key >= 0)
        def _():
            for cp in copies(nxt_key, 1 - slot):
                cp.start()

    return slot


def _moe_up_kernel(ip_ref, ie_ref, key_ref, n_ref, run_ref, xs_ref, gs_ref, w1_hbm, w3_hbm,
                   hid_ref, w1_buf, w3_buf, sem):
    w = pl.program_id(0)

    n_e = w1_buf.shape[1]
    f = w1_buf.shape[3]

    def copies(key, slot):
        experts = pl.ds(key * n_e, n_e)
        return [pltpu.make_async_copy(w1_hbm.at[experts], w1_buf.at[slot], sem.at[0, slot]),
                pltpu.make_async_copy(w3_hbm.at[experts], w3_buf.at[slot], sem.at[1, slot])]

    slot = _stream_run_weights(w, n_ref[0], key_ref, run_ref, copies)

    @pl.when(w < n_ref[0])
    def _():
        x = xs_ref[...]
        lane = lax.broadcasted_iota(jnp.int32, gs_ref.shape, 1)
        for e in range(n_e):
            a = jnp.dot(x, w1_buf[slot, e].astype(BF16), preferred_element_type=F32)
            b = jnp.dot(x, w3_buf[slot, e].astype(BF16), preferred_element_type=F32)
            in_group = ie_ref[w] * n_e + e
            gate = jnp.sum(jnp.where(lane == in_group, gs_ref[...], 0.0), axis=-1, keepdims=True)
            hid_ref[:, e * f:(e + 1) * f] = (jax.nn.silu(a) * b * gate).astype(hid_ref.dtype)

    @pl.when(w >= n_ref[0])
    def _():
        hid_ref[...] = jnp.zeros_like(hid_ref)


def _moe_up(xs, gs, w1, w3, items, *, tile):
    rows, d = xs.shape
    f = w1.shape[2]
    ip, ie, key, n_items, run = items
    grid_spec = pltpu.PrefetchScalarGridSpec(
        num_scalar_prefetch=5,
        grid=(ip.shape[0],),
        in_specs=[
            pl.BlockSpec((tile, d), lambda w, ip, ie, key, n, run: (ip[w], 0)),
            pl.BlockSpec((tile, LANES), lambda w, ip, ie, key, n, run: (ip[w], 0)),
            pl.BlockSpec(memory_space=pl.ANY),
            pl.BlockSpec(memory_space=pl.ANY),
        ],
        out_specs=pl.BlockSpec((tile, MOE_UP_EXPERTS * f),
                               lambda w, ip, ie, key, n, run: (ip[w], ie[w])),
        scratch_shapes=[
            pltpu.VMEM((2, MOE_UP_EXPERTS, d, f), w1.dtype),
            pltpu.VMEM((2, MOE_UP_EXPERTS, d, f), w3.dtype),
            pltpu.SemaphoreType.DMA((2, 2)),
        ],
    )
    return pl.pallas_call(
        _moe_up_kernel,
        grid_spec=grid_spec,
        out_shape=jax.ShapeDtypeStruct((rows, EXPERTS_PER_GROUP * f), BF16),
        compiler_params=_params(1),
        name="moe_up",
    )(ip, ie, key, n_items, run, xs, gs, w1, w3)


def _moe_down_kernel(ip_ref, ih_ref, key_ref, n_ref, run_ref, hid_ref, w2_hbm, y_ref,
                     w2_buf, sem, *, n_halves):
    w = pl.program_id(0)
    tn = w2_buf.shape[2]

    def copies(key, slot):
        g = key // n_halves
        col = pl.multiple_of((key % n_halves) * tn, tn)
        return [pltpu.make_async_copy(w2_hbm.at[g, :, pl.ds(col, tn)], w2_buf.at[slot],
                                      sem.at[slot])]

    slot = _stream_run_weights(w, n_ref[0], key_ref, run_ref, copies)

    @pl.when(w < n_ref[0])
    def _():
        y_ref[...] = jnp.dot(hid_ref[...], w2_buf[slot].astype(BF16),
                             preferred_element_type=F32)

    @pl.when(w >= n_ref[0])
    def _():
        y_ref[...] = jnp.zeros_like(y_ref)


def _moe_down(hid, w2g, items, *, tile, n_halves):
    rows, k = hid.shape
    d = w2g.shape[2]
    tn = d // n_halves
    ip, ih, key, n_items, run = items
    grid_spec = pltpu.PrefetchScalarGridSpec(
        num_scalar_prefetch=5,
        grid=(ip.shape[0],),
        in_specs=[
            pl.BlockSpec((tile, k), lambda w, ip, ih, key, n, run: (ip[w], 0)),
            pl.BlockSpec(memory_space=pl.ANY),
        ],
        out_specs=pl.BlockSpec((tile, tn), lambda w, ip, ih, key, n, run: (ip[w], ih[w])),
        scratch_shapes=[
            pltpu.VMEM((2, k, tn), w2g.dtype),
            pltpu.SemaphoreType.DMA((2,)),
        ],
    )
    return pl.pallas_call(
        functools.partial(_moe_down_kernel, n_halves=n_halves),
        grid_spec=grid_spec,
        out_shape=jax.ShapeDtypeStruct((rows, d), F32),
        compiler_params=_params(1),
        name="moe_down",
    )(ip, ih, key, n_items, run, hid, w2g)


def _final_kernel(dest_ref, x_ref, ys_hbm, g_ref, o_ref, ybuf, sem, *, tile0):
    i = pl.program_id(0)
    tm = ybuf.shape[1]
    slot = i % 2

    def pairs(s):
        return [(ys_hbm, ybuf.at[s], sem.at[s])]

    @pl.when(i == 0)
    def _():
        _start_row_gather(dest_ref, tile0 * tm, tm, pairs(0))

    _wait_row_gather(tm, pairs(slot))
    _norm_tile_and_prefetch(lambda rows: x_ref[rows, :] + ybuf[slot, rows, :], g_ref, o_ref, tm,
                            i + 1 < pl.num_programs(0), dest_ref, (tile0 + i + 1) * tm,
                            pairs(1 - slot))


def _final(x2, ys, dest, g, *, row0, n_rows, tm):
    d = x2.shape[1]
    off = row0 // tm
    grid_spec = pltpu.PrefetchScalarGridSpec(
        num_scalar_prefetch=1,
        grid=(n_rows // tm,),
        in_specs=[
            pl.BlockSpec((tm, d), lambda i, dest: (i + off, 0)),
            pl.BlockSpec(memory_space=pl.ANY),
            pl.BlockSpec((1, d), lambda i, dest: (0, 0)),
        ],
        out_specs=pl.BlockSpec((tm, d), lambda i, dest: (i, 0)),
        scratch_shapes=[pltpu.VMEM((2, tm, d), F32), pltpu.SemaphoreType.DMA((2,))],
    )
    return pl.pallas_call(
        functools.partial(_final_kernel, tile0=off),
        grid_spec=grid_spec,
        out_shape=jax.ShapeDtypeStruct((n_rows, d), F32),
        compiler_params=_params(1),
        name="final_norm",
    )(dest, x2, ys, g)


def _rope_tables(pos):
    half = ROT_DIM // 2
    inv = ROPE_THETA ** (-np.arange(half, dtype=np.float64) / half)
    ang = np.asarray(pos, np.float64)[:, None] * inv[None, :]
    cos, sin = np.cos(ang), np.sin(ang)
    ones = np.ones((ang.shape[0], HEAD_DIM - ROT_DIM))
    zeros = np.zeros((ang.shape[0], HEAD_DIM - half))
    c = np.concatenate([cos, cos, ones], axis=1)
    s_hi = np.concatenate([-sin, zeros], axis=1)
    s_lo = np.concatenate([np.zeros_like(sin), sin, 0.0 * ones], axis=1)
    rep = LANES // HEAD_DIM
    return tuple(jnp.asarray(np.tile(t, (1, rep)), F32) for t in (c, s_hi, s_lo))


def kernel(x_prompt, x_sample, cache_k, cache_v, state_conv, state_h, norm_mix, w_in, conv_w,
           conv_b, w_gate_a, b_gate_a, w_gate_x, b_gate_x, lru_lambda, sinks, norm_attn_out,
           norm_lru_out, w_out, norm_ffn, w_group, b_group, w_expert_router, b_expert_router,
           w1, w3, w2, norm_final):
    batch, seq, d_model = x_prompt.shape
    dec_batch, dec_seq, _ = x_sample.shape
    depth = w_in.shape[0]
    assert depth == 1 and dec_seq == CHUNK and seq % CHUNK == 0
    n_heads = sinks.shape[1]
    att_width = n_heads * HEAD_DIM
    n_kv = cache_k.shape[3]
    kv_cols = n_kv * HEAD_DIM
    lru_width = lru_lambda.shape[1]
    cw = cache_k.shape[2]
    assert cw == WIN_CHUNKS * CHUNK
    n_p, n_s = batch * seq, dec_batch * dec_seq
    chunks_per_seq = seq // CHUNK
    n_prompt_chunks = n_p // CHUNK

    xp = x_prompt.reshape(n_p, d_model)
    xs = x_sample.reshape(n_s, d_model)

    w_r = w_in[0]
    tn = 2 * kv_cols
    assert tn == 512 and att_width % tn == 0 and lru_width % tn == 0
    n_q_tiles = att_width // tn
    kv_tile = n_q_tiles

    tm_in = PROJ_ROWS
    assert seq % tm_in == 0 and n_s % tm_in == 0 and tm_in % dec_seq == 0
    pos = np.concatenate([np.arange(seq), np.tile(PAST_LEN + np.arange(dec_seq), tm_in // dec_seq)])
    c_tab, shi_tab, slo_tab = _rope_tables(pos)

    z = _in_proj(xp, xs, norm_mix, w_r, c_tab, shi_tab, slo_tab, tm=tm_in, tn=tn,
                 n_q_tiles=n_q_tiles, kv_tile=kv_tile, prompt_tiles_per_seq=seq // tm_in)

    k_col = att_width + 2 * lru_width
    v_col = k_col + kv_cols
    att = _attention(
        z, cache_k[0].reshape(dec_batch, cw, kv_cols), cache_v[0].reshape(dec_batch, cw, kv_cols),
        sinks[0], norm_attn_out,
        n_prompt_chunks=n_prompt_chunks, chunks_per_seq=chunks_per_seq, att_width=att_width,
        n_kv=n_kv, k_col_blk=k_col // kv_cols, v_col_blk=v_col // kv_cols)

    lru, h_tiles = _rglru(
        z, state_conv[0], state_h[0].reshape(dec_batch, 1, lru_width), conv_w[0], conv_b,
        w_gate_a[0].astype(BF16), b_gate_a[0].reshape(1, lru_width),
        w_gate_x[0].astype(BF16), b_gate_x[0].reshape(1, lru_width),
        lru_lambda, norm_lru_out,
        n_prompt_chunks=n_prompt_chunks, chunks_per_seq=chunks_per_seq, width=lru_width,
        xb_col_blk=att_width // lru_width, yb_col_blk=att_width // lru_width + 1)

    n_routes = N_GROUPS + N_GROUPS * EXPERTS_PER_GROUP
    wr = jnp.concatenate([w_group[0], w_expert_router[0],
                          jnp.zeros((d_model, LANES - n_routes), F32)], axis=1).astype(BF16)
    br = jnp.concatenate([b_group[0], b_expert_router[0],
                          jnp.zeros((LANES - n_routes,), F32)]).reshape(1, LANES)
    x2, route, counts, meta = _out_proj(att, lru, w_out[0], xp, xs, norm_ffn, wr, br,
                                        tm=PROJ_ROWS, tn=OUT_PROJ_COLS)

    n_tiles_max = (n_p + n_s) // MOE_TILE + N_GROUPS
    tabs = _moe_plan(meta[0], meta[1], counts[0, :N_GROUPS].astype(jnp.int32), tile=MOE_TILE,
                     n_tiles_max=n_tiles_max, up_parts=EXPERTS_PER_GROUP // MOE_UP_EXPERTS,
                     dn_parts=MOE_DOWN_HALVES)
    xsort, gsort = _dispatch(x2, route, norm_ffn, tabs["src"], tabs["n_used"],
                             tile=MOE_TILE, n_tiles_max=n_tiles_max)
    hid = _moe_up(xsort, gsort, w1[0], w3[0], tabs["up"], tile=MOE_TILE)
    d_expert = w2.shape[2]
    w2g = w2[0].reshape(N_GROUPS, EXPERTS_PER_GROUP * d_expert, d_model)
    ysort = _moe_down(hid, w2g, tabs["down"], tile=MOE_TILE, n_halves=MOE_DOWN_HALVES)

    g_fin = norm_final.reshape(1, d_model)
    y_prompt = _final(x2, ysort, tabs["dest"], g_fin, row0=0, n_rows=n_p,
                      tm=MOE_TILE).reshape(batch, seq, d_model)
    y_sample = _final(x2, ysort, tabs["dest"], g_fin, row0=n_p, n_rows=n_s,
                      tm=MOE_TILE).reshape(dec_batch, dec_seq, d_model)

    win = min(WIN_CHUNKS * CHUNK, seq)

    def prompt_tail(col, width, rows):
        return jnp.stack([lax.slice(z, ((b + 1) * seq - rows, col), ((b + 1) * seq, col + width))
                          for b in range(batch)])

    def sample_rows(col, width):
        return lax.slice(z, (n_p, col), (n_p + n_s, col + width)).reshape(dec_batch, dec_seq, width)

    k_prompt = prompt_tail(k_col, kv_cols, win).reshape(1, batch, win, n_kv, HEAD_DIM)
    v_prompt = prompt_tail(v_col, kv_cols, win).reshape(1, batch, win, n_kv, HEAD_DIM)
    conv_prompt = prompt_tail(att_width, lru_width, CONV_WIDTH - 1)[None]
    h_prompt = h_tiles[:n_prompt_chunks, 0].reshape(batch, chunks_per_seq, lru_width)[:, -1][None]

    ks = sample_rows(k_col, kv_cols).reshape(dec_batch, dec_seq, n_kv, HEAD_DIM)
    vs = sample_rows(v_col, kv_cols).reshape(dec_batch, dec_seq, n_kv, HEAD_DIM)
    k_sample = jnp.concatenate([cache_k[0], ks], axis=1)[:, -cw:][None]
    v_sample = jnp.concatenate([cache_v[0], vs], axis=1)[:, -cw:][None]
    xs_rows = sample_rows(att_width, lru_width)[:, dec_seq - (CONV_WIDTH - 1):]
    conv_sample = jnp.concatenate([state_conv[0], xs_rows], axis=1)[:, -(CONV_WIDTH - 1):][None]
    h_sample = h_tiles[n_prompt_chunks:, 0][None]

    return (y_prompt, y_sample, k_prompt, v_prompt, conv_prompt, h_prompt,
            k_sample, v_sample, conv_sample, h_sample)
```

```python
import functools

import numpy as np
import jax
import jax.numpy as jnp
from jax import lax
from jax.experimental import pallas as pl
from jax.experimental.pallas import tpu as pltpu

F32 = jnp.float32
BF16 = jnp.bfloat16

CHUNK = 64
HEAD_DIM = 64
KV_GROUP = 8
WIN_CHUNKS = 2
ROT_DIM = 16
ROPE_THETA = 500000.0
LRU_BLOCK = 128
CONV_WIDTH = 4
LRU_C = 8.0
N_GROUPS = 8
EXPERTS_PER_GROUP = 4
EPS = 1e-6
NEG = -1e30
PAST_LEN = 1024
LANES = 128
PROJ_ROWS = 1024
OUT_PROJ_COLS = 512
INPROJ_ROW_BLOCK = 128
MOE_TILE = 256
FINAL_ROWS = 256
MOE_UP_EXPERTS = 2
MOE_DOWN_HALVES = 1
VMEM_LIMIT = 56 * 1024 * 1024


def _params(n_axes, vmem=VMEM_LIMIT):
    return pltpu.CompilerParams(
        dimension_semantics=("arbitrary",) * n_axes, vmem_limit_bytes=vmem)


def _rms_scale(x):
    return lax.rsqrt(jnp.mean(x * x, axis=-1, keepdims=True) + EPS)


NORM_ROWS = 16


def _rmsnorm_rows(load_rows, g_ref, o_ref, n_rows, per_trip=None):
    def body(r, carry):
        row0 = r * NORM_ROWS if isinstance(r, int) else pl.multiple_of(r * NORM_ROWS, NORM_ROWS)
        rows = pl.ds(row0, NORM_ROWS)
        x = load_rows(rows)
        o_ref[rows, :] = (x * _rms_scale(x) * g_ref[...]).astype(o_ref.dtype)
        if per_trip is not None:
            per_trip(row0)
        return carry
    if per_trip is not None:
        for r in range(n_rows // NORM_ROWS):
            body(r, 0)
    else:
        lax.fori_loop(0, n_rows // NORM_ROWS, body, 0, unroll=4)


def _rope_block(zb, c, s_hi, s_lo):
    return (zb * c + pltpu.roll(zb, LANES - ROT_DIM // 2, 1) * s_hi
            + pltpu.roll(zb, ROT_DIM // 2, 1) * s_lo)


def _inproj_kernel(xp_hbm, xs_hbm, g_ref, w_ref, c_ref, shi_ref, slo_ref, o_ref, x_buf, h_ref,
                   sem, *, n_prompt_tiles, n_q_tiles, kv_tile):
    i = pl.program_id(0)
    j = pl.program_id(1)
    tm, tn = o_ref.shape

    def x_tile_copy(src_hbm, tile):
        return pltpu.make_async_copy(src_hbm.at[pl.ds(tile * tm, tm)], x_buf, sem)

    def start_x(tile):
        @pl.when(tile < n_prompt_tiles)
        def _():
            x_tile_copy(xp_hbm, tile).start()

        @pl.when(tile >= n_prompt_tiles)
        def _():
            x_tile_copy(xs_hbm, tile - n_prompt_tiles).start()

    @pl.when(jnp.logical_and(i == 0, j == 0))
    def _():
        start_x(0)

    @pl.when(j == 0)
    def _():
        x_tile_copy(xp_hbm, 0).wait()
        _rmsnorm_rows(lambda rows: x_buf[rows, :], g_ref, h_ref, tm)

    @pl.when(jnp.logical_and(j == 1, i + 1 < pl.num_programs(0)))
    def _():
        start_x(i + 1)

    w = w_ref[...].astype(BF16)
    n_blk = tn // LANES
    for r0 in range(0, tm, INPROJ_ROW_BLOCK):
        rows = slice(r0, r0 + INPROJ_ROW_BLOCK)
        z = jnp.dot(h_ref[rows, :], w, preferred_element_type=F32)
        c, shi, slo = c_ref[rows, :], shi_ref[rows, :], slo_ref[rows, :]
        for b in range(n_blk):
            is_rope = j < n_q_tiles
            if b < n_blk // 2:
                is_rope = jnp.logical_or(is_rope, j == kv_tile)
            zb = z[:, b * LANES:(b + 1) * LANES]
            o_ref[rows, b * LANES:(b + 1) * LANES] = _rope_block(
                zb, jnp.where(is_rope, c, 1.0), jnp.where(is_rope, shi, 0.0),
                jnp.where(is_rope, slo, 0.0))


def _in_proj(xp, xs, g, w, c, shi, slo, *, tm, tn, n_q_tiles, kv_tile, prompt_tiles_per_seq):
    n_p, d = xp.shape
    n_s = xs.shape[0]
    t = n_p + n_s
    npt = n_p // tm
    n_cols = w.shape[1]
    kern = functools.partial(_inproj_kernel, n_prompt_tiles=npt, n_q_tiles=n_q_tiles,
                             kv_tile=kv_tile)
    n_tiles = n_cols // tn
    assert kv_tile == n_q_tiles

    def table_tile(i, j):
        return jnp.where(i < npt, i % prompt_tiles_per_seq, prompt_tiles_per_seq), 0

    def out_tile(i, j):
        return i, jnp.where(j < kv_tile, j, jnp.where(j == kv_tile, n_tiles - 1, j - 1))

    return pl.pallas_call(
        kern,
        grid=(t // tm, n_tiles),
        in_specs=[
            pl.BlockSpec(memory_space=pl.ANY),
            pl.BlockSpec(memory_space=pl.ANY),
            pl.BlockSpec((1, d), lambda i, j: (0, 0)),
            pl.BlockSpec((d, tn), lambda i, j: (0, j)),
            pl.BlockSpec((tm, LANES), table_tile),
            pl.BlockSpec((tm, LANES), table_tile),
            pl.BlockSpec((tm, LANES), table_tile),
        ],
        out_specs=pl.BlockSpec((tm, tn), out_tile),
        out_shape=jax.ShapeDtypeStruct((t, n_cols), F32),
        scratch_shapes=[pltpu.VMEM((tm, d), xp.dtype), pltpu.VMEM((tm, d), BF16),
                        pltpu.SemaphoreType.DMA(())],
        compiler_params=_params(2),
        name="in_proj",
    )(xp, xs, g, w, c, shi, slo)


ATTN_CHUNKS = 8


def _attn_chunk(q, k_band, v_band, lo, sink_ref, n_kv):
    n_keys = k_band.shape[0]
    key = lax.broadcasted_iota(jnp.int32, (n_keys, KV_GROUP * CHUNK), 0)
    valid = key >= lo
    outs = []
    for gi in range(n_kv):
        kg = k_band[:, gi * HEAD_DIM:(gi + 1) * HEAD_DIM]
        vg = v_band[:, gi * HEAD_DIM:(gi + 1) * HEAD_DIM]
        heads = [gi * KV_GROUP + h for h in range(KV_GROUP)]
        qg = jnp.concatenate([q[:, h * HEAD_DIM:(h + 1) * HEAD_DIM] for h in heads], axis=0)
        st = lax.dot_general(kg, qg, (((1,), (1,)), ((), ())), preferred_element_type=F32)
        st = jnp.where(valid, st, NEG)
        sink = sink_ref[gi:gi + 1, :]
        m = jnp.maximum(jnp.max(st, axis=0, keepdims=True), sink)
        p = jnp.exp(st - m)
        denom = jnp.sum(p, axis=0, keepdims=True) + jnp.exp(sink - m)
        pn = (p * (1.0 / denom)).astype(BF16)
        og = lax.dot_general(pn, vg, (((0,), (0,)), ((), ())), preferred_element_type=F32)
        outs.extend(og[h * CHUNK:(h + 1) * CHUNK] for h in range(KV_GROUP))
    return jnp.concatenate(outs, axis=1)


def _attn_kernel(q_ref, kp_ref, k_ref, vp_ref, v_ref, kc_ref, vc_ref, sink_ref, g_ref, o_ref,
                 *, n_prompt_tiles, tiles_per_seq, n_kv):
    n = pl.program_id(0)
    is_s = n >= n_prompt_tiles
    has_prev = jnp.logical_or(is_s, n % tiles_per_seq > 0)

    def band(prev_ref, own_ref, cache_ref, c):
        prev, own, cache = prev_ref[...], own_ref[...], cache_ref[c]
        hist = jnp.concatenate([prev, own], axis=0)[c * CHUNK:(c + WIN_CHUNKS) * CHUNK]
        hist = jnp.where(is_s, cache, hist)
        return jnp.concatenate([hist, own[c * CHUNK:(c + 1) * CHUNK]], axis=0).astype(BF16)

    q = (q_ref[...] * (HEAD_DIM ** -0.5)).astype(BF16)
    for c in range(ATTN_CHUNKS):
        lo = jnp.where(has_prev, 0, max(WIN_CHUNKS - c, 0) * CHUNK)
        att = _attn_chunk(q[c * CHUNK:(c + 1) * CHUNK], band(kp_ref, k_ref, kc_ref, c),
                          band(vp_ref, v_ref, vc_ref, c), lo, sink_ref, n_kv)
        o_ref[c * CHUNK:(c + 1) * CHUNK, :] = (
            att * _rms_scale(att) * g_ref[...]).astype(o_ref.dtype)


def _attention(z, cache_k, cache_v, sinks, g, *, n_prompt_chunks, chunks_per_seq,
               att_width, n_kv, k_col_blk, v_col_blk):
    t = z.shape[0]
    rows = ATTN_CHUNKS * CHUNK
    assert ATTN_CHUNKS % WIN_CHUNKS == 0 and chunks_per_seq % ATTN_CHUNKS == 0
    hist_rows = WIN_CHUNKS * CHUNK
    hist_per_tile = rows // hist_rows
    assert cache_k.shape[0] % ATTN_CHUNKS == 0
    n_tiles = t // rows
    npt = n_prompt_chunks // ATTN_CHUNKS
    tps = chunks_per_seq // ATTN_CHUNKS
    kvw = n_kv * HEAD_DIM

    def prev(col):
        return lambda n: (n * hist_per_tile - jnp.minimum(1, n % tps), col)

    cache_map = lambda n: (jnp.maximum(n - npt, 0), 0, 0)
    kern = functools.partial(_attn_kernel, n_prompt_tiles=npt, tiles_per_seq=tps, n_kv=n_kv)
    sink_rows = jnp.repeat(sinks.reshape(n_kv, KV_GROUP), CHUNK, axis=1)
    return pl.pallas_call(
        kern,
        grid=(n_tiles,),
        in_specs=[
            pl.BlockSpec((rows, att_width), lambda n: (n, 0)),
            pl.BlockSpec((hist_rows, kvw), prev(k_col_blk)),
            pl.BlockSpec((rows, kvw), lambda n: (n, k_col_blk)),
            pl.BlockSpec((hist_rows, kvw), prev(v_col_blk)),
            pl.BlockSpec((rows, kvw), lambda n: (n, v_col_blk)),
            pl.BlockSpec((ATTN_CHUNKS, WIN_CHUNKS * CHUNK, kvw), cache_map),
            pl.BlockSpec((ATTN_CHUNKS, WIN_CHUNKS * CHUNK, kvw), cache_map),
            pl.BlockSpec((n_kv, KV_GROUP * CHUNK), lambda n: (0, 0)),
            pl.BlockSpec((1, att_width), lambda n: (0, 0)),
        ],
        out_specs=pl.BlockSpec((rows, att_width), lambda n: (n, 0)),
        out_shape=jax.ShapeDtypeStruct((t, att_width), BF16),
        compiler_params=_params(1),
        name="attention",
    )(z, z, z, z, z, cache_k, cache_v, sink_rows, g)


SUBLANES = 8


def _lru_kernel(xb_ref, yb_ref, sconv_ref, sh_ref, cw_ref, cb_ref, wa_ref, ba_ref,
                wx_ref, bx_ref, lam_ref, g_ref, o_ref, hl_ref, hist_ref, hcar_ref,
                a_sc, b_sc, h_sc, *, n_prompt_chunks, chunks_per_seq):
    n = pl.program_id(0)
    is_s = n >= n_prompt_chunks
    hist_rows = hist_ref.shape[0]

    @pl.when(jnp.logical_and(jnp.logical_not(is_s), n % chunks_per_seq == 0))
    def _():
        hist_ref[...] = jnp.zeros_like(hist_ref)
        hcar_ref[...] = jnp.zeros_like(hcar_ref)

    @pl.when(is_s)
    def _():
        hist_ref[...] = jnp.zeros_like(hist_ref)
        hist_ref[hist_rows - (CONV_WIDTH - 1):, :] = sconv_ref[0]
        hcar_ref[...] = sh_ref[0]

    x = xb_ref[...]
    rows, width = x.shape
    hist = hist_ref[...]
    row8 = lax.broadcasted_iota(jnp.int32, (hist_rows, width), 0)
    xc = cb_ref[...]
    for j in range(CONV_WIDTH):
        s = CONV_WIDTH - 1 - j
        if s == 0:
            xs = x
        else:
            xr = pltpu.roll(x, s, 0)
            head = jnp.where(row8 < s, pltpu.roll(hist, s, 0), xr[:hist_rows])
            xs = jnp.concatenate([head, xr[hist_rows:]], axis=0)
        xc = xc + xs * cw_ref[j:j + 1, :]
    hist_ref[...] = x[rows - hist_rows:]

    xcb = xc.astype(BF16)
    ra, rx = [], []
    for nb in range(width // LRU_BLOCK):
        blk = xcb[:, nb * LRU_BLOCK:(nb + 1) * LRU_BLOCK]
        ra.append(jnp.dot(blk, wa_ref[nb], preferred_element_type=F32))
        rx.append(jnp.dot(blk, wx_ref[nb], preferred_element_type=F32))
    r = jax.nn.sigmoid(jnp.concatenate(ra, axis=1) + ba_ref[...])
    ig = jax.nn.sigmoid(jnp.concatenate(rx, axis=1) + bx_ref[...])
    log_a = (-LRU_C * r) * jax.nn.softplus(-lam_ref[...])
    a = jnp.exp(log_a)
    b = jnp.sqrt(-jnp.tanh(log_a) * (1.0 + a * a)) * ig * xc

    seg_len = rows // SUBLANES
    h_blocks, carries = [], []
    for nb in range(width // LANES):
        lanes = slice(nb * LANES, (nb + 1) * LANES)
        a_sc[nb] = a[:, lanes]
        b_sc[nb] = b[:, lanes]
        hloc, ploc = [], []
        for k in range(seg_len):
            a_k = a_sc[nb, pl.ds(k, SUBLANES, stride=seg_len), :]
            b_k = b_sc[nb, pl.ds(k, SUBLANES, stride=seg_len), :]
            hloc.append(b_k if k == 0 else a_k * hloc[-1] + b_k)
            ploc.append(a_k if k == 0 else a_k * ploc[-1])
        carry = hcar_ref[:, lanes]
        seg_in = []
        for s in range(SUBLANES):
            seg_in.append(carry)
            carry = ploc[-1][s:s + 1] * carry + hloc[-1][s:s + 1]
        seg_in = jnp.concatenate(seg_in, axis=0)
        carries.append(carry)
        for k in range(seg_len):
            h_sc[nb, pl.ds(k, SUBLANES, stride=seg_len), :] = hloc[k] + ploc[k] * seg_in
        h_blocks.append(h_sc[nb])
    h = jnp.concatenate(h_blocks, axis=1)
    h_last = jnp.concatenate(carries, axis=1)
    hcar_ref[...] = h_last
    hl_ref[0] = h_last

    y = jax.nn.gelu(yb_ref[...]) * h
    o_ref[...] = (y * _rms_scale(y) * g_ref[...]).astype(o_ref.dtype)


def _rglru(z, state_conv, state_h, conv_w, conv_b, wa, ba, wx, bx, lam, g,
           *, n_prompt_chunks, chunks_per_seq, width, xb_col_blk, yb_col_blk):
    t = z.shape[0]
    n_items = t // CHUNK
    npc = n_prompt_chunks
    nblk = width // LRU_BLOCK
    state_map = lambda n: (jnp.maximum(n - npc, 0), 0, 0)
    full2 = lambda n: (0, 0)
    full3 = lambda n: (0, 0, 0)
    kern = functools.partial(_lru_kernel, n_prompt_chunks=npc, chunks_per_seq=chunks_per_seq)
    return pl.pallas_call(
        kern,
        grid=(n_items,),
        in_specs=[
            pl.BlockSpec((CHUNK, width), lambda n: (n, xb_col_blk)),
            pl.BlockSpec((CHUNK, width), lambda n: (n, yb_col_blk)),
            pl.BlockSpec((1, CONV_WIDTH - 1, width), state_map),
            pl.BlockSpec((1, 1, width), state_map),
            pl.BlockSpec((CONV_WIDTH, width), full2),
            pl.BlockSpec((1, width), full2),
            pl.BlockSpec((nblk, LRU_BLOCK, LRU_BLOCK), full3),
            pl.BlockSpec((1, width), full2),
            pl.BlockSpec((nblk, LRU_BLOCK, LRU_BLOCK), full3),
            pl.BlockSpec((1, width), full2),
            pl.BlockSpec((1, width), full2),
            pl.BlockSpec((1, width), full2),
        ],
        out_specs=[
            pl.BlockSpec((CHUNK, width), lambda n: (n, 0)),
            pl.BlockSpec((1, 1, width), lambda n: (n, 0, 0)),
        ],
        out_shape=[
            jax.ShapeDtypeStruct((t, width), BF16),
            jax.ShapeDtypeStruct((n_items, 1, width), F32),
        ],
        scratch_shapes=[
            pltpu.VMEM((SUBLANES, width), F32),
            pltpu.VMEM((1, width), F32),
            pltpu.VMEM((width // LANES, CHUNK, LANES), F32),
            pltpu.VMEM((width // LANES, CHUNK, LANES), F32),
            pltpu.VMEM((width // LANES, CHUNK, LANES), F32),
        ],
        compiler_params=_params(1),
        name="rglru",
    )(z, z, state_conv, state_h, conv_w, conv_b, wa, ba, wx, bx, lam, g)


def _outproj_kernel(att_ref, lru_ref, wa_ref, wl_ref, xp_ref, xs_ref, g_ref, wr_ref, br_ref,
                    o_ref, route_ref, cnt_ref, meta_ref, ssq_ref, lg_ref,
                    *, n_prompt_tiles, d_model):
    i = pl.program_id(0)
    j = pl.program_id(1)
    acc = jnp.dot(att_ref[...], wa_ref[...].astype(BF16), preferred_element_type=F32)
    acc = acc + jnp.dot(lru_ref[...], wl_ref[...].astype(BF16), preferred_element_type=F32)
    x2 = jnp.where(i < n_prompt_tiles, xp_ref[...], xs_ref[...]) + acc
    o_ref[...] = x2

    @pl.when(jnp.logical_and(i == 0, j == 0))
    def _():
        cnt_ref[...] = jnp.zeros_like(cnt_ref)

    @pl.when(j == 0)
    def _():
        ssq_ref[...] = jnp.zeros_like(ssq_ref)
        lg_ref[...] = jnp.zeros_like(lg_ref)

    ssq_ref[...] += jnp.sum(x2 * x2, axis=-1, keepdims=True)
    lg_ref[...] += jnp.dot((x2 * g_ref[...]).astype(BF16), wr_ref[...],
                           preferred_element_type=F32)

    @pl.when(j == pl.num_programs(1) - 1)
    def _():
        scale = lax.rsqrt(ssq_ref[...] * (1.0 / d_model) + EPS)
        logits = lg_ref[...] * scale + br_ref[...]
        for r0 in range(0, logits.shape[0], ROUTE_ROWS):
            route = _route_rows(logits[r0:r0 + ROUTE_ROWS], cnt_ref)
            route_ref[r0:r0 + ROUTE_ROWS, :] = route
            meta_ref[:, r0:r0 + ROUTE_ROWS] = route.T[
                ROUTE_GROUP_LANE:ROUTE_RANK_LANE + 1, :].astype(jnp.int32)


def _out_proj(att, lru, w, xp, xs, g, wr, br, *, tm, tn):
    t, aw = att.shape
    lw = lru.shape[1]
    d = w.shape[1]
    assert aw == lw and w.shape[0] == aw + lw
    npt = xp.shape[0] // tm
    kern = functools.partial(_outproj_kernel, n_prompt_tiles=npt, d_model=d)
    return pl.pallas_call(
        kern,
        grid=(t // tm, d // tn),
        in_specs=[
            pl.BlockSpec((tm, aw), lambda i, j: (i, 0)),
            pl.BlockSpec((tm, lw), lambda i, j: (i, 0)),
            pl.BlockSpec((aw, tn), lambda i, j: (0, j)),
            pl.BlockSpec((lw, tn), lambda i, j: (1, j)),
            pl.BlockSpec((tm, tn), lambda i, j: (jnp.minimum(i, npt - 1),
                                                 jnp.where(i < npt, j, 0))),
            pl.BlockSpec((tm, tn), lambda i, j: (jnp.maximum(i - npt, 0),
                                                 jnp.where(i < npt, 0, j))),
            pl.BlockSpec((1, tn), lambda i, j: (0, j)),
            pl.BlockSpec((tn, LANES), lambda i, j: (j, 0)),
            pl.BlockSpec((1, LANES), lambda i, j: (0, 0)),
        ],
        out_specs=[
            pl.BlockSpec((tm, tn), lambda i, j: (i, j)),
            pl.BlockSpec((tm, LANES), lambda i, j: (i, 0)),
            pl.BlockSpec((1, LANES), lambda i, j: (0, 0)),
            pl.BlockSpec((2, tm), lambda i, j: (0, i)),
        ],
        out_shape=[
            jax.ShapeDtypeStruct((t, d), F32),
            jax.ShapeDtypeStruct((t, LANES), F32),
            jax.ShapeDtypeStruct((1, LANES), F32),
            jax.ShapeDtypeStruct((2, t), jnp.int32),
        ],
        scratch_shapes=[pltpu.VMEM((tm, 1), F32), pltpu.VMEM((tm, LANES), F32)],
        compiler_params=_params(2),
        name="out_proj",
    )(att, lru, w, w, xp, xs, g, wr, br)


ROUTE_GROUP_LANE = EXPERTS_PER_GROUP
ROUTE_RANK_LANE = EXPERTS_PER_GROUP + 1
ROUTE_ROWS = 256


def _route_rows(logits, cnt_ref):
    lane = lax.broadcasted_iota(jnp.int32, logits.shape, 1).astype(F32)
    ninf = -jnp.inf

    def first_argmax(v, vmax):
        return jnp.min(jnp.where(v == vmax, lane, float(LANES)), axis=-1, keepdims=True)

    gl = jnp.where(lane < N_GROUPS, logits, ninf)
    gm = jnp.max(gl, axis=-1, keepdims=True)
    g_idx = first_argmax(gl, gm)
    g_w = 1.0 / jnp.sum(jnp.exp(gl - gm), axis=-1, keepdims=True)

    lo = N_GROUPS + EXPERTS_PER_GROUP * g_idx
    el = jnp.where(jnp.logical_and(lane >= lo, lane < lo + EXPERTS_PER_GROUP), logits, ninf)
    v1 = jnp.max(el, axis=-1, keepdims=True)
    i1 = first_argmax(el, v1)
    el2 = jnp.where(lane == i1, ninf, el)
    v2 = jnp.max(el2, axis=-1, keepdims=True)
    i2 = first_argmax(el2, v2)
    e2 = jnp.exp(v2 - v1)
    w1 = (1.0 / (1.0 + e2)) * g_w
    w2 = (e2 / (1.0 + e2)) * g_w
    gates = jnp.where(lane == i1 - lo, w1, 0.0) + jnp.where(lane == i2 - lo, w2, 0.0)

    rows = logits.shape[0]
    onehot = jnp.where(lane == g_idx, 1.0, 0.0)
    r_i = lax.broadcasted_iota(jnp.int32, (rows, rows), 0)
    c_i = lax.broadcasted_iota(jnp.int32, (rows, rows), 1)
    tri = jnp.where(c_i < r_i, 1.0, 0.0).astype(BF16)
    before = jnp.dot(tri, onehot.astype(BF16), preferred_element_type=F32) + cnt_ref[...]
    rank = jnp.sum(onehot * before, axis=-1, keepdims=True)
    cnt_ref[...] += jnp.sum(onehot, axis=0, keepdims=True)
    return (gates + jnp.where(lane == ROUTE_GROUP_LANE, g_idx, 0.0)
            + jnp.where(lane == ROUTE_RANK_LANE, rank, 0.0))


PLAN_CHUNK = 1024


def _plan_items(per, n_used, n_items_max, grp, p_ref, e_ref, key_ref, n_ref, run_ref):
    tstart_s, tend_s, tiles_s, before_s, next_s = grp
    n_items = per * n_used
    n_ref[0] = n_items

    def item(w, carry):
        valid = w < n_items
        wc = jnp.minimum(w, n_items - 1)
        gi = jnp.int32(0)
        for g in range(N_GROUPS):
            gi = gi + (wc >= per * tend_s[g]).astype(jnp.int32)
        local = wc - per * tstart_s[gi]
        tiles_w = tiles_s[gi]
        lpart = jnp.int32(0)
        for k in range(1, per):
            lpart = lpart + (local >= k * tiles_w).astype(jnp.int32)
        ltile = local - lpart * tiles_w
        spare = jnp.maximum(w - n_items, 0)
        part = jnp.where(valid, lpart, spare % per)
        p_ref[w] = jnp.where(valid, tstart_s[gi] + ltile, n_used + spare // per)
        e_ref[w] = part
        key = gi * per + part
        key_ref[w] = key
        nxt_g = next_s[gi]
        run_ref[3 * w] = jnp.logical_and(valid, ltile == 0).astype(jnp.int32)
        run_ref[3 * w + 1] = (per * before_s[gi] + part) % 2
        run_ref[3 * w + 2] = jnp.where(part + 1 < per, key + 1,
                                       jnp.where(nxt_g >= 0, nxt_g * per, -1))
        return carry
    lax.fori_loop(0, n_items_max, item, 0)


def _plan_kernel(group_ref, rank_ref, cnt_ref, dest_ref, src_ref, nused_ref,
                 up_p, up_e, up_key, up_n, up_run, dn_p, dn_e, dn_key, dn_n, dn_run,
                 tstart_s, tend_s, tiles_s, before_s, next_s, row0_s,
                 *, tile, n_tiles_max, up_parts, dn_parts):
    s = pl.program_id(0)

    @pl.when(s == 0)
    def _():
        start = jnp.int32(0)
        owned = jnp.int32(0)
        for g in range(N_GROUPS):
            tiles = (cnt_ref[g] + tile - 1) // tile
            tstart_s[g] = start
            row0_s[g] = start * tile
            tiles_s[g] = tiles
            before_s[g] = owned
            start = start + tiles
            tend_s[g] = start
            owned = owned + (tiles > 0).astype(jnp.int32)
        nxt = jnp.int32(-1)
        for g in reversed(range(N_GROUPS)):
            next_s[g] = nxt
            nxt = jnp.where(tiles_s[g] > 0, g, nxt)
        nused_ref[0] = start

        def clear(r, carry):
            src_ref[r] = 0
            return carry
        lax.fori_loop(0, n_tiles_max * tile, clear, 0, unroll=8)

        grp = (tstart_s, tend_s, tiles_s, before_s, next_s)
        _plan_items(up_parts, start, up_parts * n_tiles_max, grp, up_p, up_e, up_key, up_n, up_run)
        _plan_items(dn_parts, start, dn_parts * n_tiles_max, grp, dn_p, dn_e, dn_key, dn_n, dn_run)

    def place(i, carry):
        t = s * PLAN_CHUNK + i
        d = row0_s[group_ref[i]] + rank_ref[i]
        dest_ref[t] = d
        src_ref[d] = t
        return carry
    lax.fori_loop(0, PLAN_CHUNK, place, 0, unroll=8)


def _moe_plan(group, rank, counts, *, tile, n_tiles_max, up_parts, dn_parts):
    t = group.shape[0]
    i32 = jnp.int32
    smem = functools.partial(pl.BlockSpec, memory_space=pltpu.SMEM)
    n_up, n_dn = up_parts * n_tiles_max, dn_parts * n_tiles_max
    shapes = [(t,), (n_tiles_max * tile,), (1,),
              (n_up,), (n_up,), (n_up,), (1,), (3 * n_up,),
              (n_dn,), (n_dn,), (n_dn,), (1,), (3 * n_dn,)]
    outs = pl.pallas_call(
        functools.partial(_plan_kernel, tile=tile, n_tiles_max=n_tiles_max,
                          up_parts=up_parts, dn_parts=dn_parts),
        grid=(t // PLAN_CHUNK,),
        in_specs=[smem((PLAN_CHUNK,), lambda s: (s,)), smem((PLAN_CHUNK,), lambda s: (s,)),
                  smem()],
        out_specs=[smem() for _ in shapes],
        out_shape=[jax.ShapeDtypeStruct(sh, i32) for sh in shapes],
        scratch_shapes=[pltpu.SMEM((N_GROUPS,), i32) for _ in range(6)],
        compiler_params=_params(1),
        name="moe_plan",
    )(group, rank, counts)
    dest, src, n_used = outs[:3]
    return dict(dest=dest, src=src, n_used=n_used, up=tuple(outs[3:8]), down=tuple(outs[8:13]))


def _row_copy(src_hbm, dst_vmem, src_row, dst_row, sem):
    return pltpu.make_async_copy(src_hbm.at[pl.ds(src_row, 1)], dst_vmem.at[pl.ds(dst_row, 1)], sem)


def _start_row_gather(idx_ref, base, n_rows, pairs):
    def body(r, carry):
        i = idx_ref[base + r]
        for src_hbm, dst_vmem, sem in pairs:
            _row_copy(src_hbm, dst_vmem, i, r, sem).start()
        return carry
    lax.fori_loop(0, n_rows, body, 0, unroll=8)


def _start_rows(idx_ref, idx_base, row0, n_rows, pairs):
    for k in range(n_rows):
        i = idx_ref[idx_base + row0 + k]
        for src_hbm, dst_vmem, sem in pairs:
            _row_copy(src_hbm, dst_vmem, i, row0 + k, sem).start()


def _norm_tile_and_prefetch(load_rows, g_ref, o_ref, n_rows, has_next, idx_ref, next_base,
                            next_pairs):
    @pl.when(has_next)
    def _():
        _rmsnorm_rows(load_rows, g_ref, o_ref, n_rows,
                      per_trip=lambda row0: _start_rows(idx_ref, next_base, row0, NORM_ROWS,
                                                        next_pairs))

    @pl.when(jnp.logical_not(has_next))
    def _():
        _rmsnorm_rows(load_rows, g_ref, o_ref, n_rows)


def _wait_row_gather(n_rows, pairs):
    for src_hbm, dst_vmem, sem in pairs:
        pltpu.make_async_copy(src_hbm.at[pl.ds(0, n_rows)], dst_vmem, sem).wait()


def _dispatch_kernel(src_ref, nused_ref, x2_hbm, route_hbm, g_ref, xs_ref, gs_ref,
                     xbuf, gbuf, sem):
    p = pl.program_id(0)
    n_used = nused_ref[0]
    tile = xbuf.shape[1]
    slot = p % 2

    def pairs(s):
        return [(x2_hbm, xbuf.at[s], sem.at[0, s]), (route_hbm, gbuf.at[s], sem.at[1, s])]

    @pl.when(p == 0)
    def _():
        _start_row_gather(src_ref, 0, tile, pairs(0))

    @pl.when(p < n_used)
    def _():
        _wait_row_gather(tile, pairs(slot))
        gs_ref[...] = gbuf[slot]
        _norm_tile_and_prefetch(lambda rows: xbuf[slot, rows, :], g_ref, xs_ref, tile,
                                p + 1 < n_used, src_ref, (p + 1) * tile, pairs(1 - slot))

    @pl.when(p >= n_used)
    def _():
        xs_ref[...] = jnp.zeros_like(xs_ref)
        gs_ref[...] = jnp.zeros_like(gs_ref)


def _dispatch(x2, route, g, src, n_used, *, tile, n_tiles_max):
    t, d = x2.shape
    tile_map = lambda p, src_ref, n_ref: (p, 0)
    grid_spec = pltpu.PrefetchScalarGridSpec(
        num_scalar_prefetch=2,
        grid=(n_tiles_max,),
        in_specs=[
            pl.BlockSpec(memory_space=pl.ANY),
            pl.BlockSpec(memory_space=pl.ANY),
            pl.BlockSpec((1, d), lambda p, s, n: (0, 0)),
        ],
        out_specs=[
            pl.BlockSpec((tile, d), tile_map),
            pl.BlockSpec((tile, LANES), tile_map),
        ],
        scratch_shapes=[
            pltpu.VMEM((2, tile, d), F32),
            pltpu.VMEM((2, tile, LANES), F32),
            pltpu.SemaphoreType.DMA((2, 2)),
        ],
    )
    return pl.pallas_call(
        _dispatch_kernel,
        grid_spec=grid_spec,
        out_shape=[
            jax.ShapeDtypeStruct((n_tiles_max * tile, d), BF16),
            jax.ShapeDtypeStruct((n_tiles_max * tile, LANES), F32),
        ],
        compiler_params=_params(1),
        name="moe_dispatch",
    )(src, n_used, x2, route, g)


def _stream_run_weights(w, n_items, key_ref, run_ref, copies):
    is_first = run_ref[3 * w]
    slot = run_ref[3 * w + 1]
    nxt_key = run_ref[3 * w + 2]
    valid = w < n_items

    @pl.when(jnp.logical_and(valid, w == 0))
    def _():
        for cp in copies(key_ref[0], 0):
            cp.start()

    @pl.when(jnp.logical_and(valid, is_first == 1))
    def _():
        for cp in copies(key_ref[w], slot):
            cp.wait()

        @pl.when(nxt_key >= 0)
        def _():
            for cp in copies(nxt_key, 1 - slot):
                cp.start()

    return slot


def _moe_up_kernel(ip_ref, ie_ref, key_ref, n_ref, run_ref, xs_ref, gs_ref, w1_hbm, w3_hbm,
                   hid_ref, w1_buf, w3_buf, sem):
    w = pl.program_id(0)

    n_e = w1_buf.shape[1]
    f = w1_buf.shape[3]

    def copies(key, slot):
        experts = pl.ds(key * n_e, n_e)
        return [pltpu.make_async_copy(w1_hbm.at[experts], w1_buf.at[slot], sem.at[0, slot]),
                pltpu.make_async_copy(w3_hbm.at[experts], w3_buf.at[slot], sem.at[1, slot])]

    slot = _stream_run_weights(w, n_ref[0], key_ref, run_ref, copies)

    @pl.when(w < n_ref[0])
    def _():
        x = xs_ref[...]
        lane = lax.broadcasted_iota(jnp.int32, gs_ref.shape, 1)
        for e in range(n_e):
            a = jnp.dot(x, w1_buf[slot, e].astype(BF16), preferred_element_type=F32)
            b = jnp.dot(x, w3_buf[slot, e].astype(BF16), preferred_element_type=F32)
            in_group = ie_ref[w] * n_e + e
            gate = jnp.sum(jnp.where(lane == in_group, gs_ref[...], 0.0), axis=-1, keepdims=True)
            hid_ref[:, e * f:(e + 1) * f] = (jax.nn.silu(a) * b * gate).astype(hid_ref.dtype)

    @pl.when(w >= n_ref[0])
    def _():
        hid_ref[...] = jnp.zeros_like(hid_ref)


def _moe_up(xs, gs, w1, w3, items, *, tile):
    rows, d = xs.shape
    f = w1.shape[2]
    ip, ie, key, n_items, run = items
    grid_spec = pltpu.PrefetchScalarGridSpec(
        num_scalar_prefetch=5,
        grid=(ip.shape[0],),
        in_specs=[
            pl.BlockSpec((tile, d), lambda w, ip, ie, key, n, run: (ip[w], 0)),
            pl.BlockSpec((tile, LANES), lambda w, ip, ie, key, n, run: (ip[w], 0)),
            pl.BlockSpec(memory_space=pl.ANY),
            pl.BlockSpec(memory_space=pl.ANY),
        ],
        out_specs=pl.BlockSpec((tile, MOE_UP_EXPERTS * f),
                               lambda w, ip, ie, key, n, run: (ip[w], ie[w])),
        scratch_shapes=[
            pltpu.VMEM((2, MOE_UP_EXPERTS, d, f), w1.dtype),
            pltpu.VMEM((2, MOE_UP_EXPERTS, d, f), w3.dtype),
            pltpu.SemaphoreType.DMA((2, 2)),
        ],
    )
    return pl.pallas_call(
        _moe_up_kernel,
        grid_spec=grid_spec,
        out_shape=jax.ShapeDtypeStruct((rows, EXPERTS_PER_GROUP * f), BF16),
        compiler_params=_params(1),
        name="moe_up",
    )(ip, ie, key, n_items, run, xs, gs, w1, w3)


def _moe_down_kernel(ip_ref, ih_ref, key_ref, n_ref, run_ref, hid_ref, w2_hbm, y_ref,
                     w2_buf, sem, *, n_halves):
    w = pl.program_id(0)
    tn = w2_buf.shape[2]

    def copies(key, slot):
        g = key // n_halves
        col = pl.multiple_of((key % n_halves) * tn, tn)
        return [pltpu.make_async_copy(w2_hbm.at[g, :, pl.ds(col, tn)], w2_buf.at[slot],
                                      sem.at[slot])]

    slot = _stream_run_weights(w, n_ref[0], key_ref, run_ref, copies)

    @pl.when(w < n_ref[0])
    def _():
        y_ref[...] = jnp.dot(hid_ref[...], w2_buf[slot].astype(BF16),
                             preferred_element_type=F32)

    @pl.when(w >= n_ref[0])
    def _():
        y_ref[...] = jnp.zeros_like(y_ref)


def _moe_down(hid, w2g, items, *, tile, n_halves):
    rows, k = hid.shape
    d = w2g.shape[2]
    tn = d // n_halves
    ip, ih, key, n_items, run = items
    grid_spec = pltpu.PrefetchScalarGridSpec(
        num_scalar_prefetch=5,
        grid=(ip.shape[0],),
        in_specs=[
            pl.BlockSpec((tile, k), lambda w, ip, ih, key, n, run: (ip[w], 0)),
            pl.BlockSpec(memory_space=pl.ANY),
        ],
        out_specs=pl.BlockSpec((tile, tn), lambda w, ip, ih, key, n, run: (ip[w], ih[w])),
        scratch_shapes=[
            pltpu.VMEM((2, k, tn), w2g.dtype),
            pltpu.SemaphoreType.DMA((2,)),
        ],
    )
    return pl.pallas_call(
        functools.partial(_moe_down_kernel, n_halves=n_halves),
        grid_spec=grid_spec,
        out_shape=jax.ShapeDtypeStruct((rows, d), F32),
        compiler_params=_params(1),
        name="moe_down",
    )(ip, ih, key, n_items, run, hid, w2g)


def _final_kernel(dest_ref, x_ref, ys_hbm, g_ref, o_ref, ybuf, sem, *, tile0):
    i = pl.program_id(0)
    tm = ybuf.shape[1]
    slot = i % 2

    def pairs(s):
        return [(ys_hbm, ybuf.at[s], sem.at[s])]

    @pl.when(i == 0)
    def _():
        _start_row_gather(dest_ref, tile0 * tm, tm, pairs(0))

    _wait_row_gather(tm, pairs(slot))
    _norm_tile_and_prefetch(lambda rows: x_ref[rows, :] + ybuf[slot, rows, :], g_ref, o_ref, tm,
                            i + 1 < pl.num_programs(0), dest_ref, (tile0 + i + 1) * tm,
                            pairs(1 - slot))


def _final(x2, ys, dest, g, *, row0, n_rows, tm):
    d = x2.shape[1]
    off = row0 // tm
    grid_spec = pltpu.PrefetchScalarGridSpec(
        num_scalar_prefetch=1,
        grid=(n_rows // tm,),
        in_specs=[
            pl.BlockSpec((tm, d), lambda i, dest: (i + off, 0)),
            pl.BlockSpec(memory_space=pl.ANY),
            pl.BlockSpec((1, d), lambda i, dest: (0, 0)),
        ],
        out_specs=pl.BlockSpec((tm, d), lambda i, dest: (i, 0)),
        scratch_shapes=[pltpu.VMEM((2, tm, d), F32), pltpu.SemaphoreType.DMA((2,))],
    )
    return pl.pallas_call(
        functools.partial(_final_kernel, tile0=off),
        grid_spec=grid_spec,
        out_shape=jax.ShapeDtypeStruct((n_rows, d), F32),
        compiler_params=_params(1),
        name="final_norm",
    )(dest, x2, ys, g)


def _rope_tables(pos):
    half = ROT_DIM // 2
    inv = ROPE_THETA ** (-np.arange(half, dtype=np.float64) / half)
    ang = np.asarray(pos, np.float64)[:, None] * inv[None, :]
    cos, sin = np.cos(ang), np.sin(ang)
    ones = np.ones((ang.shape[0], HEAD_DIM - ROT_DIM))
    zeros = np.zeros((ang.shape[0], HEAD_DIM - half))
    c = np.concatenate([cos, cos, ones], axis=1)
    s_hi = np.concatenate([-sin, zeros], axis=1)
    s_lo = np.concatenate([np.zeros_like(sin), sin, 0.0 * ones], axis=1)
    rep = LANES // HEAD_DIM
    return tuple(jnp.asarray(np.tile(t, (1, rep)), F32) for t in (c, s_hi, s_lo))


def kernel(x_prompt, x_sample, cache_k, cache_v, state_conv, state_h, norm_mix, w_in, conv_w,
           conv_b, w_gate_a, b_gate_a, w_gate_x, b_gate_x, lru_lambda, sinks, norm_attn_out,
           norm_lru_out, w_out, norm_ffn, w_group, b_group, w_expert_router, b_expert_router,
           w1, w3, w2, norm_final):
    batch, seq, d_model = x_prompt.shape
    dec_batch, dec_seq, _ = x_sample.shape
    depth = w_in.shape[0]
    assert depth == 1 and dec_seq == CHUNK and seq % CHUNK == 0
    n_heads = sinks.shape[1]
    att_width = n_heads * HEAD_DIM
    n_kv = cache_k.shape[3]
    kv_cols = n_kv * HEAD_DIM
    lru_width = lru_lambda.shape[1]
    cw = cache_k.shape[2]
    assert cw == WIN_CHUNKS * CHUNK
    n_p, n_s = batch * seq, dec_batch * dec_seq
    chunks_per_seq = seq // CHUNK
    n_prompt_chunks = n_p // CHUNK

    xp = x_prompt.reshape(n_p, d_model)
    xs = x_sample.reshape(n_s, d_model)

    w_r = w_in[0]
    tn = 2 * kv_cols
    assert tn == 512 and att_width % tn == 0 and lru_width % tn == 0
    n_q_tiles = att_width // tn
    kv_tile = n_q_tiles

    tm_in = PROJ_ROWS
    assert seq % tm_in == 0 and n_s % tm_in == 0 and tm_in % dec_seq == 0
    pos = np.concatenate([np.arange(seq), np.tile(PAST_LEN + np.arange(dec_seq), tm_in // dec_seq)])
    c_tab, shi_tab, slo_tab = _rope_tables(pos)

    z = _in_proj(xp, xs, norm_mix, w_r, c_tab, shi_tab, slo_tab, tm=tm_in, tn=tn,
                 n_q_tiles=n_q_tiles, kv_tile=kv_tile, prompt_tiles_per_seq=seq // tm_in)

    k_col = att_width + 2 * lru_width
    v_col = k_col + kv_cols
    att = _attention(
        z, cache_k[0].reshape(dec_batch, cw, kv_cols), cache_v[0].reshape(dec_batch, cw, kv_cols),
        sinks[0], norm_attn_out,
        n_prompt_chunks=n_prompt_chunks, chunks_per_seq=chunks_per_seq, att_width=att_width,
        n_kv=n_kv, k_col_blk=k_col // kv_cols, v_col_blk=v_col // kv_cols)

    lru, h_tiles = _rglru(
        z, state_conv[0], state_h[0].reshape(dec_batch, 1, lru_width), conv_w[0], conv_b,
        w_gate_a[0].astype(BF16), b_gate_a[0].reshape(1, lru_width),
        w_gate_x[0].astype(BF16), b_gate_x[0].reshape(1, lru_width),
        lru_lambda, norm_lru_out,
        n_prompt_chunks=n_prompt_chunks, chunks_per_seq=chunks_per_seq, width=lru_width,
        xb_col_blk=att_width // lru_width, yb_col_blk=att_width // lru_width + 1)

    n_routes = N_GROUPS + N_GROUPS * EXPERTS_PER_GROUP
    wr = jnp.concatenate([w_group[0], w_expert_router[0],
                          jnp.zeros((d_model, LANES - n_routes), F32)], axis=1).astype(BF16)
    br = jnp.concatenate([b_group[0], b_expert_router[0],
                          jnp.zeros((LANES - n_routes,), F32)]).reshape(1, LANES)
    x2, route, counts, meta = _out_proj(att, lru, w_out[0], xp, xs, norm_ffn, wr, br,
                                        tm=PROJ_ROWS, tn=OUT_PROJ_COLS)

    n_tiles_max = (n_p + n_s) // MOE_TILE + N_GROUPS
    tabs = _moe_plan(meta[0], meta[1], counts[0, :N_GROUPS].astype(jnp.int32), tile=MOE_TILE,
                     n_tiles_max=n_tiles_max, up_parts=EXPERTS_PER_GROUP // MOE_UP_EXPERTS,
                     dn_parts=MOE_DOWN_HALVES)
    xsort, gsort = _dispatch(x2, route, norm_ffn, tabs["src"], tabs["n_used"],
                             tile=MOE_TILE, n_tiles_max=n_tiles_max)
    hid = _moe_up(xsort, gsort, w1[0], w3[0], tabs["up"], tile=MOE_TILE)
    d_expert = w2.shape[2]
    w2g = w2[0].reshape(N_GROUPS, EXPERTS_PER_GROUP * d_expert, d_model)
    ysort = _moe_down(hid, w2g, tabs["down"], tile=MOE_TILE, n_halves=MOE_DOWN_HALVES)

    g_fin = norm_final.reshape(1, d_model)
    y_prompt = _final(x2, ysort, tabs["dest"], g_fin, row0=0, n_rows=n_p,
                      tm=FINAL_ROWS).reshape(batch, seq, d_model)
    y_sample = _final(x2, ysort, tabs["dest"], g_fin, row0=n_p, n_rows=n_s,
                      tm=FINAL_ROWS).reshape(dec_batch, dec_seq, d_model)

    win = min(WIN_CHUNKS * CHUNK, seq)

    def prompt_tail(col, width, rows):
        return jnp.stack([lax.slice(z, ((b + 1) * seq - rows, col), ((b + 1) * seq, col + width))
                          for b in range(batch)])

    def sample_rows(col, width):
        return lax.slice(z, (n_p, col), (n_p + n_s, col + width)).reshape(dec_batch, dec_seq, width)

    k_prompt = prompt_tail(k_col, kv_cols, win).reshape(1, batch, win, n_kv, HEAD_DIM)
    v_prompt = prompt_tail(v_col, kv_cols, win).reshape(1, batch, win, n_kv, HEAD_DIM)
    conv_prompt = prompt_tail(att_width, lru_width, CONV_WIDTH - 1)[None]
    h_prompt = h_tiles[:n_prompt_chunks, 0].reshape(batch, chunks_per_seq, lru_width)[:, -1][None]

    ks = sample_rows(k_col, kv_cols).reshape(dec_batch, dec_seq, n_kv, HEAD_DIM)
    vs = sample_rows(v_col, kv_cols).reshape(dec_batch, dec_seq, n_kv, HEAD_DIM)
    k_sample = jnp.concatenate([cache_k[0], ks], axis=1)[:, -cw:][None]
    v_sample = jnp.concatenate([cache_v[0], vs], axis=1)[:, -cw:][None]
    xs_rows = sample_rows(att_width, lru_width)[:, dec_seq - (CONV_WIDTH - 1):]
    conv_sample = jnp.concatenate([state_conv[0], xs_rows], axis=1)[:, -(CONV_WIDTH - 1):][None]
    h_sample = h_tiles[n_prompt_chunks:, 0][None]

    return (y_prompt, y_sample, k_prompt, v_prompt, conv_prompt, h_prompt,
            k_sample, v_sample, conv_sample, h_sample)
```

```python
import functools

import numpy as np
import jax
import jax.numpy as jnp
from jax import lax
from jax.experimental import pallas as pl
from jax.experimental.pallas import tpu as pltpu

F32 = jnp.float32
BF16 = jnp.bfloat16

CHUNK = 64
HEAD_DIM = 64
KV_GROUP = 8
WIN_CHUNKS = 2
ROT_DIM = 16
ROPE_THETA = 500000.0
LRU_BLOCK = 128
CONV_WIDTH = 4
LRU_C = 8.0
N_GROUPS = 8
EXPERTS_PER_GROUP = 4
EPS = 1e-6
NEG = -1e30
PAST_LEN = 1024
LANES = 128
PROJ_ROWS = 1024
OUT_PROJ_COLS = 512
INPROJ_ROW_BLOCK = 128
MOE_TILE = 256
FINAL_ROWS = 256
MOE_UP_EXPERTS = 2
MOE_DOWN_HALVES = 1
VMEM_LIMIT = 56 * 1024 * 1024


def _params(n_axes, vmem=VMEM_LIMIT):
    return pltpu.CompilerParams(
        dimension_semantics=("arbitrary",) * n_axes, vmem_limit_bytes=vmem)


def _rms_scale(x):
    return lax.rsqrt(jnp.mean(x * x, axis=-1, keepdims=True) + EPS)


NORM_ROWS = 16


def _rmsnorm_rows(load_rows, g_ref, o_ref, n_rows, per_trip=None):
    def body(r, carry):
        row0 = r * NORM_ROWS if isinstance(r, int) else pl.multiple_of(r * NORM_ROWS, NORM_ROWS)
        rows = pl.ds(row0, NORM_ROWS)
        x = load_rows(rows)
        o_ref[rows, :] = (x * _rms_scale(x) * g_ref[...]).astype(o_ref.dtype)
        if per_trip is not None:
            per_trip(row0)
        return carry
    if per_trip is not None:
        for r in range(n_rows // NORM_ROWS):
            body(r, 0)
    else:
        lax.fori_loop(0, n_rows // NORM_ROWS, body, 0, unroll=4)


def _rope_block(zb, c, s_hi, s_lo):
    return (zb * c + pltpu.roll(zb, LANES - ROT_DIM // 2, 1) * s_hi
            + pltpu.roll(zb, ROT_DIM // 2, 1) * s_lo)


def _inproj_kernel(xp_hbm, xs_hbm, g_ref, w_ref, c_ref, shi_ref, slo_ref, o_ref, x_buf, h_ref,
                   sem, *, n_prompt_tiles, n_q_tiles, kv_tile):
    i = pl.program_id(0)
    j = pl.program_id(1)
    tm, tn = o_ref.shape

    def x_tile_copy(src_hbm, tile):
        return pltpu.make_async_copy(src_hbm.at[pl.ds(tile * tm, tm)], x_buf, sem)

    def start_x(tile):
        @pl.when(tile < n_prompt_tiles)
        def _():
            x_tile_copy(xp_hbm, tile).start()

        @pl.when(tile >= n_prompt_tiles)
        def _():
            x_tile_copy(xs_hbm, tile - n_prompt_tiles).start()

    @pl.when(jnp.logical_and(i == 0, j == 0))
    def _():
        start_x(0)

    @pl.when(j == 0)
    def _():
        x_tile_copy(xp_hbm, 0).wait()
        _rmsnorm_rows(lambda rows: x_buf[rows, :], g_ref, h_ref, tm)

    @pl.when(jnp.logical_and(j == 1, i + 1 < pl.num_programs(0)))
    def _():
        start_x(i + 1)

    w = w_ref[...].astype(BF16)
    n_blk = tn // LANES
    for r0 in range(0, tm, INPROJ_ROW_BLOCK):
        rows = slice(r0, r0 + INPROJ_ROW_BLOCK)
        z = jnp.dot(h_ref[rows, :], w, preferred_element_type=F32)
        c, shi, slo = c_ref[rows, :], shi_ref[rows, :], slo_ref[rows, :]
        for b in range(n_blk):
            is_rope = j < n_q_tiles
            if b < n_blk // 2:
                is_rope = jnp.logical_or(is_rope, j == kv_tile)
            zb = z[:, b * LANES:(b + 1) * LANES]
            o_ref[rows, b * LANES:(b + 1) * LANES] = _rope_block(
                zb, jnp.where(is_rope, c, 1.0), jnp.where(is_rope, shi, 0.0),
                jnp.where(is_rope, slo, 0.0))


def _in_proj(xp, xs, g, w, c, shi, slo, *, tm, tn, n_q_tiles, kv_tile, prompt_tiles_per_seq):
    n_p, d = xp.shape
    n_s = xs.shape[0]
    t = n_p + n_s
    npt = n_p // tm
    n_cols = w.shape[1]
    kern = functools.partial(_inproj_kernel, n_prompt_tiles=npt, n_q_tiles=n_q_tiles,
                             kv_tile=kv_tile)
    n_tiles = n_cols // tn
    assert kv_tile == n_q_tiles

    def table_tile(i, j):
        return jnp.where(i < npt, i % prompt_tiles_per_seq, prompt_tiles_per_seq), 0

    def out_tile(i, j):
        return i, jnp.where(j < kv_tile, j, jnp.where(j == kv_tile, n_tiles - 1, j - 1))

    return pl.pallas_call(
        kern,
        grid=(t // tm, n_tiles),
        in_specs=[
            pl.BlockSpec(memory_space=pl.ANY),
            pl.BlockSpec(memory_space=pl.ANY),
            pl.BlockSpec((1, d), lambda i, j: (0, 0)),
            pl.BlockSpec((d, tn), lambda i, j: (0, j)),
            pl.BlockSpec((tm, LANES), table_tile),
            pl.BlockSpec((tm, LANES), table_tile),
            pl.BlockSpec((tm, LANES), table_tile),
        ],
        out_specs=pl.BlockSpec((tm, tn), out_tile),
        out_shape=jax.ShapeDtypeStruct((t, n_cols), F32),
        scratch_shapes=[pltpu.VMEM((tm, d), xp.dtype), pltpu.VMEM((tm, d), BF16),
                        pltpu.SemaphoreType.DMA(())],
        compiler_params=_params(2),
        name="in_proj",
    )(xp, xs, g, w, c, shi, slo)


ATTN_CHUNKS = 8


def _attn_chunk(q, k_band, v_band, lo, sink_ref, n_kv):
    n_keys = k_band.shape[0]
    key = lax.broadcasted_iota(jnp.int32, (n_keys, KV_GROUP * CHUNK), 0)
    valid = key >= lo
    outs = []
    for gi in range(n_kv):
        kg = k_band[:, gi * HEAD_DIM:(gi + 1) * HEAD_DIM]
        vg = v_band[:, gi * HEAD_DIM:(gi + 1) * HEAD_DIM]
        heads = [gi * KV_GROUP + h for h in range(KV_GROUP)]
        qg = jnp.concatenate([q[:, h * HEAD_DIM:(h + 1) * HEAD_DIM] for h in heads], axis=0)
        st = lax.dot_general(kg, qg, (((1,), (1,)), ((), ())), preferred_element_type=F32)
        st = jnp.where(valid, st, NEG)
        sink = sink_ref[gi:gi + 1, :]
        m = jnp.maximum(jnp.max(st, axis=0, keepdims=True), sink)
        p = jnp.exp(st - m)
        denom = jnp.sum(p, axis=0, keepdims=True) + jnp.exp(sink - m)
        pn = (p * (1.0 / denom)).astype(BF16)
        og = lax.dot_general(pn, vg, (((0,), (0,)), ((), ())), preferred_element_type=F32)
        outs.extend(og[h * CHUNK:(h + 1) * CHUNK] for h in range(KV_GROUP))
    return jnp.concatenate(outs, axis=1)


def _attn_kernel(q_ref, kp_ref, k_ref, vp_ref, v_ref, kc_ref, vc_ref, sink_ref, g_ref, o_ref,
                 *, n_prompt_tiles, tiles_per_seq, n_kv):
    n = pl.program_id(0)
    is_s = n >= n_prompt_tiles
    has_prev = jnp.logical_or(is_s, n % tiles_per_seq > 0)

    def band(prev_ref, own_ref, cache_ref, c):
        prev, own, cache = prev_ref[...], own_ref[...], cache_ref[c]
        hist = jnp.concatenate([prev, own], axis=0)[c * CHUNK:(c + WIN_CHUNKS) * CHUNK]
        hist = jnp.where(is_s, cache, hist)
        return jnp.concatenate([hist, own[c * CHUNK:(c + 1) * CHUNK]], axis=0).astype(BF16)

    q = (q_ref[...] * (HEAD_DIM ** -0.5)).astype(BF16)
    for c in range(ATTN_CHUNKS):
        lo = jnp.where(has_prev, 0, max(WIN_CHUNKS - c, 0) * CHUNK)
        att = _attn_chunk(q[c * CHUNK:(c + 1) * CHUNK], band(kp_ref, k_ref, kc_ref, c),
                          band(vp_ref, v_ref, vc_ref, c), lo, sink_ref, n_kv)
        o_ref[c * CHUNK:(c + 1) * CHUNK, :] = (
            att * _rms_scale(att) * g_ref[...]).astype(o_ref.dtype)


def _attention(z, cache_k, cache_v, sinks, g, *, n_prompt_chunks, chunks_per_seq,
               att_width, n_kv, k_col_blk, v_col_blk):
    t = z.shape[0]
    rows = ATTN_CHUNKS * CHUNK
    assert ATTN_CHUNKS % WIN_CHUNKS == 0 and chunks_per_seq % ATTN_CHUNKS == 0
    hist_rows = WIN_CHUNKS * CHUNK
    hist_per_tile = rows // hist_rows
    assert cache_k.shape[0] % ATTN_CHUNKS == 0
    n_tiles = t // rows
    npt = n_prompt_chunks // ATTN_CHUNKS
    tps = chunks_per_seq // ATTN_CHUNKS
    kvw = n_kv * HEAD_DIM

    def prev(col):
        return lambda n: (n * hist_per_tile - jnp.minimum(1, n % tps), col)

    cache_map = lambda n: (jnp.maximum(n - npt, 0), 0, 0)
    kern = functools.partial(_attn_kernel, n_prompt_tiles=npt, tiles_per_seq=tps, n_kv=n_kv)
    sink_rows = jnp.repeat(sinks.reshape(n_kv, KV_GROUP), CHUNK, axis=1)
    return pl.pallas_call(
        kern,
        grid=(n_tiles,),
        in_specs=[
            pl.BlockSpec((rows, att_width), lambda n: (n, 0)),
            pl.BlockSpec((hist_rows, kvw), prev(k_col_blk)),
            pl.BlockSpec((rows, kvw), lambda n: (n, k_col_blk)),
            pl.BlockSpec((hist_rows, kvw), prev(v_col_blk)),
            pl.BlockSpec((rows, kvw), lambda n: (n, v_col_blk)),
            pl.BlockSpec((ATTN_CHUNKS, WIN_CHUNKS * CHUNK, kvw), cache_map),
            pl.BlockSpec((ATTN_CHUNKS, WIN_CHUNKS * CHUNK, kvw), cache_map),
            pl.BlockSpec((n_kv, KV_GROUP * CHUNK), lambda n: (0, 0)),
            pl.BlockSpec((1, att_width), lambda n: (0, 0)),
        ],
        out_specs=pl.BlockSpec((rows, att_width), lambda n: (n, 0)),
        out_shape=jax.ShapeDtypeStruct((t, att_width), BF16),
        compiler_params=_params(1),
        name="attention",
    )(z, z, z, z, z, cache_k, cache_v, sink_rows, g)


SUBLANES = 8


def _lru_kernel(xb_ref, yb_ref, sconv_ref, sh_ref, cw_ref, cb_ref, wa_ref, ba_ref,
                wx_ref, bx_ref, lam_ref, g_ref, o_ref, hl_ref, hist_ref, hcar_ref,
                a_sc, b_sc, h_sc, *, n_prompt_chunks, chunks_per_seq):
    n = pl.program_id(0)
    is_s = n >= n_prompt_chunks
    hist_rows = hist_ref.shape[0]

    @pl.when(jnp.logical_and(jnp.logical_not(is_s), n % chunks_per_seq == 0))
    def _():
        hist_ref[...] = jnp.zeros_like(hist_ref)
        hcar_ref[...] = jnp.zeros_like(hcar_ref)

    @pl.when(is_s)
    def _():
        hist_ref[...] = jnp.zeros_like(hist_ref)
        hist_ref[hist_rows - (CONV_WIDTH - 1):, :] = sconv_ref[0]
        hcar_ref[...] = sh_ref[0]

    x = xb_ref[...]
    rows, width = x.shape
    hist = hist_ref[...]
    row8 = lax.broadcasted_iota(jnp.int32, (hist_rows, width), 0)
    xc = cb_ref[...]
    for j in range(CONV_WIDTH):
        s = CONV_WIDTH - 1 - j
        if s == 0:
            xs = x
        else:
            xr = pltpu.roll(x, s, 0)
            head = jnp.where(row8 < s, pltpu.roll(hist, s, 0), xr[:hist_rows])
            xs = jnp.concatenate([head, xr[hist_rows:]], axis=0)
        xc = xc + xs * cw_ref[j:j + 1, :]
    hist_ref[...] = x[rows - hist_rows:]

    xcb = xc.astype(BF16)
    ra, rx = [], []
    for nb in range(width // LRU_BLOCK):
        blk = xcb[:, nb * LRU_BLOCK:(nb + 1) * LRU_BLOCK]
        ra.append(jnp.dot(blk, wa_ref[nb], preferred_element_type=F32))
        rx.append(jnp.dot(blk, wx_ref[nb], preferred_element_type=F32))
    r = jax.nn.sigmoid(jnp.concatenate(ra, axis=1) + ba_ref[...])
    ig = jax.nn.sigmoid(jnp.concatenate(rx, axis=1) + bx_ref[...])
    log_a = r * (-LRU_C * jax.nn.softplus(-lam_ref[...]))
    a = jnp.exp(log_a)
    v = -jnp.tanh(log_a) * (1.0 + a * a)
    b = jnp.where(v == 0.0, 0.0, v * lax.rsqrt(v)) * ig * xc

    seg_len = rows // SUBLANES
    h_blocks, carries = [], []
    for nb in range(width // LANES):
        lanes = slice(nb * LANES, (nb + 1) * LANES)
        a_sc[nb] = a[:, lanes]
        b_sc[nb] = b[:, lanes]
        hloc, ploc = [], []
        for k in range(seg_len):
            a_k = a_sc[nb, pl.ds(k, SUBLANES, stride=seg_len), :]
            b_k = b_sc[nb, pl.ds(k, SUBLANES, stride=seg_len), :]
            hloc.append(b_k if k == 0 else a_k * hloc[-1] + b_k)
            ploc.append(a_k if k == 0 else a_k * ploc[-1])
        carry = hcar_ref[:, lanes]
        seg_in = []
        for s in range(SUBLANES):
            seg_in.append(carry)
            carry = ploc[-1][s:s + 1] * carry + hloc[-1][s:s + 1]
        seg_in = jnp.concatenate(seg_in, axis=0)
        carries.append(carry)
        for k in range(seg_len):
            h_sc[nb, pl.ds(k, SUBLANES, stride=seg_len), :] = hloc[k] + ploc[k] * seg_in
        h_blocks.append(h_sc[nb])
    h = jnp.concatenate(h_blocks, axis=1)
    h_last = jnp.concatenate(carries, axis=1)
    hcar_ref[...] = h_last
    hl_ref[0] = h_last

    y = jax.nn.gelu(yb_ref[...]) * h
    o_ref[...] = (y * _rms_scale(y) * g_ref[...]).astype(o_ref.dtype)


def _rglru(z, state_conv, state_h, conv_w, conv_b, wa, ba, wx, bx, lam, g,
           *, n_prompt_chunks, chunks_per_seq, width, xb_col_blk, yb_col_blk):
    t = z.shape[0]
    n_items = t // CHUNK
    npc = n_prompt_chunks
    nblk = width // LRU_BLOCK
    state_map = lambda n: (jnp.maximum(n - npc, 0), 0, 0)
    full2 = lambda n: (0, 0)
    full3 = lambda n: (0, 0, 0)
    kern = functools.partial(_lru_kernel, n_prompt_chunks=npc, chunks_per_seq=chunks_per_seq)
    return pl.pallas_call(
        kern,
        grid=(n_items,),
        in_specs=[
            pl.BlockSpec((CHUNK, width), lambda n: (n, xb_col_blk)),
            pl.BlockSpec((CHUNK, width), lambda n: (n, yb_col_blk)),
            pl.BlockSpec((1, CONV_WIDTH - 1, width), state_map),
            pl.BlockSpec((1, 1, width), state_map),
            pl.BlockSpec((CONV_WIDTH, width), full2),
            pl.BlockSpec((1, width), full2),
            pl.BlockSpec((nblk, LRU_BLOCK, LRU_BLOCK), full3),
            pl.BlockSpec((1, width), full2),
            pl.BlockSpec((nblk, LRU_BLOCK, LRU_BLOCK), full3),
            pl.BlockSpec((1, width), full2),
            pl.BlockSpec((1, width), full2),
            pl.BlockSpec((1, width), full2),
        ],
        out_specs=[
            pl.BlockSpec((CHUNK, width), lambda n: (n, 0)),
            pl.BlockSpec((1, 1, width), lambda n: (n, 0, 0)),
        ],
        out_shape=[
            jax.ShapeDtypeStruct((t, width), BF16),
            jax.ShapeDtypeStruct((n_items, 1, width), F32),
        ],
        scratch_shapes=[
            pltpu.VMEM((SUBLANES, width), F32),
            pltpu.VMEM((1, width), F32),
            pltpu.VMEM((width // LANES, CHUNK, LANES), F32),
            pltpu.VMEM((width // LANES, CHUNK, LANES), F32),
            pltpu.VMEM((width // LANES, CHUNK, LANES), F32),
        ],
        compiler_params=_params(1),
        name="rglru",
    )(z, z, state_conv, state_h, conv_w, conv_b, wa, ba, wx, bx, lam, g)


def _outproj_kernel(att_ref, lru_ref, wa_ref, wl_ref, xp_ref, xs_ref, g_ref, wr_ref, br_ref,
                    o_ref, route_ref, cnt_ref, meta_ref, ssq_ref, lg_ref,
                    *, n_prompt_tiles, d_model):
    i = pl.program_id(0)
    j = pl.program_id(1)
    acc = jnp.dot(att_ref[...], wa_ref[...].astype(BF16), preferred_element_type=F32)
    acc = acc + jnp.dot(lru_ref[...], wl_ref[...].astype(BF16), preferred_element_type=F32)
    x2 = jnp.where(i < n_prompt_tiles, xp_ref[...], xs_ref[...]) + acc
    o_ref[...] = x2

    @pl.when(jnp.logical_and(i == 0, j == 0))
    def _():
        cnt_ref[...] = jnp.zeros_like(cnt_ref)

    @pl.when(j == 0)
    def _():
        ssq_ref[...] = jnp.zeros_like(ssq_ref)
        lg_ref[...] = jnp.zeros_like(lg_ref)

    ssq_ref[...] += jnp.sum(x2 * x2, axis=-1, keepdims=True)
    lg_ref[...] += jnp.dot((x2 * g_ref[...]).astype(BF16), wr_ref[...],
                           preferred_element_type=F32)

    @pl.when(j == pl.num_programs(1) - 1)
    def _():
        scale = lax.rsqrt(ssq_ref[...] * (1.0 / d_model) + EPS)
        logits = lg_ref[...] * scale + br_ref[...]
        for r0 in range(0, logits.shape[0], ROUTE_ROWS):
            route = _route_rows(logits[r0:r0 + ROUTE_ROWS], cnt_ref)
            route_ref[r0:r0 + ROUTE_ROWS, :] = route
            meta_ref[:, r0:r0 + ROUTE_ROWS] = route.T[
                ROUTE_GROUP_LANE:ROUTE_RANK_LANE + 1, :].astype(jnp.int32)


def _out_proj(att, lru, w, xp, xs, g, wr, br, *, tm, tn):
    t, aw = att.shape
    lw = lru.shape[1]
    d = w.shape[1]
    assert aw == lw and w.shape[0] == aw + lw
    npt = xp.shape[0] // tm
    kern = functools.partial(_outproj_kernel, n_prompt_tiles=npt, d_model=d)
    return pl.pallas_call(
        kern,
        grid=(t // tm, d // tn),
        in_specs=[
            pl.BlockSpec((tm, aw), lambda i, j: (i, 0)),
            pl.BlockSpec((tm, lw), lambda i, j: (i, 0)),
            pl.BlockSpec((aw, tn), lambda i, j: (0, j)),
            pl.BlockSpec((lw, tn), lambda i, j: (1, j)),
            pl.BlockSpec((tm, tn), lambda i, j: (jnp.minimum(i, npt - 1),
                                                 jnp.where(i < npt, j, 0))),
            pl.BlockSpec((tm, tn), lambda i, j: (jnp.maximum(i - npt, 0),
                                                 jnp.where(i < npt, 0, j))),
            pl.BlockSpec((1, tn), lambda i, j: (0, j)),
            pl.BlockSpec((tn, LANES), lambda i, j: (j, 0)),
            pl.BlockSpec((1, LANES), lambda i, j: (0, 0)),
        ],
        out_specs=[
            pl.BlockSpec((tm, tn), lambda i, j: (i, j)),
            pl.BlockSpec((tm, LANES), lambda i, j: (i, 0)),
            pl.BlockSpec((1, LANES), lambda i, j: (0, 0)),
            pl.BlockSpec((2, tm), lambda i, j: (0, i)),
        ],
        out_shape=[
            jax.ShapeDtypeStruct((t, d), F32),
            jax.ShapeDtypeStruct((t, LANES), F32),
            jax.ShapeDtypeStruct((1, LANES), F32),
            jax.ShapeDtypeStruct((2, t), jnp.int32),
        ],
        scratch_shapes=[pltpu.VMEM((tm, 1), F32), pltpu.VMEM((tm, LANES), F32)],
        compiler_params=_params(2),
        name="out_proj",
    )(att, lru, w, w, xp, xs, g, wr, br)


ROUTE_GROUP_LANE = EXPERTS_PER_GROUP
ROUTE_RANK_LANE = EXPERTS_PER_GROUP + 1
ROUTE_ROWS = 256


def _route_rows(logits, cnt_ref):
    lane = lax.broadcasted_iota(jnp.int32, logits.shape, 1).astype(F32)
    ninf = -jnp.inf

    def first_argmax(v, vmax):
        return jnp.min(jnp.where(v == vmax, lane, float(LANES)), axis=-1, keepdims=True)

    gl = jnp.where(lane < N_GROUPS, logits, ninf)
    gm = jnp.max(gl, axis=-1, keepdims=True)
    g_idx = first_argmax(gl, gm)
    g_w = 1.0 / jnp.sum(jnp.exp(gl - gm), axis=-1, keepdims=True)

    lo = N_GROUPS + EXPERTS_PER_GROUP * g_idx
    el = jnp.where(jnp.logical_and(lane >= lo, lane < lo + EXPERTS_PER_GROUP), logits, ninf)
    v1 = jnp.max(el, axis=-1, keepdims=True)
    i1 = first_argmax(el, v1)
    el2 = jnp.where(lane == i1, ninf, el)
    v2 = jnp.max(el2, axis=-1, keepdims=True)
    i2 = first_argmax(el2, v2)
    e2 = jnp.exp(v2 - v1)
    w1 = (1.0 / (1.0 + e2)) * g_w
    w2 = (e2 / (1.0 + e2)) * g_w
    gates = jnp.where(lane == i1 - lo, w1, 0.0) + jnp.where(lane == i2 - lo, w2, 0.0)

    rows = logits.shape[0]
    onehot = jnp.where(lane == g_idx, 1.0, 0.0)
    r_i = lax.broadcasted_iota(jnp.int32, (rows, rows), 0)
    c_i = lax.broadcasted_iota(jnp.int32, (rows, rows), 1)
    tri = jnp.where(c_i < r_i, 1.0, 0.0).astype(BF16)
    before = jnp.dot(tri, onehot.astype(BF16), preferred_element_type=F32) + cnt_ref[...]
    rank = jnp.sum(onehot * before, axis=-1, keepdims=True)
    cnt_ref[...] += jnp.sum(onehot, axis=0, keepdims=True)
    return (gates + jnp.where(lane == ROUTE_GROUP_LANE, g_idx, 0.0)
            + jnp.where(lane == ROUTE_RANK_LANE, rank, 0.0))


PLAN_CHUNK = 1024


def _plan_items(per, n_used, n_items_max, grp, p_ref, e_ref, key_ref, n_ref, run_ref):
    tstart_s, tend_s, tiles_s, before_s, next_s = grp
    n_items = per * n_used
    n_ref[0] = n_items

    def item(w, carry):
        valid = w < n_items
        wc = jnp.minimum(w, n_items - 1)
        gi = jnp.int32(0)
        for g in range(N_GROUPS):
            gi = gi + (wc >= per * tend_s[g]).astype(jnp.int32)
        local = wc - per * tstart_s[gi]
        tiles_w = tiles_s[gi]
        lpart = jnp.int32(0)
        for k in range(1, per):
            lpart = lpart + (local >= k * tiles_w).astype(jnp.int32)
        ltile = local - lpart * tiles_w
        spare = jnp.maximum(w - n_items, 0)
        part = jnp.where(valid, lpart, spare % per)
        p_ref[w] = jnp.where(valid, tstart_s[gi] + ltile, n_used + spare // per)
        e_ref[w] = part
        key = gi * per + part
        key_ref[w] = key
        nxt_g = next_s[gi]
        run_ref[3 * w] = jnp.logical_and(valid, ltile == 0).astype(jnp.int32)
        run_ref[3 * w + 1] = (per * before_s[gi] + part) % 2
        run_ref[3 * w + 2] = jnp.where(part + 1 < per, key + 1,
                                       jnp.where(nxt_g >= 0, nxt_g * per, -1))
        return carry
    lax.fori_loop(0, n_items_max, item, 0)


def _plan_kernel(group_ref, rank_ref, cnt_ref, dest_ref, src_ref, nused_ref,
                 up_p, up_e, up_key, up_n, up_run, dn_p, dn_e, dn_key, dn_n, dn_run,
                 tstart_s, tend_s, tiles_s, before_s, next_s, row0_s,
                 *, tile, n_tiles_max, up_parts, dn_parts):
    s = pl.program_id(0)

    @pl.when(s == 0)
    def _():
        start = jnp.int32(0)
        owned = jnp.int32(0)
        for g in range(N_GROUPS):
            tiles = (cnt_ref[g] + tile - 1) // tile
            tstart_s[g] = start
            row0_s[g] = start * tile
            tiles_s[g] = tiles
            before_s[g] = owned
            start = start + tiles
            tend_s[g] = start
            owned = owned + (tiles > 0).astype(jnp.int32)
        nxt = jnp.int32(-1)
        for g in reversed(range(N_GROUPS)):
            next_s[g] = nxt
            nxt = jnp.where(tiles_s[g] > 0, g, nxt)
        nused_ref[0] = start

        def clear(r, carry):
            src_ref[r] = 0
            return carry
        lax.fori_loop(0, n_tiles_max * tile, clear, 0, unroll=8)

        grp = (tstart_s, tend_s, tiles_s, before_s, next_s)
        _plan_items(up_parts, start, up_parts * n_tiles_max, grp, up_p, up_e, up_key, up_n, up_run)
        _plan_items(dn_parts, start, dn_parts * n_tiles_max, grp, dn_p, dn_e, dn_key, dn_n, dn_run)

    def place(i, carry):
        t = s * PLAN_CHUNK + i
        d = row0_s[group_ref[i]] + rank_ref[i]
        dest_ref[t] = d
        src_ref[d] = t
        return carry
    lax.fori_loop(0, PLAN_CHUNK, place, 0, unroll=8)


def _moe_plan(group, rank, counts, *, tile, n_tiles_max, up_parts, dn_parts):
    t = group.shape[0]
    i32 = jnp.int32
    smem = functools.partial(pl.BlockSpec, memory_space=pltpu.SMEM)
    n_up, n_dn = up_parts * n_tiles_max, dn_parts * n_tiles_max
    shapes = [(t,), (n_tiles_max * tile,), (1,),
              (n_up,), (n_up,), (n_up,), (1,), (3 * n_up,),
              (n_dn,), (n_dn,), (n_dn,), (1,), (3 * n_dn,)]
    outs = pl.pallas_call(
        functools.partial(_plan_kernel, tile=tile, n_tiles_max=n_tiles_max,
                          up_parts=up_parts, dn_parts=dn_parts),
        grid=(t // PLAN_CHUNK,),
        in_specs=[smem((PLAN_CHUNK,), lambda s: (s,)), smem((PLAN_CHUNK,), lambda s: (s,)),
                  smem()],
        out_specs=[smem() for _ in shapes],
        out_shape=[jax.ShapeDtypeStruct(sh, i32) for sh in shapes],
        scratch_shapes=[pltpu.SMEM((N_GROUPS,), i32) for _ in range(6)],
        compiler_params=_params(1),
        name="moe_plan",
    )(group, rank, counts)
    dest, src, n_used = outs[:3]
    return dict(dest=dest, src=src, n_used=n_used, up=tuple(outs[3:8]), down=tuple(outs[8:13]))


def _row_copy(src_hbm, dst_vmem, src_row, dst_row, sem):
    return pltpu.make_async_copy(src_hbm.at[pl.ds(src_row, 1)], dst_vmem.at[pl.ds(dst_row, 1)], sem)


def _start_row_gather(idx_ref, base, n_rows, pairs):
    def body(r, carry):
        i = idx_ref[base + r]
        for src_hbm, dst_vmem, sem in pairs:
            _row_copy(src_hbm, dst_vmem, i, r, sem).start()
        return carry
    lax.fori_loop(0, n_rows, body, 0, unroll=8)


def _start_rows(idx_ref, idx_base, row0, n_rows, pairs):
    for k in range(n_rows):
        i = idx_ref[idx_base + row0 + k]
        for src_hbm, dst_vmem, sem in pairs:
            _row_copy(src_hbm, dst_vmem, i, row0 + k, sem).start()


def _norm_tile_and_prefetch(load_rows, g_ref, o_ref, n_rows, has_next, idx_ref, next_base,
                            next_pairs):
    @pl.when(has_next)
    def _():
        _rmsnorm_rows(load_rows, g_ref, o_ref, n_rows,
                      per_trip=lambda row0: _start_rows(idx_ref, next_base, row0, NORM_ROWS,
                                                        next_pairs))

    @pl.when(jnp.logical_not(has_next))
    def _():
        _rmsnorm_rows(load_rows, g_ref, o_ref, n_rows)


def _wait_row_gather(n_rows, pairs):
    for src_hbm, dst_vmem, sem in pairs:
        pltpu.make_async_copy(src_hbm.at[pl.ds(0, n_rows)], dst_vmem, sem).wait()


def _dispatch_kernel(src_ref, nused_ref, x2_hbm, route_hbm, g_ref, xs_ref, gs_ref,
                     xbuf, gbuf, sem):
    p = pl.program_id(0)
    n_used = nused_ref[0]
    tile = xbuf.shape[1]
    slot = p % 2

    def pairs(s):
        return [(x2_hbm, xbuf.at[s], sem.at[0, s]), (route_hbm, gbuf.at[s], sem.at[1, s])]

    @pl.when(p == 0)
    def _():
        _start_row_gather(src_ref, 0, tile, pairs(0))

    @pl.when(p < n_used)
    def _():
        _wait_row_gather(tile, pairs(slot))
        gs_ref[...] = gbuf[slot]
        _norm_tile_and_prefetch(lambda rows: xbuf[slot, rows, :], g_ref, xs_ref, tile,
                                p + 1 < n_used, src_ref, (p + 1) * tile, pairs(1 - slot))

    @pl.when(p >= n_used)
    def _():
        xs_ref[...] = jnp.zeros_like(xs_ref)
        gs_ref[...] = jnp.zeros_like(gs_ref)


def _dispatch(x2, route, g, src, n_used, *, tile, n_tiles_max):
    t, d = x2.shape
    tile_map = lambda p, src_ref, n_ref: (p, 0)
    grid_spec = pltpu.PrefetchScalarGridSpec(
        num_scalar_prefetch=2,
        grid=(n_tiles_max,),
        in_specs=[
            pl.BlockSpec(memory_space=pl.ANY),
            pl.BlockSpec(memory_space=pl.ANY),
            pl.BlockSpec((1, d), lambda p, s, n: (0, 0)),
        ],
        out_specs=[
            pl.BlockSpec((tile, d), tile_map),
            pl.BlockSpec((tile, LANES), tile_map),
        ],
        scratch_shapes=[
            pltpu.VMEM((2, tile, d), F32),
            pltpu.VMEM((2, tile, LANES), F32),
            pltpu.SemaphoreType.DMA((2, 2)),
        ],
    )
    return pl.pallas_call(
        _dispatch_kernel,
        grid_spec=grid_spec,
        out_shape=[
            jax.ShapeDtypeStruct((n_tiles_max * tile, d), BF16),
            jax.ShapeDtypeStruct((n_tiles_max * tile, LANES), F32),
        ],
        compiler_params=_params(1),
        name="moe_dispatch",
    )(src, n_used, x2, route, g)


def _stream_run_weights(w, n_items, key_ref, run_ref, copies):
    is_first = run_ref[3 * w]
    slot = run_ref[3 * w + 1]
    nxt_key = run_ref[3 * w + 2]
    valid = w < n_items

    @pl.when(jnp.logical_and(valid, w == 0))
    def _():
        for cp in copies(key_ref[0], 0):
            cp.start()

    @pl.when(jnp.logical_and(valid, is_first == 1))
    def _():
        for cp in copies(key_ref[w], slot):
            cp.wait()

        @pl.when(nxt_key >= 0)
        def _():
            for cp in copies(nxt_key, 1 - slot):
                cp.start()

    return slot


def _moe_up_kernel(ip_ref, ie_ref, key_ref, n_ref, run_ref, xs_ref, gs_ref, w1_hbm, w3_hbm,
                   hid_ref, w1_buf, w3_buf, sem):
    w = pl.program_id(0)

    n_e = w1_buf.shape[1]
    f = w1_buf.shape[3]

    def copies(key, slot):
        experts = pl.ds(key * n_e, n_e)
        return [pltpu.make_async_copy(w1_hbm.at[experts], w1_buf.at[slot], sem.at[0, slot]),
                pltpu.make_async_copy(w3_hbm.at[experts], w3_buf.at[slot], sem.at[1, slot])]

    slot = _stream_run_weights(w, n_ref[0], key_ref, run_ref, copies)

    @pl.when(w < n_ref[0])
    def _():
        x = xs_ref[...]
        lane = lax.broadcasted_iota(jnp.int32, gs_ref.shape, 1)
        for e in range(n_e):
            a = jnp.dot(x, w1_buf[slot, e].astype(BF16), preferred_element_type=F32)
            b = jnp.dot(x, w3_buf[slot, e].astype(BF16), preferred_element_type=F32)
            in_group = ie_ref[w] * n_e + e
            gate = jnp.sum(jnp.where(lane == in_group, gs_ref[...], 0.0), axis=-1, keepdims=True)
            hid_ref[:, e * f:(e + 1) * f] = (jax.nn.silu(a) * b * gate).astype(hid_ref.dtype)

    @pl.when(w >= n_ref[0])
    def _():
        hid_ref[...] = jnp.zeros_like(hid_ref)


def _moe_up(xs, gs, w1, w3, items, *, tile):
    rows, d = xs.shape
    f = w1.shape[2]
    ip, ie, key, n_items, run = items
    grid_spec = pltpu.PrefetchScalarGridSpec(
        num_scalar_prefetch=5,
        grid=(ip.shape[0],),
        in_specs=[
            pl.BlockSpec((tile, d), lambda w, ip, ie, key, n, run: (ip[w], 0)),
            pl.BlockSpec((tile, LANES), lambda w, ip, ie, key, n, run: (ip[w], 0)),
            pl.BlockSpec(memory_space=pl.ANY),
            pl.BlockSpec(memory_space=pl.ANY),
        ],
        out_specs=pl.BlockSpec((tile, MOE_UP_EXPERTS * f),
                               lambda w, ip, ie, key, n, run: (ip[w], ie[w])),
        scratch_shapes=[
            pltpu.VMEM((2, MOE_UP_EXPERTS, d, f), w1.dtype),
            pltpu.VMEM((2, MOE_UP_EXPERTS, d, f), w3.dtype),
            pltpu.SemaphoreType.DMA((2, 2)),
        ],
    )
    return pl.pallas_call(
        _moe_up_kernel,
        grid_spec=grid_spec,
        out_shape=jax.ShapeDtypeStruct((rows, EXPERTS_PER_GROUP * f), BF16),
        compiler_params=_params(1),
        name="moe_up",
    )(ip, ie, key, n_items, run, xs, gs, w1, w3)


def _moe_down_kernel(ip_ref, ih_ref, key_ref, n_ref, run_ref, hid_ref, w2_hbm, y_ref,
                     w2_buf, sem, *, n_halves):
    w = pl.program_id(0)
    tn = w2_buf.shape[2]

    def copies(key, slot):
        g = key // n_halves
        col = pl.multiple_of((key % n_halves) * tn, tn)
        return [pltpu.make_async_copy(w2_hbm.at[g, :, pl.ds(col, tn)], w2_buf.at[slot],
                                      sem.at[slot])]

    slot = _stream_run_weights(w, n_ref[0], key_ref, run_ref, copies)

    @pl.when(w < n_ref[0])
    def _():
        y_ref[...] = jnp.dot(hid_ref[...], w2_buf[slot].astype(BF16),
                             preferred_element_type=F32)

    @pl.when(w >= n_ref[0])
    def _():
        y_ref[...] = jnp.zeros_like(y_ref)


def _moe_down(hid, w2g, items, *, tile, n_halves):
    rows, k = hid.shape
    d = w2g.shape[2]
    tn = d // n_halves
    ip, ih, key, n_items, run = items
    grid_spec = pltpu.PrefetchScalarGridSpec(
        num_scalar_prefetch=5,
        grid=(ip.shape[0],),
        in_specs=[
            pl.BlockSpec((tile, k), lambda w, ip, ih, key, n, run: (ip[w], 0)),
            pl.BlockSpec(memory_space=pl.ANY),
        ],
        out_specs=pl.BlockSpec((tile, tn), lambda w, ip, ih, key, n, run: (ip[w], ih[w])),
        scratch_shapes=[
            pltpu.VMEM((2, k, tn), w2g.dtype),
            pltpu.SemaphoreType.DMA((2,)),
        ],
    )
    return pl.pallas_call(
        functools.partial(_moe_down_kernel, n_halves=n_halves),
        grid_spec=grid_spec,
        out_shape=jax.ShapeDtypeStruct((rows, d), F32),
        compiler_params=_params(1),
        name="moe_down",
    )(ip, ih, key, n_items, run, hid, w2g)


def _final_kernel(dest_ref, x_ref, ys_hbm, g_ref, o_ref, ybuf, sem, *, tile0):
    i = pl.program_id(0)
    tm = ybuf.shape[1]
    slot = i % 2

    def pairs(s):
        return [(ys_hbm, ybuf.at[s], sem.at[s])]

    @pl.when(i == 0)
    def _():
        _start_row_gather(dest_ref, tile0 * tm, tm, pairs(0))

    _wait_row_gather(tm, pairs(slot))
    _norm_tile_and_prefetch(lambda rows: x_ref[rows, :] + ybuf[slot, rows, :], g_ref, o_ref, tm,
                            i + 1 < pl.num_programs(0), dest_ref, (tile0 + i + 1) * tm,
                            pairs(1 - slot))


def _final(x2, ys, dest, g, *, row0, n_rows, tm):
    d = x2.shape[1]
    off = row0 // tm
    grid_spec = pltpu.PrefetchScalarGridSpec(
        num_scalar_prefetch=1,
        grid=(n_rows // tm,),
        in_specs=[
            pl.BlockSpec((tm, d), lambda i, dest: (i + off, 0)),
            pl.BlockSpec(memory_space=pl.ANY),
            pl.BlockSpec((1, d), lambda i, dest: (0, 0)),
        ],
        out_specs=pl.BlockSpec((tm, d), lambda i, dest: (i, 0)),
        scratch_shapes=[pltpu.VMEM((2, tm, d), F32), pltpu.SemaphoreType.DMA((2,))],
    )
    return pl.pallas_call(
        functools.partial(_final_kernel, tile0=off),
        grid_spec=grid_spec,
        out_shape=jax.ShapeDtypeStruct((n_rows, d), F32),
        compiler_params=_params(1),
        name="final_norm",
    )(dest, x2, ys, g)


def _rope_tables(pos):
    half = ROT_DIM // 2
    inv = ROPE_THETA ** (-np.arange(half, dtype=np.float64) / half)
    ang = np.asarray(pos, np.float64)[:, None] * inv[None, :]
    cos, sin = np.cos(ang), np.sin(ang)
    ones = np.ones((ang.shape[0], HEAD_DIM - ROT_DIM))
    zeros = np.zeros((ang.shape[0], HEAD_DIM - half))
    c = np.concatenate([cos, cos, ones], axis=1)
    s_hi = np.concatenate([-sin, zeros], axis=1)
    s_lo = np.concatenate([np.zeros_like(sin), sin, 0.0 * ones], axis=1)
    rep = LANES // HEAD_DIM
    return tuple(jnp.asarray(np.tile(t, (1, rep)), F32) for t in (c, s_hi, s_lo))


def kernel(x_prompt, x_sample, cache_k, cache_v, state_conv, state_h, norm_mix, w_in, conv_w,
           conv_b, w_gate_a, b_gate_a, w_gate_x, b_gate_x, lru_lambda, sinks, norm_attn_out,
           norm_lru_out, w_out, norm_ffn, w_group, b_group, w_expert_router, b_expert_router,
           w1, w3, w2, norm_final):
    batch, seq, d_model = x_prompt.shape
    dec_batch, dec_seq, _ = x_sample.shape
    depth = w_in.shape[0]
    assert depth == 1 and dec_seq == CHUNK and seq % CHUNK == 0
    n_heads = sinks.shape[1]
    att_width = n_heads * HEAD_DIM
    n_kv = cache_k.shape[3]
    kv_cols = n_kv * HEAD_DIM
    lru_width = lru_lambda.shape[1]
    cw = cache_k.shape[2]
    assert cw == WIN_CHUNKS * CHUNK
    n_p, n_s = batch * seq, dec_batch * dec_seq
    chunks_per_seq = seq // CHUNK
    n_prompt_chunks = n_p // CHUNK

    xp = x_prompt.reshape(n_p, d_model)
    xs = x_sample.reshape(n_s, d_model)

    w_r = w_in[0]
    tn = 2 * kv_cols
    assert tn == 512 and att_width % tn == 0 and lru_width % tn == 0
    n_q_tiles = att_width // tn
    kv_tile = n_q_tiles

    tm_in = PROJ_ROWS
    assert seq % tm_in == 0 and n_s % tm_in == 0 and tm_in % dec_seq == 0
    pos = np.concatenate([np.arange(seq), np.tile(PAST_LEN + np.arange(dec_seq), tm_in // dec_seq)])
    c_tab, shi_tab, slo_tab = _rope_tables(pos)

    z = _in_proj(xp, xs, norm_mix, w_r, c_tab, shi_tab, slo_tab, tm=tm_in, tn=tn,
                 n_q_tiles=n_q_tiles, kv_tile=kv_tile, prompt_tiles_per_seq=seq // tm_in)

    k_col = att_width + 2 * lru_width
    v_col = k_col + kv_cols
    att = _attention(
        z, cache_k[0].reshape(dec_batch, cw, kv_cols), cache_v[0].reshape(dec_batch, cw, kv_cols),
        sinks[0], norm_attn_out,
        n_prompt_chunks=n_prompt_chunks, chunks_per_seq=chunks_per_seq, att_width=att_width,
        n_kv=n_kv, k_col_blk=k_col // kv_cols, v_col_blk=v_col // kv_cols)

    lru, h_tiles = _rglru(
        z, state_conv[0], state_h[0].reshape(dec_batch, 1, lru_width), conv_w[0], conv_b,
        w_gate_a[0].astype(BF16), b_gate_a[0].reshape(1, lru_width),
        w_gate_x[0].astype(BF16), b_gate_x[0].reshape(1, lru_width),
        lru_lambda, norm_lru_out,
        n_prompt_chunks=n_prompt_chunks, chunks_per_seq=chunks_per_seq, width=lru_width,
        xb_col_blk=att_width // lru_width, yb_col_blk=att_width // lru_width + 1)

    n_routes = N_GROUPS + N_GROUPS * EXPERTS_PER_GROUP
    wr = jnp.concatenate([w_group[0], w_expert_router[0],
                          jnp.zeros((d_model, LANES - n_routes), F32)], axis=1).astype(BF16)
    br = jnp.concatenate([b_group[0], b_expert_router[0],
                          jnp.zeros((LANES - n_routes,), F32)]).reshape(1, LANES)
    x2, route, counts, meta = _out_proj(att, lru, w_out[0], xp, xs, norm_ffn, wr, br,
                                        tm=PROJ_ROWS, tn=OUT_PROJ_COLS)

    n_tiles_max = (n_p + n_s) // MOE_TILE + N_GROUPS
    tabs = _moe_plan(meta[0], meta[1], counts[0, :N_GROUPS].astype(jnp.int32), tile=MOE_TILE,
                     n_tiles_max=n_tiles_max, up_parts=EXPERTS_PER_GROUP // MOE_UP_EXPERTS,
                     dn_parts=MOE_DOWN_HALVES)
    xsort, gsort = _dispatch(x2, route, norm_ffn, tabs["src"], tabs["n_used"],
                             tile=MOE_TILE, n_tiles_max=n_tiles_max)
    hid = _moe_up(xsort, gsort, w1[0], w3[0], tabs["up"], tile=MOE_TILE)
    d_expert = w2.shape[2]
    w2g = w2[0].reshape(N_GROUPS, EXPERTS_PER_GROUP * d_expert, d_model)
    ysort = _moe_down(hid, w2g, tabs["down"], tile=MOE_TILE, n_halves=MOE_DOWN_HALVES)

    g_fin = norm_final.reshape(1, d_model)
    y_prompt = _final(x2, ysort, tabs["dest"], g_fin, row0=0, n_rows=n_p,
                      tm=FINAL_ROWS).reshape(batch, seq, d_model)
    y_sample = _final(x2, ysort, tabs["dest"], g_fin, row0=n_p, n_rows=n_s,
                      tm=FINAL_ROWS).reshape(dec_batch, dec_seq, d_model)

    win = min(WIN_CHUNKS * CHUNK, seq)

    def prompt_tail(col, width, rows):
        return jnp.stack([lax.slice(z, ((b + 1) * seq - rows, col), ((b + 1) * seq, col + width))
                          for b in range(batch)])

    def sample_rows(col, width):
        return lax.slice(z, (n_p, col), (n_p + n_s, col + width)).reshape(dec_batch, dec_seq, width)

    k_prompt = prompt_tail(k_col, kv_cols, win).reshape(1, batch, win, n_kv, HEAD_DIM)
    v_prompt = prompt_tail(v_col, kv_cols, win).reshape(1, batch, win, n_kv, HEAD_DIM)
    conv_prompt = prompt_tail(att_width, lru_width, CONV_WIDTH - 1)[None]
    h_prompt = h_tiles[:n_prompt_chunks, 0].reshape(batch, chunks_per_seq, lru_width)[:, -1][None]

    ks = sample_rows(k_col, kv_cols).reshape(dec_batch, dec_seq, n_kv, HEAD_DIM)
    vs = sample_rows(v_col, kv_cols).reshape(dec_batch, dec_seq, n_kv, HEAD_DIM)
    k_sample = jnp.concatenate([cache_k[0], ks], axis=1)[:, -cw:][None]
    v_sample = jnp.concatenate([cache_v[0], vs], axis=1)[:, -cw:][None]
    xs_rows = sample_rows(att_width, lru_width)[:, dec_seq - (CONV_WIDTH - 1):]
    conv_sample = jnp.concatenate([state_conv[0], xs_rows], axis=1)[:, -(CONV_WIDTH - 1):][None]
    h_sample = h_tiles[n_prompt_chunks:, 0][None]

    return (y_prompt, y_sample, k_prompt, v_prompt, conv_prompt, h_prompt,
            k_sample, v_sample, conv_sample, h_sample)
```

```python
import functools

import numpy as np
import jax
import jax.numpy as jnp
from jax import lax
from jax.experimental import pallas as pl
from jax.experimental.pallas import tpu as pltpu

F32 = jnp.float32
BF16 = jnp.bfloat16

CHUNK = 64
HEAD_DIM = 64
KV_GROUP = 8
WIN_CHUNKS = 2
ROT_DIM = 16
ROPE_THETA = 500000.0
LRU_BLOCK = 128
CONV_WIDTH = 4
LRU_C = 8.0
N_GROUPS = 8
EXPERTS_PER_GROUP = 4
EPS = 1e-6
NEG = -1e30
PAST_LEN = 1024
LANES = 128
PROJ_ROWS = 1024
OUT_PROJ_COLS = 512
INPROJ_ROW_BLOCK = 128
MOE_TILE = 256
FINAL_ROWS = 256
MOE_UP_EXPERTS = 2
MOE_DOWN_HALVES = 1
VMEM_LIMIT = 56 * 1024 * 1024


def _params(n_axes, vmem=VMEM_LIMIT):
    return pltpu.CompilerParams(
        dimension_semantics=("arbitrary",) * n_axes, vmem_limit_bytes=vmem)


def _rms_scale(x):
    return lax.rsqrt(jnp.mean(x * x, axis=-1, keepdims=True) + EPS)


NORM_ROWS = 16


def _rmsnorm_rows(load_rows, g_ref, o_ref, n_rows, per_trip=None):
    def body(r, carry):
        row0 = r * NORM_ROWS if isinstance(r, int) else pl.multiple_of(r * NORM_ROWS, NORM_ROWS)
        rows = pl.ds(row0, NORM_ROWS)
        x = load_rows(rows)
        o_ref[rows, :] = (x * _rms_scale(x) * g_ref[...]).astype(o_ref.dtype)
        if per_trip is not None:
            per_trip(row0)
        return carry
    if per_trip is not None:
        for r in range(n_rows // NORM_ROWS):
            body(r, 0)
    else:
        lax.fori_loop(0, n_rows // NORM_ROWS, body, 0, unroll=4)


def _rope_block(zb, c, s_hi, s_lo):
    return (zb * c + pltpu.roll(zb, LANES - ROT_DIM // 2, 1) * s_hi
            + pltpu.roll(zb, ROT_DIM // 2, 1) * s_lo)


def _inproj_kernel(xp_hbm, xs_hbm, g_ref, w_ref, c_ref, shi_ref, slo_ref, o_ref, x_buf, h_ref,
                   sem, *, n_prompt_tiles, n_q_tiles, kv_tile):
    i = pl.program_id(0)
    j = pl.program_id(1)
    tm, tn = o_ref.shape

    def x_tile_copy(src_hbm, tile):
        return pltpu.make_async_copy(src_hbm.at[pl.ds(tile * tm, tm)], x_buf, sem)

    def start_x(tile):
        @pl.when(tile < n_prompt_tiles)
        def _():
            x_tile_copy(xp_hbm, tile).start()

        @pl.when(tile >= n_prompt_tiles)
        def _():
            x_tile_copy(xs_hbm, tile - n_prompt_tiles).start()

    @pl.when(jnp.logical_and(i == 0, j == 0))
    def _():
        start_x(0)

    @pl.when(j == 0)
    def _():
        x_tile_copy(xp_hbm, 0).wait()
        _rmsnorm_rows(lambda rows: x_buf[rows, :], g_ref, h_ref, tm)

    @pl.when(jnp.logical_and(j == 1, i + 1 < pl.num_programs(0)))
    def _():
        start_x(i + 1)

    w = w_ref[...].astype(BF16)
    n_blk = tn // LANES
    for r0 in range(0, tm, INPROJ_ROW_BLOCK):
        rows = slice(r0, r0 + INPROJ_ROW_BLOCK)
        z = jnp.dot(h_ref[rows, :], w, preferred_element_type=F32)
        c, shi, slo = c_ref[rows, :], shi_ref[rows, :], slo_ref[rows, :]
        for b in range(n_blk):
            is_rope = j < n_q_tiles
            if b < n_blk // 2:
                is_rope = jnp.logical_or(is_rope, j == kv_tile)
            zb = z[:, b * LANES:(b + 1) * LANES]
            o_ref[rows, b * LANES:(b + 1) * LANES] = _rope_block(
                zb, jnp.where(is_rope, c, 1.0), jnp.where(is_rope, shi, 0.0),
                jnp.where(is_rope, slo, 0.0))


def _in_proj(xp, xs, g, w, c, shi, slo, *, tm, tn, n_q_tiles, kv_tile, prompt_tiles_per_seq):
    n_p, d = xp.shape
    n_s = xs.shape[0]
    t = n_p + n_s
    npt = n_p // tm
    n_cols = w.shape[1]
    kern = functools.partial(_inproj_kernel, n_prompt_tiles=npt, n_q_tiles=n_q_tiles,
                             kv_tile=kv_tile)
    n_tiles = n_cols // tn
    assert kv_tile == n_q_tiles

    def table_tile(i, j):
        return jnp.where(i < npt, i % prompt_tiles_per_seq, prompt_tiles_per_seq), 0

    def out_tile(i, j):
        return i, jnp.where(j < kv_tile, j, jnp.where(j == kv_tile, n_tiles - 1, j - 1))

    return pl.pallas_call(
        kern,
        grid=(t // tm, n_tiles),
        in_specs=[
            pl.BlockSpec(memory_space=pl.ANY),
            pl.BlockSpec(memory_space=pl.ANY),
            pl.BlockSpec((1, d), lambda i, j: (0, 0)),
            pl.BlockSpec((d, tn), lambda i, j: (0, j)),
            pl.BlockSpec((tm, LANES), table_tile),
            pl.BlockSpec((tm, LANES), table_tile),
            pl.BlockSpec((tm, LANES), table_tile),
        ],
        out_specs=pl.BlockSpec((tm, tn), out_tile),
        out_shape=jax.ShapeDtypeStruct((t, n_cols), F32),
        scratch_shapes=[pltpu.VMEM((tm, d), xp.dtype), pltpu.VMEM((tm, d), BF16),
                        pltpu.SemaphoreType.DMA(())],
        compiler_params=_params(2),
        name="in_proj",
    )(xp, xs, g, w, c, shi, slo)


ATTN_CHUNKS = 8


def _attn_chunk(q, k_band, v_band, lo, sink_ref, n_kv):
    n_keys = k_band.shape[0]
    key = lax.broadcasted_iota(jnp.int32, (n_keys, KV_GROUP * CHUNK), 0)
    valid = key >= lo
    outs = []
    for gi in range(n_kv):
        kg = k_band[:, gi * HEAD_DIM:(gi + 1) * HEAD_DIM]
        vg = v_band[:, gi * HEAD_DIM:(gi + 1) * HEAD_DIM]
        heads = [gi * KV_GROUP + h for h in range(KV_GROUP)]
        qg = jnp.concatenate([q[:, h * HEAD_DIM:(h + 1) * HEAD_DIM] for h in heads], axis=0)
        st = lax.dot_general(kg, qg, (((1,), (1,)), ((), ())), preferred_element_type=F32)
        st = jnp.where(valid, st, NEG)
        sink = sink_ref[gi:gi + 1, :]
        m = jnp.maximum(jnp.max(st, axis=0, keepdims=True), sink)
        p = jnp.exp(st - m)
        denom = jnp.sum(p, axis=0, keepdims=True) + jnp.exp(sink - m)
        pn = (p * (1.0 / denom)).astype(BF16)
        og = lax.dot_general(pn, vg, (((0,), (0,)), ((), ())), preferred_element_type=F32)
        outs.extend(og[h * CHUNK:(h + 1) * CHUNK] for h in range(KV_GROUP))
    return jnp.concatenate(outs, axis=1)


def _attn_kernel(q_ref, kp_ref, k_ref, vp_ref, v_ref, kc_ref, vc_ref, sink_ref, g_ref, o_ref,
                 *, n_prompt_tiles, tiles_per_seq, n_kv):
    n = pl.program_id(0)
    is_s = n >= n_prompt_tiles
    has_prev = jnp.logical_or(is_s, n % tiles_per_seq > 0)

    def band(prev_ref, own_ref, cache_ref, c):
        prev, own, cache = prev_ref[...], own_ref[...], cache_ref[c]
        hist = jnp.concatenate([prev, own], axis=0)[c * CHUNK:(c + WIN_CHUNKS) * CHUNK]
        hist = jnp.where(is_s, cache, hist)
        return jnp.concatenate([hist, own[c * CHUNK:(c + 1) * CHUNK]], axis=0).astype(BF16)

    q = (q_ref[...] * (HEAD_DIM ** -0.5)).astype(BF16)
    for c in range(ATTN_CHUNKS):
        lo = jnp.where(has_prev, 0, max(WIN_CHUNKS - c, 0) * CHUNK)
        att = _attn_chunk(q[c * CHUNK:(c + 1) * CHUNK], band(kp_ref, k_ref, kc_ref, c),
                          band(vp_ref, v_ref, vc_ref, c), lo, sink_ref, n_kv)
        o_ref[c * CHUNK:(c + 1) * CHUNK, :] = (
            att * _rms_scale(att) * g_ref[...]).astype(o_ref.dtype)


def _attention(z, cache_k, cache_v, sinks, g, *, n_prompt_chunks, chunks_per_seq,
               att_width, n_kv, k_col_blk, v_col_blk):
    t = z.shape[0]
    rows = ATTN_CHUNKS * CHUNK
    assert ATTN_CHUNKS % WIN_CHUNKS == 0 and chunks_per_seq % ATTN_CHUNKS == 0
    hist_rows = WIN_CHUNKS * CHUNK
    hist_per_tile = rows // hist_rows
    assert cache_k.shape[0] % ATTN_CHUNKS == 0
    n_tiles = t // rows
    npt = n_prompt_chunks // ATTN_CHUNKS
    tps = chunks_per_seq // ATTN_CHUNKS
    kvw = n_kv * HEAD_DIM

    def prev(col):
        return lambda n: (n * hist_per_tile - jnp.minimum(1, n % tps), col)

    cache_map = lambda n: (jnp.maximum(n - npt, 0), 0, 0)
    kern = functools.partial(_attn_kernel, n_prompt_tiles=npt, tiles_per_seq=tps, n_kv=n_kv)
    sink_rows = jnp.repeat(sinks.reshape(n_kv, KV_GROUP), CHUNK, axis=1)
    return pl.pallas_call(
        kern,
        grid=(n_tiles,),
        in_specs=[
            pl.BlockSpec((rows, att_width), lambda n: (n, 0)),
            pl.BlockSpec((hist_rows, kvw), prev(k_col_blk)),
            pl.BlockSpec((rows, kvw), lambda n: (n, k_col_blk)),
            pl.BlockSpec((hist_rows, kvw), prev(v_col_blk)),
            pl.BlockSpec((rows, kvw), lambda n: (n, v_col_blk)),
            pl.BlockSpec((ATTN_CHUNKS, WIN_CHUNKS * CHUNK, kvw), cache_map),
            pl.BlockSpec((ATTN_CHUNKS, WIN_CHUNKS * CHUNK, kvw), cache_map),
            pl.BlockSpec((n_kv, KV_GROUP * CHUNK), lambda n: (0, 0)),
            pl.BlockSpec((1, att_width), lambda n: (0, 0)),
        ],
        out_specs=pl.BlockSpec((rows, att_width), lambda n: (n, 0)),
        out_shape=jax.ShapeDtypeStruct((t, att_width), BF16),
        compiler_params=_params(1),
        name="attention",
    )(z, z, z, z, z, cache_k, cache_v, sink_rows, g)


SUBLANES = 8


def _lru_kernel(xb_ref, yb_ref, sconv_ref, sh_ref, cw_ref, cb_ref, wa_ref, ba_ref,
                wx_ref, bx_ref, lam_ref, g_ref, o_ref, hl_ref, hist_ref, hcar_ref,
                a_sc, b_sc, h_sc, *, n_prompt_chunks, chunks_per_seq):
    n = pl.program_id(0)
    is_s = n >= n_prompt_chunks
    hist_rows = hist_ref.shape[0]

    @pl.when(jnp.logical_and(jnp.logical_not(is_s), n % chunks_per_seq == 0))
    def _():
        hist_ref[...] = jnp.zeros_like(hist_ref)
        hcar_ref[...] = jnp.zeros_like(hcar_ref)

    @pl.when(is_s)
    def _():
        hist_ref[...] = jnp.zeros_like(hist_ref)
        hist_ref[hist_rows - (CONV_WIDTH - 1):, :] = sconv_ref[0]
        hcar_ref[...] = sh_ref[0]

    x = xb_ref[...]
    rows, width = x.shape
    hist = hist_ref[...]
    row8 = lax.broadcasted_iota(jnp.int32, (hist_rows, width), 0)
    xc = cb_ref[...]
    for j in range(CONV_WIDTH):
        s = CONV_WIDTH - 1 - j
        if s == 0:
            xs = x
        else:
            xr = pltpu.roll(x, s, 0)
            head = jnp.where(row8 < s, pltpu.roll(hist, s, 0), xr[:hist_rows])
            xs = jnp.concatenate([head, xr[hist_rows:]], axis=0)
        xc = xc + xs * cw_ref[j:j + 1, :]
    hist_ref[...] = x[rows - hist_rows:]

    xcb = xc.astype(BF16)
    ra, rx = [], []
    for nb in range(width // LRU_BLOCK):
        blk = xcb[:, nb * LRU_BLOCK:(nb + 1) * LRU_BLOCK]
        ra.append(jnp.dot(blk, wa_ref[nb], preferred_element_type=F32))
        rx.append(jnp.dot(blk, wx_ref[nb], preferred_element_type=F32))
    r = jax.nn.sigmoid(jnp.concatenate(ra, axis=1) + ba_ref[...])
    ig = jax.nn.sigmoid(jnp.concatenate(rx, axis=1) + bx_ref[...])
    log_a = r * (-LRU_C * jax.nn.softplus(-lam_ref[...]))
    a = jnp.exp(log_a)
    v = -jnp.tanh(log_a) * (1.0 + a * a)
    b = jnp.where(v == 0.0, 0.0, v * lax.rsqrt(v)) * ig * xc

    seg_len = rows // SUBLANES
    h_blocks, carries = [], []
    for nb in range(width // LANES):
        lanes = slice(nb * LANES, (nb + 1) * LANES)
        a_sc[nb] = a[:, lanes]
        b_sc[nb] = b[:, lanes]
        hloc, ploc = [], []
        for k in range(seg_len):
            a_k = a_sc[nb, pl.ds(k, SUBLANES, stride=seg_len), :]
            b_k = b_sc[nb, pl.ds(k, SUBLANES, stride=seg_len), :]
            hloc.append(b_k if k == 0 else a_k * hloc[-1] + b_k)
            ploc.append(a_k if k == 0 else a_k * ploc[-1])
        carry = hcar_ref[:, lanes]
        seg_in = []
        for s in range(SUBLANES):
            seg_in.append(carry)
            carry = ploc[-1][s:s + 1] * carry + hloc[-1][s:s + 1]
        seg_in = jnp.concatenate(seg_in, axis=0)
        carries.append(carry)
        for k in range(seg_len):
            h_sc[nb, pl.ds(k, SUBLANES, stride=seg_len), :] = hloc[k] + ploc[k] * seg_in
        h_blocks.append(h_sc[nb])
    h = jnp.concatenate(h_blocks, axis=1)
    h_last = jnp.concatenate(carries, axis=1)
    hcar_ref[...] = h_last
    hl_ref[0] = h_last

    y = jax.nn.gelu(yb_ref[...]) * h
    o_ref[...] = (y * _rms_scale(y) * g_ref[...]).astype(o_ref.dtype)


def _rglru(z, state_conv, state_h, conv_w, conv_b, wa, ba, wx, bx, lam, g,
           *, n_prompt_chunks, chunks_per_seq, width, xb_col_blk, yb_col_blk):
    t = z.shape[0]
    n_items = t // CHUNK
    npc = n_prompt_chunks
    nblk = width // LRU_BLOCK
    state_map = lambda n: (jnp.maximum(n - npc, 0), 0, 0)
    full2 = lambda n: (0, 0)
    full3 = lambda n: (0, 0, 0)
    kern = functools.partial(_lru_kernel, n_prompt_chunks=npc, chunks_per_seq=chunks_per_seq)
    return pl.pallas_call(
        kern,
        grid=(n_items,),
        in_specs=[
            pl.BlockSpec((CHUNK, width), lambda n: (n, xb_col_blk)),
            pl.BlockSpec((CHUNK, width), lambda n: (n, yb_col_blk)),
            pl.BlockSpec((1, CONV_WIDTH - 1, width), state_map),
            pl.BlockSpec((1, 1, width), state_map),
            pl.BlockSpec((CONV_WIDTH, width), full2),
            pl.BlockSpec((1, width), full2),
            pl.BlockSpec((nblk, LRU_BLOCK, LRU_BLOCK), full3),
            pl.BlockSpec((1, width), full2),
            pl.BlockSpec((nblk, LRU_BLOCK, LRU_BLOCK), full3),
            pl.BlockSpec((1, width), full2),
            pl.BlockSpec((1, width), full2),
            pl.BlockSpec((1, width), full2),
        ],
        out_specs=[
            pl.BlockSpec((CHUNK, width), lambda n: (n, 0)),
            pl.BlockSpec((1, 1, width), lambda n: (n, 0, 0)),
        ],
        out_shape=[
            jax.ShapeDtypeStruct((t, width), BF16),
            jax.ShapeDtypeStruct((n_items, 1, width), F32),
        ],
        scratch_shapes=[
            pltpu.VMEM((SUBLANES, width), F32),
            pltpu.VMEM((1, width), F32),
            pltpu.VMEM((width // LANES, CHUNK, LANES), F32),
            pltpu.VMEM((width // LANES, CHUNK, LANES), F32),
            pltpu.VMEM((width // LANES, CHUNK, LANES), F32),
        ],
        compiler_params=_params(1),
        name="rglru",
    )(z, z, state_conv, state_h, conv_w, conv_b, wa, ba, wx, bx, lam, g)


def _outproj_kernel(att_ref, lru_ref, wa_ref, wl_ref, xp_ref, xs_ref, g_ref, wr_ref, br_ref,
                    o_ref, route_ref, cnt_ref, meta_ref, ssq_ref, lg_ref,
                    *, n_prompt_tiles, d_model):
    i = pl.program_id(0)
    j = pl.program_id(1)
    acc = jnp.dot(att_ref[...], wa_ref[...].astype(BF16), preferred_element_type=F32)
    acc = acc + jnp.dot(lru_ref[...], wl_ref[...].astype(BF16), preferred_element_type=F32)
    x2 = jnp.where(i < n_prompt_tiles, xp_ref[...], xs_ref[...]) + acc
    o_ref[...] = x2

    @pl.when(jnp.logical_and(i == 0, j == 0))
    def _():
        cnt_ref[...] = jnp.zeros_like(cnt_ref)

    @pl.when(j == 0)
    def _():
        ssq_ref[...] = jnp.zeros_like(ssq_ref)
        lg_ref[...] = jnp.zeros_like(lg_ref)

    ssq_ref[...] += jnp.sum(x2 * x2, axis=-1, keepdims=True)
    lg_ref[...] += jnp.dot((x2 * g_ref[...]).astype(BF16), wr_ref[...],
                           preferred_element_type=F32)

    @pl.when(j == pl.num_programs(1) - 1)
    def _():
        scale = lax.rsqrt(ssq_ref[...] * (1.0 / d_model) + EPS)
        logits = lg_ref[...] * scale + br_ref[...]
        for r0 in range(0, logits.shape[0], ROUTE_ROWS):
            route = _route_rows(logits[r0:r0 + ROUTE_ROWS], cnt_ref)
            route_ref[r0:r0 + ROUTE_ROWS, :] = route
            meta_ref[:, r0:r0 + ROUTE_ROWS] = route.T[
                ROUTE_GROUP_LANE:ROUTE_RANK_LANE + 1, :].astype(jnp.int32)


def _out_proj(att, lru, w, xp, xs, g, wr, br, *, tm, tn):
    t, aw = att.shape
    lw = lru.shape[1]
    d = w.shape[1]
    assert aw == lw and w.shape[0] == aw + lw
    npt = xp.shape[0] // tm
    kern = functools.partial(_outproj_kernel, n_prompt_tiles=npt, d_model=d)
    return pl.pallas_call(
        kern,
        grid=(t // tm, d // tn),
        in_specs=[
            pl.BlockSpec((tm, aw), lambda i, j: (i, 0)),
            pl.BlockSpec((tm, lw), lambda i, j: (i, 0)),
            pl.BlockSpec((aw, tn), lambda i, j: (0, j)),
            pl.BlockSpec((lw, tn), lambda i, j: (1, j)),
            pl.BlockSpec((tm, tn), lambda i, j: (jnp.minimum(i, npt - 1),
                                                 jnp.where(i < npt, j, 0))),
            pl.BlockSpec((tm, tn), lambda i, j: (jnp.maximum(i - npt, 0),
                                                 jnp.where(i < npt, 0, j))),
            pl.BlockSpec((1, tn), lambda i, j: (0, j)),
            pl.BlockSpec((tn, LANES), lambda i, j: (j, 0)),
            pl.BlockSpec((1, LANES), lambda i, j: (0, 0)),
        ],
        out_specs=[
            pl.BlockSpec((tm, tn), lambda i, j: (i, j)),
            pl.BlockSpec((tm, LANES), lambda i, j: (i, 0)),
            pl.BlockSpec((1, LANES), lambda i, j: (0, 0)),
            pl.BlockSpec((2, tm), lambda i, j: (0, i)),
        ],
        out_shape=[
            jax.ShapeDtypeStruct((t, d), F32),
            jax.ShapeDtypeStruct((t, LANES), F32),
            jax.ShapeDtypeStruct((1, LANES), F32),
            jax.ShapeDtypeStruct((2, t), jnp.int32),
        ],
        scratch_shapes=[pltpu.VMEM((tm, 1), F32), pltpu.VMEM((tm, LANES), F32)],
        compiler_params=_params(2),
        name="out_proj",
    )(att, lru, w, w, xp, xs, g, wr, br)


ROUTE_GROUP_LANE = EXPERTS_PER_GROUP
ROUTE_RANK_LANE = EXPERTS_PER_GROUP + 1
ROUTE_ROWS = 256


def _route_rows(logits, cnt_ref):
    lane = lax.broadcasted_iota(jnp.int32, logits.shape, 1).astype(F32)
    ninf = -jnp.inf

    def first_argmax(v, vmax):
        return jnp.min(jnp.where(v == vmax, lane, float(LANES)), axis=-1, keepdims=True)

    gl = jnp.where(lane < N_GROUPS, logits, ninf)
    gm = jnp.max(gl, axis=-1, keepdims=True)
    g_idx = first_argmax(gl, gm)
    g_w = 1.0 / jnp.sum(jnp.exp(gl - gm), axis=-1, keepdims=True)

    lo = N_GROUPS + EXPERTS_PER_GROUP * g_idx
    el = jnp.where(jnp.logical_and(lane >= lo, lane < lo + EXPERTS_PER_GROUP), logits, ninf)
    v1 = jnp.max(el, axis=-1, keepdims=True)
    i1 = first_argmax(el, v1)
    el2 = jnp.where(lane == i1, ninf, el)
    v2 = jnp.max(el2, axis=-1, keepdims=True)
    i2 = first_argmax(el2, v2)
    e2 = jnp.exp(v2 - v1)
    w1 = (1.0 / (1.0 + e2)) * g_w
    w2 = (e2 / (1.0 + e2)) * g_w
    gates = jnp.where(lane == i1 - lo, w1, 0.0) + jnp.where(lane == i2 - lo, w2, 0.0)

    rows = logits.shape[0]
    onehot = jnp.where(lane == g_idx, 1.0, 0.0)
    r_i = lax.broadcasted_iota(jnp.int32, (rows, rows), 0)
    c_i = lax.broadcasted_iota(jnp.int32, (rows, rows), 1)
    tri = jnp.where(c_i < r_i, 1.0, 0.0).astype(BF16)
    before = jnp.dot(tri, onehot.astype(BF16), preferred_element_type=F32) + cnt_ref[...]
    rank = jnp.sum(onehot * before, axis=-1, keepdims=True)
    cnt_ref[...] += jnp.sum(onehot, axis=0, keepdims=True)
    return (gates + jnp.where(lane == ROUTE_GROUP_LANE, g_idx, 0.0)
            + jnp.where(lane == ROUTE_RANK_LANE, rank, 0.0))


PLAN_CHUNK = 1024


def _plan_items(per, n_used, n_items_max, grp, p_ref, e_ref, key_ref, n_ref, run_ref):
    tstart_s, tend_s, tiles_s, before_s, next_s = grp
    n_items = per * n_used
    n_ref[0] = n_items

    def item(w, carry):
        valid = w < n_items
        wc = jnp.minimum(w, n_items - 1)
        gi = jnp.int32(0)
        for g in range(N_GROUPS):
            gi = gi + (wc >= per * tend_s[g]).astype(jnp.int32)
        local = wc - per * tstart_s[gi]
        tiles_w = tiles_s[gi]
        lpart = jnp.int32(0)
        for k in range(1, per):
            lpart = lpart + (local >= k * tiles_w).astype(jnp.int32)
        ltile = local - lpart * tiles_w
        spare = jnp.maximum(w - n_items, 0)
        part = jnp.where(valid, lpart, spare % per)
        p_ref[w] = jnp.where(valid, tstart_s[gi] + ltile, n_used + spare // per)
        e_ref[w] = part
        key = gi * per + part
        key_ref[w] = key
        nxt_g = next_s[gi]
        run_ref[3 * w] = jnp.logical_and(valid, ltile == 0).astype(jnp.int32)
        run_ref[3 * w + 1] = (per * before_s[gi] + part) % 2
        run_ref[3 * w + 2] = jnp.where(part + 1 < per, key + 1,
                                       jnp.where(nxt_g >= 0, nxt_g * per, -1))
        return carry
    lax.fori_loop(0, n_items_max, item, 0)


def _plan_kernel(group_ref, rank_ref, cnt_ref, dest_ref, src_ref, nused_ref,
                 up_p, up_e, up_key, up_n, up_run, dn_p, dn_e, dn_key, dn_n, dn_run,
                 tstart_s, tend_s, tiles_s, before_s, next_s, row0_s,
                 *, tile, n_tiles_max, up_parts, dn_parts):
    s = pl.program_id(0)

    @pl.when(s == 0)
    def _():
        start = jnp.int32(0)
        owned = jnp.int32(0)
        for g in range(N_GROUPS):
            tiles = (cnt_ref[g] + tile - 1) // tile
            tstart_s[g] = start
            row0_s[g] = start * tile
            tiles_s[g] = tiles
            before_s[g] = owned
            start = start + tiles
            tend_s[g] = start
            owned = owned + (tiles > 0).astype(jnp.int32)
        nxt = jnp.int32(-1)
        for g in reversed(range(N_GROUPS)):
            next_s[g] = nxt
            nxt = jnp.where(tiles_s[g] > 0, g, nxt)
        nused_ref[0] = start

        def clear(r, carry):
            src_ref[r] = 0
            return carry
        lax.fori_loop(0, n_tiles_max * tile, clear, 0, unroll=8)

        grp = (tstart_s, tend_s, tiles_s, before_s, next_s)
        _plan_items(up_parts, start, up_parts * n_tiles_max, grp, up_p, up_e, up_key, up_n, up_run)
        _plan_items(dn_parts, start, dn_parts * n_tiles_max, grp, dn_p, dn_e, dn_key, dn_n, dn_run)

    def place(i, carry):
        t = s * PLAN_CHUNK + i
        d = row0_s[group_ref[i]] + rank_ref[i]
        dest_ref[t] = d
        src_ref[d] = t
        return carry
    lax.fori_loop(0, PLAN_CHUNK, place, 0, unroll=8)


def _moe_plan(group, rank, counts, *, tile, n_tiles_max, up_parts, dn_parts):
    t = group.shape[0]
    i32 = jnp.int32
    smem = functools.partial(pl.BlockSpec, memory_space=pltpu.SMEM)
    n_up, n_dn = up_parts * n_tiles_max, dn_parts * n_tiles_max
    shapes = [(t,), (n_tiles_max * tile,), (1,),
              (n_up,), (n_up,), (n_up,), (1,), (3 * n_up,),
              (n_dn,), (n_dn,), (n_dn,), (1,), (3 * n_dn,)]
    outs = pl.pallas_call(
        functools.partial(_plan_kernel, tile=tile, n_tiles_max=n_tiles_max,
                          up_parts=up_parts, dn_parts=dn_parts),
        grid=(t // PLAN_CHUNK,),
        in_specs=[smem((PLAN_CHUNK,), lambda s: (s,)), smem((PLAN_CHUNK,), lambda s: (s,)),
                  smem()],
        out_specs=[smem() for _ in shapes],
        out_shape=[jax.ShapeDtypeStruct(sh, i32) for sh in shapes],
        scratch_shapes=[pltpu.SMEM((N_GROUPS,), i32) for _ in range(6)],
        compiler_params=_params(1),
        name="moe_plan",
    )(group, rank, counts)
    dest, src, n_used = outs[:3]
    return dict(dest=dest, src=src, n_used=n_used, up=tuple(outs[3:8]), down=tuple(outs[8:13]))


def _row_copy(src_hbm, dst_vmem, src_row, dst_row, sem):
    return pltpu.make_async_copy(src_hbm.at[pl.ds(src_row, 1)], dst_vmem.at[pl.ds(dst_row, 1)], sem)


def _start_row_gather(idx_ref, base, n_rows, pairs):
    def body(r, carry):
        i = idx_ref[base + r]
        for src_hbm, dst_vmem, sem in pairs:
            _row_copy(src_hbm, dst_vmem, i, r, sem).start()
        return carry
    lax.fori_loop(0, n_rows, body, 0, unroll=8)


def _start_rows(idx_ref, idx_base, row0, n_rows, pairs):
    for k in range(n_rows):
        i = idx_ref[idx_base + row0 + k]
        for n, (src_hbm, dst_vmem, sem) in enumerate(pairs):
            _row_copy(src_hbm, dst_vmem, i, row0 + k, sem).start(priority=(k + n) % 2)


def _norm_tile_and_prefetch(load_rows, g_ref, o_ref, n_rows, has_next, idx_ref, next_base,
                            next_pairs):
    @pl.when(has_next)
    def _():
        _rmsnorm_rows(load_rows, g_ref, o_ref, n_rows,
                      per_trip=lambda row0: _start_rows(idx_ref, next_base, row0, NORM_ROWS,
                                                        next_pairs))

    @pl.when(jnp.logical_not(has_next))
    def _():
        _rmsnorm_rows(load_rows, g_ref, o_ref, n_rows)


def _wait_row_gather(n_rows, pairs):
    for src_hbm, dst_vmem, sem in pairs:
        pltpu.make_async_copy(src_hbm.at[pl.ds(0, n_rows)], dst_vmem, sem).wait()


def _dispatch_kernel(src_ref, nused_ref, x2_hbm, route_hbm, g_ref, xs_ref, gs_ref,
                     xbuf, gbuf, sem):
    p = pl.program_id(0)
    n_used = nused_ref[0]
    tile = xbuf.shape[1]
    slot = p % 2

    def pairs(s):
        return [(x2_hbm, xbuf.at[s], sem.at[0, s]), (route_hbm, gbuf.at[s], sem.at[1, s])]

    @pl.when(p == 0)
    def _():
        _start_row_gather(src_ref, 0, tile, pairs(0))

    @pl.when(p < n_used)
    def _():
        _wait_row_gather(tile, pairs(slot))
        gs_ref[...] = gbuf[slot]
        _norm_tile_and_prefetch(lambda rows: xbuf[slot, rows, :], g_ref, xs_ref, tile,
                                p + 1 < n_used, src_ref, (p + 1) * tile, pairs(1 - slot))

    @pl.when(p >= n_used)
    def _():
        xs_ref[...] = jnp.zeros_like(xs_ref)
        gs_ref[...] = jnp.zeros_like(gs_ref)


def _dispatch(x2, route, g, src, n_used, *, tile, n_tiles_max):
    t, d = x2.shape
    tile_map = lambda p, src_ref, n_ref: (p, 0)
    grid_spec = pltpu.PrefetchScalarGridSpec(
        num_scalar_prefetch=2,
        grid=(n_tiles_max,),
        in_specs=[
            pl.BlockSpec(memory_space=pl.ANY),
            pl.BlockSpec(memory_space=pl.ANY),
            pl.BlockSpec((1, d), lambda p, s, n: (0, 0)),
        ],
        out_specs=[
            pl.BlockSpec((tile, d), tile_map),
            pl.BlockSpec((tile, LANES), tile_map),
        ],
        scratch_shapes=[
            pltpu.VMEM((2, tile, d), F32),
            pltpu.VMEM((2, tile, LANES), F32),
            pltpu.SemaphoreType.DMA((2, 2)),
        ],
    )
    return pl.pallas_call(
        _dispatch_kernel,
        grid_spec=grid_spec,
        out_shape=[
            jax.ShapeDtypeStruct((n_tiles_max * tile, d), BF16),
            jax.ShapeDtypeStruct((n_tiles_max * tile, LANES), F32),
        ],
        compiler_params=_params(1),
        name="moe_dispatch",
    )(src, n_used, x2, route, g)


def _stream_run_weights(w, n_items, key_ref, run_ref, copies):
    is_first = run_ref[3 * w]
    slot = run_ref[3 * w + 1]
    nxt_key = run_ref[3 * w + 2]
    valid = w < n_items

    @pl.when(jnp.logical_and(valid, w == 0))
    def _():
        for cp in copies(key_ref[0], 0):
            cp.start()

    @pl.when(jnp.logical_and(valid, is_first == 1))
    def _():
        for cp in copies(key_ref[w], slot):
            cp.wait()

        @pl.when(nxt_key >= 0)
        def _():
            for cp in copies(nxt_key, 1 - slot):
                cp.start()

    return slot


def _moe_up_kernel(ip_ref, ie_ref, key_ref, n_ref, run_ref, xs_ref, gs_ref, w1_hbm, w3_hbm,
                   hid_ref, w1_buf, w3_buf, sem):
    w = pl.program_id(0)

    n_e = w1_buf.shape[1]
    f = w1_buf.shape[3]

    def copies(key, slot):
        experts = pl.ds(key * n_e, n_e)
        return [pltpu.make_async_copy(w1_hbm.at[experts], w1_buf.at[slot], sem.at[0, slot]),
                pltpu.make_async_copy(w3_hbm.at[experts], w3_buf.at[slot], sem.at[1, slot])]

    slot = _stream_run_weights(w, n_ref[0], key_ref, run_ref, copies)

    @pl.when(w < n_ref[0])
    def _():
        x = xs_ref[...]
        lane = lax.broadcasted_iota(jnp.int32, gs_ref.shape, 1)
        for e in range(n_e):
            a = jnp.dot(x, w1_buf[slot, e].astype(BF16), preferred_element_type=F32)
            b = jnp.dot(x, w3_buf[slot, e].astype(BF16), preferred_element_type=F32)
            in_group = ie_ref[w] * n_e + e
            gate = jnp.sum(jnp.where(lane == in_group, gs_ref[...], 0.0), axis=-1, keepdims=True)
            hid_ref[:, e * f:(e + 1) * f] = (jax.nn.silu(a) * b * gate).astype(hid_ref.dtype)

    @pl.when(w >= n_ref[0])
    def _():
        hid_ref[...] = jnp.zeros_like(hid_ref)


def _moe_up(xs, gs, w1, w3, items, *, tile):
    rows, d = xs.shape
    f = w1.shape[2]
    ip, ie, key, n_items, run = items
    grid_spec = pltpu.PrefetchScalarGridSpec(
        num_scalar_prefetch=5,
        grid=(ip.shape[0],),
        in_specs=[
            pl.BlockSpec((tile, d), lambda w, ip, ie, key, n, run: (ip[w], 0)),
            pl.BlockSpec((tile, LANES), lambda w, ip, ie, key, n, run: (ip[w], 0)),
            pl.BlockSpec(memory_space=pl.ANY),
            pl.BlockSpec(memory_space=pl.ANY),
        ],
        out_specs=pl.BlockSpec((tile, MOE_UP_EXPERTS * f),
                               lambda w, ip, ie, key, n, run: (ip[w], ie[w])),
        scratch_shapes=[
            pltpu.VMEM((2, MOE_UP_EXPERTS, d, f), w1.dtype),
            pltpu.VMEM((2, MOE_UP_EXPERTS, d, f), w3.dtype),
            pltpu.SemaphoreType.DMA((2, 2)),
        ],
    )
    return pl.pallas_call(
        _moe_up_kernel,
        grid_spec=grid_spec,
        out_shape=jax.ShapeDtypeStruct((rows, EXPERTS_PER_GROUP * f), BF16),
        compiler_params=_params(1),
        name="moe_up",
    )(ip, ie, key, n_items, run, xs, gs, w1, w3)


def _moe_down_kernel(ip_ref, ih_ref, key_ref, n_ref, run_ref, hid_ref, w2_hbm, y_ref,
                     w2_buf, sem, *, n_halves):
    w = pl.program_id(0)
    tn = w2_buf.shape[2]

    def copies(key, slot):
        g = key // n_halves
        col = pl.multiple_of((key % n_halves) * tn, tn)
        return [pltpu.make_async_copy(w2_hbm.at[g, :, pl.ds(col, tn)], w2_buf.at[slot],
                                      sem.at[slot])]

    slot = _stream_run_weights(w, n_ref[0], key_ref, run_ref, copies)

    @pl.when(w < n_ref[0])
    def _():
        y_ref[...] = jnp.dot(hid_ref[...], w2_buf[slot].astype(BF16),
                             preferred_element_type=F32)

    @pl.when(w >= n_ref[0])
    def _():
        y_ref[...] = jnp.zeros_like(y_ref)


def _moe_down(hid, w2g, items, *, tile, n_halves):
    rows, k = hid.shape
    d = w2g.shape[2]
    tn = d // n_halves
    ip, ih, key, n_items, run = items
    grid_spec = pltpu.PrefetchScalarGridSpec(
        num_scalar_prefetch=5,
        grid=(ip.shape[0],),
        in_specs=[
            pl.BlockSpec((tile, k), lambda w, ip, ih, key, n, run: (ip[w], 0)),
            pl.BlockSpec(memory_space=pl.ANY),
        ],
        out_specs=pl.BlockSpec((tile, tn), lambda w, ip, ih, key, n, run: (ip[w], ih[w])),
        scratch_shapes=[
            pltpu.VMEM((2, k, tn), w2g.dtype),
            pltpu.SemaphoreType.DMA((2,)),
        ],
    )
    return pl.pallas_call(
        functools.partial(_moe_down_kernel, n_halves=n_halves),
        grid_spec=grid_spec,
        out_shape=jax.ShapeDtypeStruct((rows, d), F32),
        compiler_params=_params(1),
        name="moe_down",
    )(ip, ih, key, n_items, run, hid, w2g)


def _final_kernel(dest_ref, x_ref, ys_hbm, g_ref, o_ref, ybuf, sem, *, tile0):
    i = pl.program_id(0)
    tm = ybuf.shape[1]
    slot = i % 2

    def pairs(s):
        return [(ys_hbm, ybuf.at[s], sem.at[s])]

    @pl.when(i == 0)
    def _():
        _start_row_gather(dest_ref, tile0 * tm, tm, pairs(0))

    _wait_row_gather(tm, pairs(slot))
    _norm_tile_and_prefetch(lambda rows: x_ref[rows, :] + ybuf[slot, rows, :], g_ref, o_ref, tm,
                            i + 1 < pl.num_programs(0), dest_ref, (tile0 + i + 1) * tm,
                            pairs(1 - slot))


def _final(x2, ys, dest, g, *, row0, n_rows, tm):
    d = x2.shape[1]
    off = row0 // tm
    grid_spec = pltpu.PrefetchScalarGridSpec(
        num_scalar_prefetch=1,
        grid=(n_rows // tm,),
        in_specs=[
            pl.BlockSpec((tm, d), lambda i, dest: (i + off, 0)),
            pl.BlockSpec(memory_space=pl.ANY),
            pl.BlockSpec((1, d), lambda i, dest: (0, 0)),
        ],
        out_specs=pl.BlockSpec((tm, d), lambda i, dest: (i, 0)),
        scratch_shapes=[pltpu.VMEM((2, tm, d), F32), pltpu.SemaphoreType.DMA((2,))],
    )
    return pl.pallas_call(
        functools.partial(_final_kernel, tile0=off),
        grid_spec=grid_spec,
        out_shape=jax.ShapeDtypeStruct((n_rows, d), F32),
        compiler_params=_params(1),
        name="final_norm",
    )(dest, x2, ys, g)


def _rope_tables(pos):
    half = ROT_DIM // 2
    inv = ROPE_THETA ** (-np.arange(half, dtype=np.float64) / half)
    ang = np.asarray(pos, np.float64)[:, None] * inv[None, :]
    cos, sin = np.cos(ang), np.sin(ang)
    ones = np.ones((ang.shape[0], HEAD_DIM - ROT_DIM))
    zeros = np.zeros((ang.shape[0], HEAD_DIM - half))
    c = np.concatenate([cos, cos, ones], axis=1)
    s_hi = np.concatenate([-sin, zeros], axis=1)
    s_lo = np.concatenate([np.zeros_like(sin), sin, 0.0 * ones], axis=1)
    rep = LANES // HEAD_DIM
    return tuple(jnp.asarray(np.tile(t, (1, rep)), F32) for t in (c, s_hi, s_lo))


def kernel(x_prompt, x_sample, cache_k, cache_v, state_conv, state_h, norm_mix, w_in, conv_w,
           conv_b, w_gate_a, b_gate_a, w_gate_x, b_gate_x, lru_lambda, sinks, norm_attn_out,
           norm_lru_out, w_out, norm_ffn, w_group, b_group, w_expert_router, b_expert_router,
           w1, w3, w2, norm_final):
    batch, seq, d_model = x_prompt.shape
    dec_batch, dec_seq, _ = x_sample.shape
    depth = w_in.shape[0]
    assert depth == 1 and dec_seq == CHUNK and seq % CHUNK == 0
    n_heads = sinks.shape[1]
    att_width = n_heads * HEAD_DIM
    n_kv = cache_k.shape[3]
    kv_cols = n_kv * HEAD_DIM
    lru_width = lru_lambda.shape[1]
    cw = cache_k.shape[2]
    assert cw == WIN_CHUNKS * CHUNK
    n_p, n_s = batch * seq, dec_batch * dec_seq
    chunks_per_seq = seq // CHUNK
    n_prompt_chunks = n_p // CHUNK

    xp = x_prompt.reshape(n_p, d_model)
    xs = x_sample.reshape(n_s, d_model)

    w_r = w_in[0]
    tn = 2 * kv_cols
    assert tn == 512 and att_width % tn == 0 and lru_width % tn == 0
    n_q_tiles = att_width // tn
    kv_tile = n_q_tiles

    tm_in = PROJ_ROWS
    assert seq % tm_in == 0 and n_s % tm_in == 0 and tm_in % dec_seq == 0
    pos = np.concatenate([np.arange(seq), np.tile(PAST_LEN + np.arange(dec_seq), tm_in // dec_seq)])
    c_tab, shi_tab, slo_tab = _rope_tables(pos)

    z = _in_proj(xp, xs, norm_mix, w_r, c_tab, shi_tab, slo_tab, tm=tm_in, tn=tn,
                 n_q_tiles=n_q_tiles, kv_tile=kv_tile, prompt_tiles_per_seq=seq // tm_in)

    k_col = att_width + 2 * lru_width
    v_col = k_col + kv_cols
    att = _attention(
        z, cache_k[0].reshape(dec_batch, cw, kv_cols), cache_v[0].reshape(dec_batch, cw, kv_cols),
        sinks[0], norm_attn_out,
        n_prompt_chunks=n_prompt_chunks, chunks_per_seq=chunks_per_seq, att_width=att_width,
        n_kv=n_kv, k_col_blk=k_col // kv_cols, v_col_blk=v_col // kv_cols)

    lru, h_tiles = _rglru(
        z, state_conv[0], state_h[0].reshape(dec_batch, 1, lru_width), conv_w[0], conv_b,
        w_gate_a[0].astype(BF16), b_gate_a[0].reshape(1, lru_width),
        w_gate_x[0].astype(BF16), b_gate_x[0].reshape(1, lru_width),
        lru_lambda, norm_lru_out,
        n_prompt_chunks=n_prompt_chunks, chunks_per_seq=chunks_per_seq, width=lru_width,
        xb_col_blk=att_width // lru_width, yb_col_blk=att_width // lru_width + 1)

    n_routes = N_GROUPS + N_GROUPS * EXPERTS_PER_GROUP
    wr = jnp.concatenate([w_group[0], w_expert_router[0],
                          jnp.zeros((d_model, LANES - n_routes), F32)], axis=1).astype(BF16)
    br = jnp.concatenate([b_group[0], b_expert_router[0],
                          jnp.zeros((LANES - n_routes,), F32)]).reshape(1, LANES)
    x2, route, counts, meta = _out_proj(att, lru, w_out[0], xp, xs, norm_ffn, wr, br,
                                        tm=PROJ_ROWS, tn=OUT_PROJ_COLS)

    n_tiles_max = (n_p + n_s) // MOE_TILE + N_GROUPS
    tabs = _moe_plan(meta[0], meta[1], counts[0, :N_GROUPS].astype(jnp.int32), tile=MOE_TILE,
                     n_tiles_max=n_tiles_max, up_parts=EXPERTS_PER_GROUP // MOE_UP_EXPERTS,
                     dn_parts=MOE_DOWN_HALVES)
    xsort, gsort = _dispatch(x2, route, norm_ffn, tabs["src"], tabs["n_used"],
                             tile=MOE_TILE, n_tiles_max=n_tiles_max)
    hid = _moe_up(xsort, gsort, w1[0], w3[0], tabs["up"], tile=MOE_TILE)
    d_expert = w2.shape[2]
    w2g = w2[0].reshape(N_GROUPS, EXPERTS_PER_GROUP * d_expert, d_model)
    ysort = _moe_down(hid, w2g, tabs["down"], tile=MOE_TILE, n_halves=MOE_DOWN_HALVES)

    g_fin = norm_final.reshape(1, d_model)
    y_prompt = _final(x2, ysort, tabs["dest"], g_fin, row0=0, n_rows=n_p,
                      tm=FINAL_ROWS).reshape(batch, seq, d_model)
    y_sample = _final(x2, ysort, tabs["dest"], g_fin, row0=n_p, n_rows=n_s,
                      tm=FINAL_ROWS).reshape(dec_batch, dec_seq, d_model)

    win = min(WIN_CHUNKS * CHUNK, seq)

    def prompt_tail(col, width, rows):
        return jnp.stack([lax.slice(z, ((b + 1) * seq - rows, col), ((b + 1) * seq, col + width))
                          for b in range(batch)])

    def sample_rows(col, width):
        return lax.slice(z, (n_p, col), (n_p + n_s, col + width)).reshape(dec_batch, dec_seq, width)

    k_prompt = prompt_tail(k_col, kv_cols, win).reshape(1, batch, win, n_kv, HEAD_DIM)
    v_prompt = prompt_tail(v_col, kv_cols, win).reshape(1, batch, win, n_kv, HEAD_DIM)
    conv_prompt = prompt_tail(att_width, lru_width, CONV_WIDTH - 1)[None]
    h_prompt = h_tiles[:n_prompt_chunks, 0].reshape(batch, chunks_per_seq, lru_width)[:, -1][None]

    ks = sample_rows(k_col, kv_cols).reshape(dec_batch, dec_seq, n_kv, HEAD_DIM)
    vs = sample_rows(v_col, kv_cols).reshape(dec_batch, dec_seq, n_kv, HEAD_DIM)
    k_sample = jnp.concatenate([cache_k[0], ks], axis=1)[:, -cw:][None]
    v_sample = jnp.concatenate([cache_v[0], vs], axis=1)[:, -cw:][None]
    xs_rows = sample_rows(att_width, lru_width)[:, dec_seq - (CONV_WIDTH - 1):]
    conv_sample = jnp.concatenate([state_conv[0], xs_rows], axis=1)[:, -(CONV_WIDTH - 1):][None]
    h_sample = h_tiles[n_prompt_chunks:, 0][None]

    return (y_prompt, y_sample, k_prompt, v_prompt, conv_prompt, h_prompt,
            k_sample, v_sample, conv_sample, h_sample)
```

```python
import functools

import numpy as np
import jax
import jax.numpy as jnp
from jax import lax
from jax.experimental import pallas as pl
from jax.experimental.pallas import tpu as pltpu

F32 = jnp.float32
BF16 = jnp.bfloat16

CHUNK = 64
HEAD_DIM = 64
KV_GROUP = 8
WIN_CHUNKS = 2
ROT_DIM = 16
ROPE_THETA = 500000.0
LRU_BLOCK = 128
CONV_WIDTH = 4
LRU_C = 8.0
N_GROUPS = 8
EXPERTS_PER_GROUP = 4
EPS = 1e-6
NEG = -1e30
PAST_LEN = 1024
LANES = 128
PROJ_ROWS = 1024
OUT_PROJ_COLS = 512
INPROJ_ROW_BLOCK = 128
MOE_TILE = 256
FINAL_ROWS = 256
MOE_UP_EXPERTS = 2
MOE_DOWN_HALVES = 1
VMEM_LIMIT = 56 * 1024 * 1024


def _params(n_axes, vmem=VMEM_LIMIT):
    return pltpu.CompilerParams(
        dimension_semantics=("arbitrary",) * n_axes, vmem_limit_bytes=vmem)


def _rms_scale(x):
    return lax.rsqrt(jnp.mean(x * x, axis=-1, keepdims=True) + EPS)


NORM_ROWS = 16


def _rmsnorm_rows(load_rows, g_ref, o_ref, n_rows, per_trip=None):
    def body(r, carry):
        row0 = r * NORM_ROWS if isinstance(r, int) else pl.multiple_of(r * NORM_ROWS, NORM_ROWS)
        rows = pl.ds(row0, NORM_ROWS)
        x = load_rows(rows)
        o_ref[rows, :] = (x * _rms_scale(x) * g_ref[...]).astype(o_ref.dtype)
        if per_trip is not None:
            per_trip(row0)
        return carry
    if per_trip is not None:
        for r in range(n_rows // NORM_ROWS):
            body(r, 0)
    else:
        lax.fori_loop(0, n_rows // NORM_ROWS, body, 0, unroll=4)


def _rope_block(zb, c, s_hi, s_lo):
    return (zb * c + pltpu.roll(zb, LANES - ROT_DIM // 2, 1) * s_hi
            + pltpu.roll(zb, ROT_DIM // 2, 1) * s_lo)


def _inproj_kernel(xp_hbm, xs_hbm, g_ref, w_ref, c_ref, shi_ref, slo_ref, o_ref, x_buf, h_ref,
                   sem, *, n_prompt_tiles, n_q_tiles, kv_tile):
    i = pl.program_id(0)
    j = pl.program_id(1)
    tm, tn = o_ref.shape

    def x_tile_copy(src_hbm, tile):
        return pltpu.make_async_copy(src_hbm.at[pl.ds(tile * tm, tm)], x_buf, sem)

    def start_x(tile):
        @pl.when(tile < n_prompt_tiles)
        def _():
            x_tile_copy(xp_hbm, tile).start()

        @pl.when(tile >= n_prompt_tiles)
        def _():
            x_tile_copy(xs_hbm, tile - n_prompt_tiles).start()

    @pl.when(jnp.logical_and(i == 0, j == 0))
    def _():
        start_x(0)

    @pl.when(j == 0)
    def _():
        x_tile_copy(xp_hbm, 0).wait()
        _rmsnorm_rows(lambda rows: x_buf[rows, :], g_ref, h_ref, tm)

    @pl.when(jnp.logical_and(j == 1, i + 1 < pl.num_programs(0)))
    def _():
        start_x(i + 1)

    w = w_ref[...].astype(BF16)
    n_blk = tn // LANES
    for r0 in range(0, tm, INPROJ_ROW_BLOCK):
        rows = slice(r0, r0 + INPROJ_ROW_BLOCK)
        z = jnp.dot(h_ref[rows, :], w, preferred_element_type=F32)
        c, shi, slo = c_ref[rows, :], shi_ref[rows, :], slo_ref[rows, :]
        for b in range(n_blk):
            is_rope = j < n_q_tiles
            if b < n_blk // 2:
                is_rope = jnp.logical_or(is_rope, j == kv_tile)
            zb = z[:, b * LANES:(b + 1) * LANES]
            o_ref[rows, b * LANES:(b + 1) * LANES] = _rope_block(
                zb, jnp.where(is_rope, c, 1.0), jnp.where(is_rope, shi, 0.0),
                jnp.where(is_rope, slo, 0.0))


def _in_proj(xp, xs, g, w, c, shi, slo, *, tm, tn, n_q_tiles, kv_tile, prompt_tiles_per_seq):
    n_p, d = xp.shape
    n_s = xs.shape[0]
    t = n_p + n_s
    npt = n_p // tm
    n_cols = w.shape[1]
    kern = functools.partial(_inproj_kernel, n_prompt_tiles=npt, n_q_tiles=n_q_tiles,
                             kv_tile=kv_tile)
    n_tiles = n_cols // tn
    assert kv_tile == n_q_tiles

    def table_tile(i, j):
        return jnp.where(i < npt, i % prompt_tiles_per_seq, prompt_tiles_per_seq), 0

    def out_tile(i, j):
        return i, jnp.where(j < kv_tile, j, jnp.where(j == kv_tile, n_tiles - 1, j - 1))

    return pl.pallas_call(
        kern,
        grid=(t // tm, n_tiles),
        in_specs=[
            pl.BlockSpec(memory_space=pl.ANY),
            pl.BlockSpec(memory_space=pl.ANY),
            pl.BlockSpec((1, d), lambda i, j: (0, 0)),
            pl.BlockSpec((d, tn), lambda i, j: (0, j)),
            pl.BlockSpec((tm, LANES), table_tile),
            pl.BlockSpec((tm, LANES), table_tile),
            pl.BlockSpec((tm, LANES), table_tile),
        ],
        out_specs=pl.BlockSpec((tm, tn), out_tile),
        out_shape=jax.ShapeDtypeStruct((t, n_cols), F32),
        scratch_shapes=[pltpu.VMEM((tm, d), xp.dtype), pltpu.VMEM((tm, d), BF16),
                        pltpu.SemaphoreType.DMA(())],
        compiler_params=_params(2),
        name="in_proj",
    )(xp, xs, g, w, c, shi, slo)


ATTN_CHUNKS = 8


def _attn_chunk(q, k_band, v_band, lo, sink_ref, n_kv):
    n_keys = k_band.shape[0]
    key = lax.broadcasted_iota(jnp.int32, (n_keys, KV_GROUP * CHUNK), 0)
    valid = key >= lo
    outs = []
    for gi in range(n_kv):
        kg = k_band[:, gi * HEAD_DIM:(gi + 1) * HEAD_DIM]
        vg = v_band[:, gi * HEAD_DIM:(gi + 1) * HEAD_DIM]
        heads = [gi * KV_GROUP + h for h in range(KV_GROUP)]
        qg = jnp.concatenate([q[:, h * HEAD_DIM:(h + 1) * HEAD_DIM] for h in heads], axis=0)
        st = lax.dot_general(kg, qg, (((1,), (1,)), ((), ())), preferred_element_type=F32)
        st = jnp.where(valid, st, NEG)
        sink = sink_ref[gi:gi + 1, :]
        m = jnp.maximum(jnp.max(st, axis=0, keepdims=True), sink)
        p = jnp.exp(st - m)
        denom = jnp.sum(p, axis=0, keepdims=True) + jnp.exp(sink - m)
        pn = (p * (1.0 / denom)).astype(BF16)
        og = lax.dot_general(pn, vg, (((0,), (0,)), ((), ())), preferred_element_type=F32)
        outs.extend(og[h * CHUNK:(h + 1) * CHUNK] for h in range(KV_GROUP))
    return jnp.concatenate(outs, axis=1)


def _attn_kernel(q_ref, kp_ref, k_ref, vp_ref, v_ref, kc_ref, vc_ref, sink_ref, g_ref, o_ref,
                 *, n_prompt_tiles, tiles_per_seq, n_kv):
    n = pl.program_id(0)
    is_s = n >= n_prompt_tiles
    has_prev = jnp.logical_or(is_s, n % tiles_per_seq > 0)

    def band(prev_ref, own_ref, cache_ref, c):
        prev, own, cache = prev_ref[...], own_ref[...], cache_ref[c]
        hist = jnp.concatenate([prev, own], axis=0)[c * CHUNK:(c + WIN_CHUNKS) * CHUNK]
        hist = jnp.where(is_s, cache, hist)
        return jnp.concatenate([hist, own[c * CHUNK:(c + 1) * CHUNK]], axis=0).astype(BF16)

    q = (q_ref[...] * (HEAD_DIM ** -0.5)).astype(BF16)
    for c in range(ATTN_CHUNKS):
        lo = jnp.where(has_prev, 0, max(WIN_CHUNKS - c, 0) * CHUNK)
        att = _attn_chunk(q[c * CHUNK:(c + 1) * CHUNK], band(kp_ref, k_ref, kc_ref, c),
                          band(vp_ref, v_ref, vc_ref, c), lo, sink_ref, n_kv)
        o_ref[c * CHUNK:(c + 1) * CHUNK, :] = (
            att * _rms_scale(att) * g_ref[...]).astype(o_ref.dtype)


def _attention(z, cache_k, cache_v, sinks, g, *, n_prompt_chunks, chunks_per_seq,
               att_width, n_kv, k_col_blk, v_col_blk):
    t = z.shape[0]
    rows = ATTN_CHUNKS * CHUNK
    assert ATTN_CHUNKS % WIN_CHUNKS == 0 and chunks_per_seq % ATTN_CHUNKS == 0
    hist_rows = WIN_CHUNKS * CHUNK
    hist_per_tile = rows // hist_rows
    assert cache_k.shape[0] % ATTN_CHUNKS == 0
    n_tiles = t // rows
    npt = n_prompt_chunks // ATTN_CHUNKS
    tps = chunks_per_seq // ATTN_CHUNKS
    kvw = n_kv * HEAD_DIM

    def prev(col):
        return lambda n: (n * hist_per_tile - jnp.minimum(1, n % tps), col)

    cache_map = lambda n: (jnp.maximum(n - npt, 0), 0, 0)
    kern = functools.partial(_attn_kernel, n_prompt_tiles=npt, tiles_per_seq=tps, n_kv=n_kv)
    sink_rows = jnp.repeat(sinks.reshape(n_kv, KV_GROUP), CHUNK, axis=1)
    return pl.pallas_call(
        kern,
        grid=(n_tiles,),
        in_specs=[
            pl.BlockSpec((rows, att_width), lambda n: (n, 0)),
            pl.BlockSpec((hist_rows, kvw), prev(k_col_blk)),
            pl.BlockSpec((rows, kvw), lambda n: (n, k_col_blk)),
            pl.BlockSpec((hist_rows, kvw), prev(v_col_blk)),
            pl.BlockSpec((rows, kvw), lambda n: (n, v_col_blk)),
            pl.BlockSpec((ATTN_CHUNKS, WIN_CHUNKS * CHUNK, kvw), cache_map),
            pl.BlockSpec((ATTN_CHUNKS, WIN_CHUNKS * CHUNK, kvw), cache_map),
            pl.BlockSpec((n_kv, KV_GROUP * CHUNK), lambda n: (0, 0)),
            pl.BlockSpec((1, att_width), lambda n: (0, 0)),
        ],
        out_specs=pl.BlockSpec((rows, att_width), lambda n: (n, 0)),
        out_shape=jax.ShapeDtypeStruct((t, att_width), BF16),
        compiler_params=_params(1),
        name="attention",
    )(z, z, z, z, z, cache_k, cache_v, sink_rows, g)


SUBLANES = 8


def _lru_kernel(xb_ref, yb_ref, sconv_ref, sh_ref, cw_ref, cb_ref, wa_ref, ba_ref,
                wx_ref, bx_ref, lam_ref, g_ref, o_ref, hl_ref, hist_ref, hcar_ref,
                a_sc, b_sc, h_sc, *, n_prompt_chunks, chunks_per_seq):
    n = pl.program_id(0)
    is_s = n >= n_prompt_chunks
    hist_rows = hist_ref.shape[0]

    @pl.when(jnp.logical_and(jnp.logical_not(is_s), n % chunks_per_seq == 0))
    def _():
        hist_ref[...] = jnp.zeros_like(hist_ref)
        hcar_ref[...] = jnp.zeros_like(hcar_ref)

    @pl.when(is_s)
    def _():
        hist_ref[...] = jnp.zeros_like(hist_ref)
        hist_ref[hist_rows - (CONV_WIDTH - 1):, :] = sconv_ref[0]
        hcar_ref[...] = sh_ref[0]

    x = xb_ref[...]
    rows, width = x.shape
    hist = hist_ref[...]
    row8 = lax.broadcasted_iota(jnp.int32, (hist_rows, width), 0)
    xc = cb_ref[...]
    for j in range(CONV_WIDTH):
        s = CONV_WIDTH - 1 - j
        if s == 0:
            xs = x
        else:
            xr = pltpu.roll(x, s, 0)
            head = jnp.where(row8 < s, pltpu.roll(hist, s, 0), xr[:hist_rows])
            xs = jnp.concatenate([head, xr[hist_rows:]], axis=0)
        xc = xc + xs * cw_ref[j:j + 1, :]
    hist_ref[...] = x[rows - hist_rows:]

    xcb = xc.astype(BF16)
    ra, rx = [], []
    for nb in range(width // LRU_BLOCK):
        blk = xcb[:, nb * LRU_BLOCK:(nb + 1) * LRU_BLOCK]
        ra.append(jnp.dot(blk, wa_ref[nb], preferred_element_type=F32))
        rx.append(jnp.dot(blk, wx_ref[nb], preferred_element_type=F32))
    r = jax.nn.sigmoid(jnp.concatenate(ra, axis=1) + ba_ref[...])
    ig = jax.nn.sigmoid(jnp.concatenate(rx, axis=1) + bx_ref[...])
    log_a = r * (-LRU_C * jax.nn.softplus(-lam_ref[...]))
    a = jnp.exp(log_a)
    v = -jnp.tanh(log_a) * (1.0 + a * a)
    b = jnp.where(v == 0.0, 0.0, v * lax.rsqrt(v)) * ig * xc

    seg_len = rows // SUBLANES
    h_blocks, carries = [], []
    for nb in range(width // LANES):
        lanes = slice(nb * LANES, (nb + 1) * LANES)
        a_sc[nb] = a[:, lanes]
        b_sc[nb] = b[:, lanes]
        hloc, ploc = [], []
        for k in range(seg_len):
            a_k = a_sc[nb, pl.ds(k, SUBLANES, stride=seg_len), :]
            b_k = b_sc[nb, pl.ds(k, SUBLANES, stride=seg_len), :]
            hloc.append(b_k if k == 0 else a_k * hloc[-1] + b_k)
            ploc.append(a_k if k == 0 else a_k * ploc[-1])
        carry = hcar_ref[:, lanes]
        seg_in = []
        for s in range(SUBLANES):
            seg_in.append(carry)
            carry = ploc[-1][s:s + 1] * carry + hloc[-1][s:s + 1]
        seg_in = jnp.concatenate(seg_in, axis=0)
        carries.append(carry)
        for k in range(seg_len):
            h_sc[nb, pl.ds(k, SUBLANES, stride=seg_len), :] = hloc[k] + ploc[k] * seg_in
        h_blocks.append(h_sc[nb])
    h = jnp.concatenate(h_blocks, axis=1)
    h_last = jnp.concatenate(carries, axis=1)
    hcar_ref[...] = h_last
    hl_ref[0] = h_last

    y = jax.nn.gelu(yb_ref[...]) * h
    o_ref[...] = (y * _rms_scale(y) * g_ref[...]).astype(o_ref.dtype)


def _rglru(z, state_conv, state_h, conv_w, conv_b, wa, ba, wx, bx, lam, g,
           *, n_prompt_chunks, chunks_per_seq, width, xb_col_blk, yb_col_blk):
    t = z.shape[0]
    n_items = t // CHUNK
    npc = n_prompt_chunks
    nblk = width // LRU_BLOCK
    state_map = lambda n: (jnp.maximum(n - npc, 0), 0, 0)
    full2 = lambda n: (0, 0)
    full3 = lambda n: (0, 0, 0)
    kern = functools.partial(_lru_kernel, n_prompt_chunks=npc, chunks_per_seq=chunks_per_seq)
    return pl.pallas_call(
        kern,
        grid=(n_items,),
        in_specs=[
            pl.BlockSpec((CHUNK, width), lambda n: (n, xb_col_blk)),
            pl.BlockSpec((CHUNK, width), lambda n: (n, yb_col_blk)),
            pl.BlockSpec((1, CONV_WIDTH - 1, width), state_map),
            pl.BlockSpec((1, 1, width), state_map),
            pl.BlockSpec((CONV_WIDTH, width), full2),
            pl.BlockSpec((1, width), full2),
            pl.BlockSpec((nblk, LRU_BLOCK, LRU_BLOCK), full3),
            pl.BlockSpec((1, width), full2),
            pl.BlockSpec((nblk, LRU_BLOCK, LRU_BLOCK), full3),
            pl.BlockSpec((1, width), full2),
            pl.BlockSpec((1, width), full2),
            pl.BlockSpec((1, width), full2),
        ],
        out_specs=[
            pl.BlockSpec((CHUNK, width), lambda n: (n, 0)),
            pl.BlockSpec((1, 1, width), lambda n: (n, 0, 0)),
        ],
        out_shape=[
            jax.ShapeDtypeStruct((t, width), BF16),
            jax.ShapeDtypeStruct((n_items, 1, width), F32),
        ],
        scratch_shapes=[
            pltpu.VMEM((SUBLANES, width), F32),
            pltpu.VMEM((1, width), F32),
            pltpu.VMEM((width // LANES, CHUNK, LANES), F32),
            pltpu.VMEM((width // LANES, CHUNK, LANES), F32),
            pltpu.VMEM((width // LANES, CHUNK, LANES), F32),
        ],
        compiler_params=_params(1),
        name="rglru",
    )(z, z, state_conv, state_h, conv_w, conv_b, wa, ba, wx, bx, lam, g)


def _outproj_kernel(att_ref, lru_ref, wa_ref, wl_ref, xp_ref, xs_ref, g_ref, wr_ref, br_ref,
                    o_ref, route_ref, cnt_ref, meta_ref, ssq_ref, lg_ref,
                    *, n_prompt_tiles, d_model):
    i = pl.program_id(0)
    j = pl.program_id(1)
    acc = jnp.dot(att_ref[...], wa_ref[...].astype(BF16), preferred_element_type=F32)
    acc = acc + jnp.dot(lru_ref[...], wl_ref[...].astype(BF16), preferred_element_type=F32)
    x2 = jnp.where(i < n_prompt_tiles, xp_ref[...], xs_ref[...]) + acc
    o_ref[...] = x2

    @pl.when(jnp.logical_and(i == 0, j == 0))
    def _():
        cnt_ref[...] = jnp.zeros_like(cnt_ref)

    @pl.when(j == 0)
    def _():
        ssq_ref[...] = jnp.zeros_like(ssq_ref)
        lg_ref[...] = jnp.zeros_like(lg_ref)

    ssq_ref[...] += jnp.sum(x2 * x2, axis=-1, keepdims=True)
    lg_ref[...] += jnp.dot((x2 * g_ref[...]).astype(BF16), wr_ref[...],
                           preferred_element_type=F32)

    @pl.when(j == pl.num_programs(1) - 1)
    def _():
        scale = lax.rsqrt(ssq_ref[...] * (1.0 / d_model) + EPS)
        logits = lg_ref[...] * scale + br_ref[...]
        for r0 in range(0, logits.shape[0], ROUTE_ROWS):
            route = _route_rows(logits[r0:r0 + ROUTE_ROWS], cnt_ref)
            route_ref[r0:r0 + ROUTE_ROWS, :] = route
            meta_ref[:, r0:r0 + ROUTE_ROWS] = route.T[
                ROUTE_GROUP_LANE:ROUTE_RANK_LANE + 1, :].astype(jnp.int32)


def _out_proj(att, lru, w, xp, xs, g, wr, br, *, tm, tn):
    t, aw = att.shape
    lw = lru.shape[1]
    d = w.shape[1]
    assert aw == lw and w.shape[0] == aw + lw
    npt = xp.shape[0] // tm
    kern = functools.partial(_outproj_kernel, n_prompt_tiles=npt, d_model=d)
    return pl.pallas_call(
        kern,
        grid=(t // tm, d // tn),
        in_specs=[
            pl.BlockSpec((tm, aw), lambda i, j: (i, 0)),
            pl.BlockSpec((tm, lw), lambda i, j: (i, 0)),
            pl.BlockSpec((aw, tn), lambda i, j: (0, j)),
            pl.BlockSpec((lw, tn), lambda i, j: (1, j)),
            pl.BlockSpec((tm, tn), lambda i, j: (jnp.minimum(i, npt - 1),
                                                 jnp.where(i < npt, j, 0))),
            pl.BlockSpec((tm, tn), lambda i, j: (jnp.maximum(i - npt, 0),
                                                 jnp.where(i < npt, 0, j))),
            pl.BlockSpec((1, tn), lambda i, j: (0, j)),
            pl.BlockSpec((tn, LANES), lambda i, j: (j, 0)),
            pl.BlockSpec((1, LANES), lambda i, j: (0, 0)),
        ],
        out_specs=[
            pl.BlockSpec((tm, tn), lambda i, j: (i, j)),
            pl.BlockSpec((tm, LANES), lambda i, j: (i, 0)),
            pl.BlockSpec((1, LANES), lambda i, j: (0, 0)),
            pl.BlockSpec((2, tm), lambda i, j: (0, i)),
        ],
        out_shape=[
            jax.ShapeDtypeStruct((t, d), F32),
            jax.ShapeDtypeStruct((t, LANES), F32),
            jax.ShapeDtypeStruct((1, LANES), F32),
            jax.ShapeDtypeStruct((2, t), jnp.int32),
        ],
        scratch_shapes=[pltpu.VMEM((tm, 1), F32), pltpu.VMEM((tm, LANES), F32)],
        compiler_params=_params(2),
        name="out_proj",
    )(att, lru, w, w, xp, xs, g, wr, br)


ROUTE_GROUP_LANE = EXPERTS_PER_GROUP
ROUTE_RANK_LANE = EXPERTS_PER_GROUP + 1
ROUTE_ROWS = 256


def _route_rows(logits, cnt_ref):
    lane = lax.broadcasted_iota(jnp.int32, logits.shape, 1).astype(F32)
    ninf = -jnp.inf

    def first_argmax(v, vmax):
        return jnp.min(jnp.where(v == vmax, lane, float(LANES)), axis=-1, keepdims=True)

    gl = jnp.where(lane < N_GROUPS, logits, ninf)
    gm = jnp.max(gl, axis=-1, keepdims=True)
    g_idx = first_argmax(gl, gm)
    g_w = 1.0 / jnp.sum(jnp.exp(gl - gm), axis=-1, keepdims=True)

    lo = N_GROUPS + EXPERTS_PER_GROUP * g_idx
    el = jnp.where(jnp.logical_and(lane >= lo, lane < lo + EXPERTS_PER_GROUP), logits, ninf)
    v1 = jnp.max(el, axis=-1, keepdims=True)
    i1 = first_argmax(el, v1)
    el2 = jnp.where(lane == i1, ninf, el)
    v2 = jnp.max(el2, axis=-1, keepdims=True)
    i2 = first_argmax(el2, v2)
    e2 = jnp.exp(v2 - v1)
    w1 = (1.0 / (1.0 + e2)) * g_w
    w2 = (e2 / (1.0 + e2)) * g_w
    gates = jnp.where(lane == i1 - lo, w1, 0.0) + jnp.where(lane == i2 - lo, w2, 0.0)

    rows = logits.shape[0]
    onehot = jnp.where(lane == g_idx, 1.0, 0.0)
    r_i = lax.broadcasted_iota(jnp.int32, (rows, rows), 0)
    c_i = lax.broadcasted_iota(jnp.int32, (rows, rows), 1)
    tri = jnp.where(c_i < r_i, 1.0, 0.0).astype(BF16)
    before = jnp.dot(tri, onehot.astype(BF16), preferred_element_type=F32) + cnt_ref[...]
    rank = jnp.sum(onehot * before, axis=-1, keepdims=True)
    cnt_ref[...] += jnp.sum(onehot, axis=0, keepdims=True)
    return (gates + jnp.where(lane == ROUTE_GROUP_LANE, g_idx, 0.0)
            + jnp.where(lane == ROUTE_RANK_LANE, rank, 0.0))


PLAN_CHUNK = 1024


def _plan_items(per, n_used, n_items_max, grp, p_ref, e_ref, key_ref, n_ref, run_ref):
    tstart_s, tend_s, tiles_s, before_s, next_s = grp
    n_items = per * n_used
    n_ref[0] = n_items

    def item(w, carry):
        valid = w < n_items
        wc = jnp.minimum(w, n_items - 1)
        gi = jnp.int32(0)
        for g in range(N_GROUPS):
            gi = gi + (wc >= per * tend_s[g]).astype(jnp.int32)
        local = wc - per * tstart_s[gi]
        tiles_w = tiles_s[gi]
        lpart = jnp.int32(0)
        for k in range(1, per):
            lpart = lpart + (local >= k * tiles_w).astype(jnp.int32)
        ltile = local - lpart * tiles_w
        spare = jnp.maximum(w - n_items, 0)
        part = jnp.where(valid, lpart, spare % per)
        p_ref[w] = jnp.where(valid, tstart_s[gi] + ltile, n_used + spare // per)
        e_ref[w] = part
        key = gi * per + part
        key_ref[w] = key
        nxt_g = next_s[gi]
        run_ref[3 * w] = jnp.logical_and(valid, ltile == 0).astype(jnp.int32)
        run_ref[3 * w + 1] = (per * before_s[gi] + part) % 2
        run_ref[3 * w + 2] = jnp.where(part + 1 < per, key + 1,
                                       jnp.where(nxt_g >= 0, nxt_g * per, -1))
        return carry
    lax.fori_loop(0, n_items_max, item, 0)


def _plan_kernel(group_ref, rank_ref, cnt_ref, dest_ref, src_ref, nused_ref,
                 up_p, up_e, up_key, up_n, up_run, dn_p, dn_e, dn_key, dn_n, dn_run,
                 tstart_s, tend_s, tiles_s, before_s, next_s, row0_s,
                 *, tile, n_tiles_max, up_parts, dn_parts):
    s = pl.program_id(0)

    @pl.when(s == 0)
    def _():
        start = jnp.int32(0)
        owned = jnp.int32(0)
        for g in range(N_GROUPS):
            tiles = (cnt_ref[g] + tile - 1) // tile
            tstart_s[g] = start
            row0_s[g] = start * tile
            tiles_s[g] = tiles
            before_s[g] = owned
            start = start + tiles
            tend_s[g] = start
            owned = owned + (tiles > 0).astype(jnp.int32)
        nxt = jnp.int32(-1)
        for g in reversed(range(N_GROUPS)):
            next_s[g] = nxt
            nxt = jnp.where(tiles_s[g] > 0, g, nxt)
        nused_ref[0] = start

        def clear(r, carry):
            src_ref[r] = 0
            return carry
        lax.fori_loop(0, n_tiles_max * tile, clear, 0, unroll=8)

        grp = (tstart_s, tend_s, tiles_s, before_s, next_s)
        _plan_items(up_parts, start, up_parts * n_tiles_max, grp, up_p, up_e, up_key, up_n, up_run)
        _plan_items(dn_parts, start, dn_parts * n_tiles_max, grp, dn_p, dn_e, dn_key, dn_n, dn_run)

    def place(i, carry):
        t = s * PLAN_CHUNK + i
        d = row0_s[group_ref[i]] + rank_ref[i]
        dest_ref[t] = d
        src_ref[d] = t
        return carry
    lax.fori_loop(0, PLAN_CHUNK, place, 0, unroll=8)


def _moe_plan(group, rank, counts, *, tile, n_tiles_max, up_parts, dn_parts):
    t = group.shape[0]
    i32 = jnp.int32
    smem = functools.partial(pl.BlockSpec, memory_space=pltpu.SMEM)
    n_up, n_dn = up_parts * n_tiles_max, dn_parts * n_tiles_max
    shapes = [(t,), (n_tiles_max * tile,), (1,),
              (n_up,), (n_up,), (n_up,), (1,), (3 * n_up,),
              (n_dn,), (n_dn,), (n_dn,), (1,), (3 * n_dn,)]
    outs = pl.pallas_call(
        functools.partial(_plan_kernel, tile=tile, n_tiles_max=n_tiles_max,
                          up_parts=up_parts, dn_parts=dn_parts),
        grid=(t // PLAN_CHUNK,),
        in_specs=[smem((PLAN_CHUNK,), lambda s: (s,)), smem((PLAN_CHUNK,), lambda s: (s,)),
                  smem()],
        out_specs=[smem() for _ in shapes],
        out_shape=[jax.ShapeDtypeStruct(sh, i32) for sh in shapes],
        scratch_shapes=[pltpu.SMEM((N_GROUPS,), i32) for _ in range(6)],
        compiler_params=_params(1),
        name="moe_plan",
    )(group, rank, counts)
    dest, src, n_used = outs[:3]
    return dict(dest=dest, src=src, n_used=n_used, up=tuple(outs[3:8]), down=tuple(outs[8:13]))


def _row_copy(src_hbm, dst_vmem, src_row, dst_row, sem):
    return pltpu.make_async_copy(src_hbm.at[pl.ds(src_row, 1)], dst_vmem.at[pl.ds(dst_row, 1)], sem)


def _start_row_gather(idx_ref, base, n_rows, pairs):
    def body(r, carry):
        i = idx_ref[base + r]
        for src_hbm, dst_vmem, sem in pairs:
            _row_copy(src_hbm, dst_vmem, i, r, sem).start()
        return carry
    lax.fori_loop(0, n_rows, body, 0, unroll=8)


def _start_rows(idx_ref, idx_base, row0, n_rows, pairs):
    for k in range(n_rows):
        i = idx_ref[idx_base + row0 + k]
        for n, (src_hbm, dst_vmem, sem) in enumerate(pairs):
            _row_copy(src_hbm, dst_vmem, i, row0 + k, sem).start(priority=n % 2)


def _norm_tile_and_prefetch(load_rows, g_ref, o_ref, n_rows, has_next, idx_ref, next_base,
                            next_pairs):
    @pl.when(has_next)
    def _():
        _rmsnorm_rows(load_rows, g_ref, o_ref, n_rows,
                      per_trip=lambda row0: _start_rows(idx_ref, next_base, row0, NORM_ROWS,
                                                        next_pairs))

    @pl.when(jnp.logical_not(has_next))
    def _():
        _rmsnorm_rows(load_rows, g_ref, o_ref, n_rows)


def _wait_row_gather(n_rows, pairs):
    for src_hbm, dst_vmem, sem in pairs:
        pltpu.make_async_copy(src_hbm.at[pl.ds(0, n_rows)], dst_vmem, sem).wait()


def _dispatch_kernel(src_ref, nused_ref, x2_hbm, route_hbm, g_ref, xs_ref, gs_ref,
                     xbuf, gbuf, sem):
    p = pl.program_id(0)
    n_used = nused_ref[0]
    tile = xbuf.shape[1]
    slot = p % 2

    def pairs(s):
        return [(x2_hbm, xbuf.at[s], sem.at[0, s]), (route_hbm, gbuf.at[s], sem.at[1, s])]

    @pl.when(p == 0)
    def _():
        _start_row_gather(src_ref, 0, tile, pairs(0))

    @pl.when(p < n_used)
    def _():
        _wait_row_gather(tile, pairs(slot))
        gs_ref[...] = gbuf[slot]
        _norm_tile_and_prefetch(lambda rows: xbuf[slot, rows, :], g_ref, xs_ref, tile,
                                p + 1 < n_used, src_ref, (p + 1) * tile, pairs(1 - slot))

    @pl.when(p >= n_used)
    def _():
        xs_ref[...] = jnp.zeros_like(xs_ref)
        gs_ref[...] = jnp.zeros_like(gs_ref)


def _dispatch(x2, route, g, src, n_used, *, tile, n_tiles_max):
    t, d = x2.shape
    tile_map = lambda p, src_ref, n_ref: (p, 0)
    grid_spec = pltpu.PrefetchScalarGridSpec(
        num_scalar_prefetch=2,
        grid=(n_tiles_max,),
        in_specs=[
            pl.BlockSpec(memory_space=pl.ANY),
            pl.BlockSpec(memory_space=pl.ANY),
            pl.BlockSpec((1, d), lambda p, s, n: (0, 0)),
        ],
        out_specs=[
            pl.BlockSpec((tile, d), tile_map),
            pl.BlockSpec((tile, LANES), tile_map),
        ],
        scratch_shapes=[
            pltpu.VMEM((2, tile, d), F32),
            pltpu.VMEM((2, tile, LANES), F32),
            pltpu.SemaphoreType.DMA((2, 2)),
        ],
    )
    return pl.pallas_call(
        _dispatch_kernel,
        grid_spec=grid_spec,
        out_shape=[
            jax.ShapeDtypeStruct((n_tiles_max * tile, d), BF16),
            jax.ShapeDtypeStruct((n_tiles_max * tile, LANES), F32),
        ],
        compiler_params=_params(1),
        name="moe_dispatch",
    )(src, n_used, x2, route, g)


def _stream_run_weights(w, n_items, key_ref, run_ref, copies):
    is_first = run_ref[3 * w]
    slot = run_ref[3 * w + 1]
    nxt_key = run_ref[3 * w + 2]
    valid = w < n_items

    @pl.when(jnp.logical_and(valid, w == 0))
    def _():
        for cp in copies(key_ref[0], 0):
            cp.start()

    @pl.when(jnp.logical_and(valid, is_first == 1))
    def _():
        for cp in copies(key_ref[w], slot):
            cp.wait()

        @pl.when(nxt_key >= 0)
        def _():
            for cp in copies(nxt_key, 1 - slot):
                cp.start()

    return slot


def _moe_up_kernel(ip_ref, ie_ref, key_ref, n_ref, run_ref, xs_ref, gs_ref, w1_hbm, w3_hbm,
                   hid_ref, w1_buf, w3_buf, sem):
    w = pl.program_id(0)

    n_e = w1_buf.shape[1]
    f = w1_buf.shape[3]

    def copies(key, slot):
        experts = pl.ds(key * n_e, n_e)
        return [pltpu.make_async_copy(w1_hbm.at[experts], w1_buf.at[slot], sem.at[0, slot]),
                pltpu.make_async_copy(w3_hbm.at[experts], w3_buf.at[slot], sem.at[1, slot])]

    slot = _stream_run_weights(w, n_ref[0], key_ref, run_ref, copies)

    @pl.when(w < n_ref[0])
    def _():
        x = xs_ref[...]
        lane = lax.broadcasted_iota(jnp.int32, gs_ref.shape, 1)
        for e in range(n_e):
            a = jnp.dot(x, w1_buf[slot, e].astype(BF16), preferred_element_type=F32)
            b = jnp.dot(x, w3_buf[slot, e].astype(BF16), preferred_element_type=F32)
            in_group = ie_ref[w] * n_e + e
            gate = jnp.sum(jnp.where(lane == in_group, gs_ref[...], 0.0), axis=-1, keepdims=True)
            hid_ref[:, e * f:(e + 1) * f] = (jax.nn.silu(a) * b * gate).astype(hid_ref.dtype)

    @pl.when(w >= n_ref[0])
    def _():
        hid_ref[...] = jnp.zeros_like(hid_ref)


def _moe_up(xs, gs, w1, w3, items, *, tile):
    rows, d = xs.shape
    f = w1.shape[2]
    ip, ie, key, n_items, run = items
    grid_spec = pltpu.PrefetchScalarGridSpec(
        num_scalar_prefetch=5,
        grid=(ip.shape[0],),
        in_specs=[
            pl.BlockSpec((tile, d), lambda w, ip, ie, key, n, run: (ip[w], 0)),
            pl.BlockSpec((tile, LANES), lambda w, ip, ie, key, n, run: (ip[w], 0)),
            pl.BlockSpec(memory_space=pl.ANY),
            pl.BlockSpec(memory_space=pl.ANY),
        ],
        out_specs=pl.BlockSpec((tile, MOE_UP_EXPERTS * f),
                               lambda w, ip, ie, key, n, run: (ip[w], ie[w])),
        scratch_shapes=[
            pltpu.VMEM((2, MOE_UP_EXPERTS, d, f), w1.dtype),
            pltpu.VMEM((2, MOE_UP_EXPERTS, d, f), w3.dtype),
            pltpu.SemaphoreType.DMA((2, 2)),
        ],
    )
    return pl.pallas_call(
        _moe_up_kernel,
        grid_spec=grid_spec,
        out_shape=jax.ShapeDtypeStruct((rows, EXPERTS_PER_GROUP * f), BF16),
        compiler_params=_params(1),
        name="moe_up",
    )(ip, ie, key, n_items, run, xs, gs, w1, w3)


def _moe_down_kernel(ip_ref, ih_ref, key_ref, n_ref, run_ref, hid_ref, w2_hbm, y_ref,
                     w2_buf, sem, *, n_halves):
    w = pl.program_id(0)
    tn = w2_buf.shape[2]

    def copies(key, slot):
        g = key // n_halves
        col = pl.multiple_of((key % n_halves) * tn, tn)
        return [pltpu.make_async_copy(w2_hbm.at[g, :, pl.ds(col, tn)], w2_buf.at[slot],
                                      sem.at[slot])]

    slot = _stream_run_weights(w, n_ref[0], key_ref, run_ref, copies)

    @pl.when(w < n_ref[0])
    def _():
        y_ref[...] = jnp.dot(hid_ref[...], w2_buf[slot].astype(BF16),
                             preferred_element_type=F32)

    @pl.when(w >= n_ref[0])
    def _():
        y_ref[...] = jnp.zeros_like(y_ref)


def _moe_down(hid, w2g, items, *, tile, n_halves):
    rows, k = hid.shape
    d = w2g.shape[2]
    tn = d // n_halves
    ip, ih, key, n_items, run = items
    grid_spec = pltpu.PrefetchScalarGridSpec(
        num_scalar_prefetch=5,
        grid=(ip.shape[0],),
        in_specs=[
            pl.BlockSpec((tile, k), lambda w, ip, ih, key, n, run: (ip[w], 0)),
            pl.BlockSpec(memory_space=pl.ANY),
        ],
        out_specs=pl.BlockSpec((tile, tn), lambda w, ip, ih, key, n, run: (ip[w], ih[w])),
        scratch_shapes=[
            pltpu.VMEM((2, k, tn), w2g.dtype),
            pltpu.SemaphoreType.DMA((2,)),
        ],
    )
    return pl.pallas_call(
        functools.partial(_moe_down_kernel, n_halves=n_halves),
        grid_spec=grid_spec,
        out_shape=jax.ShapeDtypeStruct((rows, d), F32),
        compiler_params=_params(1),
        name="moe_down",
    )(ip, ih, key, n_items, run, hid, w2g)


def _final_kernel(dest_ref, x_ref, ys_hbm, g_ref, o_ref, ybuf, sem, *, tile0):
    i = pl.program_id(0)
    tm = ybuf.shape[1]
    slot = i % 2

    def pairs(s):
        return [(ys_hbm, ybuf.at[s], sem.at[s])]

    @pl.when(i == 0)
    def _():
        _start_row_gather(dest_ref, tile0 * tm, tm, pairs(0))

    _wait_row_gather(tm, pairs(slot))
    _norm_tile_and_prefetch(lambda rows: x_ref[rows, :] + ybuf[slot, rows, :], g_ref, o_ref, tm,
                            i + 1 < pl.num_programs(0), dest_ref, (tile0 + i + 1) * tm,
                            pairs(1 - slot))


def _final(x2, ys, dest, g, *, row0, n_rows, tm):
    d = x2.shape[1]
    off = row0 // tm
    grid_spec = pltpu.PrefetchScalarGridSpec(
        num_scalar_prefetch=1,
        grid=(n_rows // tm,),
        in_specs=[
            pl.BlockSpec((tm, d), lambda i, dest: (i + off, 0)),
            pl.BlockSpec(memory_space=pl.ANY),
            pl.BlockSpec((1, d), lambda i, dest: (0, 0)),
        ],
        out_specs=pl.BlockSpec((tm, d), lambda i, dest: (i, 0)),
        scratch_shapes=[pltpu.VMEM((2, tm, d), F32), pltpu.SemaphoreType.DMA((2,))],
    )
    return pl.pallas_call(
        functools.partial(_final_kernel, tile0=off),
        grid_spec=grid_spec,
        out_shape=jax.ShapeDtypeStruct((n_rows, d), F32),
        compiler_params=_params(1),
        name="final_norm",
    )(dest, x2, ys, g)


def _rope_tables(pos):
    half = ROT_DIM // 2
    inv = ROPE_THETA ** (-np.arange(half, dtype=np.float64) / half)
    ang = np.asarray(pos, np.float64)[:, None] * inv[None, :]
    cos, sin = np.cos(ang), np.sin(ang)
    ones = np.ones((ang.shape[0], HEAD_DIM - ROT_DIM))
    zeros = np.zeros((ang.shape[0], HEAD_DIM - half))
    c = np.concatenate([cos, cos, ones], axis=1)
    s_hi = np.concatenate([-sin, zeros], axis=1)
    s_lo = np.concatenate([np.zeros_like(sin), sin, 0.0 * ones], axis=1)
    rep = LANES // HEAD_DIM
    return tuple(jnp.asarray(np.tile(t, (1, rep)), F32) for t in (c, s_hi, s_lo))


def kernel(x_prompt, x_sample, cache_k, cache_v, state_conv, state_h, norm_mix, w_in, conv_w,
           conv_b, w_gate_a, b_gate_a, w_gate_x, b_gate_x, lru_lambda, sinks, norm_attn_out,
           norm_lru_out, w_out, norm_ffn, w_group, b_group, w_expert_router, b_expert_router,
           w1, w3, w2, norm_final):
    batch, seq, d_model = x_prompt.shape
    dec_batch, dec_seq, _ = x_sample.shape
    depth = w_in.shape[0]
    assert depth == 1 and dec_seq == CHUNK and seq % CHUNK == 0
    n_heads = sinks.shape[1]
    att_width = n_heads * HEAD_DIM
    n_kv = cache_k.shape[3]
    kv_cols = n_kv * HEAD_DIM
    lru_width = lru_lambda.shape[1]
    cw = cache_k.shape[2]
    assert cw == WIN_CHUNKS * CHUNK
    n_p, n_s = batch * seq, dec_batch * dec_seq
    chunks_per_seq = seq // CHUNK
    n_prompt_chunks = n_p // CHUNK

    xp = x_prompt.reshape(n_p, d_model)
    xs = x_sample.reshape(n_s, d_model)

    w_r = w_in[0]
    tn = 2 * kv_cols
    assert tn == 512 and att_width % tn == 0 and lru_width % tn == 0
    n_q_tiles = att_width // tn
    kv_tile = n_q_tiles

    tm_in = PROJ_ROWS
    assert seq % tm_in == 0 and n_s % tm_in == 0 and tm_in % dec_seq == 0
    pos = np.concatenate([np.arange(seq), np.tile(PAST_LEN + np.arange(dec_seq), tm_in // dec_seq)])
    c_tab, shi_tab, slo_tab = _rope_tables(pos)

    z = _in_proj(xp, xs, norm_mix, w_r, c_tab, shi_tab, slo_tab, tm=tm_in, tn=tn,
                 n_q_tiles=n_q_tiles, kv_tile=kv_tile, prompt_tiles_per_seq=seq // tm_in)

    k_col = att_width + 2 * lru_width
    v_col = k_col + kv_cols
    att = _attention(
        z, cache_k[0].reshape(dec_batch, cw, kv_cols), cache_v[0].reshape(dec_batch, cw, kv_cols),
        sinks[0], norm_attn_out,
        n_prompt_chunks=n_prompt_chunks, chunks_per_seq=chunks_per_seq, att_width=att_width,
        n_kv=n_kv, k_col_blk=k_col // kv_cols, v_col_blk=v_col // kv_cols)

    lru, h_tiles = _rglru(
        z, state_conv[0], state_h[0].reshape(dec_batch, 1, lru_width), conv_w[0], conv_b,
        w_gate_a[0].astype(BF16), b_gate_a[0].reshape(1, lru_width),
        w_gate_x[0].astype(BF16), b_gate_x[0].reshape(1, lru_width),
        lru_lambda, norm_lru_out,
        n_prompt_chunks=n_prompt_chunks, chunks_per_seq=chunks_per_seq, width=lru_width,
        xb_col_blk=att_width // lru_width, yb_col_blk=att_width // lru_width + 1)

    n_routes = N_GROUPS + N_GROUPS * EXPERTS_PER_GROUP
    wr = jnp.concatenate([w_group[0], w_expert_router[0],
                          jnp.zeros((d_model, LANES - n_routes), F32)], axis=1).astype(BF16)
    br = jnp.concatenate([b_group[0], b_expert_router[0],
                          jnp.zeros((LANES - n_routes,), F32)]).reshape(1, LANES)
    x2, route, counts, meta = _out_proj(att, lru, w_out[0], xp, xs, norm_ffn, wr, br,
                                        tm=PROJ_ROWS, tn=OUT_PROJ_COLS)

    n_tiles_max = (n_p + n_s) // MOE_TILE + N_GROUPS
    tabs = _moe_plan(meta[0], meta[1], counts[0, :N_GROUPS].astype(jnp.int32), tile=MOE_TILE,
                     n_tiles_max=n_tiles_max, up_parts=EXPERTS_PER_GROUP // MOE_UP_EXPERTS,
                     dn_parts=MOE_DOWN_HALVES)
    xsort, gsort = _dispatch(x2, route, norm_ffn, tabs["src"], tabs["n_used"],
                             tile=MOE_TILE, n_tiles_max=n_tiles_max)
    hid = _moe_up(xsort, gsort, w1[0], w3[0], tabs["up"], tile=MOE_TILE)
    d_expert = w2.shape[2]
    w2g = w2[0].reshape(N_GROUPS, EXPERTS_PER_GROUP * d_expert, d_model)
    ysort = _moe_down(hid, w2g, tabs["down"], tile=MOE_TILE, n_halves=MOE_DOWN_HALVES)

    g_fin = norm_final.reshape(1, d_model)
    y_prompt = _final(x2, ysort, tabs["dest"], g_fin, row0=0, n_rows=n_p,
                      tm=FINAL_ROWS).reshape(batch, seq, d_model)
    y_sample = _final(x2, ysort, tabs["dest"], g_fin, row0=n_p, n_rows=n_s,
                      tm=FINAL_ROWS).reshape(dec_batch, dec_seq, d_model)

    win = min(WIN_CHUNKS * CHUNK, seq)

    def prompt_tail(col, width, rows):
        return jnp.stack([lax.slice(z, ((b + 1) * seq - rows, col), ((b + 1) * seq, col + width))
                          for b in range(batch)])

    def sample_rows(col, width):
        return lax.slice(z, (n_p, col), (n_p + n_s, col + width)).reshape(dec_batch, dec_seq, width)

    k_prompt = prompt_tail(k_col, kv_cols, win).reshape(1, batch, win, n_kv, HEAD_DIM)
    v_prompt = prompt_tail(v_col, kv_cols, win).reshape(1, batch, win, n_kv, HEAD_DIM)
    conv_prompt = prompt_tail(att_width, lru_width, CONV_WIDTH - 1)[None]
    h_prompt = h_tiles[:n_prompt_chunks, 0].reshape(batch, chunks_per_seq, lru_width)[:, -1][None]

    ks = sample_rows(k_col, kv_cols).reshape(dec_batch, dec_seq, n_kv, HEAD_DIM)
    vs = sample_rows(v_col, kv_cols).reshape(dec_batch, dec_seq, n_kv, HEAD_DIM)
    k_sample = jnp.concatenate([cache_k[0], ks], axis=1)[:, -cw:][None]
    v_sample = jnp.concatenate([cache_v[0], vs], axis=1)[:, -cw:][None]
    xs_rows = sample_rows(att_width, lru_width)[:, dec_seq - (CONV_WIDTH - 1):]
    conv_sample = jnp.concatenate([state_conv[0], xs_rows], axis=1)[:, -(CONV_WIDTH - 1):][None]
    h_sample = h_tiles[n_prompt_chunks:, 0][None]

    return (y_prompt, y_sample, k_prompt, v_prompt, conv_prompt, h_prompt,
            k_sample, v_sample, conv_sample, h_sample)
```
